```python
import math
import jax, jax.numpy as jnp
from jax import lax
import numpy as np

D_MODEL = 1024
BATCH = 1
SEQ = 16384
DEPTH = 2
DEC_BATCH = 16
DEC_SEQ = 16
PAST_LEN = 2048

CHUNK = 64
MIX_WIDTH = D_MODEL
GROUP_WIDTH = MIX_WIDTH // 4
A_HEADS = 4
A_HEAD_DIM = GROUP_WIDTH // A_HEADS
A_PREV_CHUNKS = 8
A_BAND_PREV = A_PREV_CHUNKS * CHUNK
REL_CLIP = 128
B_HEADS = 4
B_HEAD_DIM = GROUP_WIDTH // B_HEADS
B_GROUPS = 2
B_STATE = 128
B_CONV = 4
B_CONV_DIM = GROUP_WIDTH + 2 * B_GROUPS * B_STATE
C_HEADS = 4
C_KEY_DIM = GROUP_WIDTH // C_HEADS
C_VAL_DIM = GROUP_WIDTH // C_HEADS
D_WINDOWS = (2, 4, 8, 16)
D_GROUP = GROUP_WIDTH // 4
D_BUF = 15
D_FF = -(-8 * D_MODEL // (3 * 256)) * 256
SPLIT_SIZES = (GROUP_WIDTH, GROUP_WIDTH, GROUP_WIDTH,
               GROUP_WIDTH, B_CONV_DIM, B_HEADS,
               GROUP_WIDTH, GROUP_WIDTH, GROUP_WIDTH, GROUP_WIDTH,
               GROUP_WIDTH)
IN_WIDTH = 3 * GROUP_WIDTH + GROUP_WIDTH + B_CONV_DIM + B_HEADS + 4 * GROUP_WIDTH + GROUP_WIDTH
EPS = 1e-6

kernel_name = 'hybrid_chunk_streaming_encoder_step'


def _rmsnorm(x, w):
    xf = x.astype(jnp.float32)
    y = xf * lax.rsqrt(jnp.mean(xf * xf, axis=-1, keepdims=True) + EPS)
    return (y * w.astype(jnp.float32)).astype(x.dtype)


def _to_chunks(a, lp):
    b, l = a.shape[:2]
    a = jnp.pad(a, [(0, 0), (0, lp - l)] + [(0, 0)] * (a.ndim - 2))
    return jnp.moveaxis(a.reshape((b, lp // CHUNK, CHUNK) + a.shape[2:]), 1, 0)


def _from_chunks(a, l):
    a = jnp.moveaxis(a, 0, 1)
    return a.reshape((a.shape[0], -1) + a.shape[3:])[:, :l]


def _band_attention(q, k, v, k_prev, v_prev, rel_bias):
    b, l, h, dh = q.shape
    p = k_prev.shape[1]
    lp = -(-l // CHUNK) * CHUNK
    nc = lp // CHUNK
    pad_new = ((0, 0), (0, lp - l), (0, 0), (0, 0))
    pad_old = ((0, 0), (A_BAND_PREV - p, 0), (0, 0), (0, 0))
    kf = jnp.concatenate([jnp.pad(k_prev, pad_old), jnp.pad(k, pad_new)], axis=1)
    vf = jnp.concatenate([jnp.pad(v_prev, pad_old), jnp.pad(v, pad_new)], axis=1)
    pos = jnp.arange(A_BAND_PREV + lp)
    valid = ((pos >= A_BAND_PREV - p) & (pos < A_BAND_PREV + l)).reshape(nc + A_PREV_CHUNKS, CHUNK)
    kc = kf.reshape(b, nc + A_PREV_CHUNKS, CHUNK, h, dh)
    vc = vf.reshape(b, nc + A_PREV_CHUNKS, CHUNK, h, dh)
    n_band = A_PREV_CHUNKS + 1
    kb = jnp.concatenate([kc[:, i:i + nc] for i in range(n_band)], axis=2)
    vb = jnp.concatenate([vc[:, i:i + nc] for i in range(n_band)], axis=2)
    mb = jnp.concatenate([valid[i:i + nc] for i in range(n_band)], axis=1)
    qc = jnp.pad(q, pad_new).reshape(b, nc, CHUNK, h, dh)
    rel = A_BAND_PREV + jnp.arange(CHUNK)[:, None] - jnp.arange(n_band * CHUNK)[None, :]
    bias = rel_bias[jnp.clip(rel, -REL_CLIP, REL_CLIP) + REL_CLIP]
    s = jnp.einsum('bcihd,bcjhd->bchij', qc, kb).astype(jnp.float32) * (dh ** -0.5)
    s = s + jnp.transpose(bias, (2, 0, 1)).astype(jnp.float32)
    s = jnp.where(mb[None, :, None, None, :], s, -1e30)
    pr = jax.nn.softmax(s, axis=-1).astype(vb.dtype)
    o = jnp.einsum('bchij,bcjhd->bcihd', pr, vb).reshape(b, lp, h, dh)[:, :l]
    return o


def _causal_dwconv(x, buf, w, bias):
    l = x.shape[1]
    xp = jnp.concatenate([buf, x], axis=1)
    y = bias
    for i in range(B_CONV):
        y = y + xp[:, i:i + l] * w[i]
    return jax.nn.silu(y), xp[:, -(B_CONV - 1):]


def _ssd_scan(u, loga, bm, cm, s0):
    l = u.shape[1]
    lp = -(-l // CHUNK) * CHUNK
    xs = tuple(_to_chunks(t.astype(jnp.float32), lp) for t in (u, loga, bm, cm))
    causal = jnp.tril(jnp.ones((CHUNK, CHUNK), dtype=bool))[None, :, :, None]

    def step(state, inp):
        uc, ac, bc, cc = inp
        acs = jnp.cumsum(ac, axis=1)
        decay = jnp.exp(jnp.where(causal, acs[:, :, None] - acs[:, None, :], -jnp.inf))
        y = jnp.einsum('bijh,bjhp->bihp', jnp.einsum('bihn,bjhn->bijh', cc, bc) * decay, uc)
        y = y + jnp.einsum('bihn,bhpn->bihp', cc * jnp.exp(acs)[..., None], state)
        last = acs[:, -1:]
        state = (jnp.exp(last[:, 0])[:, :, None, None] * state
                 + jnp.einsum('bjhn,bjhp->bhpn', bc * jnp.exp(last - acs)[..., None], uc))
        return state, y

    s_fin, ys = lax.scan(step, s0.astype(jnp.float32), xs)
    return _from_chunks(ys, l), s_fin.astype(s0.dtype)


def _hgrn2_scan(q, k, v, logf, s0):
    l = q.shape[1]
    lp = -(-l // CHUNK) * CHUNK
    xs = tuple(_to_chunks(t.astype(jnp.float32), lp) for t in (q, k, v, logf))
    causal = jnp.tril(jnp.ones((CHUNK, CHUNK), dtype=bool))[None, :, :, None, None]

    def step(state, inp):
        qc, kc, vc, gc = inp
        bcs = jnp.cumsum(gc, axis=1)
        decay = jnp.exp(jnp.where(causal, bcs[:, :, None] - bcs[:, None, :], -jnp.inf))
        att = jnp.einsum('bihk,bjhk,bijhk->bijh', qc, kc, decay)
        o = jnp.einsum('bijh,bjhv->bihv', att, vc) + jnp.einsum('bihk,bhkv->bihv', qc * jnp.exp(bcs), state)
        last = bcs[:, -1:]
        state = (jnp.exp(last[:, 0])[..., None] * state
                 + jnp.einsum('bjhk,bjhv->bhkv', kc * jnp.exp(last - bcs), vc))
        return state, o

    s_fin, os_ = lax.scan(step, s0.astype(jnp.float32), xs)
    return _from_chunks(os_, l), s_fin.astype(s0.dtype)


def _multiscale_pool(xd, buf, w_pool, scale):
    b, l, _ = xd.shape
    xp = jnp.concatenate([buf, xd], axis=1)
    xf = xp.astype(jnp.float32)
    cs = jnp.concatenate([jnp.zeros_like(xf[:, :1]), jnp.cumsum(xf, axis=1)], axis=1)
    groups = []
    for g, win in enumerate(D_WINDOWS):
        c0, c1 = g * D_GROUP, (g + 1) * D_GROUP
        total = cs[:, D_BUF + 1:D_BUF + 1 + l, c0:c1] - cs[:, D_BUF + 1 - win:D_BUF + 1 - win + l, c0:c1]
        groups.append(total / win - xf[:, D_BUF:, c0:c1])
    pooled = jnp.stack(groups, axis=2)
    y = jnp.einsum('blgc,gcd->blgd', pooled, w_pool.astype(jnp.float32)).reshape(b, l, GROUP_WIDTH)
    y = y * scale.astype(jnp.float32)
    return y.astype(xd.dtype), xp[:, -D_BUF:]


def _layer(x, w, st, lb):
    (n1, w_in, a_rel, b_cw, b_cb, b_dtb, b_alog, b_d, b_nrm, c_nrm,
     d_pw, d_ps, w_out, n2, w_gu, w_dn) = w
    (k_prev, v_prev, conv_buf, ssm0, hgrn0, pool_buf) = st
    b, l, _ = x.shape
    h = _rmsnorm(x, n1)
    proj = h @ w_in
    split_points = [int(s) for s in np.cumsum(np.array(SPLIT_SIZES))[:-1]]
    (aq, ak, av, bz, bxbc, bdt, cq, cf, ci, cg, dx) = jnp.split(proj, split_points, axis=-1)

    def heads(t, n):
        return t.reshape(b, l, n, -1)

    qa, ka, va = heads(aq, A_HEADS), heads(ak, A_HEADS), heads(av, A_HEADS)
    oa = _band_attention(qa, ka, va, k_prev, v_prev, a_rel).reshape(b, l, GROUP_WIDTH)
    keep = min(A_BAND_PREV, k_prev.shape[1] + l)
    new_k = jnp.concatenate([k_prev, ka], axis=1)[:, -keep:]
    new_v = jnp.concatenate([v_prev, va], axis=1)[:, -keep:]

    xbc, new_conv = _causal_dwconv(bxbc, conv_buf, b_cw, b_cb)
    xs, bb, bc = jnp.split(xbc, [GROUP_WIDTH, GROUP_WIDTH + B_GROUPS * B_STATE], axis=-1)
    xs = heads(xs, B_HEADS).astype(jnp.float32)
    rep = B_HEADS // B_GROUPS
    bb = jnp.repeat(heads(bb, B_GROUPS), rep, axis=2)
    bc = jnp.repeat(heads(bc, B_GROUPS), rep, axis=2)
    dt = jax.nn.softplus(bdt.astype(jnp.float32) + b_dtb.astype(jnp.float32))
    a_neg = -jnp.exp(b_alog.astype(jnp.float32))
    ys, new_ssm = _ssd_scan(xs * dt[..., None], dt * a_neg, bb, bc, ssm0)
    ys = ys + b_d.astype(jnp.float32)[:, None] * xs
    ob = _rmsnorm(ys.reshape(b, l, GROUP_WIDTH) * jax.nn.silu(bz.astype(jnp.float32)), b_nrm).astype(x.dtype)

    f = lb + (1.0 - lb) * jax.nn.sigmoid(cf.astype(jnp.float32))
    oc, new_hgrn = _hgrn2_scan(heads(cq, C_HEADS), heads(1.0 - f, C_HEADS), heads(ci, C_HEADS),
                               heads(jnp.log(f), C_HEADS), hgrn0)
    oc = _rmsnorm(oc, c_nrm).reshape(b, l, GROUP_WIDTH) * jax.nn.silu(cg.astype(jnp.float32))
    oc = oc.astype(x.dtype)

    od, new_pool = _multiscale_pool(dx, pool_buf, d_pw, d_ps)

    x = x + jnp.concatenate([oa, ob, oc, od], axis=-1) @ w_out
    gate, up = jnp.split(_rmsnorm(x, n2) @ w_gu, 2, axis=-1)
    x = x + (jax.nn.silu(gate) * up) @ w_dn
    return x, (new_k, new_v, new_conv, new_ssm, new_hgrn, new_pool)


def _trunk(x, states, weights, lbs, norm_f):
    new = [[] for _ in range(6)]
    for layer in range(DEPTH):
        st_l = tuple(s[layer] for s in states)
        w_l = tuple(wt[layer] for wt in weights)
        x, ns = _layer(x, w_l, st_l, lbs[layer])
        for i in range(6):
            new[i].append(ns[i])
    return _rmsnorm(x, norm_f), tuple(jnp.stack(n, axis=0) for n in new)


def setup_inputs(seed: int = 0) -> dict:
    key = jax.random.key(seed)
    ks = jax.random.split(key, 32)
    f32 = jnp.float32
    lc = min(A_BAND_PREV, PAST_LEN)
    nrm = lambda k, shape, s: jax.random.normal(k, shape, f32) * s
    dt0 = jnp.exp(jax.random.uniform(ks[12], (DEPTH, B_HEADS), f32) * (math.log(0.1) - math.log(1e-3)) + math.log(1e-3))
    return {
        'x_prompt': nrm(ks[0], (BATCH, SEQ, D_MODEL), 1.0),
        'x_sample': nrm(ks[1], (DEC_BATCH, DEC_SEQ, D_MODEL), 1.0),
        'cache_a_k': nrm(ks[2], (DEPTH, DEC_BATCH, lc, A_HEADS, A_HEAD_DIM), 1.0),
        'cache_a_v': nrm(ks[3], (DEPTH, DEC_BATCH, lc, A_HEADS, A_HEAD_DIM), 1.0),
        'state_b_conv': nrm(ks[4], (DEPTH, DEC_BATCH, B_CONV - 1, B_CONV_DIM), 1.0),
        'state_b_ssm': nrm(ks[5], (DEPTH, DEC_BATCH, B_HEADS, B_HEAD_DIM, B_STATE), 0.5),
        'state_c_hgrn': nrm(ks[6], (DEPTH, DEC_BATCH, C_HEADS, C_KEY_DIM, C_VAL_DIM), 0.5),
        'state_d_pool': nrm(ks[7], (DEPTH, DEC_BATCH, D_BUF, GROUP_WIDTH), 1.0),
        'norm1': 1.0 + nrm(ks[8], (DEPTH, D_MODEL), 0.01),
        'w_in': nrm(ks[9], (DEPTH, D_MODEL, IN_WIDTH), D_MODEL ** -0.5),
        'a_rel_bias': nrm(ks[10], (DEPTH, 2 * REL_CLIP + 1, A_HEADS), 0.1),
        'b_conv_w': nrm(ks[11], (DEPTH, B_CONV, B_CONV_DIM), B_CONV ** -0.5),
        'b_conv_b': nrm(ks[13], (DEPTH, B_CONV_DIM), 0.01),
        'b_dt_bias': dt0 + jnp.log(-jnp.expm1(-dt0)),
        'b_a_log': jnp.log(jax.random.uniform(ks[14], (DEPTH, B_HEADS), f32, 1.0, 16.0)),
        'b_d': 1.0 + nrm(ks[15], (DEPTH, B_HEADS), 0.01),
        'b_norm': 1.0 + nrm(ks[16], (DEPTH, GROUP_WIDTH), 0.01),
        'c_lb_logits': nrm(ks[17], (DEPTH, GROUP_WIDTH), 1.0),
        'c_norm': 1.0 + nrm(ks[18], (DEPTH, C_HEADS, C_VAL_DIM), 0.01),
        'd_pool_w': nrm(ks[19], (DEPTH, 4, D_GROUP, D_GROUP), D_GROUP ** -0.5),
        'd_pool_scale': 0.5 + nrm(ks[20], (DEPTH, GROUP_WIDTH), 0.1),
        'w_out': nrm(ks[21], (DEPTH, MIX_WIDTH, D_MODEL), MIX_WIDTH ** -0.5),
        'norm2': 1.0 + nrm(ks[22], (DEPTH, D_MODEL), 0.01),
        'w_gate_up': nrm(ks[23], (DEPTH, D_MODEL, 2 * D_FF), D_MODEL ** -0.5),
        'w_down': nrm(ks[24], (DEPTH, D_FF, D_MODEL), D_FF ** -0.5),
        'norm_f': 1.0 + nrm(ks[25], (D_MODEL,), 0.01),
    }


def reference(x_prompt, x_sample, cache_a_k, cache_a_v, state_b_conv, state_b_ssm, state_c_hgrn, state_d_pool,
              norm1, w_in, a_rel_bias, b_conv_w, b_conv_b, b_dt_bias, b_a_log, b_d, b_norm, c_lb_logits, c_norm,
              d_pool_w, d_pool_scale, w_out, norm2, w_gate_up, w_down, norm_f):
    weights = (norm1, w_in, a_rel_bias, b_conv_w, b_conv_b, b_dt_bias, b_a_log, b_d, b_norm, c_norm,
               d_pool_w, d_pool_scale, w_out, norm2, w_gate_up, w_down)
    lbs = jnp.cumsum(jax.nn.softmax(c_lb_logits.astype(jnp.float32), axis=0), axis=0)
    lbs = lbs - lbs[:1]
    bp = x_prompt.shape[0]
    dtp = x_prompt.dtype
    prompt_states = (
        jnp.zeros((DEPTH, bp, 0, A_HEADS, A_HEAD_DIM), dtp),
        jnp.zeros((DEPTH, bp, 0, A_HEADS, A_HEAD_DIM), dtp),
        jnp.zeros((DEPTH, bp, B_CONV - 1, B_CONV_DIM), dtp),
        jnp.zeros((DEPTH, bp, B_HEADS, B_HEAD_DIM, B_STATE), dtp),
        jnp.zeros((DEPTH, bp, C_HEADS, C_KEY_DIM, C_VAL_DIM), dtp),
        jnp.zeros((DEPTH, bp, D_BUF, GROUP_WIDTH), dtp),
    )
    y_prompt, (pk, pv, pconv, pssm, phgrn, ppool) = _trunk(x_prompt, prompt_states, weights, lbs, norm_f)
    sample_states = (cache_a_k, cache_a_v, state_b_conv, state_b_ssm, state_c_hgrn, state_d_pool)
    y_sample, (sk, sv, sconv, sssm, shgrn, spool) = _trunk(x_sample, sample_states, weights, lbs, norm_f)
    return (y_prompt, y_sample, pk, pv, pconv, pssm, phgrn, ppool, sk, sv, sconv, sssm, shgrn, spool)
```

```python
import functools

import jax
import jax.numpy as jnp
from jax import lax
from jax.experimental import pallas as pl
from jax.experimental.pallas import tpu as pltpu

F32 = jnp.float32
BF16 = jnp.bfloat16

D_MODEL = 1024
GROUP_WIDTH = 256
HEADS = 4
HEAD_DIM = 64
CHUNK = 64
A_BAND_PREV = 512
A_BAND = A_BAND_PREV + CHUNK
REL_CLIP = 128
B_STATE = 128
B_CONV = 4
B_CONV_DIM = 768
D_BUF = 15
D_FF = 2816
EPS = 1e-6
NEG = -1e30
SUB = 16

COL_XBC = 0
COL_AQ = 768
COL_AK = 1024
COL_AV = 1280
COL_BZ = 1536
COL_CQ = 1792
COL_CF = 2048
COL_CI = 2304
COL_CG = 2560
COL_DX = 2816
COL_DT = 3072
PROJ_W = 3328

VMEM_LIMIT = 56 * 1024 * 1024


def _cparams(sem):
    return pltpu.CompilerParams(dimension_semantics=sem, vmem_limit_bytes=VMEM_LIMIT)


def _rms(x, w):
    return x * lax.rsqrt(jnp.mean(x * x, axis=-1, keepdims=True) + EPS) * w


def _silu(x):
    return x * jax.nn.sigmoid(x)


def _dot(a, b):
    return jnp.dot(a, b, preferred_element_type=F32)


def _dot_nt(a, b):
    return lax.dot_general(a, b, (((1,), (1,)), ((), ())), preferred_element_type=F32)


def _dot_tn(a, b):
    return lax.dot_general(a, b, (((0,), (0,)), ((), ())), preferred_element_type=F32)


def _split3(x):
    hi = x.astype(BF16)
    r = x - hi.astype(F32)
    mid = r.astype(BF16)
    lo = (r - mid.astype(F32)).astype(BF16)
    return hi, mid, lo


def _dot01_lhs(m01, x):
    hi, mid, lo = _split3(x)
    return _dot(m01, hi) + _dot(m01, mid) + _dot(m01, lo)


def _dot01_rhs(x, m01):
    hi, mid, lo = _split3(x)
    return _dot(hi, m01) + _dot(mid, m01) + _dot(lo, m01)


def _tril(t):
    r = lax.broadcasted_iota(jnp.int32, (t, t), 0)
    c = lax.broadcasted_iota(jnp.int32, (t, t), 1)
    return r >= c


def _head_ones():
    r = lax.broadcasted_iota(jnp.int32, (GROUP_WIDTH, GROUP_WIDTH), 0) // HEAD_DIM
    c = lax.broadcasted_iota(jnp.int32, (GROUP_WIDTH, GROUP_WIDTH), 1) // HEAD_DIM
    return r == c


def _inproj_kernel(x_ref, n_ref, w_ref, o_ref):
    hb = _rms(x_ref[...], n_ref[...]).astype(BF16)
    for c0 in range(0, PROJ_W, 256):
        o_ref[:, c0:c0 + 256] = _dot(hb, w_ref[:, c0:c0 + 256])


def _inproj(x2d, n1, w_in):
    rows = x2d.shape[0]
    tm = 512 if rows % 512 == 0 else rows
    return pl.pallas_call(
        _inproj_kernel,
        out_shape=jax.ShapeDtypeStruct((rows, PROJ_W), F32),
        grid=(rows // tm,),
        in_specs=[pl.BlockSpec((tm, D_MODEL), lambda i: (i, 0)),
                  pl.BlockSpec((1, D_MODEL), lambda i: (0, 0)),
                  pl.BlockSpec((D_MODEL, PROJ_W), lambda i: (0, 0), pipeline_mode=pl.Buffered(1))],
        out_specs=pl.BlockSpec((tm, PROJ_W), lambda i: (i, 0)),
        compiler_params=_cparams(("parallel",)),
        name="inproj",
    )(x2d, n1, w_in)


FF_CHUNK = 256


def _mlp_kernel(x_ref, oa_ref, ob_ref, oc_ref, od_ref, wo_ref, n2_ref, wgu_ref, wdn_ref, nf_ref, o_ref, mix_ref,
                act_ref, *, final):
    for m, r in enumerate((oa_ref, ob_ref, oc_ref, od_ref)):
        mix_ref[:, m * GROUP_WIDTH:(m + 1) * GROUP_WIDTH] = r[...].astype(BF16)
    x1 = x_ref[...] + _dot(mix_ref[...], wo_ref[...])
    hb = _rms(x1, n2_ref[...]).astype(BF16)
    for c0 in range(0, D_FF, FF_CHUNK):
        gate = _dot(hb, wgu_ref[:, c0:c0 + FF_CHUNK])
        up = _dot(hb, wgu_ref[:, D_FF + c0:D_FF + c0 + FF_CHUNK])
        act_ref[:, c0:c0 + FF_CHUNK] = (_silu(gate) * up).astype(BF16)
    out = x1 + _dot(act_ref[...], wdn_ref[...])
    if final:
        out = _rms(out, nf_ref[...])
    o_ref[...] = out


def _mlp(x2d, mix, w_out, n2, w_gu, w_dn, nf, final):
    rows = x2d.shape[0]
    tm = 512 if rows % 512 == 0 else rows
    row_spec = lambda w: pl.BlockSpec((tm, w), lambda i: (i, 0))
    const = lambda a: pl.BlockSpec(a.shape, lambda i: (0, 0), pipeline_mode=pl.Buffered(1))
    return pl.pallas_call(
        functools.partial(_mlp_kernel, final=final),
        out_shape=jax.ShapeDtypeStruct((rows, D_MODEL), F32),
        grid=(rows // tm,),
        in_specs=[row_spec(D_MODEL)] + [row_spec(GROUP_WIDTH)] * 4
                 + [const(w_out), const(n2), const(w_gu), const(w_dn), const(nf)],
        out_specs=row_spec(D_MODEL),
        scratch_shapes=[pltpu.VMEM((tm, D_MODEL), BF16), pltpu.VMEM((tm, D_FF), BF16)],
        compiler_params=_cparams(("parallel",)),
        name="mlp",
    )(x2d, *mix, w_out, n2, w_gu, w_dn, nf)


def _attn_kernel(q_ref, k0_ref, k1_ref, v0_ref, v1_ref, bias_ref, o_ref, kb_ref, vb_ref, *, qb, p_hist, l_valid):
    i = pl.program_id(1)
    kb_ref[0:A_BAND_PREV, :] = k0_ref[0]
    kb_ref[A_BAND_PREV:A_BAND_PREV + qb, :] = k1_ref[0]
    vb_ref[0:A_BAND_PREV, :] = v0_ref[0]
    vb_ref[A_BAND_PREV:A_BAND_PREV + qb, :] = v1_ref[0]
    lane_head = lax.broadcasted_iota(jnp.int32, (CHUNK, GROUP_WIDTH), 1) // HEAD_DIM
    band_pos = lax.broadcasted_iota(jnp.int32, (1, A_BAND), 1)

    def chunk(cc, carry):
        r0 = pl.multiple_of(cc * CHUNK, CHUNK)
        qc = q_ref[0, pl.ds(r0, CHUNK), :] * (HEAD_DIM ** -0.5)
        qs = jnp.concatenate([jnp.where(lane_head == h, qc, 0.0) for h in range(HEADS)], axis=0).astype(BF16)
        s = _dot_nt(qs, kb_ref[pl.ds(r0, A_BAND), :]) + bias_ref[...]
        kpos = i * qb + r0 - A_BAND_PREV + band_pos
        valid = (kpos >= -p_hist) & (kpos < l_valid)
        s = jnp.where(valid, s, NEG)
        m = jnp.max(s, axis=-1, keepdims=True)
        e = jnp.exp(s - m)
        den = jnp.sum(e, axis=-1, keepdims=True)
        o_all = _dot(e.astype(BF16), vb_ref[pl.ds(r0, A_BAND), :]) * (1.0 / den)
        o = jnp.zeros((CHUNK, GROUP_WIDTH), F32)
        for h in range(HEADS):
            o = o + jnp.where(lane_head == h, o_all[h * CHUNK:(h + 1) * CHUNK, :], 0.0)
        o_ref[0, pl.ds(r0, CHUNK), :] = o
        return carry

    lax.fori_loop(0, qb // CHUNK, chunk, 0)


def _attention(proj3, kfull, vfull, bias, l_valid, p_hist):
    b = proj3.shape[0]
    lq = kfull.shape[1] - A_BAND_PREV
    qb = A_BAND_PREV if lq % A_BAND_PREV == 0 else lq
    nblk = lq // qb
    assert nblk == 1 or qb == A_BAND_PREV
    hist_spec = pl.BlockSpec((1, A_BAND_PREV, GROUP_WIDTH), lambda bi, i: (bi, i, 0))
    cur_spec = pl.BlockSpec((1, qb, GROUP_WIDTH), lambda bi, i: (bi, i + A_BAND_PREV // qb, 0))
    return pl.pallas_call(
        functools.partial(_attn_kernel, qb=qb, p_hist=p_hist, l_valid=l_valid),
        out_shape=jax.ShapeDtypeStruct((b, lq, GROUP_WIDTH), F32),
        grid=(b, nblk),
        in_specs=[pl.BlockSpec((1, qb, GROUP_WIDTH), lambda bi, i: (bi, i, COL_AQ // GROUP_WIDTH)),
                  hist_spec, cur_spec, hist_spec, cur_spec,
                  pl.BlockSpec((HEADS * CHUNK, A_BAND), lambda bi, i: (0, 0))],
        out_specs=pl.BlockSpec((1, qb, GROUP_WIDTH), lambda bi, i: (bi, i, 0)),
        scratch_shapes=[pltpu.VMEM((A_BAND_PREV + qb, GROUP_WIDTH), BF16),
                        pltpu.VMEM((A_BAND_PREV + qb, GROUP_WIDTH), BF16)],
        compiler_params=_cparams(("parallel", "arbitrary")),
        name="attention",
    )(proj3, kfull, kfull, vfull, vfull, bias)


CONV_PAD = 8


def _ssd_kernel(xbc_ref, z_ref, dt_ref, conv0_ref, ssm0_ref, cw_ref, cb_ref, dtb_ref, alog_ref, d_ref, nrm_ref,
                o_ref, conv_out_ref, ssm_out_ref, xpad_ref, st_ref, *, t_blk, l_valid, n_blk):
    t = pl.program_id(1)
    hist = B_CONV - 1

    @pl.when(t == 0)
    def _():
        xpad_ref[CONV_PAD - hist:CONV_PAD, :] = conv0_ref[0]
        st_ref[...] = ssm0_ref[0].reshape(GROUP_WIDTH, B_STATE).T

    xpad_ref[CONV_PAD:CONV_PAD + t_blk, :] = xbc_ref[0]
    y = cb_ref[...]
    for i in range(B_CONV):
        y = y + xpad_ref[CONV_PAD - hist + i:CONV_PAD - hist + i + t_blk, :] * cw_ref[i:i + 1, :]
    xbc = _silu(y)
    n_last = l_valid - (n_blk - 1) * t_blk
    conv_out_ref[0] = xpad_ref[CONV_PAD + n_last - hist:CONV_PAD + n_last, :]
    xpad_ref[CONV_PAD - hist:CONV_PAD, :] = xpad_ref[CONV_PAD + t_blk - hist:CONV_PAD + t_blk, :]

    xs = xbc[:, 0:GROUP_WIDTH]
    bm = xbc[:, GROUP_WIDTH:2 * GROUP_WIDTH]
    cm = xbc[:, 2 * GROUP_WIDTH:3 * GROUP_WIDTH]
    dtr = dt_ref[0] + dtb_ref[...]
    dt = jnp.maximum(dtr, 0.0) + jnp.log1p(jnp.exp(-jnp.abs(dtr)))
    if l_valid < n_blk * t_blk:
        row = t * t_blk + lax.broadcasted_iota(jnp.int32, (t_blk, GROUP_WIDTH), 0)
        dt = jnp.where(row < l_valid, dt, 0.0)
    a = dt * (-jnp.exp(alog_ref[...]))
    u = xs * dt
    tril = _tril(t_blk)
    acs = _dot01_lhs(tril.astype(BF16), a)
    acs_t = acs.T
    eacs = jnp.exp(acs)
    last = acs[t_blk - 1:t_blk, :]
    elast = jnp.exp(last)
    ub = u.astype(BF16)
    upb = (u * jnp.exp(last - acs)).astype(BF16)
    lane = lax.broadcasted_iota(jnp.int32, (t_blk, B_STATE), 1)
    ys = []
    for g in range(2):
        gs = slice(g * B_STATE, (g + 1) * B_STATE)
        cg = cm[:, gs].astype(BF16)
        bg = bm[:, gs].astype(BF16)
        cb = _dot_nt(cg, bg)
        y_state = _dot(cg, st_ref[:, gs].astype(BF16)) * eacs[:, gs]
        parts = []
        for hh in range(2):
            c0 = (2 * g + hh) * HEAD_DIM
            col = jnp.broadcast_to(acs[:, c0:c0 + 1], (t_blk, t_blk))
            rowv = jnp.broadcast_to(acs_t[c0:c0 + 1, :], (t_blk, t_blk))
            dec = jnp.exp(jnp.where(tril, col - rowv, NEG))
            parts.append(_dot((cb * dec).astype(BF16), ub[:, gs]))
        ys.append(jnp.where(lane < HEAD_DIM, parts[0], parts[1]) + y_state)
        st_ref[:, gs] = st_ref[:, gs] * elast[:, gs] + _dot_tn(bg, upb[:, gs])
    yv = jnp.concatenate(ys, axis=1) + d_ref[...] * xs
    o_ref[0] = _rms(yv * _silu(z_ref[0]), nrm_ref[...])

    @pl.when(t == n_blk - 1)
    def _():
        ssm_out_ref[0] = st_ref[...].T.reshape(HEADS, HEAD_DIM, B_STATE)


def _ssd(proj3, conv0, ssm0, cw, cb, dtb, alog, dvec, nrm, l_valid, t_blk):
    b, lrows, _ = proj3.shape
    n_blk = -(-l_valid // t_blk)
    assert n_blk * t_blk <= lrows
    vec = lambda w: pl.BlockSpec((1, w), lambda bi, t: (0, 0))
    return pl.pallas_call(
        functools.partial(_ssd_kernel, t_blk=t_blk, l_valid=l_valid, n_blk=n_blk),
        out_shape=(jax.ShapeDtypeStruct((b, n_blk * t_blk, GROUP_WIDTH), F32),
                   jax.ShapeDtypeStruct((b, B_CONV - 1, B_CONV_DIM), F32),
                   jax.ShapeDtypeStruct((b, HEADS, HEAD_DIM, B_STATE), F32)),
        grid=(b, n_blk),
        in_specs=[pl.BlockSpec((1, t_blk, B_CONV_DIM), lambda bi, t: (bi, t, COL_XBC // B_CONV_DIM)),
                  pl.BlockSpec((1, t_blk, GROUP_WIDTH), lambda bi, t: (bi, t, COL_BZ // GROUP_WIDTH)),
                  pl.BlockSpec((1, t_blk, GROUP_WIDTH), lambda bi, t: (bi, t, COL_DT // GROUP_WIDTH)),
                  pl.BlockSpec((1, B_CONV - 1, B_CONV_DIM), lambda bi, t: (bi, 0, 0)),
                  pl.BlockSpec((1, HEADS, HEAD_DIM, B_STATE), lambda bi, t: (bi, 0, 0, 0)),
                  pl.BlockSpec((B_CONV, B_CONV_DIM), lambda bi, t: (0, 0)),
                  vec(B_CONV_DIM), vec(GROUP_WIDTH), vec(GROUP_WIDTH), vec(GROUP_WIDTH), vec(GROUP_WIDTH)],
        out_specs=(pl.BlockSpec((1, t_blk, GROUP_WIDTH), lambda bi, t: (bi, t, 0)),
                   pl.BlockSpec((1, B_CONV - 1, B_CONV_DIM), lambda bi, t: (bi, 0, 0)),
                   pl.BlockSpec((1, HEADS, HEAD_DIM, B_STATE), lambda bi, t: (bi, 0, 0, 0))),
        scratch_shapes=[pltpu.VMEM((CONV_PAD + t_blk, B_CONV_DIM), F32),
                        pltpu.VMEM((B_STATE, GROUP_WIDTH), F32)],
        compiler_params=_cparams(("parallel", "arbitrary")),
        name="ssd",
    )(proj3, proj3, proj3, conv0, ssm0, cw, cb, dtb, alog, dvec, nrm)


def _hgrn_kernel(q_ref, f_ref, i_ref, g_ref, st0_ref, lb_ref, nrm_ref, o_ref, st_out_ref, st_ref, pad_ref,
                 *, t_blk, l_valid, n_blk):
    t = pl.program_id(1)

    @pl.when(t == 0)
    def _():
        st_ref[...] = st0_ref[0]
        pad_ref[:, 0:SUB, :] = jnp.zeros((3, SUB, GROUP_WIDTH), F32)

    q = q_ref[0]
    v = i_ref[0]
    lb = lb_ref[...]
    f = lb + (1.0 - lb) * jax.nn.sigmoid(f_ref[0])
    kk = 1.0 - f
    lf = jnp.log(f)
    rows = lax.broadcasted_iota(jnp.int32, (t_blk, GROUP_WIDTH), 0)
    if l_valid < n_blk * t_blk:
        ok = t * t_blk + rows < l_valid
        kk = jnp.where(ok, kk, 0.0)
        lf = jnp.where(ok, lf, 0.0)
    bcs = _dot01_lhs(_tril(t_blk).astype(BF16), lf)
    head_eq = _head_ones()
    ones_bd = head_eq.astype(BF16)
    vb = v.astype(BF16)

    o = _dot_nt((q * jnp.exp(bcs)).astype(BF16), st_ref[...].astype(BF16))

    pad_ref[0, SUB:SUB + t_blk, :] = bcs
    pad_ref[1, SUB:SUB + t_blk, :] = kk
    pad_ref[2, SUB:SUB + t_blk, :] = v
    row_in_sub = rows % SUB
    for d in range(SUB):
        if d == 0:
            yk = q * kk
            vs = v
        else:
            bs = pad_ref[0, SUB - d:SUB - d + t_blk, :]
            ks = pad_ref[1, SUB - d:SUB - d + t_blk, :]
            vs = pad_ref[2, SUB - d:SUB - d + t_blk, :]
            yk = jnp.where(row_in_sub >= d, q * ks * jnp.exp(bcs - bs), 0.0)
        o = o + _dot(yk.astype(BF16), ones_bd) * vs

    lane_head = lax.broadcasted_iota(jnp.int32, (SUB, GROUP_WIDTH), 1) // HEAD_DIM
    pieces = [jnp.zeros((SUB, GROUP_WIDTH), F32)]
    for a in range(1, t_blk // SUB):
        r0 = a * SUB
        ref_row = bcs[r0 - 1:r0, :]
        qa = q[r0:r0 + SUB, :] * jnp.exp(bcs[r0:r0 + SUB, :] - ref_row)
        ka = jnp.where(rows < r0, kk * jnp.exp(ref_row - bcs), 0.0)
        qs = jnp.concatenate([jnp.where(lane_head == h, qa, 0.0) for h in range(HEADS)], axis=0).astype(BF16)
        att = _dot_nt(qs, ka.astype(BF16))
        oa_all = _dot(att.astype(BF16), vb)
        oa = jnp.zeros((SUB, GROUP_WIDTH), F32)
        for h in range(HEADS):
            oa = oa + jnp.where(lane_head == h, oa_all[h * SUB:(h + 1) * SUB, :], 0.0)
        pieces.append(oa)
    if len(pieces) > 1:
        o = o + jnp.concatenate(pieces, axis=0)

    last = bcs[t_blk - 1:t_blk, :]
    kt = (kk * jnp.exp(last - bcs)).astype(BF16)
    st_new = st_ref[...] * jnp.exp(last) + jnp.where(head_eq, _dot_tn(vb, kt), 0.0)
    st_ref[...] = st_new

    ms = _dot01_rhs(o * o, ones_bd) * (1.0 / HEAD_DIM)
    o_ref[0] = o * lax.rsqrt(ms + EPS) * nrm_ref[...] * _silu(g_ref[0])

    @pl.when(t == n_blk - 1)
    def _():
        st_out_ref[0] = st_new


def _hgrn(proj3, st0, lb, nrm, l_valid, t_blk):
    b, lrows, _ = proj3.shape
    n_blk = -(-l_valid // t_blk)
    assert n_blk * t_blk <= lrows and t_blk % SUB == 0
    col = lambda c: pl.BlockSpec((1, t_blk, GROUP_WIDTH), lambda bi, t: (bi, t, c // GROUP_WIDTH))
    vec = pl.BlockSpec((1, GROUP_WIDTH), lambda bi, t: (0, 0))
    st_spec = pl.BlockSpec((1, GROUP_WIDTH, GROUP_WIDTH), lambda bi, t: (bi, 0, 0))
    return pl.pallas_call(
        functools.partial(_hgrn_kernel, t_blk=t_blk, l_valid=l_valid, n_blk=n_blk),
        out_shape=(jax.ShapeDtypeStruct((b, n_blk * t_blk, GROUP_WIDTH), F32),
                   jax.ShapeDtypeStruct((b, GROUP_WIDTH, GROUP_WIDTH), F32)),
        grid=(b, n_blk),
        in_specs=[col(COL_CQ), col(COL_CF), col(COL_CI), col(COL_CG), st_spec, vec, vec],
        out_specs=(pl.BlockSpec((1, t_blk, GROUP_WIDTH), lambda bi, t: (bi, t, 0)), st_spec),
        scratch_shapes=[pltpu.VMEM((GROUP_WIDTH, GROUP_WIDTH), F32),
                        pltpu.VMEM((3, SUB + t_blk, GROUP_WIDTH), F32)],
        compiler_params=_cparams(("parallel", "arbitrary")),
        name="hgrn",
    )(proj3, proj3, proj3, proj3, st0, lb, nrm)


POOL_PAD = 16


def _pool_kernel(x_ref, buf_ref, w_ref, sc_ref, o_ref, buf_out_ref, xp_ref, *, t_blk):
    t = pl.program_id(1)

    @pl.when(t == 0)
    def _():
        xp_ref[0:1, :] = jnp.zeros((1, GROUP_WIDTH), F32)
        xp_ref[POOL_PAD - D_BUF:POOL_PAD, :] = buf_ref[0]

    x = x_ref[0]
    xp_ref[POOL_PAD:POOL_PAD + t_blk, :] = x
    acc = x
    sums = {}
    for d in range(1, D_BUF + 1):
        acc = acc + xp_ref[POOL_PAD - d:POOL_PAD - d + t_blk, :]
        sums[d + 1] = acc
    lane = lax.broadcasted_iota(jnp.int32, (t_blk, GROUP_WIDTH), 1)
    pooled = jnp.where(lane < 64, sums[2] * 0.5,
                       jnp.where(lane < 128, sums[4] * 0.25,
                                 jnp.where(lane < 192, sums[8] * 0.125, sums[16] * 0.0625))) - x
    o_ref[0] = _dot(pooled.astype(BF16), w_ref[...]) * sc_ref[...]
    tail = xp_ref[t_blk + 1:t_blk + 1 + D_BUF, :]
    buf_out_ref[0] = tail
    xp_ref[POOL_PAD - D_BUF:POOL_PAD, :] = tail


def _pool(proj3, buf, w_bd, scale, l_valid, t_blk):
    b = proj3.shape[0]
    n_blk = l_valid // t_blk
    assert n_blk * t_blk == l_valid and t_blk >= D_BUF
    buf_spec = pl.BlockSpec((1, D_BUF, GROUP_WIDTH), lambda bi, t: (bi, 0, 0))
    return pl.pallas_call(
        functools.partial(_pool_kernel, t_blk=t_blk),
        out_shape=(jax.ShapeDtypeStruct((b, l_valid, GROUP_WIDTH), F32),
                   jax.ShapeDtypeStruct((b, D_BUF, GROUP_WIDTH), F32)),
        grid=(b, n_blk),
        in_specs=[pl.BlockSpec((1, t_blk, GROUP_WIDTH), lambda bi, t: (bi, t, COL_DX // GROUP_WIDTH)),
                  buf_spec,
                  pl.BlockSpec((GROUP_WIDTH, GROUP_WIDTH), lambda bi, t: (0, 0)),
                  pl.BlockSpec((1, GROUP_WIDTH), lambda bi, t: (0, 0))],
        out_specs=(pl.BlockSpec((1, t_blk, GROUP_WIDTH), lambda bi, t: (bi, t, 0)), buf_spec),
        scratch_shapes=[pltpu.VMEM((POOL_PAD + t_blk, GROUP_WIDTH), F32)],
        compiler_params=_cparams(("parallel", "arbitrary")),
        name="pool",
    )(proj3, buf, w_bd, scale)


SSD_BLOCK = 128
HGRN_BLOCK = 64
POOL_BLOCK = 512


def _prep_layer(layer, norm1, w_in, a_rel_bias, b_conv_w, b_conv_b, b_dt_bias, b_a_log, b_d, b_norm, lbs, c_norm,
                d_pool_w, d_pool_scale, w_out, norm2, w_gate_up, w_down):
    w = w_in[layer]
    o_dt = 3 * GROUP_WIDTH + GROUP_WIDTH + B_CONV_DIM
    w_r = jnp.concatenate([w[:, 1024:1792], w[:, 0:1024], w[:, o_dt + HEADS:],
                           jnp.repeat(w[:, o_dt:o_dt + HEADS], HEAD_DIM, axis=1)], axis=1).astype(BF16)
    rel = A_BAND_PREV + jnp.arange(CHUNK)[:, None] - jnp.arange(A_BAND)[None, :]
    bias = a_rel_bias[layer][jnp.clip(rel, -REL_CLIP, REL_CLIP) + REL_CLIP]
    bias = jnp.transpose(bias, (2, 0, 1)).reshape(HEADS * CHUNK, A_BAND)
    rep = lambda p: jnp.repeat(p[layer], HEAD_DIM)[None, :]
    pw = d_pool_w[layer]
    w_bd = jnp.zeros((GROUP_WIDTH, GROUP_WIDTH), F32)
    for g in range(4):
        w_bd = w_bd.at[g * 64:(g + 1) * 64, g * 64:(g + 1) * 64].set(pw[g])
    return dict(
        n1=norm1[layer][None, :], w_in=w_r, bias=bias,
        cw=b_conv_w[layer], cb=b_conv_b[layer][None, :], dtb=rep(b_dt_bias), alog=rep(b_a_log), dvec=rep(b_d),
        bnrm=b_norm[layer][None, :], lb=lbs[layer][None, :], cnrm=c_norm[layer].reshape(1, GROUP_WIDTH),
        w_bd=w_bd.astype(BF16), psc=d_pool_scale[layer][None, :],
        w_out=w_out[layer].astype(BF16), n2=norm2[layer][None, :],
        w_gu=w_gate_up[layer].astype(BF16), w_dn=w_down[layer].astype(BF16))


def _round_up(n, m):
    return -(-n // m) * m


def _group_layer(x, st, p, nf, final):
    k_prev, v_prev, conv0, ssm0, hgrn0, pool0 = st
    b, l, _ = x.shape
    p_hist = k_prev.shape[1]
    proj = _inproj(x.reshape(b * l, D_MODEL), p["n1"], p["w_in"]).reshape(b, l, PROJ_W)
    lpad = _round_up(l, SSD_BLOCK)
    proj_p = proj if lpad == l else jnp.pad(proj, ((0, 0), (0, lpad - l), (0, 0)))

    ka = proj[:, :, COL_AK:COL_AK + GROUP_WIDTH]
    va = proj[:, :, COL_AV:COL_AV + GROUP_WIDTH]
    lq = _round_up(l, CHUNK)

    def with_history(prev, new):
        prev = prev.reshape(b, p_hist, GROUP_WIDTH)
        return jnp.concatenate([jnp.zeros((b, A_BAND_PREV - p_hist, GROUP_WIDTH), F32), prev, new,
                                jnp.zeros((b, lq - l, GROUP_WIDTH), F32)], axis=1).astype(BF16)

    oa = _attention(proj_p, with_history(k_prev, ka), with_history(v_prev, va), p["bias"], l, p_hist)[:, :l]
    keep = min(A_BAND_PREV, p_hist + l)
    new_k = jnp.concatenate([k_prev.reshape(b, p_hist, GROUP_WIDTH), ka], axis=1)[:, -keep:]
    new_v = jnp.concatenate([v_prev.reshape(b, p_hist, GROUP_WIDTH), va], axis=1)[:, -keep:]
    new_k = new_k.reshape(b, keep, HEADS, HEAD_DIM)
    new_v = new_v.reshape(b, keep, HEADS, HEAD_DIM)

    ob, new_conv, new_ssm = _ssd(proj_p, conv0, ssm0, p["cw"], p["cb"], p["dtb"], p["alog"], p["dvec"], p["bnrm"],
                                 l, SSD_BLOCK)
    ob = ob[:, :l]

    eye = jnp.eye(HEADS, dtype=F32)
    st_bd = jnp.einsum("bhkv,hg->bhvgk", hgrn0, eye).reshape(b, GROUP_WIDTH, GROUP_WIDTH)
    oc, st_out = _hgrn(proj_p, st_bd, p["lb"], p["cnrm"], l, HGRN_BLOCK)
    oc = oc[:, :l]
    st_out = st_out.reshape(b, HEADS, HEAD_DIM, HEADS, HEAD_DIM)
    new_hgrn = jnp.stack([st_out[:, h, :, h, :] for h in range(HEADS)], axis=1).transpose(0, 1, 3, 2)

    od, new_pool = _pool(proj_p, pool0, p["w_bd"], p["psc"], l, POOL_BLOCK if l % POOL_BLOCK == 0 else l)

    flat = lambda a: a.reshape(b * l, GROUP_WIDTH)
    y = _mlp(x.reshape(b * l, D_MODEL), (flat(oa), flat(ob), flat(oc), flat(od)),
             p["w_out"], p["n2"], p["w_gu"], p["w_dn"], nf, final)
    return y.reshape(b, l, D_MODEL), (new_k, new_v, new_conv, new_ssm, new_hgrn, new_pool)


def _trunk(x, states, params, nf):
    depth = len(params)
    new = [[] for _ in range(6)]
    for layer in range(depth):
        st_l = tuple(s[layer] for s in states)
        x, ns = _group_layer(x, st_l, params[layer], nf, layer == depth - 1)
        for i in range(6):
            new[i].append(ns[i])
    return x, tuple(jnp.stack(n, axis=0) for n in new)


def kernel(x_prompt, x_sample, cache_a_k, cache_a_v, state_b_conv, state_b_ssm, state_c_hgrn, state_d_pool, norm1,
           w_in, a_rel_bias, b_conv_w, b_conv_b, b_dt_bias, b_a_log, b_d, b_norm, c_lb_logits, c_norm, d_pool_w,
           d_pool_scale, w_out, norm2, w_gate_up, w_down, norm_f):
    depth = w_in.shape[0]
    lbs = jnp.cumsum(jax.nn.softmax(c_lb_logits.astype(F32), axis=0), axis=0)
    lbs = lbs - lbs[:1]
    params = [_prep_layer(layer, norm1, w_in, a_rel_bias, b_conv_w, b_conv_b, b_dt_bias, b_a_log, b_d, b_norm, lbs,
                          c_norm, d_pool_w, d_pool_scale, w_out, norm2, w_gate_up, w_down)
              for layer in range(depth)]
    nf = norm_f[None, :]
    bp = x_prompt.shape[0]
    prompt_states = (
        jnp.zeros((depth, bp, 0, HEADS, HEAD_DIM), F32),
        jnp.zeros((depth, bp, 0, HEADS, HEAD_DIM), F32),
        jnp.zeros((depth, bp, B_CONV - 1, B_CONV_DIM), F32),
        jnp.zeros((depth, bp, HEADS, HEAD_DIM, B_STATE), F32),
        jnp.zeros((depth, bp, HEADS, HEAD_DIM, HEAD_DIM), F32),
        jnp.zeros((depth, bp, D_BUF, GROUP_WIDTH), F32),
    )
    y_prompt, ps = _trunk(x_prompt, prompt_states, params, nf)
    sample_states = (cache_a_k, cache_a_v, state_b_conv, state_b_ssm, state_c_hgrn, state_d_pool)
    y_sample, ss = _trunk(x_sample, sample_states, params, nf)
    return (y_prompt, y_sample) + ps + ss
```

```python
import functools

import jax
import jax.numpy as jnp
from jax import lax
from jax.experimental import pallas as pl
from jax.experimental.pallas import tpu as pltpu

F32 = jnp.float32
BF16 = jnp.bfloat16

D_MODEL = 1024
GROUP_WIDTH = 256
HEADS = 4
HEAD_DIM = 64
CHUNK = 64
A_BAND_PREV = 512
A_BAND = A_BAND_PREV + CHUNK
REL_CLIP = 128
B_STATE = 128
B_CONV = 4
B_CONV_DIM = 768
D_BUF = 15
D_FF = 2816
EPS = 1e-6
NEG = -1e30
LOG2E = 1.4426950408889634
SUB = 16

COL_XBC = 0
COL_AQ = 768
COL_AK = 1024
COL_AV = 1280
COL_BZ = 1536
COL_CQ = 1792
COL_CF = 2048
COL_CI = 2304
COL_CG = 2560
COL_DX = 2816
COL_DT = 3072
PROJ_W = 3328

VMEM_LIMIT = 56 * 1024 * 1024


def _cparams(sem):
    return pltpu.CompilerParams(dimension_semantics=sem, vmem_limit_bytes=VMEM_LIMIT)


def _rms(x, w):
    return x * lax.rsqrt(jnp.mean(x * x, axis=-1, keepdims=True) + EPS) * w


def _silu(x):
    return x * jax.nn.sigmoid(x)


def _dot(a, b):
    return jnp.dot(a, b, preferred_element_type=F32)


def _dot_nt(a, b):
    return lax.dot_general(a, b, (((1,), (1,)), ((), ())), preferred_element_type=F32)


def _dot_tn(a, b):
    return lax.dot_general(a, b, (((0,), (0,)), ((), ())), preferred_element_type=F32)


def _split3(x):
    hi = x.astype(BF16)
    r = x - hi.astype(F32)
    mid = r.astype(BF16)
    lo = (r - mid.astype(F32)).astype(BF16)
    return hi, mid, lo


def _dot01_lhs(m01, x):
    hi, mid, lo = _split3(x)
    return _dot(m01, hi) + _dot(m01, mid) + _dot(m01, lo)


def _dot01_rhs(x, m01):
    hi, mid, lo = _split3(x)
    return _dot(hi, m01) + _dot(mid, m01) + _dot(lo, m01)


def _tril(t):
    r = lax.broadcasted_iota(jnp.int32, (t, t), 0)
    c = lax.broadcasted_iota(jnp.int32, (t, t), 1)
    return r >= c


def _head_ones():
    r = lax.broadcasted_iota(jnp.int32, (GROUP_WIDTH, GROUP_WIDTH), 0) // HEAD_DIM
    c = lax.broadcasted_iota(jnp.int32, (GROUP_WIDTH, GROUP_WIDTH), 1) // HEAD_DIM
    return r == c


def _inproj_kernel(x_ref, n_ref, w_ref, o_ref):
    hb = _rms(x_ref[...], n_ref[...]).astype(BF16)
    for c0 in range(0, PROJ_W, 256):
        o_ref[:, c0:c0 + 256] = _dot(hb, w_ref[:, c0:c0 + 256])


def _inproj(x2d, n1, w_in):
    rows = x2d.shape[0]
    tm = 512 if rows % 512 == 0 else rows
    return pl.pallas_call(
        _inproj_kernel,
        out_shape=jax.ShapeDtypeStruct((rows, PROJ_W), F32),
        grid=(rows // tm,),
        in_specs=[pl.BlockSpec((tm, D_MODEL), lambda i: (i, 0)),
                  pl.BlockSpec((1, D_MODEL), lambda i: (0, 0)),
                  pl.BlockSpec((D_MODEL, PROJ_W), lambda i: (0, 0), pipeline_mode=pl.Buffered(1))],
        out_specs=pl.BlockSpec((tm, PROJ_W), lambda i: (i, 0)),
        compiler_params=_cparams(("parallel",)),
        name="inproj",
    )(x2d, n1, w_in)


FF_CHUNK = 256


def _mlp_kernel(x_ref, oa_ref, ob_ref, oc_ref, od_ref, wo_ref, n2_ref, wgu_ref, wdn_ref, nf_ref, o_ref, mix_ref,
                act_ref, *, final):
    for m, r in enumerate((oa_ref, ob_ref, oc_ref, od_ref)):
        mix_ref[:, m * GROUP_WIDTH:(m + 1) * GROUP_WIDTH] = r[...].astype(BF16)
    x1 = x_ref[...] + _dot(mix_ref[...], wo_ref[...])
    hb = _rms(x1, n2_ref[...]).astype(BF16)
    for c0 in range(0, D_FF, FF_CHUNK):
        gate = _dot(hb, wgu_ref[:, c0:c0 + FF_CHUNK])
        up = _dot(hb, wgu_ref[:, D_FF + c0:D_FF + c0 + FF_CHUNK])
        act_ref[:, c0:c0 + FF_CHUNK] = (_silu(gate) * up).astype(BF16)
    out = x1 + _dot(act_ref[...], wdn_ref[...])
    if final:
        out = _rms(out, nf_ref[...])
    o_ref[...] = out


def _mlp(x2d, mix, w_out, n2, w_gu, w_dn, nf, final):
    rows = x2d.shape[0]
    tm = 512 if rows % 512 == 0 else rows
    row_spec = lambda w: pl.BlockSpec((tm, w), lambda i: (i, 0))
    const = lambda a: pl.BlockSpec(a.shape, lambda i: (0, 0), pipeline_mode=pl.Buffered(1))
    return pl.pallas_call(
        functools.partial(_mlp_kernel, final=final),
        out_shape=jax.ShapeDtypeStruct((rows, D_MODEL), F32),
        grid=(rows // tm,),
        in_specs=[row_spec(D_MODEL)] + [row_spec(GROUP_WIDTH)] * 4
                 + [const(w_out), const(n2), const(w_gu), const(w_dn), const(nf)],
        out_specs=row_spec(D_MODEL),
        scratch_shapes=[pltpu.VMEM((tm, D_MODEL), BF16), pltpu.VMEM((tm, D_FF), BF16)],
        compiler_params=_cparams(("parallel",)),
        name="mlp",
    )(x2d, *mix, w_out, n2, w_gu, w_dn, nf)


def _attn_kernel(q_ref, kh_ref, kc_ref, vh_ref, vc_ref, bias_ref, o_ref, kb_ref, vb_ref, *, qb, p_hist, l_valid):
    i = pl.program_id(1)
    if p_hist == 0:
        @pl.when(i == 0)
        def _():
            kb_ref[0:A_BAND_PREV, :] = jnp.zeros((A_BAND_PREV, GROUP_WIDTH), BF16)
            vb_ref[0:A_BAND_PREV, :] = jnp.zeros((A_BAND_PREV, GROUP_WIDTH), BF16)

        @pl.when(i > 0)
        def _():
            kb_ref[0:A_BAND_PREV, :] = kh_ref[0].astype(BF16)
            vb_ref[0:A_BAND_PREV, :] = vh_ref[0].astype(BF16)
    else:
        kb_ref[0:A_BAND_PREV, :] = kh_ref[0].astype(BF16)
        vb_ref[0:A_BAND_PREV, :] = vh_ref[0].astype(BF16)
    kb_ref[A_BAND_PREV:A_BAND_PREV + qb, :] = kc_ref[0].astype(BF16)
    vb_ref[A_BAND_PREV:A_BAND_PREV + qb, :] = vc_ref[0].astype(BF16)
    lane_head = lax.broadcasted_iota(jnp.int32, (CHUNK, GROUP_WIDTH), 1) // HEAD_DIM
    band_pos = lax.broadcasted_iota(jnp.int32, (1, A_BAND), 1)

    def chunk(cc, carry):
        r0 = pl.multiple_of(cc * CHUNK, CHUNK)
        qc = q_ref[0, pl.ds(r0, CHUNK), :] * (HEAD_DIM ** -0.5)
        qs = jnp.concatenate([jnp.where(lane_head == h, qc, 0.0) for h in range(HEADS)], axis=0).astype(BF16)
        s = _dot_nt(qs, kb_ref[pl.ds(r0, A_BAND), :]) + bias_ref[...]
        kpos = i * qb + r0 - A_BAND_PREV + band_pos
        valid = (kpos >= -p_hist) & (kpos < l_valid)
        s = jnp.where(valid, s, NEG)
        m = jnp.max(s, axis=-1, keepdims=True)
        e = jnp.exp(s - m)
        den = jnp.sum(e, axis=-1, keepdims=True)
        o_all = _dot(e.astype(BF16), vb_ref[pl.ds(r0, A_BAND), :]) * (1.0 / den)
        o = jnp.zeros((CHUNK, GROUP_WIDTH), F32)
        for h in range(HEADS):
            o = o + jnp.where(lane_head == h, o_all[h * CHUNK:(h + 1) * CHUNK, :], 0.0)
        o_ref[0, pl.ds(r0, CHUNK), :] = o
        return carry

    lax.fori_loop(0, qb // CHUNK, chunk, 0)


def _attention(proj3, k_cache, v_cache, bias, l_valid):
    b = proj3.shape[0]
    lq = _round_up(l_valid, CHUNK)
    qb = A_BAND_PREV if lq % A_BAND_PREV == 0 else lq
    nblk = lq // qb
    col = lambda c: pl.BlockSpec((1, qb, GROUP_WIDTH), lambda bi, i: (bi, i, c // GROUP_WIDTH))
    if k_cache is None:
        assert qb == A_BAND_PREV
        p_hist = 0
        prev = lambda c: pl.BlockSpec((1, qb, GROUP_WIDTH), lambda bi, i: (bi, jnp.maximum(i - 1, 0), c // GROUP_WIDTH))
        kh, vh, kh_spec, vh_spec = proj3, proj3, prev(COL_AK), prev(COL_AV)
    else:
        assert nblk == 1 and k_cache.shape[1] == A_BAND_PREV
        p_hist = A_BAND_PREV
        kh_spec = vh_spec = pl.BlockSpec((1, A_BAND_PREV, GROUP_WIDTH), lambda bi, i: (bi, 0, 0))
        kh, vh = k_cache, v_cache
    return pl.pallas_call(
        functools.partial(_attn_kernel, qb=qb, p_hist=p_hist, l_valid=l_valid),
        out_shape=jax.ShapeDtypeStruct((b, lq, GROUP_WIDTH), F32),
        grid=(b, nblk),
        in_specs=[col(COL_AQ), kh_spec, col(COL_AK), vh_spec, col(COL_AV),
                  pl.BlockSpec((HEADS * CHUNK, A_BAND), lambda bi, i: (0, 0))],
        out_specs=pl.BlockSpec((1, qb, GROUP_WIDTH), lambda bi, i: (bi, i, 0)),
        scratch_shapes=[pltpu.VMEM((A_BAND_PREV + qb, GROUP_WIDTH), BF16),
                        pltpu.VMEM((A_BAND_PREV + qb, GROUP_WIDTH), BF16)],
        compiler_params=_cparams(("parallel", "arbitrary")),
        name="attention",
    )(proj3, kh, proj3, vh, proj3, bias)


CONV_PAD = 8


def _ssd_kernel(xbc_ref, z_ref, dt_ref, conv0_ref, ssm0_ref, cw_ref, cb_ref, dtb_ref, alog_ref, d_ref, nrm_ref,
                o_ref, conv_out_ref, ssm_out_ref, xpad_ref, st_ref, *, t_blk, l_valid, n_blk):
    t = pl.program_id(1)
    hist = B_CONV - 1

    @pl.when(t == 0)
    def _():
        xpad_ref[CONV_PAD - hist:CONV_PAD, :] = conv0_ref[0]
        st_ref[...] = ssm0_ref[0].reshape(GROUP_WIDTH, B_STATE).T

    xpad_ref[CONV_PAD:CONV_PAD + t_blk, :] = xbc_ref[0]
    y = cb_ref[...]
    for i in range(B_CONV):
        y = y + xpad_ref[CONV_PAD - hist + i:CONV_PAD - hist + i + t_blk, :] * cw_ref[i:i + 1, :]
    xbc = _silu(y)
    n_last = l_valid - (n_blk - 1) * t_blk
    conv_out_ref[0] = xpad_ref[CONV_PAD + n_last - hist:CONV_PAD + n_last, :]
    xpad_ref[CONV_PAD - hist:CONV_PAD, :] = xpad_ref[CONV_PAD + t_blk - hist:CONV_PAD + t_blk, :]

    xs = xbc[:, 0:GROUP_WIDTH]
    bm = xbc[:, GROUP_WIDTH:2 * GROUP_WIDTH]
    cm = xbc[:, 2 * GROUP_WIDTH:3 * GROUP_WIDTH]
    dtr = dt_ref[0] + dtb_ref[...]
    dt = jnp.maximum(dtr, 0.0) + jnp.log1p(jnp.exp(-jnp.abs(dtr)))
    if l_valid < n_blk * t_blk:
        row = t * t_blk + lax.broadcasted_iota(jnp.int32, (t_blk, GROUP_WIDTH), 0)
        dt = jnp.where(row < l_valid, dt, 0.0)
    a = dt * (-jnp.exp(alog_ref[...]))
    u = xs * dt
    tril = _tril(t_blk)
    acs = _dot01_lhs(tril.astype(BF16), a)
    acs_t = acs.T
    eacs = jnp.exp(acs)
    last = acs[t_blk - 1:t_blk, :]
    elast = jnp.exp(last)
    ub = u.astype(BF16)
    upb = (u * jnp.exp(last - acs)).astype(BF16)
    lane = lax.broadcasted_iota(jnp.int32, (t_blk, B_STATE), 1)
    ys = []
    for g in range(2):
        gs = slice(g * B_STATE, (g + 1) * B_STATE)
        cg = cm[:, gs].astype(BF16)
        bg = bm[:, gs].astype(BF16)
        cb = _dot_nt(cg, bg)
        y_state = _dot(cg, st_ref[:, gs].astype(BF16)) * eacs[:, gs]
        parts = []
        for hh in range(2):
            c0 = (2 * g + hh) * HEAD_DIM
            col = jnp.broadcast_to(acs[:, c0:c0 + 1], (t_blk, t_blk))
            rowv = jnp.broadcast_to(acs_t[c0:c0 + 1, :], (t_blk, t_blk))
            dec = jnp.exp(jnp.where(tril, col - rowv, NEG))
            parts.append(_dot((cb * dec).astype(BF16), ub[:, gs]))
        ys.append(jnp.where(lane < HEAD_DIM, parts[0], parts[1]) + y_state)
        st_ref[:, gs] = st_ref[:, gs] * elast[:, gs] + _dot_tn(bg, upb[:, gs])
    yv = jnp.concatenate(ys, axis=1) + d_ref[...] * xs
    o_ref[0] = _rms(yv * _silu(z_ref[0]), nrm_ref[...])

    @pl.when(t == n_blk - 1)
    def _():
        ssm_out_ref[0] = st_ref[...].T.reshape(HEADS, HEAD_DIM, B_STATE)


def _ssd(proj3, conv0, ssm0, cw, cb, dtb, alog, dvec, nrm, l_valid, t_blk):
    b, lrows, _ = proj3.shape
    n_blk = -(-l_valid // t_blk)
    assert n_blk * t_blk <= lrows
    vec = lambda w: pl.BlockSpec((1, w), lambda bi, t: (0, 0))
    return pl.pallas_call(
        functools.partial(_ssd_kernel, t_blk=t_blk, l_valid=l_valid, n_blk=n_blk),
        out_shape=(jax.ShapeDtypeStruct((b, n_blk * t_blk, GROUP_WIDTH), F32),
                   jax.ShapeDtypeStruct((b, B_CONV - 1, B_CONV_DIM), F32),
                   jax.ShapeDtypeStruct((b, HEADS, HEAD_DIM, B_STATE), F32)),
        grid=(b, n_blk),
        in_specs=[pl.BlockSpec((1, t_blk, B_CONV_DIM), lambda bi, t: (bi, t, COL_XBC // B_CONV_DIM)),
                  pl.BlockSpec((1, t_blk, GROUP_WIDTH), lambda bi, t: (bi, t, COL_BZ // GROUP_WIDTH)),
                  pl.BlockSpec((1, t_blk, GROUP_WIDTH), lambda bi, t: (bi, t, COL_DT // GROUP_WIDTH)),
                  pl.BlockSpec((1, B_CONV - 1, B_CONV_DIM), lambda bi, t: (bi, 0, 0)),
                  pl.BlockSpec((1, HEADS, HEAD_DIM, B_STATE), lambda bi, t: (bi, 0, 0, 0)),
                  pl.BlockSpec((B_CONV, B_CONV_DIM), lambda bi, t: (0, 0)),
                  vec(B_CONV_DIM), vec(GROUP_WIDTH), vec(GROUP_WIDTH), vec(GROUP_WIDTH), vec(GROUP_WIDTH)],
        out_specs=(pl.BlockSpec((1, t_blk, GROUP_WIDTH), lambda bi, t: (bi, t, 0)),
                   pl.BlockSpec((1, B_CONV - 1, B_CONV_DIM), lambda bi, t: (bi, 0, 0)),
                   pl.BlockSpec((1, HEADS, HEAD_DIM, B_STATE), lambda bi, t: (bi, 0, 0, 0))),
        scratch_shapes=[pltpu.VMEM((CONV_PAD + t_blk, B_CONV_DIM), F32),
                        pltpu.VMEM((B_STATE, GROUP_WIDTH), F32)],
        compiler_params=_cparams(("parallel", "arbitrary")),
        name="ssd",
    )(proj3, proj3, proj3, conv0, ssm0, cw, cb, dtb, alog, dvec, nrm)


def _hgrn_kernel(q_ref, f_ref, i_ref, g_ref, st0_ref, lb_ref, nrm_ref, o_ref, st_out_ref, st_ref, c_all, v_all,
                 y_all, *, t_blk, l_valid, n_blk):
    t = pl.program_id(1)

    @pl.when(t == 0)
    def _():
        st_ref[...] = st0_ref[0]

    lb = lb_ref[...]
    nrm = nrm_ref[...]
    head_eq = _head_ones()
    ones_bd = head_eq.astype(BF16)
    tril = _tril(CHUNK).astype(BF16)
    rows = lax.broadcasted_iota(jnp.int32, (CHUNK, GROUP_WIDTH), 0)
    row_in_sub = rows % SUB
    lane_head = lax.broadcasted_iota(jnp.int32, (SUB, GROUP_WIDTH), 1) // HEAD_DIM
    n_sub = CHUNK // SUB

    def sub_rows(scr, jj):
        return jnp.concatenate([jnp.broadcast_to(scr[a * SUB + jj:a * SUB + jj + 1, :], (SUB, GROUP_WIDTH))
                                for a in range(n_sub)], axis=0)

    def chunk(cc, slot):
        c_scr, v_scr, y_scr = c_all.at[slot], v_all.at[slot], y_all.at[slot]
        r0 = pl.multiple_of(cc * CHUNK, CHUNK)
        sl = pl.ds(r0, CHUNK)
        q = q_ref[0, sl, :]
        v = i_ref[0, sl, :]
        f = lb + (1.0 - lb) * jax.nn.sigmoid(f_ref[0, sl, :])
        kk = jnp.maximum(1.0 - f, 0.0)
        lf2 = jnp.log(f) * LOG2E
        if l_valid < n_blk * t_blk:
            ok = t * t_blk + r0 + rows < l_valid
            kk = jnp.where(ok, kk, 0.0)
            lf2 = jnp.where(ok, lf2, 0.0)
        bcs2 = _dot01_lhs(tril, lf2)
        c = jnp.log(kk) * LOG2E - bcs2
        c_scr[...] = c
        v_scr[...] = v
        vb = v.astype(BF16)

        o = _dot_nt((q * jnp.exp2(bcs2)).astype(BF16), st_ref[...].astype(BF16))

        for jj in range(SUB):
            w = jnp.exp2(bcs2 + sub_rows(c_scr, jj))
            y_scr[jj * CHUNK:(jj + 1) * CHUNK, :] = jnp.where(row_in_sub >= jj, q * w, 0.0).astype(BF16)
        att = _dot(y_scr[...], ones_bd)
        for jj in range(SUB):
            o = o + att[jj * CHUNK:(jj + 1) * CHUNK, :] * sub_rows(v_scr, jj)

        pieces = [jnp.zeros((SUB, GROUP_WIDTH), F32)]
        for a in range(1, n_sub):
            s0 = a * SUB
            ref_row = bcs2[s0 - 1:s0, :]
            qa = q[s0:s0 + SUB, :] * jnp.exp2(bcs2[s0:s0 + SUB, :] - ref_row)
            ka = jnp.where(rows < s0, jnp.exp2(ref_row + c), 0.0)
            qs = jnp.concatenate([jnp.where(lane_head == h, qa, 0.0) for h in range(HEADS)], axis=0).astype(BF16)
            sc = _dot_nt(qs, ka.astype(BF16))
            oa_all = _dot(sc.astype(BF16), vb)
            oa = jnp.zeros((SUB, GROUP_WIDTH), F32)
            for h in range(HEADS):
                oa = oa + jnp.where(lane_head == h, oa_all[h * SUB:(h + 1) * SUB, :], 0.0)
            pieces.append(oa)
        o = o + jnp.concatenate(pieces, axis=0)

        last = bcs2[CHUNK - 1:CHUNK, :]
        kt = jnp.exp2(last + c).astype(BF16)
        st_ref[...] = st_ref[...] * jnp.exp2(last) + jnp.where(head_eq, _dot_tn(vb, kt), 0.0)

        ms = _dot01_rhs(o * o, ones_bd) * (1.0 / HEAD_DIM)
        o_ref[0, sl, :] = o * lax.rsqrt(ms + EPS) * nrm * _silu(g_ref[0, sl, :])

    n_chunks = t_blk // CHUNK
    if n_chunks % 2 == 0:
        def pair(pp, carry):
            chunk(2 * pp, 0)
            chunk(2 * pp + 1, 1)
            return carry

        lax.fori_loop(0, n_chunks // 2, pair, 0)
    else:
        for cc in range(n_chunks):
            chunk(cc, cc % 2)

    @pl.when(t == n_blk - 1)
    def _():
        st_out_ref[0] = st_ref[...]


def _hgrn(proj3, st0, lb, nrm, l_valid, t_blk):
    b, lrows, _ = proj3.shape
    n_blk = -(-l_valid // t_blk)
    assert n_blk * t_blk <= lrows and t_blk % CHUNK == 0
    col = lambda c: pl.BlockSpec((1, t_blk, GROUP_WIDTH), lambda bi, t: (bi, t, c // GROUP_WIDTH))
    vec = pl.BlockSpec((1, GROUP_WIDTH), lambda bi, t: (0, 0))
    st_spec = pl.BlockSpec((1, GROUP_WIDTH, GROUP_WIDTH), lambda bi, t: (bi, 0, 0))
    return pl.pallas_call(
        functools.partial(_hgrn_kernel, t_blk=t_blk, l_valid=l_valid, n_blk=n_blk),
        out_shape=(jax.ShapeDtypeStruct((b, n_blk * t_blk, GROUP_WIDTH), F32),
                   jax.ShapeDtypeStruct((b, GROUP_WIDTH, GROUP_WIDTH), F32)),
        grid=(b, n_blk),
        in_specs=[col(COL_CQ), col(COL_CF), col(COL_CI), col(COL_CG), st_spec, vec, vec],
        out_specs=(pl.BlockSpec((1, t_blk, GROUP_WIDTH), lambda bi, t: (bi, t, 0)), st_spec),
        scratch_shapes=[pltpu.VMEM((GROUP_WIDTH, GROUP_WIDTH), F32),
                        pltpu.VMEM((2, CHUNK, GROUP_WIDTH), F32),
                        pltpu.VMEM((2, CHUNK, GROUP_WIDTH), F32),
                        pltpu.VMEM((2, SUB * CHUNK, GROUP_WIDTH), BF16)],
        compiler_params=_cparams(("parallel", "arbitrary")),
        name="hgrn",
    )(proj3, proj3, proj3, proj3, st0, lb, nrm)


POOL_PAD = 16


def _pool_kernel(x_ref, buf_ref, w_ref, sc_ref, o_ref, buf_out_ref, xp_ref, *, t_blk):
    t = pl.program_id(1)

    @pl.when(t == 0)
    def _():
        xp_ref[0:1, :] = jnp.zeros((1, GROUP_WIDTH), F32)
        xp_ref[POOL_PAD - D_BUF:POOL_PAD, :] = buf_ref[0]

    x = x_ref[0]
    xp_ref[POOL_PAD:POOL_PAD + t_blk, :] = x
    acc = x
    sums = {}
    for d in range(1, D_BUF + 1):
        acc = acc + xp_ref[POOL_PAD - d:POOL_PAD - d + t_blk, :]
        sums[d + 1] = acc
    lane = lax.broadcasted_iota(jnp.int32, (t_blk, GROUP_WIDTH), 1)
    pooled = jnp.where(lane < 64, sums[2] * 0.5,
                       jnp.where(lane < 128, sums[4] * 0.25,
                                 jnp.where(lane < 192, sums[8] * 0.125, sums[16] * 0.0625))) - x
    o_ref[0] = _dot(pooled.astype(BF16), w_ref[...]) * sc_ref[...]
    tail = xp_ref[t_blk + 1:t_blk + 1 + D_BUF, :]
    buf_out_ref[0] = tail
    xp_ref[POOL_PAD - D_BUF:POOL_PAD, :] = tail


def _pool(proj3, buf, w_bd, scale, l_valid, t_blk):
    b = proj3.shape[0]
    n_blk = l_valid // t_blk
    assert n_blk * t_blk == l_valid and t_blk >= D_BUF
    buf_spec = pl.BlockSpec((1, D_BUF, GROUP_WIDTH), lambda bi, t: (bi, 0, 0))
    return pl.pallas_call(
        functools.partial(_pool_kernel, t_blk=t_blk),
        out_shape=(jax.ShapeDtypeStruct((b, l_valid, GROUP_WIDTH), F32),
                   jax.ShapeDtypeStruct((b, D_BUF, GROUP_WIDTH), F32)),
        grid=(b, n_blk),
        in_specs=[pl.BlockSpec((1, t_blk, GROUP_WIDTH), lambda bi, t: (bi, t, COL_DX // GROUP_WIDTH)),
                  buf_spec,
                  pl.BlockSpec((GROUP_WIDTH, GROUP_WIDTH), lambda bi, t: (0, 0)),
                  pl.BlockSpec((1, GROUP_WIDTH), lambda bi, t: (0, 0))],
        out_specs=(pl.BlockSpec((1, t_blk, GROUP_WIDTH), lambda bi, t: (bi, t, 0)), buf_spec),
        scratch_shapes=[pltpu.VMEM((POOL_PAD + t_blk, GROUP_WIDTH), F32)],
        compiler_params=_cparams(("parallel", "arbitrary")),
        name="pool",
    )(proj3, buf, w_bd, scale)


SSD_BLOCK = 128
HGRN_BLOCK = 512
POOL_BLOCK = 512


def _prep_layer(layer, norm1, w_in, a_rel_bias, b_conv_w, b_conv_b, b_dt_bias, b_a_log, b_d, b_norm, lbs, c_norm,
                d_pool_w, d_pool_scale, w_out, norm2, w_gate_up, w_down):
    w = w_in[layer]
    o_dt = 3 * GROUP_WIDTH + GROUP_WIDTH + B_CONV_DIM
    w_r = jnp.concatenate([w[:, 1024:1792], w[:, 0:1024], w[:, o_dt + HEADS:],
                           jnp.repeat(w[:, o_dt:o_dt + HEADS], HEAD_DIM, axis=1)], axis=1).astype(BF16)
    tab = a_rel_bias[layer].T
    m = A_BAND + CHUNK
    far = tab[:, 2 * REL_CLIP:]
    n_far = A_BAND_PREV - REL_CLIP + 1
    r = jnp.concatenate([jnp.broadcast_to(far, (HEADS, n_far)),
                         tab[:, 2 * REL_CLIP - 1:CHUNK:-1],
                         jnp.broadcast_to(far, (HEADS, m - n_far - (2 * REL_CLIP - 1 - CHUNK)))], axis=1)
    bias = jnp.tile(r, (1, CHUNK))[:, :CHUNK * (m - 1)].reshape(HEADS, CHUNK, m - 1)[:, :, :A_BAND]
    bias = bias.reshape(HEADS * CHUNK, A_BAND)
    rep = lambda p: jnp.repeat(p[layer], HEAD_DIM)[None, :]
    pw = d_pool_w[layer]
    w_bd = jnp.zeros((GROUP_WIDTH, GROUP_WIDTH), F32)
    for g in range(4):
        w_bd = w_bd.at[g * 64:(g + 1) * 64, g * 64:(g + 1) * 64].set(pw[g])
    return dict(
        n1=norm1[layer][None, :], w_in=w_r, bias=bias,
        cw=b_conv_w[layer], cb=b_conv_b[layer][None, :], dtb=rep(b_dt_bias), alog=rep(b_a_log), dvec=rep(b_d),
        bnrm=b_norm[layer][None, :], lb=lbs[layer][None, :], cnrm=c_norm[layer].reshape(1, GROUP_WIDTH),
        w_bd=w_bd.astype(BF16), psc=d_pool_scale[layer][None, :],
        w_out=w_out[layer].astype(BF16), n2=norm2[layer][None, :],
        w_gu=w_gate_up[layer].astype(BF16), w_dn=w_down[layer].astype(BF16))


def _round_up(n, m):
    return -(-n // m) * m


def _group_layer(x, st, p, nf, final):
    k_prev, v_prev, conv0, ssm0, hgrn0, pool0 = st
    b, l, _ = x.shape
    p_hist = k_prev.shape[1]
    proj = _inproj(x.reshape(b * l, D_MODEL), p["n1"], p["w_in"]).reshape(b, l, PROJ_W)
    lpad = _round_up(l, SSD_BLOCK)
    proj_p = proj if lpad == l else jnp.pad(proj, ((0, 0), (0, lpad - l), (0, 0)))

    ka = proj[:, :, COL_AK:COL_AK + GROUP_WIDTH].reshape(b, l, HEADS, HEAD_DIM)
    va = proj[:, :, COL_AV:COL_AV + GROUP_WIDTH].reshape(b, l, HEADS, HEAD_DIM)
    if p_hist == 0:
        k_cache = v_cache = None
    else:
        k_cache = k_prev.astype(BF16).reshape(b, p_hist, GROUP_WIDTH)
        v_cache = v_prev.astype(BF16).reshape(b, p_hist, GROUP_WIDTH)
    oa = _attention(proj_p, k_cache, v_cache, p["bias"], l)[:, :l]
    keep = min(A_BAND_PREV, p_hist + l)
    new_k = jnp.concatenate([k_prev, ka], axis=1)[:, -keep:]
    new_v = jnp.concatenate([v_prev, va], axis=1)[:, -keep:]

    ob, new_conv, new_ssm = _ssd(proj_p, conv0, ssm0, p["cw"], p["cb"], p["dtb"], p["alog"], p["dvec"], p["bnrm"],
                                 l, SSD_BLOCK)
    ob = ob[:, :l]

    eye = jnp.eye(HEADS, dtype=F32)
    st_bd = jnp.einsum("bhkv,hg->bhvgk", hgrn0, eye).reshape(b, GROUP_WIDTH, GROUP_WIDTH)
    oc, st_out = _hgrn(proj_p, st_bd, p["lb"], p["cnrm"], l,
                       HGRN_BLOCK if l % HGRN_BLOCK == 0 else _round_up(l, CHUNK))
    oc = oc[:, :l]
    st_out = st_out.reshape(b, HEADS, HEAD_DIM, HEADS, HEAD_DIM)
    new_hgrn = jnp.stack([st_out[:, h, :, h, :] for h in range(HEADS)], axis=1).transpose(0, 1, 3, 2)

    od, new_pool = _pool(proj_p, pool0, p["w_bd"], p["psc"], l, POOL_BLOCK if l % POOL_BLOCK == 0 else l)

    flat = lambda a: a.reshape(b * l, GROUP_WIDTH)
    y = _mlp(x.reshape(b * l, D_MODEL), (flat(oa), flat(ob), flat(oc), flat(od)),
             p["w_out"], p["n2"], p["w_gu"], p["w_dn"], nf, final)
    return y.reshape(b, l, D_MODEL), (new_k, new_v, new_conv, new_ssm, new_hgrn, new_pool)


def _trunk(x, states, params, nf):
    depth = len(params)
    new = [[] for _ in range(6)]
    for layer in range(depth):
        st_l = tuple(s[layer] for s in states)
        x, ns = _group_layer(x, st_l, params[layer], nf, layer == depth - 1)
        for i in range(6):
            new[i].append(ns[i])
    return x, tuple(jnp.stack(n, axis=0) for n in new)


def kernel(x_prompt, x_sample, cache_a_k, cache_a_v, state_b_conv, state_b_ssm, state_c_hgrn, state_d_pool, norm1,
           w_in, a_rel_bias, b_conv_w, b_conv_b, b_dt_bias, b_a_log, b_d, b_norm, c_lb_logits, c_norm, d_pool_w,
           d_pool_scale, w_out, norm2, w_gate_up, w_down, norm_f):
    depth = w_in.shape[0]
    lbs = jnp.cumsum(jax.nn.softmax(c_lb_logits.astype(F32), axis=0), axis=0)
    lbs = lbs - lbs[:1]
    params = [_prep_layer(layer, norm1, w_in, a_rel_bias, b_conv_w, b_conv_b, b_dt_bias, b_a_log, b_d, b_norm, lbs,
                          c_norm, d_pool_w, d_pool_scale, w_out, norm2, w_gate_up, w_down)
              for layer in range(depth)]
    nf = norm_f[None, :]
    bp = x_prompt.shape[0]
    prompt_states = (
        jnp.zeros((depth, bp, 0, HEADS, HEAD_DIM), F32),
        jnp.zeros((depth, bp, 0, HEADS, HEAD_DIM), F32),
        jnp.zeros((depth, bp, B_CONV - 1, B_CONV_DIM), F32),
        jnp.zeros((depth, bp, HEADS, HEAD_DIM, B_STATE), F32),
        jnp.zeros((depth, bp, HEADS, HEAD_DIM, HEAD_DIM), F32),
        jnp.zeros((depth, bp, D_BUF, GROUP_WIDTH), F32),
    )
    y_prompt, ps = _trunk(x_prompt, prompt_states, params, nf)
    sample_states = (cache_a_k, cache_a_v, state_b_conv, state_b_ssm, state_c_hgrn, state_d_pool)
    y_sample, ss = _trunk(x_sample, sample_states, params, nf)
    return (y_prompt, y_sample) + ps + ss
```

```python
import functools

import jax
import jax.numpy as jnp
from jax import lax
from jax.experimental import pallas as pl
from jax.experimental.pallas import tpu as pltpu

F32 = jnp.float32
BF16 = jnp.bfloat16

D_MODEL = 1024
GROUP_WIDTH = 256
HEADS = 4
HEAD_DIM = 64
CHUNK = 64
A_BAND_PREV = 512
A_BAND = A_BAND_PREV + CHUNK
REL_CLIP = 128
B_STATE = 128
B_CONV = 4
B_CONV_DIM = 768
D_BUF = 15
D_FF = 2816
EPS = 1e-6
NEG = -1e30
LOG2E = 1.4426950408889634
SUB = 16

COL_XBC = 0
COL_AQ = 768
COL_AK = 1024
COL_AV = 1280
COL_BZ = 1536
COL_CQ = 1792
COL_CF = 2048
COL_CI = 2304
COL_CG = 2560
COL_DX = 2816
COL_DT = 3072
PROJ_W = 3328

VMEM_LIMIT = 56 * 1024 * 1024


def _cparams(sem):
    return pltpu.CompilerParams(dimension_semantics=sem, vmem_limit_bytes=VMEM_LIMIT)


def _rms(x, w):
    return x * lax.rsqrt(jnp.mean(x * x, axis=-1, keepdims=True) + EPS) * w


def _silu(x):
    return x * jax.nn.sigmoid(x)


def _dot(a, b):
    return jnp.dot(a, b, preferred_element_type=F32)


def _dot_nt(a, b):
    return lax.dot_general(a, b, (((1,), (1,)), ((), ())), preferred_element_type=F32)


def _dot_tn(a, b):
    return lax.dot_general(a, b, (((0,), (0,)), ((), ())), preferred_element_type=F32)


def _split3(x):
    hi = x.astype(BF16)
    r = x - hi.astype(F32)
    mid = r.astype(BF16)
    lo = (r - mid.astype(F32)).astype(BF16)
    return hi, mid, lo


def _dot01_lhs(m01, x):
    hi, mid, lo = _split3(x)
    return _dot(m01, hi) + _dot(m01, mid) + _dot(m01, lo)


def _dot01_rhs(x, m01):
    hi, mid, lo = _split3(x)
    return _dot(hi, m01) + _dot(mid, m01) + _dot(lo, m01)


def _tril(t):
    r = lax.broadcasted_iota(jnp.int32, (t, t), 0)
    c = lax.broadcasted_iota(jnp.int32, (t, t), 1)
    return r >= c


def _head_ones():
    r = lax.broadcasted_iota(jnp.int32, (GROUP_WIDTH, GROUP_WIDTH), 0) // HEAD_DIM
    c = lax.broadcasted_iota(jnp.int32, (GROUP_WIDTH, GROUP_WIDTH), 1) // HEAD_DIM
    return r == c


def _inproj_kernel(x_ref, n_ref, w_ref, o_ref):
    hb = _rms(x_ref[...], n_ref[...]).astype(BF16)
    for c0 in range(0, PROJ_W, 256):
        o_ref[:, c0:c0 + 256] = _dot(hb, w_ref[:, c0:c0 + 256])


def _inproj(x2d, n1, w_in):
    rows = x2d.shape[0]
    tm = 512 if rows % 512 == 0 else rows
    return pl.pallas_call(
        _inproj_kernel,
        out_shape=jax.ShapeDtypeStruct((rows, PROJ_W), F32),
        grid=(rows // tm,),
        in_specs=[pl.BlockSpec((tm, D_MODEL), lambda i: (i, 0)),
                  pl.BlockSpec((1, D_MODEL), lambda i: (0, 0)),
                  pl.BlockSpec((D_MODEL, PROJ_W), lambda i: (0, 0), pipeline_mode=pl.Buffered(1))],
        out_specs=pl.BlockSpec((tm, PROJ_W), lambda i: (i, 0)),
        compiler_params=_cparams(("parallel",)),
        name="inproj",
    )(x2d, n1, w_in)


FF_CHUNK = 256


def _mlp_kernel(x_ref, oa_ref, ob_ref, oc_ref, od_ref, wo_ref, n2_ref, wgu_ref, wdn_ref, nf_ref, o_ref, mix_ref,
                act_ref, *, final):
    for m, r in enumerate((oa_ref, ob_ref, oc_ref, od_ref)):
        mix_ref[:, m * GROUP_WIDTH:(m + 1) * GROUP_WIDTH] = r[...].astype(BF16)
    x1 = x_ref[...] + _dot(mix_ref[...], wo_ref[...])
    hb = _rms(x1, n2_ref[...]).astype(BF16)
    for c0 in range(0, D_FF, FF_CHUNK):
        gate = _dot(hb, wgu_ref[:, c0:c0 + FF_CHUNK])
        up = _dot(hb, wgu_ref[:, D_FF + c0:D_FF + c0 + FF_CHUNK])
        act_ref[:, c0:c0 + FF_CHUNK] = (_silu(gate) * up).astype(BF16)
    out = x1 + _dot(act_ref[...], wdn_ref[...])
    if final:
        out = _rms(out, nf_ref[...])
    o_ref[...] = out


def _mlp(x2d, mix, w_out, n2, w_gu, w_dn, nf, final):
    rows = x2d.shape[0]
    tm = 512 if rows % 512 == 0 else rows
    row_spec = lambda w: pl.BlockSpec((tm, w), lambda i: (i, 0))
    const = lambda a: pl.BlockSpec(a.shape, lambda i: (0, 0), pipeline_mode=pl.Buffered(1))
    return pl.pallas_call(
        functools.partial(_mlp_kernel, final=final),
        out_shape=jax.ShapeDtypeStruct((rows, D_MODEL), F32),
        grid=(rows // tm,),
        in_specs=[row_spec(D_MODEL)] + [row_spec(GROUP_WIDTH)] * 4
                 + [const(w_out), const(n2), const(w_gu), const(w_dn), const(nf)],
        out_specs=row_spec(D_MODEL),
        scratch_shapes=[pltpu.VMEM((tm, D_MODEL), BF16), pltpu.VMEM((tm, D_FF), BF16)],
        compiler_params=_cparams(("parallel",)),
        name="mlp",
    )(x2d, *mix, w_out, n2, w_gu, w_dn, nf)


ATTN_UNROLL = 4


def _attn_kernel(q_ref, kh_ref, kc_ref, vh_ref, vc_ref, bias_ref, o_ref, kb_ref, vb_ref, *, qb, p_hist, l_valid):
    i = pl.program_id(1)
    if p_hist == 0:
        @pl.when(i == 0)
        def _():
            kb_ref[0:A_BAND_PREV, :] = jnp.zeros((A_BAND_PREV, GROUP_WIDTH), BF16)
            vb_ref[0:A_BAND_PREV, :] = jnp.zeros((A_BAND_PREV, GROUP_WIDTH), BF16)

        @pl.when(i > 0)
        def _():
            kb_ref[0:A_BAND_PREV, :] = kh_ref[0].astype(BF16)
            vb_ref[0:A_BAND_PREV, :] = vh_ref[0].astype(BF16)
    else:
        kb_ref[0:A_BAND_PREV, :] = kh_ref[0].astype(BF16)
        vb_ref[0:A_BAND_PREV, :] = vh_ref[0].astype(BF16)
    kb_ref[A_BAND_PREV:A_BAND_PREV + qb, :] = kc_ref[0].astype(BF16)
    vb_ref[A_BAND_PREV:A_BAND_PREV + qb, :] = vc_ref[0].astype(BF16)
    lane_head = lax.broadcasted_iota(jnp.int32, (CHUNK, GROUP_WIDTH), 1) // HEAD_DIM
    band_pos = lax.broadcasted_iota(jnp.int32, (1, A_BAND), 1)

    def chunk(cc, carry):
        r0 = pl.multiple_of(cc * CHUNK, CHUNK)
        qc = q_ref[0, pl.ds(r0, CHUNK), :] * (HEAD_DIM ** -0.5)
        qs = jnp.concatenate([jnp.where(lane_head == h, qc, 0.0) for h in range(HEADS)], axis=0).astype(BF16)
        s = _dot_nt(qs, kb_ref[pl.ds(r0, A_BAND), :]) + bias_ref[...]
        kpos = i * qb + r0 - A_BAND_PREV + band_pos
        valid = (kpos >= -p_hist) & (kpos < l_valid)
        s = jnp.where(valid, s, NEG)
        m = jnp.max(s, axis=-1, keepdims=True)
        e = jnp.exp(s - m)
        den = jnp.sum(e, axis=-1, keepdims=True)
        o_all = _dot(e.astype(BF16), vb_ref[pl.ds(r0, A_BAND), :]) * (1.0 / den)
        o = jnp.zeros((CHUNK, GROUP_WIDTH), F32)
        for h in range(HEADS):
            o = o + jnp.where(lane_head == h, o_all[h * CHUNK:(h + 1) * CHUNK, :], 0.0)
        o_ref[0, pl.ds(r0, CHUNK), :] = o
        return carry

    n_chunks = qb // CHUNK
    lax.fori_loop(0, n_chunks, chunk, 0, unroll=ATTN_UNROLL if n_chunks % ATTN_UNROLL == 0 else 1)


def _attention(proj3, k_cache, v_cache, bias, l_valid):
    b = proj3.shape[0]
    lq = _round_up(l_valid, CHUNK)
    qb = A_BAND_PREV if lq % A_BAND_PREV == 0 else lq
    nblk = lq // qb
    col = lambda c: pl.BlockSpec((1, qb, GROUP_WIDTH), lambda bi, i: (bi, i, c // GROUP_WIDTH))
    if k_cache is None:
        assert qb == A_BAND_PREV
        p_hist = 0
        prev = lambda c: pl.BlockSpec((1, qb, GROUP_WIDTH), lambda bi, i: (bi, jnp.maximum(i - 1, 0), c // GROUP_WIDTH))
        kh, vh, kh_spec, vh_spec = proj3, proj3, prev(COL_AK), prev(COL_AV)
    else:
        assert nblk == 1 and k_cache.shape[1] == A_BAND_PREV
        p_hist = A_BAND_PREV
        kh_spec = vh_spec = pl.BlockSpec((1, A_BAND_PREV, GROUP_WIDTH), lambda bi, i: (bi, 0, 0))
        kh, vh = k_cache, v_cache
    return pl.pallas_call(
        functools.partial(_attn_kernel, qb=qb, p_hist=p_hist, l_valid=l_valid),
        out_shape=jax.ShapeDtypeStruct((b, lq, GROUP_WIDTH), F32),
        grid=(b, nblk),
        in_specs=[col(COL_AQ), kh_spec, col(COL_AK), vh_spec, col(COL_AV),
                  pl.BlockSpec((HEADS * CHUNK, A_BAND), lambda bi, i: (0, 0))],
        out_specs=pl.BlockSpec((1, qb, GROUP_WIDTH), lambda bi, i: (bi, i, 0)),
        scratch_shapes=[pltpu.VMEM((A_BAND_PREV + qb, GROUP_WIDTH), BF16),
                        pltpu.VMEM((A_BAND_PREV + qb, GROUP_WIDTH), BF16)],
        compiler_params=_cparams(("parallel", "arbitrary")),
        name="attention",
    )(proj3, kh, proj3, vh, proj3, bias)


CONV_PAD = 8


def _ssd_kernel(xbc_ref, z_ref, dt_ref, conv0_ref, ssm0_ref, cw_ref, cb_ref, dtb_ref, alog_ref, d_ref, nrm_ref,
                o_ref, conv_out_ref, ssm_out_ref, xpad_ref, st_ref, *, t_blk, l_valid, n_blk):
    t = pl.program_id(1)
    hist = B_CONV - 1

    @pl.when(t == 0)
    def _():
        xpad_ref[CONV_PAD - hist:CONV_PAD, :] = conv0_ref[0]
        st_ref[...] = ssm0_ref[0].reshape(GROUP_WIDTH, B_STATE).T

    xpad_ref[CONV_PAD:CONV_PAD + t_blk, :] = xbc_ref[0]
    y = cb_ref[...]
    for i in range(B_CONV):
        y = y + xpad_ref[CONV_PAD - hist + i:CONV_PAD - hist + i + t_blk, :] * cw_ref[i:i + 1, :]
    xbc = _silu(y)
    n_last = l_valid - (n_blk - 1) * t_blk
    conv_out_ref[0] = xpad_ref[CONV_PAD + n_last - hist:CONV_PAD + n_last, :]
    xpad_ref[CONV_PAD - hist:CONV_PAD, :] = xpad_ref[CONV_PAD + t_blk - hist:CONV_PAD + t_blk, :]

    xs = xbc[:, 0:GROUP_WIDTH]
    bm = xbc[:, GROUP_WIDTH:2 * GROUP_WIDTH]
    cm = xbc[:, 2 * GROUP_WIDTH:3 * GROUP_WIDTH]
    dtr = dt_ref[0] + dtb_ref[...]
    dt = jnp.maximum(dtr, 0.0) + jnp.log1p(jnp.exp(-jnp.abs(dtr)))
    if l_valid < n_blk * t_blk:
        row = t * t_blk + lax.broadcasted_iota(jnp.int32, (t_blk, GROUP_WIDTH), 0)
        dt = jnp.where(row < l_valid, dt, 0.0)
    a = dt * (-jnp.exp(alog_ref[...]))
    u = xs * dt
    tril = _tril(t_blk)
    acs = _dot01_lhs(tril.astype(BF16), a)
    acs_t = acs.T
    eacs = jnp.exp(acs)
    last = acs[t_blk - 1:t_blk, :]
    elast = jnp.exp(last)
    ub = u.astype(BF16)
    upb = (u * jnp.exp(last - acs)).astype(BF16)
    lane = lax.broadcasted_iota(jnp.int32, (t_blk, B_STATE), 1)
    ys = []
    for g in range(2):
        gs = slice(g * B_STATE, (g + 1) * B_STATE)
        cg = cm[:, gs].astype(BF16)
        bg = bm[:, gs].astype(BF16)
        cb = _dot_nt(cg, bg)
        y_state = _dot(cg, st_ref[:, gs].astype(BF16)) * eacs[:, gs]
        parts = []
        for hh in range(2):
            c0 = (2 * g + hh) * HEAD_DIM
            col = jnp.broadcast_to(acs[:, c0:c0 + 1], (t_blk, t_blk))
            rowv = jnp.broadcast_to(acs_t[c0:c0 + 1, :], (t_blk, t_blk))
            dec = jnp.exp(jnp.where(tril, col - rowv, NEG))
            parts.append(_dot((cb * dec).astype(BF16), ub[:, gs]))
        ys.append(jnp.where(lane < HEAD_DIM, parts[0], parts[1]) + y_state)
        st_ref[:, gs] = st_ref[:, gs] * elast[:, gs] + _dot_tn(bg, upb[:, gs])
    yv = jnp.concatenate(ys, axis=1) + d_ref[...] * xs
    o_ref[0] = _rms(yv * _silu(z_ref[0]), nrm_ref[...])

    @pl.when(t == n_blk - 1)
    def _():
        ssm_out_ref[0] = st_ref[...].T.reshape(HEADS, HEAD_DIM, B_STATE)


def _ssd(proj3, conv0, ssm0, cw, cb, dtb, alog, dvec, nrm, l_valid, t_blk):
    b, lrows, _ = proj3.shape
    n_blk = -(-l_valid // t_blk)
    assert n_blk * t_blk <= lrows
    vec = lambda w: pl.BlockSpec((1, w), lambda bi, t: (0, 0))
    return pl.pallas_call(
        functools.partial(_ssd_kernel, t_blk=t_blk, l_valid=l_valid, n_blk=n_blk),
        out_shape=(jax.ShapeDtypeStruct((b, n_blk * t_blk, GROUP_WIDTH), F32),
                   jax.ShapeDtypeStruct((b, B_CONV - 1, B_CONV_DIM), F32),
                   jax.ShapeDtypeStruct((b, HEADS, HEAD_DIM, B_STATE), F32)),
        grid=(b, n_blk),
        in_specs=[pl.BlockSpec((1, t_blk, B_CONV_DIM), lambda bi, t: (bi, t, COL_XBC // B_CONV_DIM)),
                  pl.BlockSpec((1, t_blk, GROUP_WIDTH), lambda bi, t: (bi, t, COL_BZ // GROUP_WIDTH)),
                  pl.BlockSpec((1, t_blk, GROUP_WIDTH), lambda bi, t: (bi, t, COL_DT // GROUP_WIDTH)),
                  pl.BlockSpec((1, B_CONV - 1, B_CONV_DIM), lambda bi, t: (bi, 0, 0)),
                  pl.BlockSpec((1, HEADS, HEAD_DIM, B_STATE), lambda bi, t: (bi, 0, 0, 0)),
                  pl.BlockSpec((B_CONV, B_CONV_DIM), lambda bi, t: (0, 0)),
                  vec(B_CONV_DIM), vec(GROUP_WIDTH), vec(GROUP_WIDTH), vec(GROUP_WIDTH), vec(GROUP_WIDTH)],
        out_specs=(pl.BlockSpec((1, t_blk, GROUP_WIDTH), lambda bi, t: (bi, t, 0)),
                   pl.BlockSpec((1, B_CONV - 1, B_CONV_DIM), lambda bi, t: (bi, 0, 0)),
                   pl.BlockSpec((1, HEADS, HEAD_DIM, B_STATE), lambda bi, t: (bi, 0, 0, 0))),
        scratch_shapes=[pltpu.VMEM((CONV_PAD + t_blk, B_CONV_DIM), F32),
                        pltpu.VMEM((B_STATE, GROUP_WIDTH), F32)],
        compiler_params=_cparams(("parallel", "arbitrary")),
        name="ssd",
    )(proj3, proj3, proj3, conv0, ssm0, cw, cb, dtb, alog, dvec, nrm)


def _hgrn_kernel(q_ref, f_ref, i_ref, g_ref, st0_ref, lb_ref, nrm_ref, o_ref, st_out_ref, st_ref, c_all, v_all,
                 y_all, *, t_blk, l_valid, n_blk):
    t = pl.program_id(1)

    @pl.when(t == 0)
    def _():
        st_ref[...] = st0_ref[0]

    lb = lb_ref[...]
    nrm = nrm_ref[...]
    head_eq = _head_ones()
    ones_bd = head_eq.astype(BF16)
    tril = _tril(CHUNK).astype(BF16)
    rows = lax.broadcasted_iota(jnp.int32, (CHUNK, GROUP_WIDTH), 0)
    row_in_sub = rows % SUB
    lane_head = lax.broadcasted_iota(jnp.int32, (SUB, GROUP_WIDTH), 1) // HEAD_DIM
    n_sub = CHUNK // SUB

    def sub_rows(scr, jj):
        return jnp.concatenate([jnp.broadcast_to(scr[a * SUB + jj:a * SUB + jj + 1, :], (SUB, GROUP_WIDTH))
                                for a in range(n_sub)], axis=0)

    def chunk(cc, slot):
        c_scr, v_scr, y_scr = c_all.at[slot], v_all.at[slot], y_all.at[slot]
        r0 = pl.multiple_of(cc * CHUNK, CHUNK)
        sl = pl.ds(r0, CHUNK)
        q = q_ref[0, sl, :]
        v = i_ref[0, sl, :]
        f = lb + (1.0 - lb) * jax.nn.sigmoid(f_ref[0, sl, :])
        kk = jnp.maximum(1.0 - f, 0.0)
        lf2 = jnp.log(f) * LOG2E
        if l_valid < n_blk * t_blk:
            ok = t * t_blk + r0 + rows < l_valid
            kk = jnp.where(ok, kk, 0.0)
            lf2 = jnp.where(ok, lf2, 0.0)
        bcs2 = _dot01_lhs(tril, lf2)
        c = jnp.log(kk) * LOG2E - bcs2
        c_scr[...] = c
        v_scr[...] = v
        vb = v.astype(BF16)

        o = _dot_nt((q * jnp.exp2(bcs2)).astype(BF16), st_ref[...].astype(BF16))

        for jj in range(SUB):
            w = jnp.exp2(bcs2 + sub_rows(c_scr, jj))
            y_scr[jj * CHUNK:(jj + 1) * CHUNK, :] = jnp.where(row_in_sub >= jj, q * w, 0.0).astype(BF16)
        att = _dot(y_scr[...], ones_bd)
        for jj in range(SUB):
            o = o + att[jj * CHUNK:(jj + 1) * CHUNK, :] * sub_rows(v_scr, jj)

        pieces = [jnp.zeros((SUB, GROUP_WIDTH), F32)]
        for a in range(1, n_sub):
            s0 = a * SUB
            ref_row = bcs2[s0 - 1:s0, :]
            qa = q[s0:s0 + SUB, :] * jnp.exp2(bcs2[s0:s0 + SUB, :] - ref_row)
            ka = jnp.where(rows < s0, jnp.exp2(ref_row + c), 0.0)
            qs = jnp.concatenate([jnp.where(lane_head == h, qa, 0.0) for h in range(HEADS)], axis=0).astype(BF16)
            sc = _dot_nt(qs, ka.astype(BF16))
            oa_all = _dot(sc.astype(BF16), vb)
            oa = jnp.zeros((SUB, GROUP_WIDTH), F32)
            for h in range(HEADS):
                oa = oa + jnp.where(lane_head == h, oa_all[h * SUB:(h + 1) * SUB, :], 0.0)
            pieces.append(oa)
        o = o + jnp.concatenate(pieces, axis=0)

        last = bcs2[CHUNK - 1:CHUNK, :]
        kt = jnp.exp2(last + c).astype(BF16)
        st_ref[...] = st_ref[...] * jnp.exp2(last) + jnp.where(head_eq, _dot_tn(vb, kt), 0.0)

        ms = _dot01_rhs(o * o, ones_bd) * (1.0 / HEAD_DIM)
        o_ref[0, sl, :] = o * lax.rsqrt(ms + EPS) * nrm * _silu(g_ref[0, sl, :])

    n_chunks = t_blk // CHUNK
    if n_chunks % 2 == 0:
        def pair(pp, carry):
            chunk(2 * pp, 0)
            chunk(2 * pp + 1, 1)
            return carry

        lax.fori_loop(0, n_chunks // 2, pair, 0)
    else:
        for cc in range(n_chunks):
            chunk(cc, cc % 2)

    @pl.when(t == n_blk - 1)
    def _():
        st_out_ref[0] = st_ref[...]


def _hgrn(proj3, st0, lb, nrm, l_valid, t_blk):
    b, lrows, _ = proj3.shape
    n_blk = -(-l_valid // t_blk)
    assert n_blk * t_blk <= lrows and t_blk % CHUNK == 0
    col = lambda c: pl.BlockSpec((1, t_blk, GROUP_WIDTH), lambda bi, t: (bi, t, c // GROUP_WIDTH))
    vec = pl.BlockSpec((1, GROUP_WIDTH), lambda bi, t: (0, 0))
    st_spec = pl.BlockSpec((1, GROUP_WIDTH, GROUP_WIDTH), lambda bi, t: (bi, 0, 0))
    return pl.pallas_call(
        functools.partial(_hgrn_kernel, t_blk=t_blk, l_valid=l_valid, n_blk=n_blk),
        out_shape=(jax.ShapeDtypeStruct((b, n_blk * t_blk, GROUP_WIDTH), F32),
                   jax.ShapeDtypeStruct((b, GROUP_WIDTH, GROUP_WIDTH), F32)),
        grid=(b, n_blk),
        in_specs=[col(COL_CQ), col(COL_CF), col(COL_CI), col(COL_CG), st_spec, vec, vec],
        out_specs=(pl.BlockSpec((1, t_blk, GROUP_WIDTH), lambda bi, t: (bi, t, 0)), st_spec),
        scratch_shapes=[pltpu.VMEM((GROUP_WIDTH, GROUP_WIDTH), F32),
                        pltpu.VMEM((2, CHUNK, GROUP_WIDTH), F32),
                        pltpu.VMEM((2, CHUNK, GROUP_WIDTH), F32),
                        pltpu.VMEM((2, SUB * CHUNK, GROUP_WIDTH), BF16)],
        compiler_params=_cparams(("parallel", "arbitrary")),
        name="hgrn",
    )(proj3, proj3, proj3, proj3, st0, lb, nrm)


POOL_PAD = 16


def _pool_kernel(x_ref, buf_ref, w_ref, sc_ref, o_ref, buf_out_ref, xp_ref, *, t_blk):
    t = pl.program_id(1)

    @pl.when(t == 0)
    def _():
        xp_ref[0:1, :] = jnp.zeros((1, GROUP_WIDTH), F32)
        xp_ref[POOL_PAD - D_BUF:POOL_PAD, :] = buf_ref[0]

    x = x_ref[0]
    xp_ref[POOL_PAD:POOL_PAD + t_blk, :] = x
    acc = x
    sums = {}
    for d in range(1, D_BUF + 1):
        acc = acc + xp_ref[POOL_PAD - d:POOL_PAD - d + t_blk, :]
        sums[d + 1] = acc
    lane = lax.broadcasted_iota(jnp.int32, (t_blk, GROUP_WIDTH), 1)
    pooled = jnp.where(lane < 64, sums[2] * 0.5,
                       jnp.where(lane < 128, sums[4] * 0.25,
                                 jnp.where(lane < 192, sums[8] * 0.125, sums[16] * 0.0625))) - x
    o_ref[0] = _dot(pooled.astype(BF16), w_ref[...]) * sc_ref[...]
    tail = xp_ref[t_blk + 1:t_blk + 1 + D_BUF, :]
    buf_out_ref[0] = tail
    xp_ref[POOL_PAD - D_BUF:POOL_PAD, :] = tail


def _pool(proj3, buf, w_bd, scale, l_valid, t_blk):
    b = proj3.shape[0]
    n_blk = l_valid // t_blk
    assert n_blk * t_blk == l_valid and t_blk >= D_BUF
    buf_spec = pl.BlockSpec((1, D_BUF, GROUP_WIDTH), lambda bi, t: (bi, 0, 0))
    return pl.pallas_call(
        functools.partial(_pool_kernel, t_blk=t_blk),
        out_shape=(jax.ShapeDtypeStruct((b, l_valid, GROUP_WIDTH), F32),
                   jax.ShapeDtypeStruct((b, D_BUF, GROUP_WIDTH), F32)),
        grid=(b, n_blk),
        in_specs=[pl.BlockSpec((1, t_blk, GROUP_WIDTH), lambda bi, t: (bi, t, COL_DX // GROUP_WIDTH)),
                  buf_spec,
                  pl.BlockSpec((GROUP_WIDTH, GROUP_WIDTH), lambda bi, t: (0, 0)),
                  pl.BlockSpec((1, GROUP_WIDTH), lambda bi, t: (0, 0))],
        out_specs=(pl.BlockSpec((1, t_blk, GROUP_WIDTH), lambda bi, t: (bi, t, 0)), buf_spec),
        scratch_shapes=[pltpu.VMEM((POOL_PAD + t_blk, GROUP_WIDTH), F32)],
        compiler_params=_cparams(("parallel", "arbitrary")),
        name="pool",
    )(proj3, buf, w_bd, scale)


SSD_BLOCK = 128
HGRN_BLOCK = 512
POOL_BLOCK = 512


def _prep_layer(layer, norm1, w_in, a_rel_bias, b_conv_w, b_conv_b, b_dt_bias, b_a_log, b_d, b_norm, lbs, c_norm,
                d_pool_w, d_pool_scale, w_out, norm2, w_gate_up, w_down):
    w = w_in[layer]
    o_dt = 3 * GROUP_WIDTH + GROUP_WIDTH + B_CONV_DIM
    w_r = jnp.concatenate([w[:, 1024:1792], w[:, 0:1024], w[:, o_dt + HEADS:],
                           jnp.repeat(w[:, o_dt:o_dt + HEADS], HEAD_DIM, axis=1)], axis=1).astype(BF16)
    tab = a_rel_bias[layer].T
    m = A_BAND + CHUNK
    far = tab[:, 2 * REL_CLIP:]
    n_far = A_BAND_PREV - REL_CLIP + 1
    r = jnp.concatenate([jnp.broadcast_to(far, (HEADS, n_far)),
                         tab[:, 2 * REL_CLIP - 1:CHUNK:-1],
                         jnp.broadcast_to(far, (HEADS, m - n_far - (2 * REL_CLIP - 1 - CHUNK)))], axis=1)
    bias = jnp.tile(r, (1, CHUNK))[:, :CHUNK * (m - 1)].reshape(HEADS, CHUNK, m - 1)[:, :, :A_BAND]
    bias = bias.reshape(HEADS * CHUNK, A_BAND)
    rep = lambda p: jnp.repeat(p[layer], HEAD_DIM)[None, :]
    pw = d_pool_w[layer]
    w_bd = jnp.zeros((GROUP_WIDTH, GROUP_WIDTH), F32)
    for g in range(4):
        w_bd = w_bd.at[g * 64:(g + 1) * 64, g * 64:(g + 1) * 64].set(pw[g])
    return dict(
        n1=norm1[layer][None, :], w_in=w_r, bias=bias,
        cw=b_conv_w[layer], cb=b_conv_b[layer][None, :], dtb=rep(b_dt_bias), alog=rep(b_a_log), dvec=rep(b_d),
        bnrm=b_norm[layer][None, :], lb=lbs[layer][None, :], cnrm=c_norm[layer].reshape(1, GROUP_WIDTH),
        w_bd=w_bd.astype(BF16), psc=d_pool_scale[layer][None, :],
        w_out=w_out[layer].astype(BF16), n2=norm2[layer][None, :],
        w_gu=w_gate_up[layer].astype(BF16), w_dn=w_down[layer].astype(BF16))


def _round_up(n, m):
    return -(-n // m) * m


def _group_layer(x, st, p, nf, final):
    k_prev, v_prev, conv0, ssm0, hgrn0, pool0 = st
    b, l, _ = x.shape
    p_hist = k_prev.shape[1]
    proj = _inproj(x.reshape(b * l, D_MODEL), p["n1"], p["w_in"]).reshape(b, l, PROJ_W)
    lpad = _round_up(l, SSD_BLOCK)
    proj_p = proj if lpad == l else jnp.pad(proj, ((0, 0), (0, lpad - l), (0, 0)))

    keep = min(A_BAND_PREV, p_hist + l)
    n_new = min(l, keep)
    ka = proj[:, l - n_new:, COL_AK:COL_AK + GROUP_WIDTH].reshape(b, n_new, HEADS, HEAD_DIM)
    va = proj[:, l - n_new:, COL_AV:COL_AV + GROUP_WIDTH].reshape(b, n_new, HEADS, HEAD_DIM)
    if p_hist == 0:
        k_cache = v_cache = None
    else:
        k_cache = k_prev.astype(BF16).reshape(b, p_hist, GROUP_WIDTH)
        v_cache = v_prev.astype(BF16).reshape(b, p_hist, GROUP_WIDTH)
    oa = _attention(proj_p, k_cache, v_cache, p["bias"], l)[:, :l]
    new_k = jnp.concatenate([k_prev[:, p_hist - (keep - n_new):], ka], axis=1)
    new_v = jnp.concatenate([v_prev[:, p_hist - (keep - n_new):], va], axis=1)

    ob, new_conv, new_ssm = _ssd(proj_p, conv0, ssm0, p["cw"], p["cb"], p["dtb"], p["alog"], p["dvec"], p["bnrm"],
                                 l, SSD_BLOCK)
    ob = ob[:, :l]

    eye = jnp.eye(HEADS, dtype=F32)
    st_bd = jnp.einsum("bhkv,hg->bhvgk", hgrn0, eye).reshape(b, GROUP_WIDTH, GROUP_WIDTH)
    oc, st_out = _hgrn(proj_p, st_bd, p["lb"], p["cnrm"], l,
                       HGRN_BLOCK if l % HGRN_BLOCK == 0 else _round_up(l, CHUNK))
    oc = oc[:, :l]
    st_out = st_out.reshape(b, HEADS, HEAD_DIM, HEADS, HEAD_DIM)
    new_hgrn = jnp.stack([st_out[:, h, :, h, :] for h in range(HEADS)], axis=1).transpose(0, 1, 3, 2)

    od, new_pool = _pool(proj_p, pool0, p["w_bd"], p["psc"], l, POOL_BLOCK if l % POOL_BLOCK == 0 else l)

    flat = lambda a: a.reshape(b * l, GROUP_WIDTH)
    y = _mlp(x.reshape(b * l, D_MODEL), (flat(oa), flat(ob), flat(oc), flat(od)),
             p["w_out"], p["n2"], p["w_gu"], p["w_dn"], nf, final)
    return y.reshape(b, l, D_MODEL), (new_k, new_v, new_conv, new_ssm, new_hgrn, new_pool)


def _trunk(x, states, params, nf):
    depth = len(params)
    new = [[] for _ in range(6)]
    for layer in range(depth):
        st_l = tuple(s[layer] for s in states)
        x, ns = _group_layer(x, st_l, params[layer], nf, layer == depth - 1)
        for i in range(6):
            new[i].append(ns[i])
    return x, tuple(jnp.stack(n, axis=0) for n in new)


def kernel(x_prompt, x_sample, cache_a_k, cache_a_v, state_b_conv, state_b_ssm, state_c_hgrn, state_d_pool, norm1,
           w_in, a_rel_bias, b_conv_w, b_conv_b, b_dt_bias, b_a_log, b_d, b_norm, c_lb_logits, c_norm, d_pool_w,
           d_pool_scale, w_out, norm2, w_gate_up, w_down, norm_f):
    depth = w_in.shape[0]
    lbs = jnp.cumsum(jax.nn.softmax(c_lb_logits.astype(F32), axis=0), axis=0)
    lbs = lbs - lbs[:1]
    params = [_prep_layer(layer, norm1, w_in, a_rel_bias, b_conv_w, b_conv_b, b_dt_bias, b_a_log, b_d, b_norm, lbs,
                          c_norm, d_pool_w, d_pool_scale, w_out, norm2, w_gate_up, w_down)
              for layer in range(depth)]
    nf = norm_f[None, :]
    bp = x_prompt.shape[0]
    prompt_states = (
        jnp.zeros((depth, bp, 0, HEADS, HEAD_DIM), F32),
        jnp.zeros((depth, bp, 0, HEADS, HEAD_DIM), F32),
        jnp.zeros((depth, bp, B_CONV - 1, B_CONV_DIM), F32),
        jnp.zeros((depth, bp, HEADS, HEAD_DIM, B_STATE), F32),
        jnp.zeros((depth, bp, HEADS, HEAD_DIM, HEAD_DIM), F32),
        jnp.zeros((depth, bp, D_BUF, GROUP_WIDTH), F32),
    )
    y_prompt, ps = _trunk(x_prompt, prompt_states, params, nf)
    sample_states = (cache_a_k, cache_a_v, state_b_conv, state_b_ssm, state_c_hgrn, state_d_pool)
    y_sample, ss = _trunk(x_sample, sample_states, params, nf)
    return (y_prompt, y_sample) + ps + ss
```

```python
import functools

import jax
import jax.numpy as jnp
from jax import lax
from jax.experimental import pallas as pl
from jax.experimental.pallas import tpu as pltpu

F32 = jnp.float32
BF16 = jnp.bfloat16

D_MODEL = 1024
GROUP_WIDTH = 256
HEADS = 4
HEAD_DIM = 64
CHUNK = 64
A_BAND_PREV = 512
A_BAND = A_BAND_PREV + CHUNK
REL_CLIP = 128
B_STATE = 128
B_CONV = 4
B_CONV_DIM = 768
D_BUF = 15
D_FF = 2816
EPS = 1e-6
NEG = -1e30
LOG2E = 1.4426950408889634
SUB = 16

COL_XBC = 0
COL_AQ = 768
COL_AK = 1024
COL_AV = 1280
COL_BZ = 1536
COL_CQ = 1792
COL_CF = 2048
COL_CI = 2304
COL_CG = 2560
COL_DX = 2816
COL_DT = 3072
PROJ_W = 3328

VMEM_LIMIT = 56 * 1024 * 1024


def _cparams(sem):
    return pltpu.CompilerParams(dimension_semantics=sem, vmem_limit_bytes=VMEM_LIMIT)


def _rms(x, w):
    return x * lax.rsqrt(jnp.mean(x * x, axis=-1, keepdims=True) + EPS) * w


def _silu(x):
    return x * jax.nn.sigmoid(x)


def _dot(a, b):
    return jnp.dot(a, b, preferred_element_type=F32)


def _dot_nt(a, b):
    return lax.dot_general(a, b, (((1,), (1,)), ((), ())), preferred_element_type=F32)


def _dot_tn(a, b):
    return lax.dot_general(a, b, (((0,), (0,)), ((), ())), preferred_element_type=F32)


def _split3(x):
    hi = x.astype(BF16)
    r = x - hi.astype(F32)
    mid = r.astype(BF16)
    lo = (r - mid.astype(F32)).astype(BF16)
    return hi, mid, lo


def _dot01_lhs(m01, x):
    hi, mid, lo = _split3(x)
    return _dot(m01, hi) + _dot(m01, mid) + _dot(m01, lo)


def _dot01_rhs(x, m01):
    hi, mid, lo = _split3(x)
    return _dot(hi, m01) + _dot(mid, m01) + _dot(lo, m01)


def _tril(t):
    r = lax.broadcasted_iota(jnp.int32, (t, t), 0)
    c = lax.broadcasted_iota(jnp.int32, (t, t), 1)
    return r >= c


def _head_ones():
    r = lax.broadcasted_iota(jnp.int32, (GROUP_WIDTH, GROUP_WIDTH), 0) // HEAD_DIM
    c = lax.broadcasted_iota(jnp.int32, (GROUP_WIDTH, GROUP_WIDTH), 1) // HEAD_DIM
    return r == c


W_PREP_COLS = 256


def _w_in_prep_kernel(wt_ref, o_ref):
    o_ref[...] = wt_ref[...].T.astype(BF16)


def _w_in_prep(wt):
    return pl.pallas_call(
        _w_in_prep_kernel,
        out_shape=jax.ShapeDtypeStruct((D_MODEL, PROJ_W), BF16),
        grid=(PROJ_W // W_PREP_COLS,),
        in_specs=[pl.BlockSpec((W_PREP_COLS, D_MODEL), lambda i: (i, 0))],
        out_specs=pl.BlockSpec((D_MODEL, W_PREP_COLS), lambda i: (0, i)),
        compiler_params=_cparams(("parallel",)),
        name="w_in_prep",
    )(wt)


def _inproj_kernel(x_ref, n_ref, w_ref, o_ref):
    hb = _rms(x_ref[...], n_ref[...]).astype(BF16)
    for c0 in range(0, PROJ_W, 256):
        o_ref[:, c0:c0 + 256] = _dot(hb, w_ref[:, c0:c0 + 256])


def _inproj(x2d, n1, w_in):
    rows = x2d.shape[0]
    tm = 512 if rows % 512 == 0 else rows
    return pl.pallas_call(
        _inproj_kernel,
        out_shape=jax.ShapeDtypeStruct((rows, PROJ_W), F32),
        grid=(rows // tm,),
        in_specs=[pl.BlockSpec((tm, D_MODEL), lambda i: (i, 0)),
                  pl.BlockSpec((1, D_MODEL), lambda i: (0, 0)),
                  pl.BlockSpec((D_MODEL, PROJ_W), lambda i: (0, 0), pipeline_mode=pl.Buffered(1))],
        out_specs=pl.BlockSpec((tm, PROJ_W), lambda i: (i, 0)),
        compiler_params=_cparams(("parallel",)),
        name="inproj",
    )(x2d, n1, w_in)


FF_CHUNK = 256


def _mlp_kernel(x_ref, oa_ref, ob_ref, oc_ref, od_ref, wo_ref, n2_ref, wgu_ref, wdn_ref, nf_ref, o_ref, mix_ref,
                act_ref, *, final):
    for m, r in enumerate((oa_ref, ob_ref, oc_ref, od_ref)):
        mix_ref[:, m * GROUP_WIDTH:(m + 1) * GROUP_WIDTH] = r[...].astype(BF16)
    x1 = x_ref[...] + _dot(mix_ref[...], wo_ref[0])
    hb = _rms(x1, n2_ref[...]).astype(BF16)
    for c0 in range(0, D_FF, FF_CHUNK):
        gate = _dot(hb, wgu_ref[0, :, c0:c0 + FF_CHUNK])
        up = _dot(hb, wgu_ref[0, :, D_FF + c0:D_FF + c0 + FF_CHUNK])
        act_ref[:, c0:c0 + FF_CHUNK] = (_silu(gate) * up).astype(BF16)
    out = x1 + _dot(act_ref[...], wdn_ref[0])
    if final:
        out = _rms(out, nf_ref[...])
    o_ref[...] = out


def _mlp(x2d, mix, layer, w_out, n2, w_gu, w_dn, nf, final):
    rows = x2d.shape[0]
    tm = 512 if rows % 512 == 0 else rows
    row_spec = lambda w: pl.BlockSpec((tm, w), lambda i: (i, 0))
    vec = lambda a: pl.BlockSpec(a.shape, lambda i: (0, 0))
    wspec = lambda a: pl.BlockSpec((1,) + a.shape[1:], lambda i: (layer, 0, 0), pipeline_mode=pl.Buffered(1))
    return pl.pallas_call(
        functools.partial(_mlp_kernel, final=final),
        out_shape=jax.ShapeDtypeStruct((rows, D_MODEL), F32),
        grid=(rows // tm,),
        in_specs=[row_spec(D_MODEL)] + [row_spec(GROUP_WIDTH)] * 4
                 + [wspec(w_out), vec(n2), wspec(w_gu), wspec(w_dn), vec(nf)],
        out_specs=row_spec(D_MODEL),
        scratch_shapes=[pltpu.VMEM((tm, D_MODEL), BF16), pltpu.VMEM((tm, D_FF), BF16)],
        compiler_params=_cparams(("parallel",)),
        name="mlp",
    )(x2d, *mix, w_out, n2, w_gu, w_dn, nf)


ATTN_UNROLL = 4


def _attn_kernel(q_ref, kh_ref, kc_ref, vh_ref, vc_ref, bias_ref, o_ref, kb_ref, vb_ref, *, qb, p_hist, l_valid):
    i = pl.program_id(1)
    if p_hist == 0:
        @pl.when(i == 0)
        def _():
            kb_ref[0:A_BAND_PREV, :] = jnp.zeros((A_BAND_PREV, GROUP_WIDTH), BF16)
            vb_ref[0:A_BAND_PREV, :] = jnp.zeros((A_BAND_PREV, GROUP_WIDTH), BF16)

        @pl.when(i > 0)
        def _():
            kb_ref[0:A_BAND_PREV, :] = kh_ref[0].astype(BF16)
            vb_ref[0:A_BAND_PREV, :] = vh_ref[0].astype(BF16)
    else:
        kb_ref[0:A_BAND_PREV, :] = kh_ref[0].astype(BF16)
        vb_ref[0:A_BAND_PREV, :] = vh_ref[0].astype(BF16)
    kb_ref[A_BAND_PREV:A_BAND_PREV + qb, :] = kc_ref[0].astype(BF16)
    vb_ref[A_BAND_PREV:A_BAND_PREV + qb, :] = vc_ref[0].astype(BF16)
    lane_head = lax.broadcasted_iota(jnp.int32, (CHUNK, GROUP_WIDTH), 1) // HEAD_DIM
    band_pos = lax.broadcasted_iota(jnp.int32, (1, A_BAND), 1)

    def chunk(cc, carry):
        r0 = pl.multiple_of(cc * CHUNK, CHUNK)
        qc = q_ref[0, pl.ds(r0, CHUNK), :] * (HEAD_DIM ** -0.5)
        qs = jnp.concatenate([jnp.where(lane_head == h, qc, 0.0) for h in range(HEADS)], axis=0).astype(BF16)
        s = _dot_nt(qs, kb_ref[pl.ds(r0, A_BAND), :]) + bias_ref[...]
        kpos = i * qb + r0 - A_BAND_PREV + band_pos
        valid = (kpos >= -p_hist) & (kpos < l_valid)
        s = jnp.where(valid, s, NEG)
        m = jnp.max(s, axis=-1, keepdims=True)
        e = jnp.exp(s - m)
        den = jnp.sum(e, axis=-1, keepdims=True)
        o_all = _dot(e.astype(BF16), vb_ref[pl.ds(r0, A_BAND), :]) * (1.0 / den)
        o = jnp.zeros((CHUNK, GROUP_WIDTH), F32)
        for h in range(HEADS):
            o = o + jnp.where(lane_head == h, o_all[h * CHUNK:(h + 1) * CHUNK, :], 0.0)
        o_ref[0, pl.ds(r0, CHUNK), :] = o
        return carry

    n_chunks = qb // CHUNK
    lax.fori_loop(0, n_chunks, chunk, 0, unroll=ATTN_UNROLL if n_chunks % ATTN_UNROLL == 0 else 1)


def _attention(proj3, k_cache, v_cache, bias, l_valid):
    b = proj3.shape[0]
    lq = _round_up(l_valid, CHUNK)
    qb = A_BAND_PREV if lq % A_BAND_PREV == 0 else lq
    nblk = lq // qb
    col = lambda c: pl.BlockSpec((1, qb, GROUP_WIDTH), lambda bi, i: (bi, i, c // GROUP_WIDTH))
    if k_cache is None:
        assert qb == A_BAND_PREV
        p_hist = 0
        prev = lambda c: pl.BlockSpec((1, qb, GROUP_WIDTH), lambda bi, i: (bi, jnp.maximum(i - 1, 0), c // GROUP_WIDTH))
        kh, vh, kh_spec, vh_spec = proj3, proj3, prev(COL_AK), prev(COL_AV)
    else:
        assert nblk == 1 and k_cache.shape[1] == A_BAND_PREV
        p_hist = A_BAND_PREV
        kh_spec = vh_spec = pl.BlockSpec((1, A_BAND_PREV, GROUP_WIDTH), lambda bi, i: (bi, 0, 0))
        kh, vh = k_cache, v_cache
    return pl.pallas_call(
        functools.partial(_attn_kernel, qb=qb, p_hist=p_hist, l_valid=l_valid),
        out_shape=jax.ShapeDtypeStruct((b, lq, GROUP_WIDTH), F32),
        grid=(b, nblk),
        in_specs=[col(COL_AQ), kh_spec, col(COL_AK), vh_spec, col(COL_AV),
                  pl.BlockSpec((HEADS * CHUNK, A_BAND), lambda bi, i: (0, 0))],
        out_specs=pl.BlockSpec((1, qb, GROUP_WIDTH), lambda bi, i: (bi, i, 0)),
        scratch_shapes=[pltpu.VMEM((A_BAND_PREV + qb, GROUP_WIDTH), BF16),
                        pltpu.VMEM((A_BAND_PREV + qb, GROUP_WIDTH), BF16)],
        compiler_params=_cparams(("parallel", "arbitrary")),
        name="attention",
    )(proj3, kh, proj3, vh, proj3, bias)


CONV_PAD = 8


def _ssd_kernel(xbc_ref, z_ref, dt_ref, conv0_ref, ssm0_ref, cw_ref, cb_ref, dtb_ref, alog_ref, d_ref, nrm_ref,
                o_ref, conv_out_ref, ssm_out_ref, xpad_ref, st_ref, *, t_blk, l_valid, n_blk):
    t = pl.program_id(1)
    hist = B_CONV - 1

    @pl.when(t == 0)
    def _():
        xpad_ref[CONV_PAD - hist:CONV_PAD, :] = conv0_ref[0]
        st_ref[...] = ssm0_ref[0].reshape(GROUP_WIDTH, B_STATE).T

    xpad_ref[CONV_PAD:CONV_PAD + t_blk, :] = xbc_ref[0]
    y = cb_ref[...]
    for i in range(B_CONV):
        y = y + xpad_ref[CONV_PAD - hist + i:CONV_PAD - hist + i + t_blk, :] * cw_ref[i:i + 1, :]
    xbc = _silu(y)
    n_last = l_valid - (n_blk - 1) * t_blk
    conv_out_ref[0] = xpad_ref[CONV_PAD + n_last - hist:CONV_PAD + n_last, :]
    xpad_ref[CONV_PAD - hist:CONV_PAD, :] = xpad_ref[CONV_PAD + t_blk - hist:CONV_PAD + t_blk, :]

    xs = xbc[:, 0:GROUP_WIDTH]
    bm = xbc[:, GROUP_WIDTH:2 * GROUP_WIDTH]
    cm = xbc[:, 2 * GROUP_WIDTH:3 * GROUP_WIDTH]
    dtr = dt_ref[0] + dtb_ref[...]
    dt = jnp.maximum(dtr, 0.0) + jnp.log1p(jnp.exp(-jnp.abs(dtr)))
    if l_valid < n_blk * t_blk:
        row = t * t_blk + lax.broadcasted_iota(jnp.int32, (t_blk, GROUP_WIDTH), 0)
        dt = jnp.where(row < l_valid, dt, 0.0)
    a = dt * (-jnp.exp(alog_ref[...]))
    u = xs * dt
    tril = _tril(t_blk)
    acs = _dot01_lhs(tril.astype(BF16), a)
    acs_t = acs.T
    eacs = jnp.exp(acs)
    last = acs[t_blk - 1:t_blk, :]
    elast = jnp.exp(last)
    ub = u.astype(BF16)
    upb = (u * jnp.exp(last - acs)).astype(BF16)
    lane = lax.broadcasted_iota(jnp.int32, (t_blk, B_STATE), 1)
    ys = []
    for g in range(2):
        gs = slice(g * B_STATE, (g + 1) * B_STATE)
        cg = cm[:, gs].astype(BF16)
        bg = bm[:, gs].astype(BF16)
        cb = _dot_nt(cg, bg)
        y_state = _dot(cg, st_ref[:, gs].astype(BF16)) * eacs[:, gs]
        parts = []
        for hh in range(2):
            c0 = (2 * g + hh) * HEAD_DIM
            col = jnp.broadcast_to(acs[:, c0:c0 + 1], (t_blk, t_blk))
            rowv = jnp.broadcast_to(acs_t[c0:c0 + 1, :], (t_blk, t_blk))
            dec = jnp.exp(jnp.where(tril, col - rowv, NEG))
            parts.append(_dot((cb * dec).astype(BF16), ub[:, gs]))
        ys.append(jnp.where(lane < HEAD_DIM, parts[0], parts[1]) + y_state)
        st_ref[:, gs] = st_ref[:, gs] * elast[:, gs] + _dot_tn(bg, upb[:, gs])
    yv = jnp.concatenate(ys, axis=1) + d_ref[...] * xs
    o_ref[0] = _rms(yv * _silu(z_ref[0]), nrm_ref[...])

    @pl.when(t == n_blk - 1)
    def _():
        ssm_out_ref[0] = st_ref[...].T.reshape(HEADS, HEAD_DIM, B_STATE)


def _ssd(proj3, conv0, ssm0, cw, cb, dtb, alog, dvec, nrm, l_valid, t_blk):
    b, lrows, _ = proj3.shape
    n_blk = -(-l_valid // t_blk)
    assert n_blk * t_blk <= lrows
    vec = lambda w: pl.BlockSpec((1, w), lambda bi, t: (0, 0))
    return pl.pallas_call(
        functools.partial(_ssd_kernel, t_blk=t_blk, l_valid=l_valid, n_blk=n_blk),
        out_shape=(jax.ShapeDtypeStruct((b, n_blk * t_blk, GROUP_WIDTH), F32),
                   jax.ShapeDtypeStruct((b, B_CONV - 1, B_CONV_DIM), F32),
                   jax.ShapeDtypeStruct((b, HEADS, HEAD_DIM, B_STATE), F32)),
        grid=(b, n_blk),
        in_specs=[pl.BlockSpec((1, t_blk, B_CONV_DIM), lambda bi, t: (bi, t, COL_XBC // B_CONV_DIM)),
                  pl.BlockSpec((1, t_blk, GROUP_WIDTH), lambda bi, t: (bi, t, COL_BZ // GROUP_WIDTH)),
                  pl.BlockSpec((1, t_blk, GROUP_WIDTH), lambda bi, t: (bi, t, COL_DT // GROUP_WIDTH)),
                  pl.BlockSpec((1, B_CONV - 1, B_CONV_DIM), lambda bi, t: (bi, 0, 0)),
                  pl.BlockSpec((1, HEADS, HEAD_DIM, B_STATE), lambda bi, t: (bi, 0, 0, 0)),
                  pl.BlockSpec((B_CONV, B_CONV_DIM), lambda bi, t: (0, 0)),
                  vec(B_CONV_DIM), vec(GROUP_WIDTH), vec(GROUP_WIDTH), vec(GROUP_WIDTH), vec(GROUP_WIDTH)],
        out_specs=(pl.BlockSpec((1, t_blk, GROUP_WIDTH), lambda bi, t: (bi, t, 0)),
                   pl.BlockSpec((1, B_CONV - 1, B_CONV_DIM), lambda bi, t: (bi, 0, 0)),
                   pl.BlockSpec((1, HEADS, HEAD_DIM, B_STATE), lambda bi, t: (bi, 0, 0, 0))),
        scratch_shapes=[pltpu.VMEM((CONV_PAD + t_blk, B_CONV_DIM), F32),
                        pltpu.VMEM((B_STATE, GROUP_WIDTH), F32)],
        compiler_params=_cparams(("parallel", "arbitrary")),
        name="ssd",
    )(proj3, proj3, proj3, conv0, ssm0, cw, cb, dtb, alog, dvec, nrm)


def _hgrn_kernel(q_ref, f_ref, i_ref, g_ref, st0_ref, lb_ref, nrm_ref, o_ref, st_out_ref, st_ref, c_all, v_all,
                 y_all, *, t_blk, l_valid, n_blk):
    t = pl.program_id(1)

    @pl.when(t == 0)
    def _():
        st_ref[...] = st0_ref[0]

    lb = lb_ref[...]
    nrm = nrm_ref[...]
    head_eq = _head_ones()
    ones_bd = head_eq.astype(BF16)
    tril = _tril(CHUNK).astype(BF16)
    rows = lax.broadcasted_iota(jnp.int32, (CHUNK, GROUP_WIDTH), 0)
    row_in_sub = rows % SUB
    lane_head = lax.broadcasted_iota(jnp.int32, (SUB, GROUP_WIDTH), 1) // HEAD_DIM
    n_sub = CHUNK // SUB

    def sub_rows(scr, jj):
        return jnp.concatenate([jnp.broadcast_to(scr[a * SUB + jj:a * SUB + jj + 1, :], (SUB, GROUP_WIDTH))
                                for a in range(n_sub)], axis=0)

    def chunk(cc, slot):
        c_scr, v_scr, y_scr = c_all.at[slot], v_all.at[slot], y_all.at[slot]
        r0 = pl.multiple_of(cc * CHUNK, CHUNK)
        sl = pl.ds(r0, CHUNK)
        q = q_ref[0, sl, :]
        v = i_ref[0, sl, :]
        f = lb + (1.0 - lb) * jax.nn.sigmoid(f_ref[0, sl, :])
        kk = jnp.maximum(1.0 - f, 0.0)
        lf2 = jnp.log(f) * LOG2E
        if l_valid < n_blk * t_blk:
            ok = t * t_blk + r0 + rows < l_valid
            kk = jnp.where(ok, kk, 0.0)
            lf2 = jnp.where(ok, lf2, 0.0)
        bcs2 = _dot01_lhs(tril, lf2)
        c = jnp.log(kk) * LOG2E - bcs2
        c_scr[...] = c
        v_scr[...] = v
        vb = v.astype(BF16)

        o = _dot_nt((q * jnp.exp2(bcs2)).astype(BF16), st_ref[...].astype(BF16))

        for jj in range(SUB):
            w = jnp.exp2(bcs2 + sub_rows(c_scr, jj))
            y_scr[jj * CHUNK:(jj + 1) * CHUNK, :] = jnp.where(row_in_sub >= jj, q * w, 0.0).astype(BF16)
        att = _dot(y_scr[...], ones_bd)
        for jj in range(SUB):
            o = o + att[jj * CHUNK:(jj + 1) * CHUNK, :] * sub_rows(v_scr, jj)

        pieces = [jnp.zeros((SUB, GROUP_WIDTH), F32)]
        for a in range(1, n_sub):
            s0 = a * SUB
            ref_row = bcs2[s0 - 1:s0, :]
            qa = q[s0:s0 + SUB, :] * jnp.exp2(bcs2[s0:s0 + SUB, :] - ref_row)
            ka = jnp.where(rows < s0, jnp.exp2(ref_row + c), 0.0)
            qs = jnp.concatenate([jnp.where(lane_head == h, qa, 0.0) for h in range(HEADS)], axis=0).astype(BF16)
            sc = _dot_nt(qs, ka.astype(BF16))
            oa_all = _dot(sc.astype(BF16), vb)
            oa = jnp.zeros((SUB, GROUP_WIDTH), F32)
            for h in range(HEADS):
                oa = oa + jnp.where(lane_head == h, oa_all[h * SUB:(h + 1) * SUB, :], 0.0)
            pieces.append(oa)
        o = o + jnp.concatenate(pieces, axis=0)

        last = bcs2[CHUNK - 1:CHUNK, :]
        kt = jnp.exp2(last + c).astype(BF16)
        st_ref[...] = st_ref[...] * jnp.exp2(last) + jnp.where(head_eq, _dot_tn(vb, kt), 0.0)

        ms = _dot01_rhs(o * o, ones_bd) * (1.0 / HEAD_DIM)
        o_ref[0, sl, :] = o * lax.rsqrt(ms + EPS) * nrm * _silu(g_ref[0, sl, :])

    n_chunks = t_blk // CHUNK
    if n_chunks % 2 == 0:
        def pair(pp, carry):
            chunk(2 * pp, 0)
            chunk(2 * pp + 1, 1)
            return carry

        lax.fori_loop(0, n_chunks // 2, pair, 0)
    else:
        for cc in range(n_chunks):
            chunk(cc, cc % 2)

    @pl.when(t == n_blk - 1)
    def _():
        st_out_ref[0] = st_ref[...]


def _hgrn(proj3, st0, lb, nrm, l_valid, t_blk):
    b, lrows, _ = proj3.shape
    n_blk = -(-l_valid // t_blk)
    assert n_blk * t_blk <= lrows and t_blk % CHUNK == 0
    col = lambda c: pl.BlockSpec((1, t_blk, GROUP_WIDTH), lambda bi, t: (bi, t, c // GROUP_WIDTH))
    vec = pl.BlockSpec((1, GROUP_WIDTH), lambda bi, t: (0, 0))
    st_spec = pl.BlockSpec((1, GROUP_WIDTH, GROUP_WIDTH), lambda bi, t: (bi, 0, 0))
    return pl.pallas_call(
        functools.partial(_hgrn_kernel, t_blk=t_blk, l_valid=l_valid, n_blk=n_blk),
        out_shape=(jax.ShapeDtypeStruct((b, n_blk * t_blk, GROUP_WIDTH), F32),
                   jax.ShapeDtypeStruct((b, GROUP_WIDTH, GROUP_WIDTH), F32)),
        grid=(b, n_blk),
        in_specs=[col(COL_CQ), col(COL_CF), col(COL_CI), col(COL_CG), st_spec, vec, vec],
        out_specs=(pl.BlockSpec((1, t_blk, GROUP_WIDTH), lambda bi, t: (bi, t, 0)), st_spec),
        scratch_shapes=[pltpu.VMEM((GROUP_WIDTH, GROUP_WIDTH), F32),
                        pltpu.VMEM((2, CHUNK, GROUP_WIDTH), F32),
                        pltpu.VMEM((2, CHUNK, GROUP_WIDTH), F32),
                        pltpu.VMEM((2, SUB * CHUNK, GROUP_WIDTH), BF16)],
        compiler_params=_cparams(("parallel", "arbitrary")),
        name="hgrn",
    )(proj3, proj3, proj3, proj3, st0, lb, nrm)


POOL_PAD = 16


def _pool_kernel(x_ref, buf_ref, w_ref, sc_ref, o_ref, buf_out_ref, xp_ref, *, t_blk):
    t = pl.program_id(1)

    @pl.when(t == 0)
    def _():
        xp_ref[0:1, :] = jnp.zeros((1, GROUP_WIDTH), F32)
        xp_ref[POOL_PAD - D_BUF:POOL_PAD, :] = buf_ref[0]

    x = x_ref[0]
    xp_ref[POOL_PAD:POOL_PAD + t_blk, :] = x
    acc = x
    sums = {}
    for d in range(1, D_BUF + 1):
        acc = acc + xp_ref[POOL_PAD - d:POOL_PAD - d + t_blk, :]
        sums[d + 1] = acc
    lane = lax.broadcasted_iota(jnp.int32, (t_blk, GROUP_WIDTH), 1)
    pooled = jnp.where(lane < 64, sums[2] * 0.5,
                       jnp.where(lane < 128, sums[4] * 0.25,
                                 jnp.where(lane < 192, sums[8] * 0.125, sums[16] * 0.0625))) - x
    o_ref[0] = _dot(pooled.astype(BF16), w_ref[...]) * sc_ref[...]
    tail = xp_ref[t_blk + 1:t_blk + 1 + D_BUF, :]
    buf_out_ref[0] = tail
    xp_ref[POOL_PAD - D_BUF:POOL_PAD, :] = tail


def _pool(proj3, buf, w_bd, scale, l_valid, t_blk):
    b = proj3.shape[0]
    n_blk = l_valid // t_blk
    assert n_blk * t_blk == l_valid and t_blk >= D_BUF
    buf_spec = pl.BlockSpec((1, D_BUF, GROUP_WIDTH), lambda bi, t: (bi, 0, 0))
    return pl.pallas_call(
        functools.partial(_pool_kernel, t_blk=t_blk),
        out_shape=(jax.ShapeDtypeStruct((b, l_valid, GROUP_WIDTH), F32),
                   jax.ShapeDtypeStruct((b, D_BUF, GROUP_WIDTH), F32)),
        grid=(b, n_blk),
        in_specs=[pl.BlockSpec((1, t_blk, GROUP_WIDTH), lambda bi, t: (bi, t, COL_DX // GROUP_WIDTH)),
                  buf_spec,
                  pl.BlockSpec((GROUP_WIDTH, GROUP_WIDTH), lambda bi, t: (0, 0)),
                  pl.BlockSpec((1, GROUP_WIDTH), lambda bi, t: (0, 0))],
        out_specs=(pl.BlockSpec((1, t_blk, GROUP_WIDTH), lambda bi, t: (bi, t, 0)), buf_spec),
        scratch_shapes=[pltpu.VMEM((POOL_PAD + t_blk, GROUP_WIDTH), F32)],
        compiler_params=_cparams(("parallel", "arbitrary")),
        name="pool",
    )(proj3, buf, w_bd, scale)


SSD_BLOCK = 128
HGRN_BLOCK = 512
POOL_BLOCK = 512


def _prep_layer(layer, norm1, w_in, a_rel_bias, b_conv_w, b_conv_b, b_dt_bias, b_a_log, b_d, b_norm, lbs, c_norm,
                d_pool_w, d_pool_scale, w_out, norm2, w_gate_up, w_down):
    wt = jnp.transpose(w_in, (2, 0, 1))[:, layer, :]
    o_dt = 3 * GROUP_WIDTH + GROUP_WIDTH + B_CONV_DIM
    w_r = _w_in_prep(jnp.concatenate([wt[1024:1792], wt[0:1024], wt[o_dt + HEADS:],
                                      jnp.repeat(wt[o_dt:o_dt + HEADS], HEAD_DIM, axis=0)], axis=0))
    tab = a_rel_bias[layer].T
    m = A_BAND + CHUNK
    far = tab[:, 2 * REL_CLIP:]
    n_far = A_BAND_PREV - REL_CLIP + 1
    r = jnp.concatenate([jnp.broadcast_to(far, (HEADS, n_far)),
                         tab[:, 2 * REL_CLIP - 1:CHUNK:-1],
                         jnp.broadcast_to(far, (HEADS, m - n_far - (2 * REL_CLIP - 1 - CHUNK)))], axis=1)
    bias = jnp.tile(r, (1, CHUNK))[:, :CHUNK * (m - 1)].reshape(HEADS, CHUNK, m - 1)[:, :, :A_BAND]
    bias = bias.reshape(HEADS * CHUNK, A_BAND)
    rep = lambda p: jnp.repeat(p[layer], HEAD_DIM)[None, :]
    pw = d_pool_w[layer]
    w_bd = jnp.zeros((GROUP_WIDTH, GROUP_WIDTH), F32)
    for g in range(4):
        w_bd = w_bd.at[g * 64:(g + 1) * 64, g * 64:(g + 1) * 64].set(pw[g])
    return dict(
        n1=norm1[layer][None, :], w_in=w_r, bias=bias,
        cw=b_conv_w[layer], cb=b_conv_b[layer][None, :], dtb=rep(b_dt_bias), alog=rep(b_a_log), dvec=rep(b_d),
        bnrm=b_norm[layer][None, :], lb=lbs[layer][None, :], cnrm=c_norm[layer].reshape(1, GROUP_WIDTH),
        w_bd=w_bd.astype(BF16), psc=d_pool_scale[layer][None, :], n2=norm2[layer][None, :], layer=layer,
        w_out=w_out.astype(BF16), w_gu=w_gate_up.astype(BF16), w_dn=w_down.astype(BF16))


def _round_up(n, m):
    return -(-n // m) * m


def _group_layer(x, st, p, nf, final):
    k_prev, v_prev, conv0, ssm0, hgrn0, pool0 = st
    b, l, _ = x.shape
    p_hist = k_prev.shape[1]
    proj = _inproj(x.reshape(b * l, D_MODEL), p["n1"], p["w_in"]).reshape(b, l, PROJ_W)
    lpad = _round_up(l, SSD_BLOCK)
    proj_p = proj if lpad == l else jnp.pad(proj, ((0, 0), (0, lpad - l), (0, 0)))

    keep = min(A_BAND_PREV, p_hist + l)
    n_new = min(l, keep)
    ka = proj[:, l - n_new:, COL_AK:COL_AK + GROUP_WIDTH].reshape(b, n_new, HEADS, HEAD_DIM)
    va = proj[:, l - n_new:, COL_AV:COL_AV + GROUP_WIDTH].reshape(b, n_new, HEADS, HEAD_DIM)
    if p_hist == 0:
        k_cache = v_cache = None
    else:
        k_cache = k_prev.astype(BF16).reshape(b, p_hist, GROUP_WIDTH)
        v_cache = v_prev.astype(BF16).reshape(b, p_hist, GROUP_WIDTH)
    oa = _attention(proj_p, k_cache, v_cache, p["bias"], l)[:, :l]
    new_k = jnp.concatenate([k_prev[:, p_hist - (keep - n_new):], ka], axis=1)
    new_v = jnp.concatenate([v_prev[:, p_hist - (keep - n_new):], va], axis=1)

    ob, new_conv, new_ssm = _ssd(proj_p, conv0, ssm0, p["cw"], p["cb"], p["dtb"], p["alog"], p["dvec"], p["bnrm"],
                                 l, SSD_BLOCK)
    ob = ob[:, :l]

    eye = jnp.eye(HEADS, dtype=F32)
    st_bd = jnp.einsum("bhkv,hg->bhvgk", hgrn0, eye).reshape(b, GROUP_WIDTH, GROUP_WIDTH)
    oc, st_out = _hgrn(proj_p, st_bd, p["lb"], p["cnrm"], l,
                       HGRN_BLOCK if l % HGRN_BLOCK == 0 else _round_up(l, CHUNK))
    oc = oc[:, :l]
    st_out = st_out.reshape(b, HEADS, HEAD_DIM, HEADS, HEAD_DIM)
    new_hgrn = jnp.stack([st_out[:, h, :, h, :] for h in range(HEADS)], axis=1).transpose(0, 1, 3, 2)

    od, new_pool = _pool(proj_p, pool0, p["w_bd"], p["psc"], l, POOL_BLOCK if l % POOL_BLOCK == 0 else l)

    flat = lambda a: a.reshape(b * l, GROUP_WIDTH)
    y = _mlp(x.reshape(b * l, D_MODEL), (flat(oa), flat(ob), flat(oc), flat(od)), p["layer"],
             p["w_out"], p["n2"], p["w_gu"], p["w_dn"], nf, final)
    return y.reshape(b, l, D_MODEL), (new_k, new_v, new_conv, new_ssm, new_hgrn, new_pool)


def _trunk(x, states, params, nf):
    depth = len(params)
    new = [[] for _ in range(6)]
    for layer in range(depth):
        st_l = tuple(s[layer] for s in states)
        x, ns = _group_layer(x, st_l, params[layer], nf, layer == depth - 1)
        for i in range(6):
            new[i].append(ns[i])
    return x, tuple(jnp.stack(n, axis=0) for n in new)


def kernel(x_prompt, x_sample, cache_a_k, cache_a_v, state_b_conv, state_b_ssm, state_c_hgrn, state_d_pool, norm1,
           w_in, a_rel_bias, b_conv_w, b_conv_b, b_dt_bias, b_a_log, b_d, b_norm, c_lb_logits, c_norm, d_pool_w,
           d_pool_scale, w_out, norm2, w_gate_up, w_down, norm_f):
    depth = w_in.shape[0]
    lbs = jnp.cumsum(jax.nn.softmax(c_lb_logits.astype(F32), axis=0), axis=0)
    lbs = lbs - lbs[:1]
    params = [_prep_layer(layer, norm1, w_in, a_rel_bias, b_conv_w, b_conv_b, b_dt_bias, b_a_log, b_d, b_norm, lbs,
                          c_norm, d_pool_w, d_pool_scale, w_out, norm2, w_gate_up, w_down)
              for layer in range(depth)]
    nf = norm_f[None, :]
    bp = x_prompt.shape[0]
    prompt_states = (
        jnp.zeros((depth, bp, 0, HEADS, HEAD_DIM), F32),
        jnp.zeros((depth, bp, 0, HEADS, HEAD_DIM), F32),
        jnp.zeros((depth, bp, B_CONV - 1, B_CONV_DIM), F32),
        jnp.zeros((depth, bp, HEADS, HEAD_DIM, B_STATE), F32),
        jnp.zeros((depth, bp, HEADS, HEAD_DIM, HEAD_DIM), F32),
        jnp.zeros((depth, bp, D_BUF, GROUP_WIDTH), F32),
    )
    y_prompt, ps = _trunk(x_prompt, prompt_states, params, nf)
    sample_states = (cache_a_k, cache_a_v, state_b_conv, state_b_ssm, state_c_hgrn, state_d_pool)
    y_sample, ss = _trunk(x_sample, sample_states, params, nf)
    return (y_prompt, y_sample) + ps + ss
```

```python
import functools

import jax
import jax.numpy as jnp
from jax import lax
from jax.experimental import pallas as pl
from jax.experimental.pallas import tpu as pltpu

F32 = jnp.float32
BF16 = jnp.bfloat16

D_MODEL = 1024
GROUP_WIDTH = 256
HEADS = 4
HEAD_DIM = 64
CHUNK = 64
A_BAND_PREV = 512
A_BAND = A_BAND_PREV + CHUNK
REL_CLIP = 128
B_STATE = 128
B_CONV = 4
B_CONV_DIM = 768
D_BUF = 15
D_FF = 2816
EPS = 1e-6
NEG = -1e30
LOG2E = 1.4426950408889634
SUB = 16

COL_XBC = 0
COL_AQ = 768
COL_AK = 1024
COL_AV = 1280
COL_BZ = 1536
COL_CQ = 1792
COL_CF = 2048
COL_CI = 2304
COL_CG = 2560
COL_DX = 2816
COL_DT = 3072
PROJ_W = 3328

VMEM_LIMIT = 56 * 1024 * 1024


def _cparams(sem):
    return pltpu.CompilerParams(dimension_semantics=sem, vmem_limit_bytes=VMEM_LIMIT)


def _rms(x, w):
    return x * lax.rsqrt(jnp.mean(x * x, axis=-1, keepdims=True) + EPS) * w


def _silu(x):
    return x * jax.nn.sigmoid(x)


def _dot(a, b):
    return jnp.dot(a, b, preferred_element_type=F32)


def _dot_nt(a, b):
    return lax.dot_general(a, b, (((1,), (1,)), ((), ())), preferred_element_type=F32)


def _dot_tn(a, b):
    return lax.dot_general(a, b, (((0,), (0,)), ((), ())), preferred_element_type=F32)


def _split3(x):
    hi = x.astype(BF16)
    r = x - hi.astype(F32)
    mid = r.astype(BF16)
    lo = (r - mid.astype(F32)).astype(BF16)
    return hi, mid, lo


def _dot01_lhs(m01, x):
    hi, mid, lo = _split3(x)
    return _dot(m01, hi) + _dot(m01, mid) + _dot(m01, lo)


def _dot01_rhs(x, m01):
    hi, mid, lo = _split3(x)
    return _dot(hi, m01) + _dot(mid, m01) + _dot(lo, m01)


def _tril(t):
    r = lax.broadcasted_iota(jnp.int32, (t, t), 0)
    c = lax.broadcasted_iota(jnp.int32, (t, t), 1)
    return r >= c


def _head_ones():
    r = lax.broadcasted_iota(jnp.int32, (GROUP_WIDTH, GROUP_WIDTH), 0) // HEAD_DIM
    c = lax.broadcasted_iota(jnp.int32, (GROUP_WIDTH, GROUP_WIDTH), 1) // HEAD_DIM
    return r == c


W_PREP_COLS = 256


def _w_in_prep_kernel(wt_ref, o_ref):
    o_ref[...] = wt_ref[...].T.astype(BF16)


def _w_in_prep(wt):
    return pl.pallas_call(
        _w_in_prep_kernel,
        out_shape=jax.ShapeDtypeStruct((D_MODEL, PROJ_W), BF16),
        grid=(PROJ_W // W_PREP_COLS,),
        in_specs=[pl.BlockSpec((W_PREP_COLS, D_MODEL), lambda i: (i, 0))],
        out_specs=pl.BlockSpec((D_MODEL, W_PREP_COLS), lambda i: (0, i)),
        compiler_params=_cparams(("parallel",)),
        name="w_in_prep",
    )(wt)


def _inproj_kernel(x_ref, n_ref, w_ref, o_ref):
    hb = _rms(x_ref[...], n_ref[...]).astype(BF16)
    for c0 in range(0, PROJ_W, 256):
        o_ref[:, c0:c0 + 256] = _dot(hb, w_ref[:, c0:c0 + 256])


def _inproj(x2d, n1, w_in):
    rows = x2d.shape[0]
    tm = 512 if rows % 512 == 0 else rows
    return pl.pallas_call(
        _inproj_kernel,
        out_shape=jax.ShapeDtypeStruct((rows, PROJ_W), F32),
        grid=(rows // tm,),
        in_specs=[pl.BlockSpec((tm, D_MODEL), lambda i: (i, 0)),
                  pl.BlockSpec((1, D_MODEL), lambda i: (0, 0)),
                  pl.BlockSpec((D_MODEL, PROJ_W), lambda i: (0, 0), pipeline_mode=pl.Buffered(1))],
        out_specs=pl.BlockSpec((tm, PROJ_W), lambda i: (i, 0)),
        compiler_params=_cparams(("parallel",)),
        name="inproj",
    )(x2d, n1, w_in)


FF_CHUNK = 256


def _mlp_kernel(x_ref, oa_ref, ob_ref, oc_ref, od_ref, wo_ref, n2_ref, wgu_ref, wdn_ref, nf_ref, o_ref, mix_ref,
                act_ref, *, final):
    for m, r in enumerate((oa_ref, ob_ref, oc_ref, od_ref)):
        mix_ref[:, m * GROUP_WIDTH:(m + 1) * GROUP_WIDTH] = r[...].astype(BF16)
    x1 = x_ref[...] + _dot(mix_ref[...], wo_ref[0])
    hb = _rms(x1, n2_ref[...]).astype(BF16)
    for c0 in range(0, D_FF, FF_CHUNK):
        gate = _dot(hb, wgu_ref[0, :, c0:c0 + FF_CHUNK])
        up = _dot(hb, wgu_ref[0, :, D_FF + c0:D_FF + c0 + FF_CHUNK])
        act_ref[:, c0:c0 + FF_CHUNK] = (_silu(gate) * up).astype(BF16)
    out = x1 + _dot(act_ref[...], wdn_ref[0])
    if final:
        out = _rms(out, nf_ref[...])
    o_ref[...] = out


def _mlp(x2d, mix, layer, w_out, n2, w_gu, w_dn, nf, final):
    rows = x2d.shape[0]
    tm = 512 if rows % 512 == 0 else rows
    row_spec = lambda w: pl.BlockSpec((tm, w), lambda i: (i, 0))
    vec = lambda a: pl.BlockSpec(a.shape, lambda i: (0, 0))
    wspec = lambda a: pl.BlockSpec((1,) + a.shape[1:], lambda i: (layer, 0, 0), pipeline_mode=pl.Buffered(1))
    return pl.pallas_call(
        functools.partial(_mlp_kernel, final=final),
        out_shape=jax.ShapeDtypeStruct((rows, D_MODEL), F32),
        grid=(rows // tm,),
        in_specs=[row_spec(D_MODEL)] + [row_spec(GROUP_WIDTH)] * 4
                 + [wspec(w_out), vec(n2), wspec(w_gu), wspec(w_dn), vec(nf)],
        out_specs=row_spec(D_MODEL),
        scratch_shapes=[pltpu.VMEM((tm, D_MODEL), BF16), pltpu.VMEM((tm, D_FF), BF16)],
        compiler_params=_cparams(("parallel",)),
        name="mlp",
    )(x2d, *mix, w_out, n2, w_gu, w_dn, nf)


ATTN_UNROLL = 4


def _attn_kernel(q_ref, kh_ref, kc_ref, vh_ref, vc_ref, bias_ref, o_ref, kb_ref, vb_ref, *, qb, p_hist, l_valid):
    i = pl.program_id(1)
    if p_hist == 0:
        @pl.when(i == 0)
        def _():
            kb_ref[0:A_BAND_PREV, :] = jnp.zeros((A_BAND_PREV, GROUP_WIDTH), BF16)
            vb_ref[0:A_BAND_PREV, :] = jnp.zeros((A_BAND_PREV, GROUP_WIDTH), BF16)

        @pl.when(i > 0)
        def _():
            kb_ref[0:A_BAND_PREV, :] = kh_ref[0].astype(BF16)
            vb_ref[0:A_BAND_PREV, :] = vh_ref[0].astype(BF16)
    else:
        kb_ref[0:A_BAND_PREV, :] = kh_ref[0].astype(BF16)
        vb_ref[0:A_BAND_PREV, :] = vh_ref[0].astype(BF16)
    kb_ref[A_BAND_PREV:A_BAND_PREV + qb, :] = kc_ref[0].astype(BF16)
    vb_ref[A_BAND_PREV:A_BAND_PREV + qb, :] = vc_ref[0].astype(BF16)
    lane_head = lax.broadcasted_iota(jnp.int32, (CHUNK, GROUP_WIDTH), 1) // HEAD_DIM
    band_pos = lax.broadcasted_iota(jnp.int32, (1, A_BAND), 1)

    def chunk(cc, carry):
        r0 = pl.multiple_of(cc * CHUNK, CHUNK)
        qc = q_ref[0, pl.ds(r0, CHUNK), :] * (HEAD_DIM ** -0.5)
        qs = jnp.concatenate([jnp.where(lane_head == h, qc, 0.0) for h in range(HEADS)], axis=0).astype(BF16)
        s = _dot_nt(qs, kb_ref[pl.ds(r0, A_BAND), :]) + bias_ref[...]
        kpos = i * qb + r0 - A_BAND_PREV + band_pos
        valid = (kpos >= -p_hist) & (kpos < l_valid)
        s = jnp.where(valid, s, NEG)
        m = jnp.max(s, axis=-1, keepdims=True)
        e = jnp.exp(s - m)
        den = jnp.sum(e, axis=-1, keepdims=True)
        o_all = _dot(e.astype(BF16), vb_ref[pl.ds(r0, A_BAND), :]) * (1.0 / den)
        o = jnp.zeros((CHUNK, GROUP_WIDTH), F32)
        for h in range(HEADS):
            o = o + jnp.where(lane_head == h, o_all[h * CHUNK:(h + 1) * CHUNK, :], 0.0)
        o_ref[0, pl.ds(r0, CHUNK), :] = o
        return carry

    n_chunks = qb // CHUNK
    lax.fori_loop(0, n_chunks, chunk, 0, unroll=ATTN_UNROLL if n_chunks % ATTN_UNROLL == 0 else 1)


def _attention(proj3, k_cache, v_cache, bias, l_valid):
    b = proj3.shape[0]
    lq = _round_up(l_valid, CHUNK)
    qb = A_BAND_PREV if lq % A_BAND_PREV == 0 else lq
    nblk = lq // qb
    col = lambda c: pl.BlockSpec((1, qb, GROUP_WIDTH), lambda bi, i: (bi, i, c // GROUP_WIDTH))
    if k_cache is None:
        assert qb == A_BAND_PREV
        p_hist = 0
        prev = lambda c: pl.BlockSpec((1, qb, GROUP_WIDTH), lambda bi, i: (bi, jnp.maximum(i - 1, 0), c // GROUP_WIDTH))
        kh, vh, kh_spec, vh_spec = proj3, proj3, prev(COL_AK), prev(COL_AV)
    else:
        assert nblk == 1 and k_cache.shape[1] == A_BAND_PREV
        p_hist = A_BAND_PREV
        kh_spec = vh_spec = pl.BlockSpec((1, A_BAND_PREV, GROUP_WIDTH), lambda bi, i: (bi, 0, 0))
        kh, vh = k_cache, v_cache
    return pl.pallas_call(
        functools.partial(_attn_kernel, qb=qb, p_hist=p_hist, l_valid=l_valid),
        out_shape=jax.ShapeDtypeStruct((b, lq, GROUP_WIDTH), F32),
        grid=(b, nblk),
        in_specs=[col(COL_AQ), kh_spec, col(COL_AK), vh_spec, col(COL_AV),
                  pl.BlockSpec((HEADS * CHUNK, A_BAND), lambda bi, i: (0, 0))],
        out_specs=pl.BlockSpec((1, qb, GROUP_WIDTH), lambda bi, i: (bi, i, 0)),
        scratch_shapes=[pltpu.VMEM((A_BAND_PREV + qb, GROUP_WIDTH), BF16),
                        pltpu.VMEM((A_BAND_PREV + qb, GROUP_WIDTH), BF16)],
        compiler_params=_cparams(("parallel", "arbitrary")),
        name="attention",
    )(proj3, kh, proj3, vh, proj3, bias)


CONV_PAD = 8
SSD_CHUNK = 128


def _ssd_kernel(xbc_ref, z_ref, dt_ref, conv0_ref, ssm0_ref, cw_ref, cb_ref, dtb_ref, alog_ref, d_ref, nrm_ref,
                o_ref, conv_out_ref, ssm_out_ref, xpad_ref, st_ref, *, t_blk, l_valid, n_blk):
    t = pl.program_id(1)
    hist = B_CONV - 1

    @pl.when(t == 0)
    def _():
        xpad_ref[CONV_PAD - hist:CONV_PAD, :] = conv0_ref[0]
        st_ref[...] = ssm0_ref[0].reshape(GROUP_WIDTH, B_STATE).T

    n_last = l_valid - (n_blk - 1) * t_blk
    conv_out_ref[0] = xbc_ref[0, n_last - hist:n_last, :]

    tc = SSD_CHUNK
    tril = _tril(tc)
    tril_b = tril.astype(BF16)
    lane = lax.broadcasted_iota(jnp.int32, (tc, B_STATE), 1)
    rows = lax.broadcasted_iota(jnp.int32, (tc, GROUP_WIDTH), 0)
    a_neg = -jnp.exp(alog_ref[...])
    conv_b = cb_ref[...]
    conv_w = cw_ref[...]
    dt_bias = dtb_ref[...]
    d_skip = d_ref[...]
    nrm = nrm_ref[...]

    def chunk(j, carry):
        r0 = pl.multiple_of(j * tc, tc)
        sl = pl.ds(r0, tc)
        xpad_ref[CONV_PAD:CONV_PAD + tc, :] = xbc_ref[0, sl, :]
        y = conv_b
        for i in range(B_CONV):
            y = y + xpad_ref[CONV_PAD - hist + i:CONV_PAD - hist + i + tc, :] * conv_w[i:i + 1, :]
        xbc = _silu(y)
        xpad_ref[CONV_PAD - hist:CONV_PAD, :] = xpad_ref[CONV_PAD + tc - hist:CONV_PAD + tc, :]

        xs = xbc[:, 0:GROUP_WIDTH]
        bm = xbc[:, GROUP_WIDTH:2 * GROUP_WIDTH]
        cm = xbc[:, 2 * GROUP_WIDTH:3 * GROUP_WIDTH]
        dtr = dt_ref[0, sl, :] + dt_bias
        dt = jnp.maximum(dtr, 0.0) + jnp.log1p(jnp.exp(-jnp.abs(dtr)))
        if l_valid < n_blk * t_blk:
            dt = jnp.where(t * t_blk + r0 + rows < l_valid, dt, 0.0)
        a = dt * a_neg
        u = xs * dt
        acs = _dot01_lhs(tril_b, a)
        acs_t = acs.T
        eacs = jnp.exp(acs)
        last = acs[tc - 1:tc, :]
        elast = jnp.exp(last)
        ub = u.astype(BF16)
        upb = (u * jnp.exp(last - acs)).astype(BF16)
        ys = []
        for g in range(2):
            gs = slice(g * B_STATE, (g + 1) * B_STATE)
            cg = cm[:, gs].astype(BF16)
            bg = bm[:, gs].astype(BF16)
            cb = _dot_nt(cg, bg)
            y_state = _dot(cg, st_ref[:, gs].astype(BF16)) * eacs[:, gs]
            parts = []
            for hh in range(2):
                c0 = (2 * g + hh) * HEAD_DIM
                col = jnp.broadcast_to(acs[:, c0:c0 + 1], (tc, tc))
                rowv = jnp.broadcast_to(acs_t[c0:c0 + 1, :], (tc, tc))
                dec = jnp.exp(jnp.where(tril, col - rowv, NEG))
                parts.append(_dot((cb * dec).astype(BF16), ub[:, gs]))
            ys.append(jnp.where(lane < HEAD_DIM, parts[0], parts[1]) + y_state)
            st_ref[:, gs] = st_ref[:, gs] * elast[:, gs] + _dot_tn(bg, upb[:, gs])
        yv = jnp.concatenate(ys, axis=1) + d_skip * xs
        o_ref[0, sl, :] = _rms(yv * _silu(z_ref[0, sl, :]), nrm)
        return carry

    lax.fori_loop(0, t_blk // tc, chunk, 0)

    @pl.when(t == n_blk - 1)
    def _():
        ssm_out_ref[0] = st_ref[...].T.reshape(HEADS, HEAD_DIM, B_STATE)


def _ssd(proj3, conv0, ssm0, cw, cb, dtb, alog, dvec, nrm, l_valid, t_blk):
    b, lrows, _ = proj3.shape
    n_blk = -(-l_valid // t_blk)
    assert n_blk * t_blk <= lrows and t_blk % SSD_CHUNK == 0
    assert l_valid - (n_blk - 1) * t_blk >= B_CONV - 1
    vec = lambda w: pl.BlockSpec((1, w), lambda bi, t: (0, 0))
    return pl.pallas_call(
        functools.partial(_ssd_kernel, t_blk=t_blk, l_valid=l_valid, n_blk=n_blk),
        out_shape=(jax.ShapeDtypeStruct((b, n_blk * t_blk, GROUP_WIDTH), F32),
                   jax.ShapeDtypeStruct((b, B_CONV - 1, B_CONV_DIM), F32),
                   jax.ShapeDtypeStruct((b, HEADS, HEAD_DIM, B_STATE), F32)),
        grid=(b, n_blk),
        in_specs=[pl.BlockSpec((1, t_blk, B_CONV_DIM), lambda bi, t: (bi, t, COL_XBC // B_CONV_DIM)),
                  pl.BlockSpec((1, t_blk, GROUP_WIDTH), lambda bi, t: (bi, t, COL_BZ // GROUP_WIDTH)),
                  pl.BlockSpec((1, t_blk, GROUP_WIDTH), lambda bi, t: (bi, t, COL_DT // GROUP_WIDTH)),
                  pl.BlockSpec((1, B_CONV - 1, B_CONV_DIM), lambda bi, t: (bi, 0, 0)),
                  pl.BlockSpec((1, HEADS, HEAD_DIM, B_STATE), lambda bi, t: (bi, 0, 0, 0)),
                  pl.BlockSpec((B_CONV, B_CONV_DIM), lambda bi, t: (0, 0)),
                  vec(B_CONV_DIM), vec(GROUP_WIDTH), vec(GROUP_WIDTH), vec(GROUP_WIDTH), vec(GROUP_WIDTH)],
        out_specs=(pl.BlockSpec((1, t_blk, GROUP_WIDTH), lambda bi, t: (bi, t, 0)),
                   pl.BlockSpec((1, B_CONV - 1, B_CONV_DIM), lambda bi, t: (bi, 0, 0)),
                   pl.BlockSpec((1, HEADS, HEAD_DIM, B_STATE), lambda bi, t: (bi, 0, 0, 0))),
        scratch_shapes=[pltpu.VMEM((CONV_PAD + SSD_CHUNK, B_CONV_DIM), F32),
                        pltpu.VMEM((B_STATE, GROUP_WIDTH), F32)],
        compiler_params=_cparams(("parallel", "arbitrary")),
        name="ssd",
    )(proj3, proj3, proj3, conv0, ssm0, cw, cb, dtb, alog, dvec, nrm)


def _hgrn_kernel(q_ref, f_ref, i_ref, g_ref, st0_ref, lb_ref, nrm_ref, o_ref, st_out_ref, st_ref, c_all, v_all,
                 y_all, *, t_blk, l_valid, n_blk):
    t = pl.program_id(1)

    @pl.when(t == 0)
    def _():
        st_ref[...] = st0_ref[0]

    lb = lb_ref[...]
    nrm = nrm_ref[...]
    head_eq = _head_ones()
    ones_bd = head_eq.astype(BF16)
    tril = _tril(CHUNK).astype(BF16)
    rows = lax.broadcasted_iota(jnp.int32, (CHUNK, GROUP_WIDTH), 0)
    row_in_sub = rows % SUB
    lane_head = lax.broadcasted_iota(jnp.int32, (SUB, GROUP_WIDTH), 1) // HEAD_DIM
    n_sub = CHUNK // SUB

    def sub_rows(scr, jj):
        return jnp.concatenate([jnp.broadcast_to(scr[a * SUB + jj:a * SUB + jj + 1, :], (SUB, GROUP_WIDTH))
                                for a in range(n_sub)], axis=0)

    def chunk(cc, slot):
        c_scr, v_scr, y_scr = c_all.at[slot], v_all.at[slot], y_all.at[slot]
        r0 = pl.multiple_of(cc * CHUNK, CHUNK)
        sl = pl.ds(r0, CHUNK)
        q = q_ref[0, sl, :]
        v = i_ref[0, sl, :]
        f = lb + (1.0 - lb) * jax.nn.sigmoid(f_ref[0, sl, :])
        kk = jnp.maximum(1.0 - f, 0.0)
        lf2 = jnp.log(f) * LOG2E
        if l_valid < n_blk * t_blk:
            ok = t * t_blk + r0 + rows < l_valid
            kk = jnp.where(ok, kk, 0.0)
            lf2 = jnp.where(ok, lf2, 0.0)
        bcs2 = _dot01_lhs(tril, lf2)
        c = jnp.log(kk) * LOG2E - bcs2
        c_scr[...] = c
        v_scr[...] = v
        vb = v.astype(BF16)

        o = _dot_nt((q * jnp.exp2(bcs2)).astype(BF16), st_ref[...].astype(BF16))

        for jj in range(SUB):
            w = jnp.exp2(bcs2 + sub_rows(c_scr, jj))
            y_scr[jj * CHUNK:(jj + 1) * CHUNK, :] = jnp.where(row_in_sub >= jj, q * w, 0.0).astype(BF16)
        att = _dot(y_scr[...], ones_bd)
        for jj in range(SUB):
            o = o + att[jj * CHUNK:(jj + 1) * CHUNK, :] * sub_rows(v_scr, jj)

        pieces = [jnp.zeros((SUB, GROUP_WIDTH), F32)]
        for a in range(1, n_sub):
            s0 = a * SUB
            ref_row = bcs2[s0 - 1:s0, :]
            qa = q[s0:s0 + SUB, :] * jnp.exp2(bcs2[s0:s0 + SUB, :] - ref_row)
            ka = jnp.where(rows < s0, jnp.exp2(ref_row + c), 0.0)
            qs = jnp.concatenate([jnp.where(lane_head == h, qa, 0.0) for h in range(HEADS)], axis=0).astype(BF16)
            sc = _dot_nt(qs, ka.astype(BF16))
            oa_all = _dot(sc.astype(BF16), vb)
            oa = jnp.zeros((SUB, GROUP_WIDTH), F32)
            for h in range(HEADS):
                oa = oa + jnp.where(lane_head == h, oa_all[h * SUB:(h + 1) * SUB, :], 0.0)
            pieces.append(oa)
        o = o + jnp.concatenate(pieces, axis=0)

        last = bcs2[CHUNK - 1:CHUNK, :]
        kt = jnp.exp2(last + c).astype(BF16)
        st_ref[...] = st_ref[...] * jnp.exp2(last) + jnp.where(head_eq, _dot_tn(vb, kt), 0.0)

        ms = _dot01_rhs(o * o, ones_bd) * (1.0 / HEAD_DIM)
        o_ref[0, sl, :] = o * lax.rsqrt(ms + EPS) * nrm * _silu(g_ref[0, sl, :])

    n_chunks = t_blk // CHUNK
    if n_chunks % 2 == 0:
        def pair(pp, carry):
            chunk(2 * pp, 0)
            chunk(2 * pp + 1, 1)
            return carry

        lax.fori_loop(0, n_chunks // 2, pair, 0)
    else:
        for cc in range(n_chunks):
            chunk(cc, cc % 2)

    @pl.when(t == n_blk - 1)
    def _():
        st_out_ref[0] = st_ref[...]


def _hgrn(proj3, st0, lb, nrm, l_valid, t_blk):
    b, lrows, _ = proj3.shape
    n_blk = -(-l_valid // t_blk)
    assert n_blk * t_blk <= lrows and t_blk % CHUNK == 0
    col = lambda c: pl.BlockSpec((1, t_blk, GROUP_WIDTH), lambda bi, t: (bi, t, c // GROUP_WIDTH))
    vec = pl.BlockSpec((1, GROUP_WIDTH), lambda bi, t: (0, 0))
    st_spec = pl.BlockSpec((1, GROUP_WIDTH, GROUP_WIDTH), lambda bi, t: (bi, 0, 0))
    return pl.pallas_call(
        functools.partial(_hgrn_kernel, t_blk=t_blk, l_valid=l_valid, n_blk=n_blk),
        out_shape=(jax.ShapeDtypeStruct((b, n_blk * t_blk, GROUP_WIDTH), F32),
                   jax.ShapeDtypeStruct((b, GROUP_WIDTH, GROUP_WIDTH), F32)),
        grid=(b, n_blk),
        in_specs=[col(COL_CQ), col(COL_CF), col(COL_CI), col(COL_CG), st_spec, vec, vec],
        out_specs=(pl.BlockSpec((1, t_blk, GROUP_WIDTH), lambda bi, t: (bi, t, 0)), st_spec),
        scratch_shapes=[pltpu.VMEM((GROUP_WIDTH, GROUP_WIDTH), F32),
                        pltpu.VMEM((2, CHUNK, GROUP_WIDTH), F32),
                        pltpu.VMEM((2, CHUNK, GROUP_WIDTH), F32),
                        pltpu.VMEM((2, SUB * CHUNK, GROUP_WIDTH), BF16)],
        compiler_params=_cparams(("parallel", "arbitrary")),
        name="hgrn",
    )(proj3, proj3, proj3, proj3, st0, lb, nrm)


POOL_PAD = 16


def _pool_kernel(x_ref, buf_ref, w_ref, sc_ref, o_ref, buf_out_ref, xp_ref, *, t_blk):
    t = pl.program_id(1)

    @pl.when(t == 0)
    def _():
        xp_ref[0:1, :] = jnp.zeros((1, GROUP_WIDTH), F32)
        xp_ref[POOL_PAD - D_BUF:POOL_PAD, :] = buf_ref[0]

    x = x_ref[0]
    xp_ref[POOL_PAD:POOL_PAD + t_blk, :] = x
    acc = x
    sums = {}
    for d in range(1, D_BUF + 1):
        acc = acc + xp_ref[POOL_PAD - d:POOL_PAD - d + t_blk, :]
        sums[d + 1] = acc
    lane = lax.broadcasted_iota(jnp.int32, (t_blk, GROUP_WIDTH), 1)
    pooled = jnp.where(lane < 64, sums[2] * 0.5,
                       jnp.where(lane < 128, sums[4] * 0.25,
                                 jnp.where(lane < 192, sums[8] * 0.125, sums[16] * 0.0625))) - x
    o_ref[0] = _dot(pooled.astype(BF16), w_ref[...]) * sc_ref[...]
    tail = xp_ref[t_blk + 1:t_blk + 1 + D_BUF, :]
    buf_out_ref[0] = tail
    xp_ref[POOL_PAD - D_BUF:POOL_PAD, :] = tail


def _pool(proj3, buf, w_bd, scale, l_valid, t_blk):
    b = proj3.shape[0]
    n_blk = l_valid // t_blk
    assert n_blk * t_blk == l_valid and t_blk >= D_BUF
    buf_spec = pl.BlockSpec((1, D_BUF, GROUP_WIDTH), lambda bi, t: (bi, 0, 0))
    return pl.pallas_call(
        functools.partial(_pool_kernel, t_blk=t_blk),
        out_shape=(jax.ShapeDtypeStruct((b, l_valid, GROUP_WIDTH), F32),
                   jax.ShapeDtypeStruct((b, D_BUF, GROUP_WIDTH), F32)),
        grid=(b, n_blk),
        in_specs=[pl.BlockSpec((1, t_blk, GROUP_WIDTH), lambda bi, t: (bi, t, COL_DX // GROUP_WIDTH)),
                  buf_spec,
                  pl.BlockSpec((GROUP_WIDTH, GROUP_WIDTH), lambda bi, t: (0, 0)),
                  pl.BlockSpec((1, GROUP_WIDTH), lambda bi, t: (0, 0))],
        out_specs=(pl.BlockSpec((1, t_blk, GROUP_WIDTH), lambda bi, t: (bi, t, 0)), buf_spec),
        scratch_shapes=[pltpu.VMEM((POOL_PAD + t_blk, GROUP_WIDTH), F32)],
        compiler_params=_cparams(("parallel", "arbitrary")),
        name="pool",
    )(proj3, buf, w_bd, scale)


SSD_BLOCK = 512
HGRN_BLOCK = 512
POOL_BLOCK = 512


def _prep_layer(layer, norm1, w_in, a_rel_bias, b_conv_w, b_conv_b, b_dt_bias, b_a_log, b_d, b_norm, lbs, c_norm,
                d_pool_w, d_pool_scale, w_out, norm2, w_gate_up, w_down):
    wt = jnp.transpose(w_in, (2, 0, 1))[:, layer, :]
    o_dt = 3 * GROUP_WIDTH + GROUP_WIDTH + B_CONV_DIM
    w_r = _w_in_prep(jnp.concatenate([wt[1024:1792], wt[0:1024], wt[o_dt + HEADS:],
                                      jnp.repeat(wt[o_dt:o_dt + HEADS], HEAD_DIM, axis=0)], axis=0))
    tab = a_rel_bias[layer].T
    m = A_BAND + CHUNK
    far = tab[:, 2 * REL_CLIP:]
    n_far = A_BAND_PREV - REL_CLIP + 1
    r = jnp.concatenate([jnp.broadcast_to(far, (HEADS, n_far)),
                         tab[:, 2 * REL_CLIP - 1:CHUNK:-1],
                         jnp.broadcast_to(far, (HEADS, m - n_far - (2 * REL_CLIP - 1 - CHUNK)))], axis=1)
    bias = jnp.tile(r, (1, CHUNK))[:, :CHUNK * (m - 1)].reshape(HEADS, CHUNK, m - 1)[:, :, :A_BAND]
    bias = bias.reshape(HEADS * CHUNK, A_BAND)
    rep = lambda p: jnp.repeat(p[layer], HEAD_DIM)[None, :]
    pw = d_pool_w[layer]
    w_bd = jnp.zeros((GROUP_WIDTH, GROUP_WIDTH), F32)
    for g in range(4):
        w_bd = w_bd.at[g * 64:(g + 1) * 64, g * 64:(g + 1) * 64].set(pw[g])
    return dict(
        n1=norm1[layer][None, :], w_in=w_r, bias=bias,
        cw=b_conv_w[layer], cb=b_conv_b[layer][None, :], dtb=rep(b_dt_bias), alog=rep(b_a_log), dvec=rep(b_d),
        bnrm=b_norm[layer][None, :], lb=lbs[layer][None, :], cnrm=c_norm[layer].reshape(1, GROUP_WIDTH),
        w_bd=w_bd.astype(BF16), psc=d_pool_scale[layer][None, :], n2=norm2[layer][None, :], layer=layer,
        w_out=w_out.astype(BF16), w_gu=w_gate_up.astype(BF16), w_dn=w_down.astype(BF16))


def _round_up(n, m):
    return -(-n // m) * m


def _group_layer(x, st, p, nf, final):
    k_prev, v_prev, conv0, ssm0, hgrn0, pool0 = st
    b, l, _ = x.shape
    p_hist = k_prev.shape[1]
    proj = _inproj(x.reshape(b * l, D_MODEL), p["n1"], p["w_in"]).reshape(b, l, PROJ_W)
    lpad = _round_up(l, SSD_CHUNK)
    proj_p = proj if lpad == l else jnp.pad(proj, ((0, 0), (0, lpad - l), (0, 0)))

    keep = min(A_BAND_PREV, p_hist + l)
    n_new = min(l, keep)
    ka = proj[:, l - n_new:, COL_AK:COL_AK + GROUP_WIDTH].reshape(b, n_new, HEADS, HEAD_DIM)
    va = proj[:, l - n_new:, COL_AV:COL_AV + GROUP_WIDTH].reshape(b, n_new, HEADS, HEAD_DIM)
    if p_hist == 0:
        k_cache = v_cache = None
    else:
        k_cache = k_prev.astype(BF16).reshape(b, p_hist, GROUP_WIDTH)
        v_cache = v_prev.astype(BF16).reshape(b, p_hist, GROUP_WIDTH)
    oa = _attention(proj_p, k_cache, v_cache, p["bias"], l)[:, :l]
    new_k = jnp.concatenate([k_prev[:, p_hist - (keep - n_new):], ka], axis=1)
    new_v = jnp.concatenate([v_prev[:, p_hist - (keep - n_new):], va], axis=1)

    ob, new_conv, new_ssm = _ssd(proj_p, conv0, ssm0, p["cw"], p["cb"], p["dtb"], p["alog"], p["dvec"], p["bnrm"],
                                 l, SSD_BLOCK if l % SSD_BLOCK == 0 else lpad)
    ob = ob[:, :l]

    eye = jnp.eye(HEADS, dtype=F32)
    st_bd = jnp.einsum("bhkv,hg->bhvgk", hgrn0, eye).reshape(b, GROUP_WIDTH, GROUP_WIDTH)
    oc, st_out = _hgrn(proj_p, st_bd, p["lb"], p["cnrm"], l,
                       HGRN_BLOCK if l % HGRN_BLOCK == 0 else _round_up(l, CHUNK))
    oc = oc[:, :l]
    st_out = st_out.reshape(b, HEADS, HEAD_DIM, HEADS, HEAD_DIM)
    new_hgrn = jnp.stack([st_out[:, h, :, h, :] for h in range(HEADS)], axis=1).transpose(0, 1, 3, 2)

    od, new_pool = _pool(proj_p, pool0, p["w_bd"], p["psc"], l, POOL_BLOCK if l % POOL_BLOCK == 0 else l)

    flat = lambda a: a.reshape(b * l, GROUP_WIDTH)
    y = _mlp(x.reshape(b * l, D_MODEL), (flat(oa), flat(ob), flat(oc), flat(od)), p["layer"],
             p["w_out"], p["n2"], p["w_gu"], p["w_dn"], nf, final)
    return y.reshape(b, l, D_MODEL), (new_k, new_v, new_conv, new_ssm, new_hgrn, new_pool)


def _trunk(x, states, params, nf):
    depth = len(params)
    new = [[] for _ in range(6)]
    for layer in range(depth):
        st_l = tuple(s[layer] for s in states)
        x, ns = _group_layer(x, st_l, params[layer], nf, layer == depth - 1)
        for i in range(6):
            new[i].append(ns[i])
    return x, tuple(jnp.stack(n, axis=0) for n in new)


def kernel(x_prompt, x_sample, cache_a_k, cache_a_v, state_b_conv, state_b_ssm, state_c_hgrn, state_d_pool, norm1,
           w_in, a_rel_bias, b_conv_w, b_conv_b, b_dt_bias, b_a_log, b_d, b_norm, c_lb_logits, c_norm, d_pool_w,
           d_pool_scale, w_out, norm2, w_gate_up, w_down, norm_f):
    depth = w_in.shape[0]
    lbs = jnp.cumsum(jax.nn.softmax(c_lb_logits.astype(F32), axis=0), axis=0)
    lbs = lbs - lbs[:1]
    params = [_prep_layer(layer, norm1, w_in, a_rel_bias, b_conv_w, b_conv_b, b_dt_bias, b_a_log, b_d, b_norm, lbs,
                          c_norm, d_pool_w, d_pool_scale, w_out, norm2, w_gate_up, w_down)
              for layer in range(depth)]
    nf = norm_f[None, :]
    bp = x_prompt.shape[0]
    prompt_states = (
        jnp.zeros((depth, bp, 0, HEADS, HEAD_DIM), F32),
        jnp.zeros((depth, bp, 0, HEADS, HEAD_DIM), F32),
        jnp.zeros((depth, bp, B_CONV - 1, B_CONV_DIM), F32),
        jnp.zeros((depth, bp, HEADS, HEAD_DIM, B_STATE), F32),
        jnp.zeros((depth, bp, HEADS, HEAD_DIM, HEAD_DIM), F32),
        jnp.zeros((depth, bp, D_BUF, GROUP_WIDTH), F32),
    )
    y_prompt, ps = _trunk(x_prompt, prompt_states, params, nf)
    sample_states = (cache_a_k, cache_a_v, state_b_conv, state_b_ssm, state_c_hgrn, state_d_pool)
    y_sample, ss = _trunk(x_sample, sample_states, params, nf)
    return (y_prompt, y_sample) + ps + ss
```

```python
import functools

import jax
import jax.numpy as jnp
from jax import lax
from jax.experimental import pallas as pl
from jax.experimental.pallas import tpu as pltpu

F32 = jnp.float32
BF16 = jnp.bfloat16

D_MODEL = 1024
GROUP_WIDTH = 256
HEADS = 4
HEAD_DIM = 64
CHUNK = 64
A_BAND_PREV = 512
A_BAND = A_BAND_PREV + CHUNK
REL_CLIP = 128
B_STATE = 128
B_CONV = 4
B_CONV_DIM = 768
D_BUF = 15
D_FF = 2816
EPS = 1e-6
NEG = -1e30
LOG2E = 1.4426950408889634
SUB = 16

COL_XBC = 0
COL_AQ = 768
COL_AK = 1024
COL_AV = 1280
COL_BZ = 1536
COL_CQ = 1792
COL_CF = 2048
COL_CI = 2304
COL_CG = 2560
COL_DX = 2816
COL_DT = 3072
PROJ_W = 3328

VMEM_LIMIT = 56 * 1024 * 1024


def _cparams(sem):
    return pltpu.CompilerParams(dimension_semantics=sem, vmem_limit_bytes=VMEM_LIMIT)


def _rms(x, w):
    return x * lax.rsqrt(jnp.mean(x * x, axis=-1, keepdims=True) + EPS) * w


def _silu(x):
    return x * jax.nn.sigmoid(x)


def _dot(a, b):
    return jnp.dot(a, b, preferred_element_type=F32)


def _dot_nt(a, b):
    return lax.dot_general(a, b, (((1,), (1,)), ((), ())), preferred_element_type=F32)


def _dot_tn(a, b):
    return lax.dot_general(a, b, (((0,), (0,)), ((), ())), preferred_element_type=F32)


def _split3(x):
    hi = x.astype(BF16)
    r = x - hi.astype(F32)
    mid = r.astype(BF16)
    lo = (r - mid.astype(F32)).astype(BF16)
    return hi, mid, lo


def _dot01_lhs(m01, x):
    hi, mid, lo = _split3(x)
    return _dot(m01, hi) + _dot(m01, mid) + _dot(m01, lo)


def _dot01_rhs(x, m01):
    hi, mid, lo = _split3(x)
    return _dot(hi, m01) + _dot(mid, m01) + _dot(lo, m01)


def _tril(t):
    r = lax.broadcasted_iota(jnp.int32, (t, t), 0)
    c = lax.broadcasted_iota(jnp.int32, (t, t), 1)
    return r >= c


def _head_ones():
    r = lax.broadcasted_iota(jnp.int32, (GROUP_WIDTH, GROUP_WIDTH), 0) // HEAD_DIM
    c = lax.broadcasted_iota(jnp.int32, (GROUP_WIDTH, GROUP_WIDTH), 1) // HEAD_DIM
    return r == c


W_PREP_COLS = 256


def _w_in_prep_kernel(wt_ref, o_ref):
    o_ref[...] = wt_ref[...].T.astype(BF16)


def _w_in_prep(wt):
    return pl.pallas_call(
        _w_in_prep_kernel,
        out_shape=jax.ShapeDtypeStruct((D_MODEL, PROJ_W), BF16),
        grid=(PROJ_W // W_PREP_COLS,),
        in_specs=[pl.BlockSpec((W_PREP_COLS, D_MODEL), lambda i: (i, 0))],
        out_specs=pl.BlockSpec((D_MODEL, W_PREP_COLS), lambda i: (0, i)),
        compiler_params=_cparams(("parallel",)),
        name="w_in_prep",
    )(wt)


def _inproj_kernel(x_ref, n_ref, w_ref, o_ref):
    hb = _rms(x_ref[...], n_ref[...]).astype(BF16)
    for c0 in range(0, PROJ_W, 256):
        o_ref[:, c0:c0 + 256] = _dot(hb, w_ref[:, c0:c0 + 256])


def _inproj(x2d, n1, w_in):
    rows = x2d.shape[0]
    tm = 512 if rows % 512 == 0 else rows
    return pl.pallas_call(
        _inproj_kernel,
        out_shape=jax.ShapeDtypeStruct((rows, PROJ_W), F32),
        grid=(rows // tm,),
        in_specs=[pl.BlockSpec((tm, D_MODEL), lambda i: (i, 0)),
                  pl.BlockSpec((1, D_MODEL), lambda i: (0, 0)),
                  pl.BlockSpec((D_MODEL, PROJ_W), lambda i: (0, 0), pipeline_mode=pl.Buffered(1))],
        out_specs=pl.BlockSpec((tm, PROJ_W), lambda i: (i, 0)),
        compiler_params=_cparams(("parallel",)),
        name="inproj",
    )(x2d, n1, w_in)


FF_CHUNK = 256


def _mlp_kernel(x_ref, oa_ref, ob_ref, oc_ref, od_ref, wo_ref, n2_ref, wgu_ref, wdn_ref, nf_ref, o_ref, mix_ref,
                act_ref, *, final):
    for m, r in enumerate((oa_ref, ob_ref, oc_ref, od_ref)):
        mix_ref[:, m * GROUP_WIDTH:(m + 1) * GROUP_WIDTH] = r[...].astype(BF16)
    x1 = x_ref[...] + _dot(mix_ref[...], wo_ref[0])
    hb = _rms(x1, n2_ref[...]).astype(BF16)
    for c0 in range(0, D_FF, FF_CHUNK):
        gate = _dot(hb, wgu_ref[0, :, c0:c0 + FF_CHUNK])
        up = _dot(hb, wgu_ref[0, :, D_FF + c0:D_FF + c0 + FF_CHUNK])
        act_ref[:, c0:c0 + FF_CHUNK] = (_silu(gate) * up).astype(BF16)
    out = x1 + _dot(act_ref[...], wdn_ref[0])
    if final:
        out = _rms(out, nf_ref[...])
    o_ref[...] = out


def _mlp(x2d, mix, layer, w_out, n2, w_gu, w_dn, nf, final):
    rows = x2d.shape[0]
    tm = 512 if rows % 512 == 0 else rows
    row_spec = lambda w: pl.BlockSpec((tm, w), lambda i: (i, 0))
    vec = lambda a: pl.BlockSpec(a.shape, lambda i: (0, 0))
    wspec = lambda a: pl.BlockSpec((1,) + a.shape[1:], lambda i: (layer, 0, 0), pipeline_mode=pl.Buffered(1))
    return pl.pallas_call(
        functools.partial(_mlp_kernel, final=final),
        out_shape=jax.ShapeDtypeStruct((rows, D_MODEL), F32),
        grid=(rows // tm,),
        in_specs=[row_spec(D_MODEL)] + [row_spec(GROUP_WIDTH)] * 4
                 + [wspec(w_out), vec(n2), wspec(w_gu), wspec(w_dn), vec(nf)],
        out_specs=row_spec(D_MODEL),
        scratch_shapes=[pltpu.VMEM((tm, D_MODEL), BF16), pltpu.VMEM((tm, D_FF), BF16)],
        compiler_params=_cparams(("parallel",)),
        name="mlp",
    )(x2d, *mix, w_out, n2, w_gu, w_dn, nf)


ATTN_UNROLL = 4


def _attn_kernel(q_ref, kh_ref, kc_ref, vh_ref, vc_ref, bias_ref, o_ref, kb_ref, vb_ref, *, qb, p_hist, l_valid):
    i = pl.program_id(1)
    if p_hist == 0:
        @pl.when(i == 0)
        def _():
            kb_ref[0:A_BAND_PREV, :] = jnp.zeros((A_BAND_PREV, GROUP_WIDTH), BF16)
            vb_ref[0:A_BAND_PREV, :] = jnp.zeros((A_BAND_PREV, GROUP_WIDTH), BF16)

        @pl.when(i > 0)
        def _():
            kb_ref[0:A_BAND_PREV, :] = kh_ref[0].astype(BF16)
            vb_ref[0:A_BAND_PREV, :] = vh_ref[0].astype(BF16)
    else:
        kb_ref[0:A_BAND_PREV, :] = kh_ref[0].astype(BF16)
        vb_ref[0:A_BAND_PREV, :] = vh_ref[0].astype(BF16)
    kb_ref[A_BAND_PREV:A_BAND_PREV + qb, :] = kc_ref[0].astype(BF16)
    vb_ref[A_BAND_PREV:A_BAND_PREV + qb, :] = vc_ref[0].astype(BF16)
    lane_head = lax.broadcasted_iota(jnp.int32, (CHUNK, GROUP_WIDTH), 1) // HEAD_DIM
    band_pos = lax.broadcasted_iota(jnp.int32, (1, A_BAND), 1)

    def chunk(cc, carry):
        r0 = pl.multiple_of(cc * CHUNK, CHUNK)
        qc = q_ref[0, pl.ds(r0, CHUNK), :] * (HEAD_DIM ** -0.5)
        qs = jnp.concatenate([jnp.where(lane_head == h, qc, 0.0) for h in range(HEADS)], axis=0).astype(BF16)
        s = _dot_nt(qs, kb_ref[pl.ds(r0, A_BAND), :]) + bias_ref[...]
        kpos = i * qb + r0 - A_BAND_PREV + band_pos
        valid = (kpos >= -p_hist) & (kpos < l_valid)
        s = jnp.where(valid, s, NEG)
        m = jnp.max(s, axis=-1, keepdims=True)
        e = jnp.exp(s - m)
        den = jnp.sum(e, axis=-1, keepdims=True)
        o_all = _dot(e.astype(BF16), vb_ref[pl.ds(r0, A_BAND), :]) * (1.0 / den)
        o = jnp.zeros((CHUNK, GROUP_WIDTH), F32)
        for h in range(HEADS):
            o = o + jnp.where(lane_head == h, o_all[h * CHUNK:(h + 1) * CHUNK, :], 0.0)
        o_ref[0, pl.ds(r0, CHUNK), :] = o
        return carry

    n_chunks = qb // CHUNK
    lax.fori_loop(0, n_chunks, chunk, 0, unroll=ATTN_UNROLL if n_chunks % ATTN_UNROLL == 0 else 1)


def _attention(proj3, k_cache, v_cache, bias, l_valid):
    b = proj3.shape[0]
    lq = _round_up(l_valid, CHUNK)
    qb = A_BAND_PREV if lq % A_BAND_PREV == 0 else lq
    nblk = lq // qb
    col = lambda c: pl.BlockSpec((1, qb, GROUP_WIDTH), lambda bi, i: (bi, i, c // GROUP_WIDTH))
    if k_cache is None:
        assert qb == A_BAND_PREV
        p_hist = 0
        prev = lambda c: pl.BlockSpec((1, qb, GROUP_WIDTH), lambda bi, i: (bi, jnp.maximum(i - 1, 0), c // GROUP_WIDTH))
        kh, vh, kh_spec, vh_spec = proj3, proj3, prev(COL_AK), prev(COL_AV)
    else:
        assert nblk == 1 and k_cache.shape[1] == A_BAND_PREV
        p_hist = A_BAND_PREV
        kh_spec = vh_spec = pl.BlockSpec((1, A_BAND_PREV, GROUP_WIDTH), lambda bi, i: (bi, 0, 0))
        kh, vh = k_cache, v_cache
    return pl.pallas_call(
        functools.partial(_attn_kernel, qb=qb, p_hist=p_hist, l_valid=l_valid),
        out_shape=jax.ShapeDtypeStruct((b, lq, GROUP_WIDTH), F32),
        grid=(b, nblk),
        in_specs=[col(COL_AQ), kh_spec, col(COL_AK), vh_spec, col(COL_AV),
                  pl.BlockSpec((HEADS * CHUNK, A_BAND), lambda bi, i: (0, 0))],
        out_specs=pl.BlockSpec((1, qb, GROUP_WIDTH), lambda bi, i: (bi, i, 0)),
        scratch_shapes=[pltpu.VMEM((A_BAND_PREV + qb, GROUP_WIDTH), BF16),
                        pltpu.VMEM((A_BAND_PREV + qb, GROUP_WIDTH), BF16)],
        compiler_params=_cparams(("parallel", "arbitrary")),
        name="attention",
    )(proj3, kh, proj3, vh, proj3, bias)


CONV_PAD = 8
SSD_CHUNK = 128


def _ssd_kernel(xbc_ref, z_ref, dt_ref, conv0_ref, ssm0_ref, cw_ref, cb_ref, dtb_ref, alog_ref, d_ref, nrm_ref,
                o_ref, conv_out_ref, ssm_out_ref, xpad_ref, st_ref, *, t_blk, l_valid, n_blk):
    t = pl.program_id(1)
    hist = B_CONV - 1

    @pl.when(t == 0)
    def _():
        xpad_ref[CONV_PAD - hist:CONV_PAD, :] = conv0_ref[0]
        st_ref[...] = ssm0_ref[0].reshape(GROUP_WIDTH, B_STATE).T

    n_last = l_valid - (n_blk - 1) * t_blk
    conv_out_ref[0] = xbc_ref[0, n_last - hist:n_last, :]

    tc = SSD_CHUNK
    tril = _tril(tc)
    tril_b = tril.astype(BF16)
    lane = lax.broadcasted_iota(jnp.int32, (tc, B_STATE), 1)
    rows = lax.broadcasted_iota(jnp.int32, (tc, GROUP_WIDTH), 0)
    a_neg = -jnp.exp(alog_ref[...])
    conv_b = cb_ref[...]
    conv_w = cw_ref[...]
    dt_bias = dtb_ref[...]
    d_skip = d_ref[...]
    nrm = nrm_ref[...]

    def chunk(j, carry):
        r0 = pl.multiple_of(j * tc, tc)
        sl = pl.ds(r0, tc)
        xpad_ref[CONV_PAD:CONV_PAD + tc, :] = xbc_ref[0, sl, :]
        y = conv_b
        for i in range(B_CONV):
            y = y + xpad_ref[CONV_PAD - hist + i:CONV_PAD - hist + i + tc, :] * conv_w[i:i + 1, :]
        xbc = _silu(y)
        xpad_ref[CONV_PAD - hist:CONV_PAD, :] = xpad_ref[CONV_PAD + tc - hist:CONV_PAD + tc, :]

        xs = xbc[:, 0:GROUP_WIDTH]
        bm = xbc[:, GROUP_WIDTH:2 * GROUP_WIDTH]
        cm = xbc[:, 2 * GROUP_WIDTH:3 * GROUP_WIDTH]
        dtr = dt_ref[0, sl, :] + dt_bias
        dt = jnp.maximum(dtr, 0.0) + jnp.log1p(jnp.exp(-jnp.abs(dtr)))
        if l_valid < n_blk * t_blk:
            dt = jnp.where(t * t_blk + r0 + rows < l_valid, dt, 0.0)
        a = dt * a_neg
        u = xs * dt
        acs = _dot01_lhs(tril_b, a)
        acs_t = acs.T
        eacs = jnp.exp(acs)
        last = acs[tc - 1:tc, :]
        elast = jnp.exp(last)
        ub = u.astype(BF16)
        upb = (u * jnp.exp(last - acs)).astype(BF16)
        ys = []
        for g in range(2):
            gs = slice(g * B_STATE, (g + 1) * B_STATE)
            cg = cm[:, gs].astype(BF16)
            bg = bm[:, gs].astype(BF16)
            cb = _dot_nt(cg, bg)
            y_state = _dot(cg, st_ref[:, gs].astype(BF16)) * eacs[:, gs]
            parts = []
            for hh in range(2):
                c0 = (2 * g + hh) * HEAD_DIM
                col = jnp.broadcast_to(acs[:, c0:c0 + 1], (tc, tc))
                rowv = jnp.broadcast_to(acs_t[c0:c0 + 1, :], (tc, tc))
                dec = jnp.exp(jnp.where(tril, col - rowv, NEG))
                parts.append(_dot((cb * dec).astype(BF16), ub[:, gs]))
            ys.append(jnp.where(lane < HEAD_DIM, parts[0], parts[1]) + y_state)
            st_ref[:, gs] = st_ref[:, gs] * elast[:, gs] + _dot_tn(bg, upb[:, gs])
        yv = jnp.concatenate(ys, axis=1) + d_skip * xs
        o_ref[0, sl, :] = _rms(yv * _silu(z_ref[0, sl, :]), nrm)
        return carry

    lax.fori_loop(0, t_blk // tc, chunk, 0)

    @pl.when(t == n_blk - 1)
    def _():
        ssm_out_ref[0] = st_ref[...].T.reshape(HEADS, HEAD_DIM, B_STATE)


def _ssd(proj3, conv0, ssm0, cw, cb, dtb, alog, dvec, nrm, l_valid, t_blk):
    b, lrows, _ = proj3.shape
    n_blk = -(-l_valid // t_blk)
    assert n_blk * t_blk <= lrows and t_blk % SSD_CHUNK == 0
    assert l_valid - (n_blk - 1) * t_blk >= B_CONV - 1
    vec = lambda w: pl.BlockSpec((1, w), lambda bi, t: (0, 0))
    return pl.pallas_call(
        functools.partial(_ssd_kernel, t_blk=t_blk, l_valid=l_valid, n_blk=n_blk),
        out_shape=(jax.ShapeDtypeStruct((b, n_blk * t_blk, GROUP_WIDTH), F32),
                   jax.ShapeDtypeStruct((b, B_CONV - 1, B_CONV_DIM), F32),
                   jax.ShapeDtypeStruct((b, HEADS, HEAD_DIM, B_STATE), F32)),
        grid=(b, n_blk),
        in_specs=[pl.BlockSpec((1, t_blk, B_CONV_DIM), lambda bi, t: (bi, t, COL_XBC // B_CONV_DIM)),
                  pl.BlockSpec((1, t_blk, GROUP_WIDTH), lambda bi, t: (bi, t, COL_BZ // GROUP_WIDTH)),
                  pl.BlockSpec((1, t_blk, GROUP_WIDTH), lambda bi, t: (bi, t, COL_DT // GROUP_WIDTH)),
                  pl.BlockSpec((1, B_CONV - 1, B_CONV_DIM), lambda bi, t: (bi, 0, 0)),
                  pl.BlockSpec((1, HEADS, HEAD_DIM, B_STATE), lambda bi, t: (bi, 0, 0, 0)),
                  pl.BlockSpec((B_CONV, B_CONV_DIM), lambda bi, t: (0, 0)),
                  vec(B_CONV_DIM), vec(GROUP_WIDTH), vec(GROUP_WIDTH), vec(GROUP_WIDTH), vec(GROUP_WIDTH)],
        out_specs=(pl.BlockSpec((1, t_blk, GROUP_WIDTH), lambda bi, t: (bi, t, 0)),
                   pl.BlockSpec((1, B_CONV - 1, B_CONV_DIM), lambda bi, t: (bi, 0, 0)),
                   pl.BlockSpec((1, HEADS, HEAD_DIM, B_STATE), lambda bi, t: (bi, 0, 0, 0))),
        scratch_shapes=[pltpu.VMEM((CONV_PAD + SSD_CHUNK, B_CONV_DIM), F32),
                        pltpu.VMEM((B_STATE, GROUP_WIDTH), F32)],
        compiler_params=_cparams(("parallel", "arbitrary")),
        name="ssd",
    )(proj3, proj3, proj3, conv0, ssm0, cw, cb, dtb, alog, dvec, nrm)


HGRN_INFLIGHT = 8
HGRN_SAFE_LOG2 = 64.0


def _hgrn_kernel(q_ref, f_ref, i_ref, g_ref, st0_ref, lb_ref, nrm_ref, o_ref, st_out_ref, st_ref, b_blk, c_blk,
                 y_scr, *, t_blk, l_valid, n_blk):
    t = pl.program_id(1)

    @pl.when(t == 0)
    def _():
        st_ref[...] = st0_ref[0]

    lb = lb_ref[...]
    nrm = nrm_ref[...]
    head_eq = _head_ones()
    ones_bd = head_eq.astype(BF16)
    tril = _tril(CHUNK).astype(BF16)
    rows = lax.broadcasted_iota(jnp.int32, (CHUNK, GROUP_WIDTH), 0)
    row_in_sub = rows % SUB
    lane_head = lax.broadcasted_iota(jnp.int32, (SUB, GROUP_WIDTH), 1) // HEAD_DIM
    n_sub = CHUNK // SUB
    n_chunks = t_blk // CHUNK
    sc_row = lax.broadcasted_iota(jnp.int32, (HEADS * SUB, CHUNK), 0) % SUB
    sc_col = lax.broadcasted_iota(jnp.int32, (HEADS * SUB, CHUNK), 1)

    def gates(cc, gmax):
        r0 = pl.multiple_of(cc * CHUNK, CHUNK)
        sl = pl.ds(r0, CHUNK)
        f = lb + (1.0 - lb) * jax.nn.sigmoid(f_ref[0, sl, :])
        kk = jnp.maximum(1.0 - f, 0.0)
        lf2 = jnp.log(f) * LOG2E
        if l_valid < n_blk * t_blk:
            ok = t * t_blk + r0 + rows < l_valid
            kk = jnp.where(ok, kk, 0.0)
            lf2 = jnp.where(ok, lf2, 0.0)
        bcs2 = _dot01_lhs(tril, lf2)
        b_blk[sl, :] = bcs2
        c_blk[sl, :] = jnp.log(kk) * LOG2E - bcs2
        prev = jnp.zeros((1, GROUP_WIDTH), F32)
        for a in range(n_sub):
            end = bcs2[(a + 1) * SUB - 1:(a + 1) * SUB, :]
            gmax = jnp.maximum(gmax, prev - end)
            prev = end
        return gmax

    gmax = lax.fori_loop(0, n_chunks, gates, jnp.zeros((1, GROUP_WIDTH), F32),
                         unroll=HGRN_INFLIGHT if n_chunks % HGRN_INFLIGHT == 0 else 1)
    factored_ok = jnp.max(gmax) <= HGRN_SAFE_LOG2

    def stack_heads(x):
        return jnp.concatenate([jnp.where(lane_head == h, x, 0.0) for h in range(HEADS)], axis=0).astype(BF16)

    def unstack_heads(x_all):
        out = jnp.zeros((SUB, GROUP_WIDTH), F32)
        for h in range(HEADS):
            out = out + jnp.where(lane_head == h, x_all[h * SUB:(h + 1) * SUB, :], 0.0)
        return out

    def chunks(ccs, factored):
        n = len(ccs)
        r0s = [pl.multiple_of(cc * CHUNK, CHUNK) for cc in ccs]
        sls = [pl.ds(r0, CHUNK) for r0 in r0s]
        qs = [q_ref[0, sl, :] for sl in sls]
        vs = [i_ref[0, sl, :] for sl in sls]
        bs = [b_blk[sl, :] for sl in sls]
        cs = [c_blk[sl, :] for sl in sls]
        vbs = [v.astype(BF16) for v in vs]

        os_ = []
        for s in range(n):
            os_.append(_dot_nt((qs[s] * jnp.exp2(bs[s])).astype(BF16), st_ref[...].astype(BF16)))
            last = bs[s][CHUNK - 1:CHUNK, :]
            kt = jnp.exp2(last + cs[s]).astype(BF16)
            st_ref[...] = st_ref[...] * jnp.exp2(last) + jnp.where(head_eq, _dot_tn(vbs[s], kt), 0.0)

        pieces = [[] for _ in range(n)]
        for a in range(n_sub):
            s0 = a * SUB
            s_end = s0 + SUB if factored else s0
            if s_end == 0:
                for s in range(n):
                    pieces[s].append(jnp.zeros((SUB, GROUP_WIDTH), F32))
                continue
            scs = []
            for s in range(n):
                ref_row = bs[s][s0 - 1:s0, :] if a > 0 else jnp.zeros((1, GROUP_WIDTH), F32)
                qa = qs[s][s0:s0 + SUB, :] * jnp.exp2(bs[s][s0:s0 + SUB, :] - ref_row)
                ka = jnp.where(rows < s_end, jnp.exp2(ref_row + cs[s]), 0.0)
                sc = _dot_nt(stack_heads(qa), ka.astype(BF16))
                if factored:
                    sc = jnp.where(sc_col <= s0 + sc_row, sc, 0.0)
                scs.append(sc.astype(BF16))
            for s in range(n):
                pieces[s].append(unstack_heads(_dot(scs[s], vbs[s])))
        for s in range(n):
            os_[s] = os_[s] + jnp.concatenate(pieces[s], axis=0)

        if not factored:
            for s in range(n):
                def sub_rows(ref, jj):
                    return jnp.concatenate(
                        [jnp.broadcast_to(ref[pl.ds(r0s[s] + a * SUB + jj, 1), :], (SUB, GROUP_WIDTH))
                         for a in range(n_sub)], axis=0)

                for jj in range(SUB):
                    w = jnp.exp2(bs[s] + sub_rows(c_blk, jj))
                    y_scr[jj * CHUNK:(jj + 1) * CHUNK, :] = jnp.where(row_in_sub >= jj, qs[s] * w, 0.0).astype(BF16)
                att = _dot(y_scr[...], ones_bd)
                for jj in range(SUB):
                    os_[s] = os_[s] + att[jj * CHUNK:(jj + 1) * CHUNK, :] * sub_rows(i_ref.at[0], jj)

        mss = [_dot01_rhs(o * o, ones_bd) * (1.0 / HEAD_DIM) for o in os_]
        for s in range(n):
            o_ref[0, sls[s], :] = os_[s] * lax.rsqrt(mss[s] + EPS) * nrm * _silu(g_ref[0, sls[s], :])

    @pl.when(factored_ok)
    def _():
        inflight = HGRN_INFLIGHT if n_chunks % HGRN_INFLIGHT == 0 else 1

        def group(pp, carry):
            chunks([inflight * pp + s for s in range(inflight)], True)
            return carry

        lax.fori_loop(0, n_chunks // inflight, group, 0)

    @pl.when(jnp.logical_not(factored_ok))
    def _():
        def one(cc, carry):
            chunks([cc], False)
            return carry

        lax.fori_loop(0, n_chunks, one, 0)

    @pl.when(t == n_blk - 1)
    def _():
        st_out_ref[0] = st_ref[...]


def _hgrn(proj3, st0, lb, nrm, l_valid, t_blk):
    b, lrows, _ = proj3.shape
    n_blk = -(-l_valid // t_blk)
    assert n_blk * t_blk <= lrows and t_blk % CHUNK == 0
    col = lambda c: pl.BlockSpec((1, t_blk, GROUP_WIDTH), lambda bi, t: (bi, t, c // GROUP_WIDTH))
    vec = pl.BlockSpec((1, GROUP_WIDTH), lambda bi, t: (0, 0))
    st_spec = pl.BlockSpec((1, GROUP_WIDTH, GROUP_WIDTH), lambda bi, t: (bi, 0, 0))
    return pl.pallas_call(
        functools.partial(_hgrn_kernel, t_blk=t_blk, l_valid=l_valid, n_blk=n_blk),
        out_shape=(jax.ShapeDtypeStruct((b, n_blk * t_blk, GROUP_WIDTH), F32),
                   jax.ShapeDtypeStruct((b, GROUP_WIDTH, GROUP_WIDTH), F32)),
        grid=(b, n_blk),
        in_specs=[col(COL_CQ), col(COL_CF), col(COL_CI), col(COL_CG), st_spec, vec, vec],
        out_specs=(pl.BlockSpec((1, t_blk, GROUP_WIDTH), lambda bi, t: (bi, t, 0)), st_spec),
        scratch_shapes=[pltpu.VMEM((GROUP_WIDTH, GROUP_WIDTH), F32),
                        pltpu.VMEM((t_blk, GROUP_WIDTH), F32),
                        pltpu.VMEM((t_blk, GROUP_WIDTH), F32),
                        pltpu.VMEM((SUB * CHUNK, GROUP_WIDTH), BF16)],
        compiler_params=_cparams(("parallel", "arbitrary")),
        name="hgrn",
    )(proj3, proj3, proj3, proj3, st0, lb, nrm)


POOL_PAD = 16


def _pool_kernel(x_ref, buf_ref, w_ref, sc_ref, o_ref, buf_out_ref, xp_ref, *, t_blk):
    t = pl.program_id(1)

    @pl.when(t == 0)
    def _():
        xp_ref[0:1, :] = jnp.zeros((1, GROUP_WIDTH), F32)
        xp_ref[POOL_PAD - D_BUF:POOL_PAD, :] = buf_ref[0]

    x = x_ref[0]
    xp_ref[POOL_PAD:POOL_PAD + t_blk, :] = x
    acc = x
    sums = {}
    for d in range(1, D_BUF + 1):
        acc = acc + xp_ref[POOL_PAD - d:POOL_PAD - d + t_blk, :]
        sums[d + 1] = acc
    lane = lax.broadcasted_iota(jnp.int32, (t_blk, GROUP_WIDTH), 1)
    pooled = jnp.where(lane < 64, sums[2] * 0.5,
                       jnp.where(lane < 128, sums[4] * 0.25,
                                 jnp.where(lane < 192, sums[8] * 0.125, sums[16] * 0.0625))) - x
    o_ref[0] = _dot(pooled.astype(BF16), w_ref[...]) * sc_ref[...]
    tail = xp_ref[t_blk + 1:t_blk + 1 + D_BUF, :]
    buf_out_ref[0] = tail
    xp_ref[POOL_PAD - D_BUF:POOL_PAD, :] = tail


def _pool(proj3, buf, w_bd, scale, l_valid, t_blk):
    b = proj3.shape[0]
    n_blk = l_valid // t_blk
    assert n_blk * t_blk == l_valid and t_blk >= D_BUF
    buf_spec = pl.BlockSpec((1, D_BUF, GROUP_WIDTH), lambda bi, t: (bi, 0, 0))
    return pl.pallas_call(
        functools.partial(_pool_kernel, t_blk=t_blk),
        out_shape=(jax.ShapeDtypeStruct((b, l_valid, GROUP_WIDTH), F32),
                   jax.ShapeDtypeStruct((b, D_BUF, GROUP_WIDTH), F32)),
        grid=(b, n_blk),
        in_specs=[pl.BlockSpec((1, t_blk, GROUP_WIDTH), lambda bi, t: (bi, t, COL_DX // GROUP_WIDTH)),
                  buf_spec,
                  pl.BlockSpec((GROUP_WIDTH, GROUP_WIDTH), lambda bi, t: (0, 0)),
                  pl.BlockSpec((1, GROUP_WIDTH), lambda bi, t: (0, 0))],
        out_specs=(pl.BlockSpec((1, t_blk, GROUP_WIDTH), lambda bi, t: (bi, t, 0)), buf_spec),
        scratch_shapes=[pltpu.VMEM((POOL_PAD + t_blk, GROUP_WIDTH), F32)],
        compiler_params=_cparams(("parallel", "arbitrary")),
        name="pool",
    )(proj3, buf, w_bd, scale)


SSD_BLOCK = 512
HGRN_BLOCK = 512
POOL_BLOCK = 512


def _prep_layer(layer, norm1, w_in, a_rel_bias, b_conv_w, b_conv_b, b_dt_bias, b_a_log, b_d, b_norm, lbs, c_norm,
                d_pool_w, d_pool_scale, w_out, norm2, w_gate_up, w_down):
    wt = jnp.transpose(w_in, (2, 0, 1))[:, layer, :]
    o_dt = 3 * GROUP_WIDTH + GROUP_WIDTH + B_CONV_DIM
    w_r = _w_in_prep(jnp.concatenate([wt[1024:1792], wt[0:1024], wt[o_dt + HEADS:],
                                      jnp.repeat(wt[o_dt:o_dt + HEADS], HEAD_DIM, axis=0)], axis=0))
    tab = a_rel_bias[layer].T
    m = A_BAND + CHUNK
    far = tab[:, 2 * REL_CLIP:]
    n_far = A_BAND_PREV - REL_CLIP + 1
    r = jnp.concatenate([jnp.broadcast_to(far, (HEADS, n_far)),
                         tab[:, 2 * REL_CLIP - 1:CHUNK:-1],
                         jnp.broadcast_to(far, (HEADS, m - n_far - (2 * REL_CLIP - 1 - CHUNK)))], axis=1)
    bias = jnp.tile(r, (1, CHUNK))[:, :CHUNK * (m - 1)].reshape(HEADS, CHUNK, m - 1)[:, :, :A_BAND]
    bias = bias.reshape(HEADS * CHUNK, A_BAND)
    rep = lambda p: jnp.repeat(p[layer], HEAD_DIM)[None, :]
    pw = d_pool_w[layer]
    w_bd = jnp.zeros((GROUP_WIDTH, GROUP_WIDTH), F32)
    for g in range(4):
        w_bd = w_bd.at[g * 64:(g + 1) * 64, g * 64:(g + 1) * 64].set(pw[g])
    return dict(
        n1=norm1[layer][None, :], w_in=w_r, bias=bias,
        cw=b_conv_w[layer], cb=b_conv_b[layer][None, :], dtb=rep(b_dt_bias), alog=rep(b_a_log), dvec=rep(b_d),
        bnrm=b_norm[layer][None, :], lb=lbs[layer][None, :], cnrm=c_norm[layer].reshape(1, GROUP_WIDTH),
        w_bd=w_bd.astype(BF16), psc=d_pool_scale[layer][None, :], n2=norm2[layer][None, :], layer=layer,
        w_out=w_out.astype(BF16), w_gu=w_gate_up.astype(BF16), w_dn=w_down.astype(BF16))


def _round_up(n, m):
    return -(-n // m) * m


def _group_layer(x, st, p, nf, final):
    k_prev, v_prev, conv0, ssm0, hgrn0, pool0 = st
    b, l, _ = x.shape
    p_hist = k_prev.shape[1]
    proj = _inproj(x.reshape(b * l, D_MODEL), p["n1"], p["w_in"]).reshape(b, l, PROJ_W)
    lpad = _round_up(l, SSD_CHUNK)
    proj_p = proj if lpad == l else jnp.pad(proj, ((0, 0), (0, lpad - l), (0, 0)))

    keep = min(A_BAND_PREV, p_hist + l)
    n_new = min(l, keep)
    ka = proj[:, l - n_new:, COL_AK:COL_AK + GROUP_WIDTH].reshape(b, n_new, HEADS, HEAD_DIM)
    va = proj[:, l - n_new:, COL_AV:COL_AV + GROUP_WIDTH].reshape(b, n_new, HEADS, HEAD_DIM)
    if p_hist == 0:
        k_cache = v_cache = None
    else:
        k_cache = k_prev.astype(BF16).reshape(b, p_hist, GROUP_WIDTH)
        v_cache = v_prev.astype(BF16).reshape(b, p_hist, GROUP_WIDTH)
    oa = _attention(proj_p, k_cache, v_cache, p["bias"], l)[:, :l]
    new_k = jnp.concatenate([k_prev[:, p_hist - (keep - n_new):], ka], axis=1)
    new_v = jnp.concatenate([v_prev[:, p_hist - (keep - n_new):], va], axis=1)

    ob, new_conv, new_ssm = _ssd(proj_p, conv0, ssm0, p["cw"], p["cb"], p["dtb"], p["alog"], p["dvec"], p["bnrm"],
                                 l, SSD_BLOCK if l % SSD_BLOCK == 0 else lpad)
    ob = ob[:, :l]

    eye = jnp.eye(HEADS, dtype=F32)
    st_bd = jnp.einsum("bhkv,hg->bhvgk", hgrn0, eye).reshape(b, GROUP_WIDTH, GROUP_WIDTH)
    oc, st_out = _hgrn(proj_p, st_bd, p["lb"], p["cnrm"], l,
                       HGRN_BLOCK if l % HGRN_BLOCK == 0 else _round_up(l, CHUNK))
    oc = oc[:, :l]
    st_out = st_out.reshape(b, HEADS, HEAD_DIM, HEADS, HEAD_DIM)
    new_hgrn = jnp.stack([st_out[:, h, :, h, :] for h in range(HEADS)], axis=1).transpose(0, 1, 3, 2)

    od, new_pool = _pool(proj_p, pool0, p["w_bd"], p["psc"], l, POOL_BLOCK if l % POOL_BLOCK == 0 else l)

    flat = lambda a: a.reshape(b * l, GROUP_WIDTH)
    y = _mlp(x.reshape(b * l, D_MODEL), (flat(oa), flat(ob), flat(oc), flat(od)), p["layer"],
             p["w_out"], p["n2"], p["w_gu"], p["w_dn"], nf, final)
    return y.reshape(b, l, D_MODEL), (new_k, new_v, new_conv, new_ssm, new_hgrn, new_pool)


def _trunk(x, states, params, nf):
    depth = len(params)
    new = [[] for _ in range(6)]
    for layer in range(depth):
        st_l = tuple(s[layer] for s in states)
        x, ns = _group_layer(x, st_l, params[layer], nf, layer == depth - 1)
        for i in range(6):
            new[i].append(ns[i])
    return x, tuple(jnp.stack(n, axis=0) for n in new)


def kernel(x_prompt, x_sample, cache_a_k, cache_a_v, state_b_conv, state_b_ssm, state_c_hgrn, state_d_pool, norm1,
           w_in, a_rel_bias, b_conv_w, b_conv_b, b_dt_bias, b_a_log, b_d, b_norm, c_lb_logits, c_norm, d_pool_w,
           d_pool_scale, w_out, norm2, w_gate_up, w_down, norm_f):
    depth = w_in.shape[0]
    lbs = jnp.cumsum(jax.nn.softmax(c_lb_logits.astype(F32), axis=0), axis=0)
    lbs = lbs - lbs[:1]
    params = [_prep_layer(layer, norm1, w_in, a_rel_bias, b_conv_w, b_conv_b, b_dt_bias, b_a_log, b_d, b_norm, lbs,
                          c_norm, d_pool_w, d_pool_scale, w_out, norm2, w_gate_up, w_down)
              for layer in range(depth)]
    nf = norm_f[None, :]
    bp = x_prompt.shape[0]
    prompt_states = (
        jnp.zeros((depth, bp, 0, HEADS, HEAD_DIM), F32),
        jnp.zeros((depth, bp, 0, HEADS, HEAD_DIM), F32),
        jnp.zeros((depth, bp, B_CONV - 1, B_CONV_DIM), F32),
        jnp.zeros((depth, bp, HEADS, HEAD_DIM, B_STATE), F32),
        jnp.zeros((depth, bp, HEADS, HEAD_DIM, HEAD_DIM), F32),
        jnp.zeros((depth, bp, D_BUF, GROUP_WIDTH), F32),
    )
    y_prompt, ps = _trunk(x_prompt, prompt_states, params, nf)
    sample_states = (cache_a_k, cache_a_v, state_b_conv, state_b_ssm, state_c_hgrn, state_d_pool)
    y_sample, ss = _trunk(x_sample, sample_states, params, nf)
    return (y_prompt, y_sample) + ps + ss
```

```python
import functools

import jax
import jax.numpy as jnp
from jax import lax
from jax.experimental import pallas as pl
from jax.experimental.pallas import tpu as pltpu

F32 = jnp.float32
BF16 = jnp.bfloat16

D_MODEL = 1024
GROUP_WIDTH = 256
HEADS = 4
HEAD_DIM = 64
CHUNK = 64
A_BAND_PREV = 512
A_BAND = A_BAND_PREV + CHUNK
REL_CLIP = 128
B_STATE = 128
B_CONV = 4
B_CONV_DIM = 768
D_BUF = 15
D_FF = 2816
EPS = 1e-6
NEG = -1e30
LOG2E = 1.4426950408889634
SUB = 16

COL_XBC = 0
COL_AQ = 768
COL_AK = 1024
COL_AV = 1280
COL_BZ = 1536
COL_CQ = 1792
COL_CF = 2048
COL_CI = 2304
COL_CG = 2560
COL_DX = 2816
COL_DT = 3072
PROJ_W = 3328

VMEM_LIMIT = 56 * 1024 * 1024


def _cparams(sem):
    return pltpu.CompilerParams(dimension_semantics=sem, vmem_limit_bytes=VMEM_LIMIT)


def _rms(x, w):
    return x * lax.rsqrt(jnp.mean(x * x, axis=-1, keepdims=True) + EPS) * w


def _silu(x):
    return x * jax.nn.sigmoid(x)


def _dot(a, b):
    return jnp.dot(a, b, preferred_element_type=F32)


def _dot_nt(a, b):
    return lax.dot_general(a, b, (((1,), (1,)), ((), ())), preferred_element_type=F32)


def _dot_tn(a, b):
    return lax.dot_general(a, b, (((0,), (0,)), ((), ())), preferred_element_type=F32)


def _split3(x):
    hi = x.astype(BF16)
    r = x - hi.astype(F32)
    mid = r.astype(BF16)
    lo = (r - mid.astype(F32)).astype(BF16)
    return hi, mid, lo


def _dot01_lhs(m01, x):
    hi, mid, lo = _split3(x)
    return _dot(m01, hi) + _dot(m01, mid) + _dot(m01, lo)


def _dot01_rhs(x, m01):
    hi, mid, lo = _split3(x)
    return _dot(hi, m01) + _dot(mid, m01) + _dot(lo, m01)


def _tril(t):
    r = lax.broadcasted_iota(jnp.int32, (t, t), 0)
    c = lax.broadcasted_iota(jnp.int32, (t, t), 1)
    return r >= c


def _head_ones():
    r = lax.broadcasted_iota(jnp.int32, (GROUP_WIDTH, GROUP_WIDTH), 0) // HEAD_DIM
    c = lax.broadcasted_iota(jnp.int32, (GROUP_WIDTH, GROUP_WIDTH), 1) // HEAD_DIM
    return r == c


IN_WIDTH = 3076
ORIG_DT = 1792
W_PREP_COLS = 256


def _w_in_prep_kernel(wt_ref, o_ref):
    def put(c0, rows):
        o_ref[:, c0:c0 + W_PREP_COLS] = rows.T.astype(BF16)

    for j in range(3):
        put(COL_XBC + W_PREP_COLS * j, wt_ref[0, 1024 + W_PREP_COLS * j:1024 + W_PREP_COLS * (j + 1), :])
    for j in range(4):
        put(COL_AQ + W_PREP_COLS * j, wt_ref[0, W_PREP_COLS * j:W_PREP_COLS * (j + 1), :])
    for j in range(5):
        r0 = ORIG_DT + HEADS + W_PREP_COLS * j
        put(COL_CQ + W_PREP_COLS * j, wt_ref[0, r0:r0 + W_PREP_COLS, :])
    put(COL_DT, jnp.concatenate([jnp.broadcast_to(wt_ref[0, ORIG_DT + h:ORIG_DT + h + 1, :], (HEAD_DIM, D_MODEL))
                                 for h in range(HEADS)], axis=0))


def _w_in_prep(wt_all, layer):
    return pl.pallas_call(
        _w_in_prep_kernel,
        out_shape=jax.ShapeDtypeStruct((D_MODEL, PROJ_W), BF16),
        grid=(1,),
        in_specs=[pl.BlockSpec((1, IN_WIDTH, D_MODEL), lambda i: (layer, 0, 0), pipeline_mode=pl.Buffered(1))],
        out_specs=pl.BlockSpec((D_MODEL, PROJ_W), lambda i: (0, 0)),
        compiler_params=_cparams(("arbitrary",)),
        name="w_in_prep",
    )(wt_all)


def _inproj_kernel(x_ref, n_ref, w_ref, o_ref):
    hb = _rms(x_ref[...], n_ref[...]).astype(BF16)
    for c0 in range(0, PROJ_W, 256):
        o_ref[:, c0:c0 + 256] = _dot(hb, w_ref[:, c0:c0 + 256])


def _inproj(x2d, n1, w_in):
    rows = x2d.shape[0]
    tm = 512 if rows % 512 == 0 else rows
    return pl.pallas_call(
        _inproj_kernel,
        out_shape=jax.ShapeDtypeStruct((rows, PROJ_W), F32),
        grid=(rows // tm,),
        in_specs=[pl.BlockSpec((tm, D_MODEL), lambda i: (i, 0)),
                  pl.BlockSpec((1, D_MODEL), lambda i: (0, 0)),
                  pl.BlockSpec((D_MODEL, PROJ_W), lambda i: (0, 0), pipeline_mode=pl.Buffered(1))],
        out_specs=pl.BlockSpec((tm, PROJ_W), lambda i: (i, 0)),
        compiler_params=_cparams(("parallel",)),
        name="inproj",
    )(x2d, n1, w_in)


FF_CHUNK = 256


def _mlp_kernel(x_ref, oa_ref, ob_ref, oc_ref, od_ref, wo_ref, n2_ref, wgu_ref, wdn_ref, nf_ref, o_ref, mix_ref,
                act_ref, *, final):
    for m, r in enumerate((oa_ref, ob_ref, oc_ref, od_ref)):
        mix_ref[:, m * GROUP_WIDTH:(m + 1) * GROUP_WIDTH] = r[...].astype(BF16)
    x1 = x_ref[...] + _dot(mix_ref[...], wo_ref[0])
    hb = _rms(x1, n2_ref[...]).astype(BF16)
    for c0 in range(0, D_FF, FF_CHUNK):
        gate = _dot(hb, wgu_ref[0, :, c0:c0 + FF_CHUNK])
        up = _dot(hb, wgu_ref[0, :, D_FF + c0:D_FF + c0 + FF_CHUNK])
        act_ref[:, c0:c0 + FF_CHUNK] = (_silu(gate) * up).astype(BF16)
    out = x1 + _dot(act_ref[...], wdn_ref[0])
    if final:
        out = _rms(out, nf_ref[...])
    o_ref[...] = out


def _mlp(x2d, mix, layer, w_out, n2, w_gu, w_dn, nf, final):
    rows = x2d.shape[0]
    tm = 512 if rows % 512 == 0 else rows
    row_spec = lambda w: pl.BlockSpec((tm, w), lambda i: (i, 0))
    vec = lambda a: pl.BlockSpec(a.shape, lambda i: (0, 0))
    wspec = lambda a: pl.BlockSpec((1,) + a.shape[1:], lambda i: (layer, 0, 0), pipeline_mode=pl.Buffered(1))
    return pl.pallas_call(
        functools.partial(_mlp_kernel, final=final),
        out_shape=jax.ShapeDtypeStruct((rows, D_MODEL), F32),
        grid=(rows // tm,),
        in_specs=[row_spec(D_MODEL)] + [row_spec(GROUP_WIDTH)] * 4
                 + [wspec(w_out), vec(n2), wspec(w_gu), wspec(w_dn), vec(nf)],
        out_specs=row_spec(D_MODEL),
        scratch_shapes=[pltpu.VMEM((tm, D_MODEL), BF16), pltpu.VMEM((tm, D_FF), BF16)],
        compiler_params=_cparams(("parallel",)),
        name="mlp",
    )(x2d, *mix, w_out, n2, w_gu, w_dn, nf)


ATTN_UNROLL = 4


def _attn_kernel(q_ref, kh_ref, kc_ref, vh_ref, vc_ref, bias_ref, o_ref, kb_ref, vb_ref, *, qb, p_hist, l_valid):
    i = pl.program_id(1)
    if p_hist == 0:
        @pl.when(i == 0)
        def _():
            kb_ref[0:A_BAND_PREV, :] = jnp.zeros((A_BAND_PREV, GROUP_WIDTH), BF16)
            vb_ref[0:A_BAND_PREV, :] = jnp.zeros((A_BAND_PREV, GROUP_WIDTH), BF16)

        @pl.when(i > 0)
        def _():
            kb_ref[0:A_BAND_PREV, :] = kh_ref[0].astype(BF16)
            vb_ref[0:A_BAND_PREV, :] = vh_ref[0].astype(BF16)
    else:
        kb_ref[0:A_BAND_PREV, :] = kh_ref[0].astype(BF16)
        vb_ref[0:A_BAND_PREV, :] = vh_ref[0].astype(BF16)
    kb_ref[A_BAND_PREV:A_BAND_PREV + qb, :] = kc_ref[0].astype(BF16)
    vb_ref[A_BAND_PREV:A_BAND_PREV + qb, :] = vc_ref[0].astype(BF16)
    lane_head = lax.broadcasted_iota(jnp.int32, (CHUNK, GROUP_WIDTH), 1) // HEAD_DIM
    band_pos = lax.broadcasted_iota(jnp.int32, (1, A_BAND), 1)

    def chunk(cc, carry):
        r0 = pl.multiple_of(cc * CHUNK, CHUNK)
        qc = q_ref[0, pl.ds(r0, CHUNK), :] * (HEAD_DIM ** -0.5)
        qs = jnp.concatenate([jnp.where(lane_head == h, qc, 0.0) for h in range(HEADS)], axis=0).astype(BF16)
        s = _dot_nt(qs, kb_ref[pl.ds(r0, A_BAND), :]) + bias_ref[...]
        kpos = i * qb + r0 - A_BAND_PREV + band_pos
        valid = (kpos >= -p_hist) & (kpos < l_valid)
        s = jnp.where(valid, s, NEG)
        m = jnp.max(s, axis=-1, keepdims=True)
        e = jnp.exp(s - m)
        den = jnp.sum(e, axis=-1, keepdims=True)
        o_all = _dot(e.astype(BF16), vb_ref[pl.ds(r0, A_BAND), :]) * (1.0 / den)
        o = jnp.zeros((CHUNK, GROUP_WIDTH), F32)
        for h in range(HEADS):
            o = o + jnp.where(lane_head == h, o_all[h * CHUNK:(h + 1) * CHUNK, :], 0.0)
        o_ref[0, pl.ds(r0, CHUNK), :] = o
        return carry

    n_chunks = qb // CHUNK
    lax.fori_loop(0, n_chunks, chunk, 0, unroll=ATTN_UNROLL if n_chunks % ATTN_UNROLL == 0 else 1)


def _attention(proj3, k_cache, v_cache, bias, l_valid):
    b = proj3.shape[0]
    lq = _round_up(l_valid, CHUNK)
    qb = A_BAND_PREV if lq % A_BAND_PREV == 0 else lq
    nblk = lq // qb
    col = lambda c: pl.BlockSpec((1, qb, GROUP_WIDTH), lambda bi, i: (bi, i, c // GROUP_WIDTH))
    if k_cache is None:
        assert qb == A_BAND_PREV
        p_hist = 0
        prev = lambda c: pl.BlockSpec((1, qb, GROUP_WIDTH), lambda bi, i: (bi, jnp.maximum(i - 1, 0), c // GROUP_WIDTH))
        kh, vh, kh_spec, vh_spec = proj3, proj3, prev(COL_AK), prev(COL_AV)
    else:
        assert nblk == 1 and k_cache.shape[1] == A_BAND_PREV
        p_hist = A_BAND_PREV
        kh_spec = vh_spec = pl.BlockSpec((1, A_BAND_PREV, GROUP_WIDTH), lambda bi, i: (bi, 0, 0))
        kh, vh = k_cache, v_cache
    return pl.pallas_call(
        functools.partial(_attn_kernel, qb=qb, p_hist=p_hist, l_valid=l_valid),
        out_shape=jax.ShapeDtypeStruct((b, lq, GROUP_WIDTH), F32),
        grid=(b, nblk),
        in_specs=[col(COL_AQ), kh_spec, col(COL_AK), vh_spec, col(COL_AV),
                  pl.BlockSpec((HEADS * CHUNK, A_BAND), lambda bi, i: (0, 0))],
        out_specs=pl.BlockSpec((1, qb, GROUP_WIDTH), lambda bi, i: (bi, i, 0)),
        scratch_shapes=[pltpu.VMEM((A_BAND_PREV + qb, GROUP_WIDTH), BF16),
                        pltpu.VMEM((A_BAND_PREV + qb, GROUP_WIDTH), BF16)],
        compiler_params=_cparams(("parallel", "arbitrary")),
        name="attention",
    )(proj3, kh, proj3, vh, proj3, bias)


def _attn_cached_kernel(q_ref, kn_ref, vn_ref, kt_ref, vt_ref, bias_ref, o_ref, kt_out_ref, vt_out_ref, *, l_new):
    lane_head = lax.broadcasted_iota(jnp.int32, (l_new, GROUP_WIDTH), 1) // HEAD_DIM
    q = q_ref[0] * (HEAD_DIM ** -0.5)
    qs = jnp.concatenate([jnp.where(lane_head == h, q, 0.0) for h in range(HEADS)], axis=0).astype(BF16)
    kn = kn_ref[0]
    vn = vn_ref[0]
    kt = kt_ref[0, 0]
    vt = vt_ref[0, 0]
    s_old = _dot(qs, kt.astype(BF16)) + bias_ref[:, 0:A_BAND_PREV]
    s_new = _dot_nt(qs, kn.astype(BF16)) + bias_ref[:, A_BAND_PREV:A_BAND_PREV + l_new]
    m = jnp.maximum(jnp.max(s_old, axis=-1, keepdims=True), jnp.max(s_new, axis=-1, keepdims=True))
    e_old = jnp.exp(s_old - m)
    e_new = jnp.exp(s_new - m)
    den = jnp.sum(e_old, axis=-1, keepdims=True) + jnp.sum(e_new, axis=-1, keepdims=True)
    o_all = (_dot_nt(e_old.astype(BF16), vt.astype(BF16)) + _dot(e_new.astype(BF16), vn.astype(BF16))) * (1.0 / den)
    o = jnp.zeros((l_new, GROUP_WIDTH), F32)
    for h in range(HEADS):
        o = o + jnp.where(lane_head == h, o_all[h * l_new:(h + 1) * l_new, :], 0.0)
    o_ref[0] = o

    keep = A_BAND_PREV - l_new
    lane = lax.broadcasted_iota(jnp.int32, (GROUP_WIDTH, A_BAND_PREV), 1)
    r = lax.broadcasted_iota(jnp.int32, (l_new, A_BAND_PREV), 0)
    c = lax.broadcasted_iota(jnp.int32, (l_new, A_BAND_PREV), 1)
    place = (c == r + keep).astype(BF16)

    def appended(old_t, new_rows):
        hi, mid, lo = _split3(new_rows)
        new_cols = _dot_tn(hi, place) + _dot_tn(mid, place) + _dot_tn(lo, place)
        return jnp.where(lane < keep, pltpu.roll(old_t, keep, 1), new_cols)

    kt_out_ref[0] = appended(kt, kn)
    vt_out_ref[0] = appended(vt, vn)


def _attention_cached(proj3, kt_cache, vt_cache, layer, bias, l_valid):
    b = proj3.shape[0]
    col = lambda c: pl.BlockSpec((1, l_valid, GROUP_WIDTH), lambda bi: (bi, 0, c // GROUP_WIDTH))
    cache_spec = pl.BlockSpec((1, 1, GROUP_WIDTH, A_BAND_PREV), lambda bi: (layer, bi, 0, 0))
    out_t = pl.BlockSpec((1, GROUP_WIDTH, A_BAND_PREV), lambda bi: (bi, 0, 0))
    return pl.pallas_call(
        functools.partial(_attn_cached_kernel, l_new=l_valid),
        out_shape=(jax.ShapeDtypeStruct((b, l_valid, GROUP_WIDTH), F32),
                   jax.ShapeDtypeStruct((b, GROUP_WIDTH, A_BAND_PREV), F32),
                   jax.ShapeDtypeStruct((b, GROUP_WIDTH, A_BAND_PREV), F32)),
        grid=(b,),
        in_specs=[col(COL_AQ), col(COL_AK), col(COL_AV), cache_spec, cache_spec,
                  pl.BlockSpec((HEADS * l_valid, A_BAND), lambda bi: (0, 0))],
        out_specs=(pl.BlockSpec((1, l_valid, GROUP_WIDTH), lambda bi: (bi, 0, 0)), out_t, out_t),
        compiler_params=_cparams(("parallel",)),
        name="attention_cached",
    )(proj3, proj3, proj3, kt_cache, vt_cache, bias)


CONV_PAD = 8
SSD_CHUNK = 128


def _ssd_kernel(xbc_ref, z_ref, dt_ref, conv0_ref, ssm0_ref, cw_ref, cb_ref, dtb_ref, alog_ref, d_ref, nrm_ref,
                o_ref, conv_out_ref, ssm_out_ref, xpad_ref, st_ref, *, t_blk, l_valid, n_blk):
    t = pl.program_id(1)
    hist = B_CONV - 1

    @pl.when(t == 0)
    def _():
        xpad_ref[CONV_PAD - hist:CONV_PAD, :] = conv0_ref[0]
        st_ref[...] = ssm0_ref[0].reshape(GROUP_WIDTH, B_STATE).T

    n_last = l_valid - (n_blk - 1) * t_blk
    conv_out_ref[0] = xbc_ref[0, n_last - hist:n_last, :]

    tc = SSD_CHUNK
    short = l_valid < t_blk

    def rows_of(ref, sl):
        if short:
            return jnp.concatenate([ref[0], jnp.zeros((tc - l_valid, ref.shape[2]), F32)], axis=0)
        return ref[0, sl, :]

    tril = _tril(tc)
    tril_b = tril.astype(BF16)
    lane = lax.broadcasted_iota(jnp.int32, (tc, B_STATE), 1)
    rows = lax.broadcasted_iota(jnp.int32, (tc, GROUP_WIDTH), 0)
    a_neg = -jnp.exp(alog_ref[...])
    conv_b = cb_ref[...]
    conv_w = cw_ref[...]
    dt_bias = dtb_ref[...]
    d_skip = d_ref[...]
    nrm = nrm_ref[...]

    def chunk(j, carry):
        r0 = pl.multiple_of(j * tc, tc)
        sl = pl.ds(r0, tc)
        xpad_ref[CONV_PAD:CONV_PAD + tc, :] = rows_of(xbc_ref, sl)
        y = conv_b
        for i in range(B_CONV):
            y = y + xpad_ref[CONV_PAD - hist + i:CONV_PAD - hist + i + tc, :] * conv_w[i:i + 1, :]
        xbc = _silu(y)
        xpad_ref[CONV_PAD - hist:CONV_PAD, :] = xpad_ref[CONV_PAD + tc - hist:CONV_PAD + tc, :]

        xs = xbc[:, 0:GROUP_WIDTH]
        bm = xbc[:, GROUP_WIDTH:2 * GROUP_WIDTH]
        cm = xbc[:, 2 * GROUP_WIDTH:3 * GROUP_WIDTH]
        dtr = rows_of(dt_ref, sl) + dt_bias
        dt = jnp.maximum(dtr, 0.0) + jnp.log1p(jnp.exp(-jnp.abs(dtr)))
        if l_valid < n_blk * t_blk:
            dt = jnp.where(t * t_blk + r0 + rows < l_valid, dt, 0.0)
        a = dt * a_neg
        u = xs * dt
        acs = _dot01_lhs(tril_b, a)
        acs_t = acs.T
        eacs = jnp.exp(acs)
        last = acs[tc - 1:tc, :]
        elast = jnp.exp(last)
        ub = u.astype(BF16)
        upb = (u * jnp.exp(last - acs)).astype(BF16)
        ys = []
        for g in range(2):
            gs = slice(g * B_STATE, (g + 1) * B_STATE)
            cg = cm[:, gs].astype(BF16)
            bg = bm[:, gs].astype(BF16)
            cb = _dot_nt(cg, bg)
            y_state = _dot(cg, st_ref[:, gs].astype(BF16)) * eacs[:, gs]
            parts = []
            for hh in range(2):
                c0 = (2 * g + hh) * HEAD_DIM
                col = jnp.broadcast_to(acs[:, c0:c0 + 1], (tc, tc))
                rowv = jnp.broadcast_to(acs_t[c0:c0 + 1, :], (tc, tc))
                dec = jnp.exp(jnp.where(tril, col - rowv, NEG))
                parts.append(_dot((cb * dec).astype(BF16), ub[:, gs]))
            ys.append(jnp.where(lane < HEAD_DIM, parts[0], parts[1]) + y_state)
            st_ref[:, gs] = st_ref[:, gs] * elast[:, gs] + _dot_tn(bg, upb[:, gs])
        yv = jnp.concatenate(ys, axis=1) + d_skip * xs
        res = _rms(yv * _silu(rows_of(z_ref, sl)), nrm)
        if short:
            o_ref[0] = res[0:l_valid, :]
        else:
            o_ref[0, sl, :] = res
        return carry

    lax.fori_loop(0, t_blk // tc, chunk, 0)

    @pl.when(t == n_blk - 1)
    def _():
        ssm_out_ref[0] = st_ref[...].T.reshape(HEADS, HEAD_DIM, B_STATE)


def _ssd(proj3, conv0, ssm0, cw, cb, dtb, alog, dvec, nrm, l_valid, t_blk):
    b, lrows, _ = proj3.shape
    n_blk = -(-l_valid // t_blk)
    assert t_blk % SSD_CHUNK == 0 and lrows == l_valid
    assert l_valid == n_blk * t_blk or (t_blk == SSD_CHUNK and B_CONV - 1 <= l_valid < t_blk and l_valid % 8 == 0)
    t_in = min(t_blk, l_valid)
    vec = lambda w: pl.BlockSpec((1, w), lambda bi, t: (0, 0))
    return pl.pallas_call(
        functools.partial(_ssd_kernel, t_blk=t_blk, l_valid=l_valid, n_blk=n_blk),
        out_shape=(jax.ShapeDtypeStruct((b, l_valid, GROUP_WIDTH), F32),
                   jax.ShapeDtypeStruct((b, B_CONV - 1, B_CONV_DIM), F32),
                   jax.ShapeDtypeStruct((b, HEADS, HEAD_DIM, B_STATE), F32)),
        grid=(b, n_blk),
        in_specs=[pl.BlockSpec((1, t_in, B_CONV_DIM), lambda bi, t: (bi, t, COL_XBC // B_CONV_DIM)),
                  pl.BlockSpec((1, t_in, GROUP_WIDTH), lambda bi, t: (bi, t, COL_BZ // GROUP_WIDTH)),
                  pl.BlockSpec((1, t_in, GROUP_WIDTH), lambda bi, t: (bi, t, COL_DT // GROUP_WIDTH)),
                  pl.BlockSpec((1, B_CONV - 1, B_CONV_DIM), lambda bi, t: (bi, 0, 0)),
                  pl.BlockSpec((1, HEADS, HEAD_DIM, B_STATE), lambda bi, t: (bi, 0, 0, 0)),
                  pl.BlockSpec((B_CONV, B_CONV_DIM), lambda bi, t: (0, 0)),
                  vec(B_CONV_DIM), vec(GROUP_WIDTH), vec(GROUP_WIDTH), vec(GROUP_WIDTH), vec(GROUP_WIDTH)],
        out_specs=(pl.BlockSpec((1, t_in, GROUP_WIDTH), lambda bi, t: (bi, t, 0)),
                   pl.BlockSpec((1, B_CONV - 1, B_CONV_DIM), lambda bi, t: (bi, 0, 0)),
                   pl.BlockSpec((1, HEADS, HEAD_DIM, B_STATE), lambda bi, t: (bi, 0, 0, 0))),
        scratch_shapes=[pltpu.VMEM((CONV_PAD + SSD_CHUNK, B_CONV_DIM), F32),
                        pltpu.VMEM((B_STATE, GROUP_WIDTH), F32)],
        compiler_params=_cparams(("parallel", "arbitrary")),
        name="ssd",
    )(proj3, proj3, proj3, conv0, ssm0, cw, cb, dtb, alog, dvec, nrm)


HGRN_INFLIGHT = 8
HGRN_SAFE_LOG2 = 64.0


def _hgrn_kernel(q_ref, f_ref, i_ref, g_ref, st0_ref, lb_ref, nrm_ref, o_ref, st_out_ref, st_ref, b_blk, c_blk,
                 y_scr, *, t_blk, l_valid, n_blk):
    t = pl.program_id(1)

    @pl.when(t == 0)
    def _():
        for h in range(HEADS):
            parts = [jnp.zeros((HEAD_DIM, HEAD_DIM * h), F32)] if h > 0 else []
            parts.append(st0_ref[0, h].T)
            if h < HEADS - 1:
                parts.append(jnp.zeros((HEAD_DIM, HEAD_DIM * (HEADS - 1 - h)), F32))
            st_ref[h * HEAD_DIM:(h + 1) * HEAD_DIM, :] = jnp.concatenate(parts, axis=1)

    lb = lb_ref[...]
    nrm = nrm_ref[...]
    head_eq = _head_ones()
    ones_bd = head_eq.astype(BF16)
    tril = _tril(CHUNK).astype(BF16)
    rows = lax.broadcasted_iota(jnp.int32, (CHUNK, GROUP_WIDTH), 0)
    row_in_sub = rows % SUB
    lane_head = lax.broadcasted_iota(jnp.int32, (SUB, GROUP_WIDTH), 1) // HEAD_DIM
    n_sub = CHUNK // SUB
    n_chunks = t_blk // CHUNK
    sc_row = lax.broadcasted_iota(jnp.int32, (HEADS * SUB, CHUNK), 0) % SUB
    sc_col = lax.broadcasted_iota(jnp.int32, (HEADS * SUB, CHUNK), 1)

    def gates(cc, gmax):
        r0 = pl.multiple_of(cc * CHUNK, CHUNK)
        sl = pl.ds(r0, CHUNK)
        f = lb + (1.0 - lb) * jax.nn.sigmoid(f_ref[0, sl, :])
        kk = jnp.maximum(1.0 - f, 0.0)
        lf2 = jnp.log(f) * LOG2E
        if l_valid < n_blk * t_blk:
            ok = t * t_blk + r0 + rows < l_valid
            kk = jnp.where(ok, kk, 0.0)
            lf2 = jnp.where(ok, lf2, 0.0)
        bcs2 = _dot01_lhs(tril, lf2)
        b_blk[sl, :] = bcs2
        c_blk[sl, :] = jnp.log(kk) * LOG2E - bcs2
        prev = jnp.zeros((1, GROUP_WIDTH), F32)
        for a in range(n_sub):
            end = bcs2[(a + 1) * SUB - 1:(a + 1) * SUB, :]
            gmax = jnp.maximum(gmax, prev - end)
            prev = end
        return gmax

    gmax = lax.fori_loop(0, n_chunks, gates, jnp.zeros((1, GROUP_WIDTH), F32),
                         unroll=HGRN_INFLIGHT if n_chunks % HGRN_INFLIGHT == 0 else 1)
    factored_ok = jnp.max(gmax) <= HGRN_SAFE_LOG2

    def stack_heads(x):
        return jnp.concatenate([jnp.where(lane_head == h, x, 0.0) for h in range(HEADS)], axis=0).astype(BF16)

    def unstack_heads(x_all):
        out = jnp.zeros((SUB, GROUP_WIDTH), F32)
        for h in range(HEADS):
            out = out + jnp.where(lane_head == h, x_all[h * SUB:(h + 1) * SUB, :], 0.0)
        return out

    def chunks(ccs, factored):
        n = len(ccs)
        r0s = [pl.multiple_of(cc * CHUNK, CHUNK) for cc in ccs]
        sls = [pl.ds(r0, CHUNK) for r0 in r0s]
        qs = [q_ref[0, sl, :] for sl in sls]
        vs = [i_ref[0, sl, :] for sl in sls]
        bs = [b_blk[sl, :] for sl in sls]
        cs = [c_blk[sl, :] for sl in sls]
        vbs = [v.astype(BF16) for v in vs]

        os_ = []
        for s in range(n):
            os_.append(_dot_nt((qs[s] * jnp.exp2(bs[s])).astype(BF16), st_ref[...].astype(BF16)))
            last = bs[s][CHUNK - 1:CHUNK, :]
            kt = jnp.exp2(last + cs[s]).astype(BF16)
            st_ref[...] = st_ref[...] * jnp.exp2(last) + jnp.where(head_eq, _dot_tn(vbs[s], kt), 0.0)

        pieces = [[] for _ in range(n)]
        for a in range(n_sub):
            s0 = a * SUB
            s_end = s0 + SUB if factored else s0
            if s_end == 0:
                for s in range(n):
                    pieces[s].append(jnp.zeros((SUB, GROUP_WIDTH), F32))
                continue
            scs = []
            for s in range(n):
                ref_row = bs[s][s0 - 1:s0, :] if a > 0 else jnp.zeros((1, GROUP_WIDTH), F32)
                qa = qs[s][s0:s0 + SUB, :] * jnp.exp2(bs[s][s0:s0 + SUB, :] - ref_row)
                ka = jnp.where(rows < s_end, jnp.exp2(ref_row + cs[s]), 0.0)
                sc = _dot_nt(stack_heads(qa), ka.astype(BF16))
                if factored:
                    sc = jnp.where(sc_col <= s0 + sc_row, sc, 0.0)
                scs.append(sc.astype(BF16))
            for s in range(n):
                pieces[s].append(unstack_heads(_dot(scs[s], vbs[s])))
        for s in range(n):
            os_[s] = os_[s] + jnp.concatenate(pieces[s], axis=0)

        if not factored:
            for s in range(n):
                def sub_rows(ref, jj):
                    return jnp.concatenate(
                        [jnp.broadcast_to(ref[pl.ds(r0s[s] + a * SUB + jj, 1), :], (SUB, GROUP_WIDTH))
                         for a in range(n_sub)], axis=0)

                for jj in range(SUB):
                    w = jnp.exp2(bs[s] + sub_rows(c_blk, jj))
                    y_scr[jj * CHUNK:(jj + 1) * CHUNK, :] = jnp.where(row_in_sub >= jj, qs[s] * w, 0.0).astype(BF16)
                att = _dot(y_scr[...], ones_bd)
                for jj in range(SUB):
                    os_[s] = os_[s] + att[jj * CHUNK:(jj + 1) * CHUNK, :] * sub_rows(i_ref.at[0], jj)

        mss = [_dot01_rhs(o * o, ones_bd) * (1.0 / HEAD_DIM) for o in os_]
        for s in range(n):
            o_ref[0, sls[s], :] = os_[s] * lax.rsqrt(mss[s] + EPS) * nrm * _silu(g_ref[0, sls[s], :])

    @pl.when(factored_ok)
    def _():
        inflight = HGRN_INFLIGHT if n_chunks % HGRN_INFLIGHT == 0 else 1

        def group(pp, carry):
            chunks([inflight * pp + s for s in range(inflight)], True)
            return carry

        lax.fori_loop(0, n_chunks // inflight, group, 0)

    @pl.when(jnp.logical_not(factored_ok))
    def _():
        def one(cc, carry):
            chunks([cc], False)
            return carry

        lax.fori_loop(0, n_chunks, one, 0)

    @pl.when(t == n_blk - 1)
    def _():
        for h in range(HEADS):
            hs = slice(h * HEAD_DIM, (h + 1) * HEAD_DIM)
            st_out_ref[0, h] = st_ref[hs, hs].T


def _hgrn(proj3, col0, st0, lb, nrm, l_valid, t_blk):
    b, lrows, _ = proj3.shape
    n_blk = -(-l_valid // t_blk)
    assert n_blk * t_blk <= lrows and t_blk % CHUNK == 0
    col = lambda c: pl.BlockSpec((1, t_blk, GROUP_WIDTH),
                                 lambda bi, t: (bi, t, (c - COL_CQ + col0) // GROUP_WIDTH))
    vec = pl.BlockSpec((1, GROUP_WIDTH), lambda bi, t: (0, 0))
    st_spec = pl.BlockSpec((1, HEADS, HEAD_DIM, HEAD_DIM), lambda bi, t: (bi, 0, 0, 0))
    return pl.pallas_call(
        functools.partial(_hgrn_kernel, t_blk=t_blk, l_valid=l_valid, n_blk=n_blk),
        out_shape=(jax.ShapeDtypeStruct((b, n_blk * t_blk, GROUP_WIDTH), F32),
                   jax.ShapeDtypeStruct((b, HEADS, HEAD_DIM, HEAD_DIM), F32)),
        grid=(b, n_blk),
        in_specs=[col(COL_CQ), col(COL_CF), col(COL_CI), col(COL_CG), st_spec, vec, vec],
        out_specs=(pl.BlockSpec((1, t_blk, GROUP_WIDTH), lambda bi, t: (bi, t, 0)), st_spec),
        scratch_shapes=[pltpu.VMEM((GROUP_WIDTH, GROUP_WIDTH), F32),
                        pltpu.VMEM((t_blk, GROUP_WIDTH), F32),
                        pltpu.VMEM((t_blk, GROUP_WIDTH), F32),
                        pltpu.VMEM((SUB * CHUNK, GROUP_WIDTH), BF16)],
        compiler_params=_cparams(("parallel", "arbitrary")),
        name="hgrn",
    )(proj3, proj3, proj3, proj3, st0, lb, nrm)


POOL_PAD = 16


def _pool_kernel(x_ref, buf_ref, w_ref, sc_ref, o_ref, buf_out_ref, xp_ref, *, t_blk):
    t = pl.program_id(1)

    @pl.when(t == 0)
    def _():
        xp_ref[0:1, :] = jnp.zeros((1, GROUP_WIDTH), F32)
        xp_ref[POOL_PAD - D_BUF:POOL_PAD, :] = buf_ref[0]

    x = x_ref[0]
    xp_ref[POOL_PAD:POOL_PAD + t_blk, :] = x
    acc = x
    sums = {}
    for d in range(1, D_BUF + 1):
        acc = acc + xp_ref[POOL_PAD - d:POOL_PAD - d + t_blk, :]
        sums[d + 1] = acc
    lane = lax.broadcasted_iota(jnp.int32, (t_blk, GROUP_WIDTH), 1)
    pooled = jnp.where(lane < 64, sums[2] * 0.5,
                       jnp.where(lane < 128, sums[4] * 0.25,
                                 jnp.where(lane < 192, sums[8] * 0.125, sums[16] * 0.0625))) - x
    o_ref[0] = _dot(pooled.astype(BF16), w_ref[...]) * sc_ref[...]
    tail = xp_ref[t_blk + 1:t_blk + 1 + D_BUF, :]
    buf_out_ref[0] = tail
    xp_ref[POOL_PAD - D_BUF:POOL_PAD, :] = tail


def _pool(proj3, buf, w_bd, scale, l_valid, t_blk):
    b = proj3.shape[0]
    n_blk = l_valid // t_blk
    assert n_blk * t_blk == l_valid and t_blk >= D_BUF
    buf_spec = pl.BlockSpec((1, D_BUF, GROUP_WIDTH), lambda bi, t: (bi, 0, 0))
    return pl.pallas_call(
        functools.partial(_pool_kernel, t_blk=t_blk),
        out_shape=(jax.ShapeDtypeStruct((b, l_valid, GROUP_WIDTH), F32),
                   jax.ShapeDtypeStruct((b, D_BUF, GROUP_WIDTH), F32)),
        grid=(b, n_blk),
        in_specs=[pl.BlockSpec((1, t_blk, GROUP_WIDTH), lambda bi, t: (bi, t, COL_DX // GROUP_WIDTH)),
                  buf_spec,
                  pl.BlockSpec((GROUP_WIDTH, GROUP_WIDTH), lambda bi, t: (0, 0)),
                  pl.BlockSpec((1, GROUP_WIDTH), lambda bi, t: (0, 0))],
        out_specs=(pl.BlockSpec((1, t_blk, GROUP_WIDTH), lambda bi, t: (bi, t, 0)), buf_spec),
        scratch_shapes=[pltpu.VMEM((POOL_PAD + t_blk, GROUP_WIDTH), F32)],
        compiler_params=_cparams(("parallel", "arbitrary")),
        name="pool",
    )(proj3, buf, w_bd, scale)


SSD_BLOCK = 512
HGRN_BLOCK = 512
POOL_BLOCK = 512


def _prep_layer(layer, norm1, w_in, a_rel_bias, b_conv_w, b_conv_b, b_dt_bias, b_a_log, b_d, b_norm, lbs, c_norm,
                d_pool_w, d_pool_scale, w_out, norm2, w_gate_up, w_down):
    w_r = _w_in_prep(jnp.transpose(w_in, (0, 2, 1)), layer)
    tab = a_rel_bias[layer].T
    m = A_BAND + CHUNK
    far = tab[:, 2 * REL_CLIP:]
    n_far = A_BAND_PREV - REL_CLIP + 1
    r = jnp.concatenate([jnp.broadcast_to(far, (HEADS, n_far)),
                         tab[:, 2 * REL_CLIP - 1:CHUNK:-1],
                         jnp.broadcast_to(far, (HEADS, m - n_far - (2 * REL_CLIP - 1 - CHUNK)))], axis=1)
    bias = jnp.tile(r, (1, CHUNK))[:, :CHUNK * (m - 1)].reshape(HEADS, CHUNK, m - 1)[:, :, :A_BAND]
    bias = bias.reshape(HEADS * CHUNK, A_BAND)
    rep = lambda p: jnp.repeat(p[layer], HEAD_DIM)[None, :]
    pw = d_pool_w[layer]
    w_bd = jnp.zeros((GROUP_WIDTH, GROUP_WIDTH), F32)
    for g in range(4):
        w_bd = w_bd.at[g * 64:(g + 1) * 64, g * 64:(g + 1) * 64].set(pw[g])
    return dict(
        n1=norm1[layer][None, :], w_in=w_r, bias=bias,
        cw=b_conv_w[layer], cb=b_conv_b[layer][None, :], dtb=rep(b_dt_bias), alog=rep(b_a_log), dvec=rep(b_d),
        bnrm=b_norm[layer][None, :], lb=lbs[layer][None, :], cnrm=c_norm[layer].reshape(1, GROUP_WIDTH),
        w_bd=w_bd.astype(BF16), psc=d_pool_scale[layer][None, :], n2=norm2[layer][None, :], layer=layer,
        w_out=w_out.astype(BF16), w_gu=w_gate_up.astype(BF16), w_dn=w_down.astype(BF16))


def _round_up(n, m):
    return -(-n // m) * m


def _group_layer(x, st, p, nf, final, cache_t=None):
    k_prev, v_prev, conv0, ssm0, hgrn0, pool0 = st
    b, l, _ = x.shape
    p_hist = k_prev.shape[1]
    proj = _inproj(x.reshape(b * l, D_MODEL), p["n1"], p["w_in"]).reshape(b, l, PROJ_W)
    lq = _round_up(l, CHUNK)
    pad_rows = lambda a: a if lq == l else jnp.pad(a, ((0, 0), (0, lq - l), (0, 0)))

    keep = min(A_BAND_PREV, p_hist + l)
    n_new = min(l, keep)
    ka = proj[:, l - n_new:, COL_AK:COL_AK + GROUP_WIDTH].reshape(b, n_new, HEADS, HEAD_DIM)
    va = proj[:, l - n_new:, COL_AV:COL_AV + GROUP_WIDTH].reshape(b, n_new, HEADS, HEAD_DIM)
    if cache_t is not None:
        bias_s = p["bias"].reshape(HEADS, CHUNK, A_BAND)[:, :l].reshape(HEADS * l, A_BAND)
        oa, kt_new, vt_new = _attention_cached(proj, cache_t[0], cache_t[1], p["layer"], bias_s, l)
        untranspose = lambda a: a.reshape(b, HEADS, HEAD_DIM, A_BAND_PREV).transpose(0, 3, 1, 2)
        new_k, new_v = untranspose(kt_new), untranspose(vt_new)
    else:
        if p_hist == 0:
            k_cache = v_cache = None
        else:
            k_cache = k_prev.astype(BF16).reshape(b, p_hist, GROUP_WIDTH)
            v_cache = v_prev.astype(BF16).reshape(b, p_hist, GROUP_WIDTH)
        oa = _attention(pad_rows(proj), k_cache, v_cache, p["bias"], l)[:, :l]
        new_k = jnp.concatenate([k_prev[:, p_hist - (keep - n_new):], ka], axis=1)
        new_v = jnp.concatenate([v_prev[:, p_hist - (keep - n_new):], va], axis=1)

    ob, new_conv, new_ssm = _ssd(proj, conv0, ssm0, p["cw"], p["cb"], p["dtb"], p["alog"], p["dvec"], p["bnrm"],
                                 l, SSD_BLOCK if l % SSD_BLOCK == 0 else SSD_CHUNK)

    if l % HGRN_BLOCK == 0:
        oc, new_hgrn = _hgrn(proj, COL_CQ, hgrn0, p["lb"], p["cnrm"], l, HGRN_BLOCK)
    else:
        oc, new_hgrn = _hgrn(pad_rows(proj[:, :, COL_CQ:COL_CQ + 4 * GROUP_WIDTH]), 0, hgrn0, p["lb"], p["cnrm"], l, lq)
        oc = oc[:, :l]

    od, new_pool = _pool(proj, pool0, p["w_bd"], p["psc"], l, POOL_BLOCK if l % POOL_BLOCK == 0 else l)

    flat = lambda a: a.reshape(b * l, GROUP_WIDTH)
    y = _mlp(x.reshape(b * l, D_MODEL), (flat(oa), flat(ob), flat(oc), flat(od)), p["layer"],
             p["w_out"], p["n2"], p["w_gu"], p["w_dn"], nf, final)
    return y.reshape(b, l, D_MODEL), (new_k, new_v, new_conv, new_ssm, new_hgrn, new_pool)


def _trunk(x, states, params, nf):
    depth = len(params)
    new = [[] for _ in range(6)]
    b, l, _ = x.shape
    cache_t = None
    if states[0].shape[2] == A_BAND_PREV and l <= CHUNK and l % 8 == 0:
        as_t = lambda c: jnp.transpose(c, (0, 1, 3, 4, 2)).reshape(depth, b, GROUP_WIDTH, A_BAND_PREV)
        cache_t = (as_t(states[0]), as_t(states[1]))
    for layer in range(depth):
        st_l = tuple(s[layer] for s in states)
        x, ns = _group_layer(x, st_l, params[layer], nf, layer == depth - 1, cache_t)
        for i in range(6):
            new[i].append(ns[i])
    return x, tuple(jnp.stack(n, axis=0) for n in new)


def kernel(x_prompt, x_sample, cache_a_k, cache_a_v, state_b_conv, state_b_ssm, state_c_hgrn, state_d_pool, norm1,
           w_in, a_rel_bias, b_conv_w, b_conv_b, b_dt_bias, b_a_log, b_d, b_norm, c_lb_logits, c_norm, d_pool_w,
           d_pool_scale, w_out, norm2, w_gate_up, w_down, norm_f):
    depth = w_in.shape[0]
    lbs = jnp.cumsum(jax.nn.softmax(c_lb_logits.astype(F32), axis=0), axis=0)
    lbs = lbs - lbs[:1]
    params = [_prep_layer(layer, norm1, w_in, a_rel_bias, b_conv_w, b_conv_b, b_dt_bias, b_a_log, b_d, b_norm, lbs,
                          c_norm, d_pool_w, d_pool_scale, w_out, norm2, w_gate_up, w_down)
              for layer in range(depth)]
    nf = norm_f[None, :]
    bp = x_prompt.shape[0]
    prompt_states = (
        jnp.zeros((depth, bp, 0, HEADS, HEAD_DIM), F32),
        jnp.zeros((depth, bp, 0, HEADS, HEAD_DIM), F32),
        jnp.zeros((depth, bp, B_CONV - 1, B_CONV_DIM), F32),
        jnp.zeros((depth, bp, HEADS, HEAD_DIM, B_STATE), F32),
        jnp.zeros((depth, bp, HEADS, HEAD_DIM, HEAD_DIM), F32),
        jnp.zeros((depth, bp, D_BUF, GROUP_WIDTH), F32),
    )
    y_prompt, ps = _trunk(x_prompt, prompt_states, params, nf)
    sample_states = (cache_a_k, cache_a_v, state_b_conv, state_b_ssm, state_c_hgrn, state_d_pool)
    y_sample, ss = _trunk(x_sample, sample_states, params, nf)
    return (y_prompt, y_sample) + ps + ss
```

```python
import functools

import jax
import jax.numpy as jnp
from jax import lax
from jax.experimental import pallas as pl
from jax.experimental.pallas import tpu as pltpu

F32 = jnp.float32
BF16 = jnp.bfloat16

D_MODEL = 1024
GROUP_WIDTH = 256
HEADS = 4
HEAD_DIM = 64
CHUNK = 64
A_BAND_PREV = 512
A_BAND = A_BAND_PREV + CHUNK
REL_CLIP = 128
B_STATE = 128
B_CONV = 4
B_CONV_DIM = 768
D_BUF = 15
D_FF = 2816
EPS = 1e-6
NEG = -1e30
LOG2E = 1.4426950408889634
SUB = 16

COL_XBC = 0
COL_AQ = 768
COL_AK = 1024
COL_AV = 1280
COL_BZ = 1536
COL_CQ = 1792
COL_CF = 2048
COL_CI = 2304
COL_CG = 2560
COL_DX = 2816
COL_DT = 3072
PROJ_W = 3328

VMEM_LIMIT = 56 * 1024 * 1024


def _cparams(sem):
    return pltpu.CompilerParams(dimension_semantics=sem, vmem_limit_bytes=VMEM_LIMIT)


def _rms(x, w):
    return x * lax.rsqrt(jnp.mean(x * x, axis=-1, keepdims=True) + EPS) * w


def _silu(x):
    return x * jax.nn.sigmoid(x)


def _dot(a, b):
    return jnp.dot(a, b, preferred_element_type=F32)


def _dot_nt(a, b):
    return lax.dot_general(a, b, (((1,), (1,)), ((), ())), preferred_element_type=F32)


def _dot_tn(a, b):
    return lax.dot_general(a, b, (((0,), (0,)), ((), ())), preferred_element_type=F32)


def _split3(x):
    hi = x.astype(BF16)
    r = x - hi.astype(F32)
    mid = r.astype(BF16)
    lo = (r - mid.astype(F32)).astype(BF16)
    return hi, mid, lo


def _dot01_lhs(m01, x):
    hi, mid, lo = _split3(x)
    return _dot(m01, hi) + _dot(m01, mid) + _dot(m01, lo)


def _dot01_rhs(x, m01):
    hi, mid, lo = _split3(x)
    return _dot(hi, m01) + _dot(mid, m01) + _dot(lo, m01)


def _tril(t):
    r = lax.broadcasted_iota(jnp.int32, (t, t), 0)
    c = lax.broadcasted_iota(jnp.int32, (t, t), 1)
    return r >= c


def _head_ones():
    r = lax.broadcasted_iota(jnp.int32, (GROUP_WIDTH, GROUP_WIDTH), 0) // HEAD_DIM
    c = lax.broadcasted_iota(jnp.int32, (GROUP_WIDTH, GROUP_WIDTH), 1) // HEAD_DIM
    return r == c


IN_WIDTH = 3076
ORIG_DT = 1792
W_PREP_COLS = 256


def _w_in_prep_kernel(wt_ref, o_ref):
    def put(c0, rows):
        o_ref[:, c0:c0 + W_PREP_COLS] = rows.T.astype(BF16)

    for j in range(3):
        put(COL_XBC + W_PREP_COLS * j, wt_ref[0, 1024 + W_PREP_COLS * j:1024 + W_PREP_COLS * (j + 1), :])
    for j in range(4):
        put(COL_AQ + W_PREP_COLS * j, wt_ref[0, W_PREP_COLS * j:W_PREP_COLS * (j + 1), :])
    for j in range(5):
        r0 = ORIG_DT + HEADS + W_PREP_COLS * j
        put(COL_CQ + W_PREP_COLS * j, wt_ref[0, r0:r0 + W_PREP_COLS, :])
    put(COL_DT, jnp.concatenate([jnp.broadcast_to(wt_ref[0, ORIG_DT + h:ORIG_DT + h + 1, :], (HEAD_DIM, D_MODEL))
                                 for h in range(HEADS)], axis=0))


def _w_in_prep(wt_all, layer):
    return pl.pallas_call(
        _w_in_prep_kernel,
        out_shape=jax.ShapeDtypeStruct((D_MODEL, PROJ_W), BF16),
        grid=(1,),
        in_specs=[pl.BlockSpec((1, IN_WIDTH, D_MODEL), lambda i: (layer, 0, 0), pipeline_mode=pl.Buffered(1))],
        out_specs=pl.BlockSpec((D_MODEL, PROJ_W), lambda i: (0, 0)),
        compiler_params=_cparams(("arbitrary",)),
        name="w_in_prep",
    )(wt_all)


def _inproj_kernel(x_ref, n_ref, w_ref, o_ref):
    hb = _rms(x_ref[...], n_ref[...]).astype(BF16)
    for c0 in range(0, PROJ_W, 256):
        o_ref[:, c0:c0 + 256] = _dot(hb, w_ref[:, c0:c0 + 256])


def _inproj(x2d, n1, w_in):
    rows = x2d.shape[0]
    tm = 512 if rows % 512 == 0 else rows
    return pl.pallas_call(
        _inproj_kernel,
        out_shape=jax.ShapeDtypeStruct((rows, PROJ_W), F32),
        grid=(rows // tm,),
        in_specs=[pl.BlockSpec((tm, D_MODEL), lambda i: (i, 0)),
                  pl.BlockSpec((1, D_MODEL), lambda i: (0, 0)),
                  pl.BlockSpec((D_MODEL, PROJ_W), lambda i: (0, 0), pipeline_mode=pl.Buffered(1))],
        out_specs=pl.BlockSpec((tm, PROJ_W), lambda i: (i, 0)),
        compiler_params=_cparams(("parallel",)),
        name="inproj",
    )(x2d, n1, w_in)


FF_CHUNK = 256


def _mlp_kernel(x_ref, oa_ref, ob_ref, oc_ref, od_ref, wo_ref, n2_ref, wgu_ref, wdn_ref, nf_ref, o_ref, mix_ref,
                act_ref, *, final):
    for m, r in enumerate((oa_ref, ob_ref, oc_ref, od_ref)):
        mix_ref[:, m * GROUP_WIDTH:(m + 1) * GROUP_WIDTH] = r[...].astype(BF16)
    x1 = x_ref[...] + _dot(mix_ref[...], wo_ref[0])
    hb = _rms(x1, n2_ref[...]).astype(BF16)
    for c0 in range(0, D_FF, FF_CHUNK):
        gate = _dot(hb, wgu_ref[0, :, c0:c0 + FF_CHUNK])
        up = _dot(hb, wgu_ref[0, :, D_FF + c0:D_FF + c0 + FF_CHUNK])
        act_ref[:, c0:c0 + FF_CHUNK] = (_silu(gate) * up).astype(BF16)
    out = x1 + _dot(act_ref[...], wdn_ref[0])
    if final:
        out = _rms(out, nf_ref[...])
    o_ref[...] = out


def _mlp(x2d, mix, layer, w_out, n2, w_gu, w_dn, nf, final):
    rows = x2d.shape[0]
    tm = 512 if rows % 512 == 0 else rows
    row_spec = lambda w: pl.BlockSpec((tm, w), lambda i: (i, 0))
    vec = lambda a: pl.BlockSpec(a.shape, lambda i: (0, 0))
    wspec = lambda a: pl.BlockSpec((1,) + a.shape[1:], lambda i: (layer, 0, 0), pipeline_mode=pl.Buffered(1))
    return pl.pallas_call(
        functools.partial(_mlp_kernel, final=final),
        out_shape=jax.ShapeDtypeStruct((rows, D_MODEL), F32),
        grid=(rows // tm,),
        in_specs=[row_spec(D_MODEL)] + [row_spec(GROUP_WIDTH)] * 4
                 + [wspec(w_out), vec(n2), wspec(w_gu), wspec(w_dn), vec(nf)],
        out_specs=row_spec(D_MODEL),
        scratch_shapes=[pltpu.VMEM((tm, D_MODEL), BF16), pltpu.VMEM((tm, D_FF), BF16)],
        compiler_params=_cparams(("parallel",)),
        name="mlp",
    )(x2d, *mix, w_out, n2, w_gu, w_dn, nf)


ATTN_UNROLL = 8


def _attn_kernel(q_ref, kh_ref, kc_ref, vh_ref, vc_ref, bias_ref, o_ref, kb_ref, vb_ref, *, qb, p_hist, l_valid):
    i = pl.program_id(1)
    if p_hist == 0:
        @pl.when(i == 0)
        def _():
            kb_ref[0:A_BAND_PREV, :] = jnp.zeros((A_BAND_PREV, GROUP_WIDTH), BF16)
            vb_ref[0:A_BAND_PREV, :] = jnp.zeros((A_BAND_PREV, GROUP_WIDTH), BF16)

        @pl.when(i > 0)
        def _():
            kb_ref[0:A_BAND_PREV, :] = kh_ref[0].astype(BF16)
            vb_ref[0:A_BAND_PREV, :] = vh_ref[0].astype(BF16)
    else:
        kb_ref[0:A_BAND_PREV, :] = kh_ref[0].astype(BF16)
        vb_ref[0:A_BAND_PREV, :] = vh_ref[0].astype(BF16)
    kb_ref[A_BAND_PREV:A_BAND_PREV + qb, :] = kc_ref[0].astype(BF16)
    vb_ref[A_BAND_PREV:A_BAND_PREV + qb, :] = vc_ref[0].astype(BF16)
    lane_head = lax.broadcasted_iota(jnp.int32, (CHUNK, GROUP_WIDTH), 1) // HEAD_DIM
    band_pos = lax.broadcasted_iota(jnp.int32, (1, A_BAND), 1)

    n_chunks = qb // CHUNK
    group = ATTN_UNROLL if n_chunks % ATTN_UNROLL == 0 else 1

    def chunks(gi, carry):
        r0s = [pl.multiple_of((gi * group + u) * CHUNK, CHUNK) for u in range(group)]
        es, dens = [], []
        for r0 in r0s:
            qc = q_ref[0, pl.ds(r0, CHUNK), :] * (HEAD_DIM ** -0.5)
            qs = jnp.concatenate([jnp.where(lane_head == h, qc, 0.0) for h in range(HEADS)], axis=0).astype(BF16)
            s = _dot_nt(qs, kb_ref[pl.ds(r0, A_BAND), :]) + bias_ref[...]
            kpos = i * qb + r0 - A_BAND_PREV + band_pos
            s = jnp.where((kpos >= -p_hist) & (kpos < l_valid), s, NEG)
            e = jnp.exp(s - jnp.max(s, axis=-1, keepdims=True))
            dens.append(jnp.sum(e, axis=-1, keepdims=True))
            es.append(e.astype(BF16))
        o_alls = [_dot(es[u], vb_ref[pl.ds(r0s[u], A_BAND), :]) * (1.0 / dens[u]) for u in range(group)]
        for u in range(group):
            o = jnp.zeros((CHUNK, GROUP_WIDTH), F32)
            for h in range(HEADS):
                o = o + jnp.where(lane_head == h, o_alls[u][h * CHUNK:(h + 1) * CHUNK, :], 0.0)
            o_ref[0, pl.ds(r0s[u], CHUNK), :] = o
        return carry

    lax.fori_loop(0, n_chunks // group, chunks, 0)


def _attention(proj3, k_cache, v_cache, bias, l_valid):
    b = proj3.shape[0]
    lq = _round_up(l_valid, CHUNK)
    qb = A_BAND_PREV if lq % A_BAND_PREV == 0 else lq
    nblk = lq // qb
    col = lambda c: pl.BlockSpec((1, qb, GROUP_WIDTH), lambda bi, i: (bi, i, c // GROUP_WIDTH))
    if k_cache is None:
        assert qb == A_BAND_PREV
        p_hist = 0
        prev = lambda c: pl.BlockSpec((1, qb, GROUP_WIDTH), lambda bi, i: (bi, jnp.maximum(i - 1, 0), c // GROUP_WIDTH))
        kh, vh, kh_spec, vh_spec = proj3, proj3, prev(COL_AK), prev(COL_AV)
    else:
        assert nblk == 1 and k_cache.shape[1] == A_BAND_PREV
        p_hist = A_BAND_PREV
        kh_spec = vh_spec = pl.BlockSpec((1, A_BAND_PREV, GROUP_WIDTH), lambda bi, i: (bi, 0, 0))
        kh, vh = k_cache, v_cache
    return pl.pallas_call(
        functools.partial(_attn_kernel, qb=qb, p_hist=p_hist, l_valid=l_valid),
        out_shape=jax.ShapeDtypeStruct((b, lq, GROUP_WIDTH), F32),
        grid=(b, nblk),
        in_specs=[col(COL_AQ), kh_spec, col(COL_AK), vh_spec, col(COL_AV),
                  pl.BlockSpec((HEADS * CHUNK, A_BAND), lambda bi, i: (0, 0))],
        out_specs=pl.BlockSpec((1, qb, GROUP_WIDTH), lambda bi, i: (bi, i, 0)),
        scratch_shapes=[pltpu.VMEM((A_BAND_PREV + qb, GROUP_WIDTH), BF16),
                        pltpu.VMEM((A_BAND_PREV + qb, GROUP_WIDTH), BF16)],
        compiler_params=_cparams(("parallel", "arbitrary")),
        name="attention",
    )(proj3, kh, proj3, vh, proj3, bias)


def _attn_cached_kernel(q_ref, kn_ref, vn_ref, kt_ref, vt_ref, bias_ref, o_ref, kt_out_ref, vt_out_ref, *, l_new):
    lane_head = lax.broadcasted_iota(jnp.int32, (l_new, GROUP_WIDTH), 1) // HEAD_DIM
    q = q_ref[0] * (HEAD_DIM ** -0.5)
    qs = jnp.concatenate([jnp.where(lane_head == h, q, 0.0) for h in range(HEADS)], axis=0).astype(BF16)
    kn = kn_ref[0]
    vn = vn_ref[0]
    kt = kt_ref[0, 0]
    vt = vt_ref[0, 0]
    s_old = _dot(qs, kt.astype(BF16)) + bias_ref[:, 0:A_BAND_PREV]
    s_new = _dot_nt(qs, kn.astype(BF16)) + bias_ref[:, A_BAND_PREV:A_BAND_PREV + l_new]
    m = jnp.maximum(jnp.max(s_old, axis=-1, keepdims=True), jnp.max(s_new, axis=-1, keepdims=True))
    e_old = jnp.exp(s_old - m)
    e_new = jnp.exp(s_new - m)
    den = jnp.sum(e_old, axis=-1, keepdims=True) + jnp.sum(e_new, axis=-1, keepdims=True)
    o_all = (_dot_nt(e_old.astype(BF16), vt.astype(BF16)) + _dot(e_new.astype(BF16), vn.astype(BF16))) * (1.0 / den)
    o = jnp.zeros((l_new, GROUP_WIDTH), F32)
    for h in range(HEADS):
        o = o + jnp.where(lane_head == h, o_all[h * l_new:(h + 1) * l_new, :], 0.0)
    o_ref[0] = o

    keep = A_BAND_PREV - l_new
    lane = lax.broadcasted_iota(jnp.int32, (GROUP_WIDTH, A_BAND_PREV), 1)
    r = lax.broadcasted_iota(jnp.int32, (l_new, A_BAND_PREV), 0)
    c = lax.broadcasted_iota(jnp.int32, (l_new, A_BAND_PREV), 1)
    place = (c == r + keep).astype(BF16)

    def appended(old_t, new_rows):
        hi, mid, lo = _split3(new_rows)
        new_cols = _dot_tn(hi, place) + _dot_tn(mid, place) + _dot_tn(lo, place)
        return jnp.where(lane < keep, pltpu.roll(old_t, keep, 1), new_cols)

    kt_out_ref[0] = appended(kt, kn)
    vt_out_ref[0] = appended(vt, vn)


def _attention_cached(proj3, kt_cache, vt_cache, layer, bias, l_valid):
    b = proj3.shape[0]
    col = lambda c: pl.BlockSpec((1, l_valid, GROUP_WIDTH), lambda bi: (bi, 0, c // GROUP_WIDTH))
    cache_spec = pl.BlockSpec((1, 1, GROUP_WIDTH, A_BAND_PREV), lambda bi: (layer, bi, 0, 0))
    out_t = pl.BlockSpec((1, GROUP_WIDTH, A_BAND_PREV), lambda bi: (bi, 0, 0))
    return pl.pallas_call(
        functools.partial(_attn_cached_kernel, l_new=l_valid),
        out_shape=(jax.ShapeDtypeStruct((b, l_valid, GROUP_WIDTH), F32),
                   jax.ShapeDtypeStruct((b, GROUP_WIDTH, A_BAND_PREV), F32),
                   jax.ShapeDtypeStruct((b, GROUP_WIDTH, A_BAND_PREV), F32)),
        grid=(b,),
        in_specs=[col(COL_AQ), col(COL_AK), col(COL_AV), cache_spec, cache_spec,
                  pl.BlockSpec((HEADS * l_valid, A_BAND), lambda bi: (0, 0))],
        out_specs=(pl.BlockSpec((1, l_valid, GROUP_WIDTH), lambda bi: (bi, 0, 0)), out_t, out_t),
        compiler_params=_cparams(("parallel",)),
        name="attention_cached",
    )(proj3, proj3, proj3, kt_cache, vt_cache, bias)


CONV_PAD = 8
SSD_CHUNK = 128


def _ssd_kernel(xbc_ref, z_ref, dt_ref, conv0_ref, ssm0_ref, cw_ref, cb_ref, dtb_ref, alog_ref, d_ref, nrm_ref,
                o_ref, conv_out_ref, ssm_out_ref, xpad_ref, st_ref, *, t_blk, l_valid, n_blk):
    t = pl.program_id(1)
    hist = B_CONV - 1

    @pl.when(t == 0)
    def _():
        xpad_ref[CONV_PAD - hist:CONV_PAD, :] = conv0_ref[0]
        st_ref[...] = ssm0_ref[0].reshape(GROUP_WIDTH, B_STATE).T

    n_last = l_valid - (n_blk - 1) * t_blk
    conv_out_ref[0] = xbc_ref[0, n_last - hist:n_last, :]

    tc = SSD_CHUNK
    short = l_valid < t_blk

    def rows_of(ref, sl):
        if short:
            return jnp.concatenate([ref[0], jnp.zeros((tc - l_valid, ref.shape[2]), F32)], axis=0)
        return ref[0, sl, :]

    tril = _tril(tc)
    tril_b = tril.astype(BF16)
    lane = lax.broadcasted_iota(jnp.int32, (tc, B_STATE), 1)
    rows = lax.broadcasted_iota(jnp.int32, (tc, GROUP_WIDTH), 0)
    a_neg = -jnp.exp(alog_ref[...])
    conv_b = cb_ref[...]
    conv_w = cw_ref[...]
    dt_bias = dtb_ref[...]
    d_skip = d_ref[...]
    nrm = nrm_ref[...]

    def chunk(j, carry):
        r0 = pl.multiple_of(j * tc, tc)
        sl = pl.ds(r0, tc)
        xpad_ref[CONV_PAD:CONV_PAD + tc, :] = rows_of(xbc_ref, sl)
        y = conv_b
        for i in range(B_CONV):
            y = y + xpad_ref[CONV_PAD - hist + i:CONV_PAD - hist + i + tc, :] * conv_w[i:i + 1, :]
        xbc = _silu(y)
        xpad_ref[CONV_PAD - hist:CONV_PAD, :] = xpad_ref[CONV_PAD + tc - hist:CONV_PAD + tc, :]

        xs = xbc[:, 0:GROUP_WIDTH]
        bm = xbc[:, GROUP_WIDTH:2 * GROUP_WIDTH]
        cm = xbc[:, 2 * GROUP_WIDTH:3 * GROUP_WIDTH]
        dtr = rows_of(dt_ref, sl) + dt_bias
        dt = jnp.maximum(dtr, 0.0) + jnp.log1p(jnp.exp(-jnp.abs(dtr)))
        if l_valid < n_blk * t_blk:
            dt = jnp.where(t * t_blk + r0 + rows < l_valid, dt, 0.0)
        a = dt * a_neg
        u = xs * dt
        acs = _dot01_lhs(tril_b, a)
        acs_t = acs.T
        eacs = jnp.exp(acs)
        last = acs[tc - 1:tc, :]
        elast = jnp.exp(last)
        ub = u.astype(BF16)
        upb = (u * jnp.exp(last - acs)).astype(BF16)
        ys = []
        for g in range(2):
            gs = slice(g * B_STATE, (g + 1) * B_STATE)
            cg = cm[:, gs].astype(BF16)
            bg = bm[:, gs].astype(BF16)
            cb = _dot_nt(cg, bg)
            y_state = _dot(cg, st_ref[:, gs].astype(BF16)) * eacs[:, gs]
            parts = []
            for hh in range(2):
                c0 = (2 * g + hh) * HEAD_DIM
                col = jnp.broadcast_to(acs[:, c0:c0 + 1], (tc, tc))
                rowv = jnp.broadcast_to(acs_t[c0:c0 + 1, :], (tc, tc))
                dec = jnp.exp(jnp.where(tril, col - rowv, NEG))
                parts.append(_dot((cb * dec).astype(BF16), ub[:, gs]))
            ys.append(jnp.where(lane < HEAD_DIM, parts[0], parts[1]) + y_state)
            st_ref[:, gs] = st_ref[:, gs] * elast[:, gs] + _dot_tn(bg, upb[:, gs])
        yv = jnp.concatenate(ys, axis=1) + d_skip * xs
        res = _rms(yv * _silu(rows_of(z_ref, sl)), nrm)
        if short:
            o_ref[0] = res[0:l_valid, :]
        else:
            o_ref[0, sl, :] = res
        return carry

    lax.fori_loop(0, t_blk // tc, chunk, 0)

    @pl.when(t == n_blk - 1)
    def _():
        ssm_out_ref[0] = st_ref[...].T.reshape(HEADS, HEAD_DIM, B_STATE)


def _ssd(proj3, conv0, ssm0, cw, cb, dtb, alog, dvec, nrm, l_valid, t_blk):
    b, lrows, _ = proj3.shape
    n_blk = -(-l_valid // t_blk)
    assert t_blk % SSD_CHUNK == 0 and lrows == l_valid
    assert l_valid == n_blk * t_blk or (t_blk == SSD_CHUNK and B_CONV - 1 <= l_valid < t_blk and l_valid % 8 == 0)
    t_in = min(t_blk, l_valid)
    vec = lambda w: pl.BlockSpec((1, w), lambda bi, t: (0, 0))
    return pl.pallas_call(
        functools.partial(_ssd_kernel, t_blk=t_blk, l_valid=l_valid, n_blk=n_blk),
        out_shape=(jax.ShapeDtypeStruct((b, l_valid, GROUP_WIDTH), F32),
                   jax.ShapeDtypeStruct((b, B_CONV - 1, B_CONV_DIM), F32),
                   jax.ShapeDtypeStruct((b, HEADS, HEAD_DIM, B_STATE), F32)),
        grid=(b, n_blk),
        in_specs=[pl.BlockSpec((1, t_in, B_CONV_DIM), lambda bi, t: (bi, t, COL_XBC // B_CONV_DIM)),
                  pl.BlockSpec((1, t_in, GROUP_WIDTH), lambda bi, t: (bi, t, COL_BZ // GROUP_WIDTH)),
                  pl.BlockSpec((1, t_in, GROUP_WIDTH), lambda bi, t: (bi, t, COL_DT // GROUP_WIDTH)),
                  pl.BlockSpec((1, B_CONV - 1, B_CONV_DIM), lambda bi, t: (bi, 0, 0)),
                  pl.BlockSpec((1, HEADS, HEAD_DIM, B_STATE), lambda bi, t: (bi, 0, 0, 0)),
                  pl.BlockSpec((B_CONV, B_CONV_DIM), lambda bi, t: (0, 0)),
                  vec(B_CONV_DIM), vec(GROUP_WIDTH), vec(GROUP_WIDTH), vec(GROUP_WIDTH), vec(GROUP_WIDTH)],
        out_specs=(pl.BlockSpec((1, t_in, GROUP_WIDTH), lambda bi, t: (bi, t, 0)),
                   pl.BlockSpec((1, B_CONV - 1, B_CONV_DIM), lambda bi, t: (bi, 0, 0)),
                   pl.BlockSpec((1, HEADS, HEAD_DIM, B_STATE), lambda bi, t: (bi, 0, 0, 0))),
        scratch_shapes=[pltpu.VMEM((CONV_PAD + SSD_CHUNK, B_CONV_DIM), F32),
                        pltpu.VMEM((B_STATE, GROUP_WIDTH), F32)],
        compiler_params=_cparams(("parallel", "arbitrary")),
        name="ssd",
    )(proj3, proj3, proj3, conv0, ssm0, cw, cb, dtb, alog, dvec, nrm)


HGRN_INFLIGHT = 8
HGRN_SAFE_LOG2 = 64.0


def _hgrn_kernel(q_ref, f_ref, i_ref, g_ref, st0_ref, lb_ref, nrm_ref, o_ref, st_out_ref, st_ref, b_blk, c_blk,
                 y_scr, *, t_blk, l_valid, n_blk):
    t = pl.program_id(1)

    @pl.when(t == 0)
    def _():
        for h in range(HEADS):
            parts = [jnp.zeros((HEAD_DIM, HEAD_DIM * h), F32)] if h > 0 else []
            parts.append(st0_ref[0, h].T)
            if h < HEADS - 1:
                parts.append(jnp.zeros((HEAD_DIM, HEAD_DIM * (HEADS - 1 - h)), F32))
            st_ref[h * HEAD_DIM:(h + 1) * HEAD_DIM, :] = jnp.concatenate(parts, axis=1)

    lb = lb_ref[...]
    nrm = nrm_ref[...]
    head_eq = _head_ones()
    ones_bd = head_eq.astype(BF16)
    tril = _tril(CHUNK).astype(BF16)
    rows = lax.broadcasted_iota(jnp.int32, (CHUNK, GROUP_WIDTH), 0)
    row_in_sub = rows % SUB
    lane_head = lax.broadcasted_iota(jnp.int32, (SUB, GROUP_WIDTH), 1) // HEAD_DIM
    n_sub = CHUNK // SUB
    n_chunks = t_blk // CHUNK
    sc_row = lax.broadcasted_iota(jnp.int32, (HEADS * SUB, CHUNK), 0) % SUB
    sc_col = lax.broadcasted_iota(jnp.int32, (HEADS * SUB, CHUNK), 1)

    def gates(cc, gmax):
        r0 = pl.multiple_of(cc * CHUNK, CHUNK)
        sl = pl.ds(r0, CHUNK)
        f = lb + (1.0 - lb) * jax.nn.sigmoid(f_ref[0, sl, :])
        kk = jnp.maximum(1.0 - f, 0.0)
        lf2 = jnp.log(f) * LOG2E
        if l_valid < n_blk * t_blk:
            ok = t * t_blk + r0 + rows < l_valid
            kk = jnp.where(ok, kk, 0.0)
            lf2 = jnp.where(ok, lf2, 0.0)
        bcs2 = _dot01_lhs(tril, lf2)
        b_blk[sl, :] = bcs2
        c_blk[sl, :] = jnp.log(kk) * LOG2E - bcs2
        prev = jnp.zeros((1, GROUP_WIDTH), F32)
        for a in range(n_sub):
            end = bcs2[(a + 1) * SUB - 1:(a + 1) * SUB, :]
            gmax = jnp.maximum(gmax, prev - end)
            prev = end
        return gmax

    gmax = lax.fori_loop(0, n_chunks, gates, jnp.zeros((1, GROUP_WIDTH), F32),
                         unroll=HGRN_INFLIGHT if n_chunks % HGRN_INFLIGHT == 0 else 1)
    factored_ok = jnp.max(gmax) <= HGRN_SAFE_LOG2

    def stack_heads(x):
        return jnp.concatenate([jnp.where(lane_head == h, x, 0.0) for h in range(HEADS)], axis=0).astype(BF16)

    def unstack_heads(x_all):
        out = jnp.zeros((SUB, GROUP_WIDTH), F32)
        for h in range(HEADS):
            out = out + jnp.where(lane_head == h, x_all[h * SUB:(h + 1) * SUB, :], 0.0)
        return out

    def chunks(ccs, factored):
        n = len(ccs)
        r0s = [pl.multiple_of(cc * CHUNK, CHUNK) for cc in ccs]
        sls = [pl.ds(r0, CHUNK) for r0 in r0s]
        qs = [q_ref[0, sl, :] for sl in sls]
        vs = [i_ref[0, sl, :] for sl in sls]
        bs = [b_blk[sl, :] for sl in sls]
        cs = [c_blk[sl, :] for sl in sls]
        vbs = [v.astype(BF16) for v in vs]

        os_ = []
        for s in range(n):
            os_.append(_dot_nt((qs[s] * jnp.exp2(bs[s])).astype(BF16), st_ref[...].astype(BF16)))
            last = bs[s][CHUNK - 1:CHUNK, :]
            kt = jnp.exp2(last + cs[s]).astype(BF16)
            st_ref[...] = st_ref[...] * jnp.exp2(last) + jnp.where(head_eq, _dot_tn(vbs[s], kt), 0.0)

        pieces = [[] for _ in range(n)]
        for a in range(n_sub):
            s0 = a * SUB
            s_end = s0 + SUB if factored else s0
            if s_end == 0:
                for s in range(n):
                    pieces[s].append(jnp.zeros((SUB, GROUP_WIDTH), F32))
                continue
            scs = []
            for s in range(n):
                ref_row = bs[s][s0 - 1:s0, :] if a > 0 else jnp.zeros((1, GROUP_WIDTH), F32)
                qa = qs[s][s0:s0 + SUB, :] * jnp.exp2(bs[s][s0:s0 + SUB, :] - ref_row)
                ka = jnp.where(rows < s_end, jnp.exp2(ref_row + cs[s]), 0.0)
                sc = _dot_nt(stack_heads(qa), ka.astype(BF16))
                if factored:
                    sc = jnp.where(sc_col <= s0 + sc_row, sc, 0.0)
                scs.append(sc.astype(BF16))
            for s in range(n):
                pieces[s].append(unstack_heads(_dot(scs[s], vbs[s])))
        for s in range(n):
            os_[s] = os_[s] + jnp.concatenate(pieces[s], axis=0)

        if not factored:
            for s in range(n):
                def sub_rows(ref, jj):
                    return jnp.concatenate(
                        [jnp.broadcast_to(ref[pl.ds(r0s[s] + a * SUB + jj, 1), :], (SUB, GROUP_WIDTH))
                         for a in range(n_sub)], axis=0)

                for jj in range(SUB):
                    w = jnp.exp2(bs[s] + sub_rows(c_blk, jj))
                    y_scr[jj * CHUNK:(jj + 1) * CHUNK, :] = jnp.where(row_in_sub >= jj, qs[s] * w, 0.0).astype(BF16)
                att = _dot(y_scr[...], ones_bd)
                for jj in range(SUB):
                    os_[s] = os_[s] + att[jj * CHUNK:(jj + 1) * CHUNK, :] * sub_rows(i_ref.at[0], jj)

        mss = [_dot01_rhs(o * o, ones_bd) * (1.0 / HEAD_DIM) for o in os_]
        for s in range(n):
            o_ref[0, sls[s], :] = os_[s] * lax.rsqrt(mss[s] + EPS) * nrm * _silu(g_ref[0, sls[s], :])

    @pl.when(factored_ok)
    def _():
        inflight = HGRN_INFLIGHT if n_chunks % HGRN_INFLIGHT == 0 else 1

        def group(pp, carry):
            chunks([inflight * pp + s for s in range(inflight)], True)
            return carry

        lax.fori_loop(0, n_chunks // inflight, group, 0)

    @pl.when(jnp.logical_not(factored_ok))
    def _():
        def one(cc, carry):
            chunks([cc], False)
            return carry

        lax.fori_loop(0, n_chunks, one, 0)

    @pl.when(t == n_blk - 1)
    def _():
        for h in range(HEADS):
            hs = slice(h * HEAD_DIM, (h + 1) * HEAD_DIM)
            st_out_ref[0, h] = st_ref[hs, hs].T


def _hgrn(proj3, col0, st0, lb, nrm, l_valid, t_blk):
    b, lrows, _ = proj3.shape
    n_blk = -(-l_valid // t_blk)
    assert n_blk * t_blk <= lrows and t_blk % CHUNK == 0
    col = lambda c: pl.BlockSpec((1, t_blk, GROUP_WIDTH),
                                 lambda bi, t: (bi, t, (c - COL_CQ + col0) // GROUP_WIDTH))
    vec = pl.BlockSpec((1, GROUP_WIDTH), lambda bi, t: (0, 0))
    st_spec = pl.BlockSpec((1, HEADS, HEAD_DIM, HEAD_DIM), lambda bi, t: (bi, 0, 0, 0))
    return pl.pallas_call(
        functools.partial(_hgrn_kernel, t_blk=t_blk, l_valid=l_valid, n_blk=n_blk),
        out_shape=(jax.ShapeDtypeStruct((b, n_blk * t_blk, GROUP_WIDTH), F32),
                   jax.ShapeDtypeStruct((b, HEADS, HEAD_DIM, HEAD_DIM), F32)),
        grid=(b, n_blk),
        in_specs=[col(COL_CQ), col(COL_CF), col(COL_CI), col(COL_CG), st_spec, vec, vec],
        out_specs=(pl.BlockSpec((1, t_blk, GROUP_WIDTH), lambda bi, t: (bi, t, 0)), st_spec),
        scratch_shapes=[pltpu.VMEM((GROUP_WIDTH, GROUP_WIDTH), F32),
                        pltpu.VMEM((t_blk, GROUP_WIDTH), F32),
                        pltpu.VMEM((t_blk, GROUP_WIDTH), F32),
                        pltpu.VMEM((SUB * CHUNK, GROUP_WIDTH), BF16)],
        compiler_params=_cparams(("parallel", "arbitrary")),
        name="hgrn",
    )(proj3, proj3, proj3, proj3, st0, lb, nrm)


POOL_PAD = 24
POOL_FRONT = 8
D_GROUP = 64


def _pool_kernel(x_ref, buf_ref, w_ref, sc_ref, o_ref, buf_out_ref, xp_ref, s2_ref, s4_ref, *, t_blk):
    t = pl.program_id(1)
    hi = POOL_PAD + t_blk
    half = GROUP_WIDTH // 2

    @pl.when(t == 0)
    def _():
        zeros = jnp.zeros((POOL_PAD - D_BUF, GROUP_WIDTH), F32)
        xp_ref[0:POOL_PAD - D_BUF, :] = zeros
        s2_ref[0:POOL_FRONT, :] = zeros[0:POOL_FRONT]
        s4_ref[0:POOL_FRONT, :] = zeros[0:POOL_FRONT]
        xp_ref[POOL_PAD - D_BUF:POOL_PAD, :] = buf_ref[0]

    x = x_ref[0]
    xp_ref[POOL_PAD:hi, :] = x
    s2 = xp_ref[POOL_FRONT:hi, :] + xp_ref[POOL_FRONT - 1:hi - 1, :]
    s2_ref[POOL_FRONT:hi, :] = s2
    s4 = s2 + s2_ref[POOL_FRONT - 2:hi - 2, :]
    s4_ref[POOL_FRONT:hi, :] = s4
    s8 = s4[:, half:] + s4_ref[POOL_FRONT - 4:hi - 4, half:]
    n_ext = hi - POOL_FRONT
    s16 = s8[8:n_ext, :] + s8[0:n_ext - 8, :]
    lo_sum = s2[n_ext - t_blk:, :half]
    lane = lax.broadcasted_iota(jnp.int32, (t_blk, half), 1)
    pooled_lo = jnp.where(lane < D_GROUP, lo_sum * 0.5, s4[n_ext - t_blk:, :half] * 0.25)
    pooled_hi = jnp.where(lane < D_GROUP, s8[n_ext - t_blk:, :] * 0.125, s16[n_ext - 8 - t_blk:, :] * 0.0625)
    pooled = jnp.concatenate([pooled_lo, pooled_hi], axis=1) - x
    o_ref[0] = _dot(pooled.astype(BF16), w_ref[...]) * sc_ref[...]
    tail = xp_ref[hi - D_BUF:hi, :]
    buf_out_ref[0] = tail
    xp_ref[POOL_PAD - D_BUF:POOL_PAD, :] = tail


def _pool(proj3, buf, w_bd, scale, l_valid, t_blk):
    b = proj3.shape[0]
    n_blk = l_valid // t_blk
    assert n_blk * t_blk == l_valid and t_blk >= D_BUF
    buf_spec = pl.BlockSpec((1, D_BUF, GROUP_WIDTH), lambda bi, t: (bi, 0, 0))
    return pl.pallas_call(
        functools.partial(_pool_kernel, t_blk=t_blk),
        out_shape=(jax.ShapeDtypeStruct((b, l_valid, GROUP_WIDTH), F32),
                   jax.ShapeDtypeStruct((b, D_BUF, GROUP_WIDTH), F32)),
        grid=(b, n_blk),
        in_specs=[pl.BlockSpec((1, t_blk, GROUP_WIDTH), lambda bi, t: (bi, t, COL_DX // GROUP_WIDTH)),
                  buf_spec,
                  pl.BlockSpec((GROUP_WIDTH, GROUP_WIDTH), lambda bi, t: (0, 0)),
                  pl.BlockSpec((1, GROUP_WIDTH), lambda bi, t: (0, 0))],
        out_specs=(pl.BlockSpec((1, t_blk, GROUP_WIDTH), lambda bi, t: (bi, t, 0)), buf_spec),
        scratch_shapes=[pltpu.VMEM((POOL_PAD + t_blk, GROUP_WIDTH), F32)] * 3,
        compiler_params=_cparams(("parallel", "arbitrary")),
        name="pool",
    )(proj3, buf, w_bd, scale)


SSD_BLOCK = 512
HGRN_BLOCK = 512
POOL_BLOCK = 512


def _prep_layer(layer, norm1, w_in, a_rel_bias, b_conv_w, b_conv_b, b_dt_bias, b_a_log, b_d, b_norm, lbs, c_norm,
                d_pool_w, d_pool_scale, w_out, norm2, w_gate_up, w_down):
    w_r = _w_in_prep(jnp.transpose(w_in, (0, 2, 1)), layer)
    tab = a_rel_bias[layer].T
    m = A_BAND + CHUNK
    far = tab[:, 2 * REL_CLIP:]
    n_far = A_BAND_PREV - REL_CLIP + 1
    r = jnp.concatenate([jnp.broadcast_to(far, (HEADS, n_far)),
                         tab[:, 2 * REL_CLIP - 1:CHUNK:-1],
                         jnp.broadcast_to(far, (HEADS, m - n_far - (2 * REL_CLIP - 1 - CHUNK)))], axis=1)
    bias = jnp.tile(r, (1, CHUNK))[:, :CHUNK * (m - 1)].reshape(HEADS, CHUNK, m - 1)[:, :, :A_BAND]
    bias = bias.reshape(HEADS * CHUNK, A_BAND)
    rep = lambda p: jnp.repeat(p[layer], HEAD_DIM)[None, :]
    pw = d_pool_w[layer]
    w_bd = jnp.zeros((GROUP_WIDTH, GROUP_WIDTH), F32)
    for g in range(4):
        w_bd = w_bd.at[g * 64:(g + 1) * 64, g * 64:(g + 1) * 64].set(pw[g])
    return dict(
        n1=norm1[layer][None, :], w_in=w_r, bias=bias,
        cw=b_conv_w[layer], cb=b_conv_b[layer][None, :], dtb=rep(b_dt_bias), alog=rep(b_a_log), dvec=rep(b_d),
        bnrm=b_norm[layer][None, :], lb=lbs[layer][None, :], cnrm=c_norm[layer].reshape(1, GROUP_WIDTH),
        w_bd=w_bd.astype(BF16), psc=d_pool_scale[layer][None, :], n2=norm2[layer][None, :], layer=layer,
        w_out=w_out.astype(BF16), w_gu=w_gate_up.astype(BF16), w_dn=w_down.astype(BF16))


def _round_up(n, m):
    return -(-n // m) * m


def _group_layer(x, st, p, nf, final, cache_t=None):
    k_prev, v_prev, conv0, ssm0, hgrn0, pool0 = st
    b, l, _ = x.shape
    p_hist = k_prev.shape[1]
    proj = _inproj(x.reshape(b * l, D_MODEL), p["n1"], p["w_in"]).reshape(b, l, PROJ_W)
    lq = _round_up(l, CHUNK)
    pad_rows = lambda a: a if lq == l else jnp.pad(a, ((0, 0), (0, lq - l), (0, 0)))

    keep = min(A_BAND_PREV, p_hist + l)
    n_new = min(l, keep)
    ka = proj[:, l - n_new:, COL_AK:COL_AK + GROUP_WIDTH].reshape(b, n_new, HEADS, HEAD_DIM)
    va = proj[:, l - n_new:, COL_AV:COL_AV + GROUP_WIDTH].reshape(b, n_new, HEADS, HEAD_DIM)
    if cache_t is not None:
        bias_s = p["bias"].reshape(HEADS, CHUNK, A_BAND)[:, :l].reshape(HEADS * l, A_BAND)
        oa, kt_new, vt_new = _attention_cached(proj, cache_t[0], cache_t[1], p["layer"], bias_s, l)
        untranspose = lambda a: a.reshape(b, HEADS, HEAD_DIM, A_BAND_PREV).transpose(0, 3, 1, 2)
        new_k, new_v = untranspose(kt_new), untranspose(vt_new)
    else:
        if p_hist == 0:
            k_cache = v_cache = None
        else:
            k_cache = k_prev.astype(BF16).reshape(b, p_hist, GROUP_WIDTH)
            v_cache = v_prev.astype(BF16).reshape(b, p_hist, GROUP_WIDTH)
        oa = _attention(pad_rows(proj), k_cache, v_cache, p["bias"], l)[:, :l]
        new_k = jnp.concatenate([k_prev[:, p_hist - (keep - n_new):], ka], axis=1)
        new_v = jnp.concatenate([v_prev[:, p_hist - (keep - n_new):], va], axis=1)

    ob, new_conv, new_ssm = _ssd(proj, conv0, ssm0, p["cw"], p["cb"], p["dtb"], p["alog"], p["dvec"], p["bnrm"],
                                 l, SSD_BLOCK if l % SSD_BLOCK == 0 else SSD_CHUNK)

    if l % HGRN_BLOCK == 0:
        oc, new_hgrn = _hgrn(proj, COL_CQ, hgrn0, p["lb"], p["cnrm"], l, HGRN_BLOCK)
    else:
        oc, new_hgrn = _hgrn(pad_rows(proj[:, :, COL_CQ:COL_CQ + 4 * GROUP_WIDTH]), 0, hgrn0, p["lb"], p["cnrm"], l, lq)
        oc = oc[:, :l]

    od, new_pool = _pool(proj, pool0, p["w_bd"], p["psc"], l, POOL_BLOCK if l % POOL_BLOCK == 0 else l)

    flat = lambda a: a.reshape(b * l, GROUP_WIDTH)
    y = _mlp(x.reshape(b * l, D_MODEL), (flat(oa), flat(ob), flat(oc), flat(od)), p["layer"],
             p["w_out"], p["n2"], p["w_gu"], p["w_dn"], nf, final)
    return y.reshape(b, l, D_MODEL), (new_k, new_v, new_conv, new_ssm, new_hgrn, new_pool)


def _trunk(x, states, params, nf):
    depth = len(params)
    new = [[] for _ in range(6)]
    b, l, _ = x.shape
    cache_t = None
    if states[0].shape[2] == A_BAND_PREV and l <= CHUNK and l % 8 == 0:
        as_t = lambda c: jnp.transpose(c, (0, 1, 3, 4, 2)).reshape(depth, b, GROUP_WIDTH, A_BAND_PREV)
        cache_t = (as_t(states[0]), as_t(states[1]))
    for layer in range(depth):
        st_l = tuple(s[layer] for s in states)
        x, ns = _group_layer(x, st_l, params[layer], nf, layer == depth - 1, cache_t)
        for i in range(6):
            new[i].append(ns[i])
    return x, tuple(jnp.stack(n, axis=0) for n in new)


def kernel(x_prompt, x_sample, cache_a_k, cache_a_v, state_b_conv, state_b_ssm, state_c_hgrn, state_d_pool, norm1,
           w_in, a_rel_bias, b_conv_w, b_conv_b, b_dt_bias, b_a_log, b_d, b_norm, c_lb_logits, c_norm, d_pool_w,
           d_pool_scale, w_out, norm2, w_gate_up, w_down, norm_f):
    depth = w_in.shape[0]
    lbs = jnp.cumsum(jax.nn.softmax(c_lb_logits.astype(F32), axis=0), axis=0)
    lbs = lbs - lbs[:1]
    params = [_prep_layer(layer, norm1, w_in, a_rel_bias, b_conv_w, b_conv_b, b_dt_bias, b_a_log, b_d, b_norm, lbs,
                          c_norm, d_pool_w, d_pool_scale, w_out, norm2, w_gate_up, w_down)
              for layer in range(depth)]
    nf = norm_f[None, :]
    bp = x_prompt.shape[0]
    prompt_states = (
        jnp.zeros((depth, bp, 0, HEADS, HEAD_DIM), F32),
        jnp.zeros((depth, bp, 0, HEADS, HEAD_DIM), F32),
        jnp.zeros((depth, bp, B_CONV - 1, B_CONV_DIM), F32),
        jnp.zeros((depth, bp, HEADS, HEAD_DIM, B_STATE), F32),
        jnp.zeros((depth, bp, HEADS, HEAD_DIM, HEAD_DIM), F32),
        jnp.zeros((depth, bp, D_BUF, GROUP_WIDTH), F32),
    )
    y_prompt, ps = _trunk(x_prompt, prompt_states, params, nf)
    sample_states = (cache_a_k, cache_a_v, state_b_conv, state_b_ssm, state_c_hgrn, state_d_pool)
    y_sample, ss = _trunk(x_sample, sample_states, params, nf)
    return (y_prompt, y_sample) + ps + ss
```

```python
import functools

import jax
import jax.numpy as jnp
from jax import lax
from jax.experimental import pallas as pl
from jax.experimental.pallas import tpu as pltpu

F32 = jnp.float32
BF16 = jnp.bfloat16

D_MODEL = 1024
GROUP_WIDTH = 256
HEADS = 4
HEAD_DIM = 64
CHUNK = 64
A_BAND_PREV = 512
A_BAND = A_BAND_PREV + CHUNK
REL_CLIP = 128
B_STATE = 128
B_CONV = 4
B_CONV_DIM = 768
D_BUF = 15
D_FF = 2816
EPS = 1e-6
NEG = -1e30
LOG2E = 1.4426950408889634
SUB = 16

COL_XBC = 0
COL_AQ = 768
COL_AK = 1024
COL_AV = 1280
COL_BZ = 1536
COL_CQ = 1792
COL_CF = 2048
COL_CI = 2304
COL_CG = 2560
COL_DX = 2816
COL_DT = 3072
PROJ_W = 3328

VMEM_LIMIT = 56 * 1024 * 1024


def _cparams(sem):
    return pltpu.CompilerParams(dimension_semantics=sem, vmem_limit_bytes=VMEM_LIMIT)


def _rms(x, w):
    return x * lax.rsqrt(jnp.mean(x * x, axis=-1, keepdims=True) + EPS) * w


def _silu(x):
    return x * jax.nn.sigmoid(x)


def _dot(a, b):
    return jnp.dot(a, b, preferred_element_type=F32)


def _dot_nt(a, b):
    return lax.dot_general(a, b, (((1,), (1,)), ((), ())), preferred_element_type=F32)


def _dot_tn(a, b):
    return lax.dot_general(a, b, (((0,), (0,)), ((), ())), preferred_element_type=F32)


def _split3(x):
    hi = x.astype(BF16)
    r = x - hi.astype(F32)
    mid = r.astype(BF16)
    lo = (r - mid.astype(F32)).astype(BF16)
    return hi, mid, lo


def _dot01_lhs(m01, x):
    hi, mid, lo = _split3(x)
    return _dot(m01, hi) + _dot(m01, mid) + _dot(m01, lo)


def _dot01_rhs(x, m01):
    hi, mid, lo = _split3(x)
    return _dot(hi, m01) + _dot(mid, m01) + _dot(lo, m01)


def _tril(t):
    r = lax.broadcasted_iota(jnp.int32, (t, t), 0)
    c = lax.broadcasted_iota(jnp.int32, (t, t), 1)
    return r >= c


def _head_ones():
    r = lax.broadcasted_iota(jnp.int32, (GROUP_WIDTH, GROUP_WIDTH), 0) // HEAD_DIM
    c = lax.broadcasted_iota(jnp.int32, (GROUP_WIDTH, GROUP_WIDTH), 1) // HEAD_DIM
    return r == c


IN_WIDTH = 3076
ORIG_DT = 1792
W_PREP_COLS = 256


def _w_in_prep_kernel(wt_ref, o_ref):
    def put(c0, rows):
        o_ref[:, c0:c0 + W_PREP_COLS] = rows.T.astype(BF16)

    for j in range(3):
        put(COL_XBC + W_PREP_COLS * j, wt_ref[0, 1024 + W_PREP_COLS * j:1024 + W_PREP_COLS * (j + 1), :])
    for j in range(4):
        put(COL_AQ + W_PREP_COLS * j, wt_ref[0, W_PREP_COLS * j:W_PREP_COLS * (j + 1), :])
    for j in range(5):
        r0 = ORIG_DT + HEADS + W_PREP_COLS * j
        put(COL_CQ + W_PREP_COLS * j, wt_ref[0, r0:r0 + W_PREP_COLS, :])
    put(COL_DT, jnp.concatenate([jnp.broadcast_to(wt_ref[0, ORIG_DT + h:ORIG_DT + h + 1, :], (HEAD_DIM, D_MODEL))
                                 for h in range(HEADS)], axis=0))


def _w_in_prep(wt_all, layer):
    return pl.pallas_call(
        _w_in_prep_kernel,
        out_shape=jax.ShapeDtypeStruct((D_MODEL, PROJ_W), BF16),
        grid=(1,),
        in_specs=[pl.BlockSpec((1, IN_WIDTH, D_MODEL), lambda i: (layer, 0, 0), pipeline_mode=pl.Buffered(1))],
        out_specs=pl.BlockSpec((D_MODEL, PROJ_W), lambda i: (0, 0)),
        compiler_params=_cparams(("arbitrary",)),
        name="w_in_prep",
    )(wt_all)


def _inproj_kernel(x_ref, n_ref, w_ref, o_ref):
    hb = _rms(x_ref[...], n_ref[...]).astype(BF16)
    for c0 in range(0, PROJ_W, 256):
        o_ref[:, c0:c0 + 256] = _dot(hb, w_ref[:, c0:c0 + 256])


def _inproj(x2d, n1, w_in):
    rows = x2d.shape[0]
    tm = next((m for m in (1024, 512) if rows % m == 0), rows)
    return pl.pallas_call(
        _inproj_kernel,
        out_shape=jax.ShapeDtypeStruct((rows, PROJ_W), F32),
        grid=(rows // tm,),
        in_specs=[pl.BlockSpec((tm, D_MODEL), lambda i: (i, 0)),
                  pl.BlockSpec((1, D_MODEL), lambda i: (0, 0)),
                  pl.BlockSpec((D_MODEL, PROJ_W), lambda i: (0, 0), pipeline_mode=pl.Buffered(1))],
        out_specs=pl.BlockSpec((tm, PROJ_W), lambda i: (i, 0)),
        compiler_params=_cparams(("parallel",)),
        name="inproj",
    )(x2d, n1, w_in)


FF_CHUNK = 256


def _mlp_kernel(x_ref, oa_ref, ob_ref, oc_ref, od_ref, wo_ref, n2_ref, wgu_ref, wdn_ref, nf_ref, o_ref, mix_ref,
                act_ref, *, final):
    for m, r in enumerate((oa_ref, ob_ref, oc_ref, od_ref)):
        mix_ref[:, m * GROUP_WIDTH:(m + 1) * GROUP_WIDTH] = r[...].astype(BF16)
    x1 = x_ref[...] + _dot(mix_ref[...], wo_ref[0])
    hb = _rms(x1, n2_ref[...]).astype(BF16)
    for c0 in range(0, D_FF, FF_CHUNK):
        gate = _dot(hb, wgu_ref[0, :, c0:c0 + FF_CHUNK])
        up = _dot(hb, wgu_ref[0, :, D_FF + c0:D_FF + c0 + FF_CHUNK])
        act_ref[:, c0:c0 + FF_CHUNK] = (_silu(gate) * up).astype(BF16)
    out = x1 + _dot(act_ref[...], wdn_ref[0])
    if final:
        out = _rms(out, nf_ref[...])
    o_ref[...] = out


def _mlp(x2d, mix, layer, w_out, n2, w_gu, w_dn, nf, final):
    rows = x2d.shape[0]
    tm = 512 if rows % 512 == 0 else rows
    row_spec = lambda w: pl.BlockSpec((tm, w), lambda i: (i, 0))
    vec = lambda a: pl.BlockSpec(a.shape, lambda i: (0, 0))
    wspec = lambda a: pl.BlockSpec((1,) + a.shape[1:], lambda i: (layer, 0, 0), pipeline_mode=pl.Buffered(1))
    return pl.pallas_call(
        functools.partial(_mlp_kernel, final=final),
        out_shape=jax.ShapeDtypeStruct((rows, D_MODEL), F32),
        grid=(rows // tm,),
        in_specs=[row_spec(D_MODEL)] + [row_spec(GROUP_WIDTH)] * 4
                 + [wspec(w_out), vec(n2), wspec(w_gu), wspec(w_dn), vec(nf)],
        out_specs=row_spec(D_MODEL),
        scratch_shapes=[pltpu.VMEM((tm, D_MODEL), BF16), pltpu.VMEM((tm, D_FF), BF16)],
        compiler_params=_cparams(("parallel",)),
        name="mlp",
    )(x2d, *mix, w_out, n2, w_gu, w_dn, nf)


ATTN_BLOCK = 1024
ATTN_UNROLL = 8


def _attn_kernel(q_ref, kh_ref, kc_ref, vh_ref, vc_ref, bias_ref, o_ref, kb_ref, vb_ref, *, qb, p_hist, l_valid):
    i = pl.program_id(1)
    if p_hist == 0:
        @pl.when(i == 0)
        def _():
            kb_ref[0:A_BAND_PREV, :] = jnp.zeros((A_BAND_PREV, GROUP_WIDTH), BF16)
            vb_ref[0:A_BAND_PREV, :] = jnp.zeros((A_BAND_PREV, GROUP_WIDTH), BF16)

        @pl.when(i > 0)
        def _():
            kb_ref[0:A_BAND_PREV, :] = kh_ref[0].astype(BF16)
            vb_ref[0:A_BAND_PREV, :] = vh_ref[0].astype(BF16)
    else:
        kb_ref[0:A_BAND_PREV, :] = kh_ref[0].astype(BF16)
        vb_ref[0:A_BAND_PREV, :] = vh_ref[0].astype(BF16)
    kb_ref[A_BAND_PREV:A_BAND_PREV + qb, :] = kc_ref[0].astype(BF16)
    vb_ref[A_BAND_PREV:A_BAND_PREV + qb, :] = vc_ref[0].astype(BF16)
    lane_head = lax.broadcasted_iota(jnp.int32, (CHUNK, GROUP_WIDTH), 1) // HEAD_DIM
    band_pos = lax.broadcasted_iota(jnp.int32, (1, A_BAND), 1)

    n_chunks = qb // CHUNK
    group = ATTN_UNROLL if n_chunks % ATTN_UNROLL == 0 else 1

    def chunks(gi, carry):
        r0s = [pl.multiple_of((gi * group + u) * CHUNK, CHUNK) for u in range(group)]
        es, dens = [], []
        for r0 in r0s:
            qc = q_ref[0, pl.ds(r0, CHUNK), :] * (HEAD_DIM ** -0.5)
            qs = jnp.concatenate([jnp.where(lane_head == h, qc, 0.0) for h in range(HEADS)], axis=0).astype(BF16)
            s = _dot_nt(qs, kb_ref[pl.ds(r0, A_BAND), :]) + bias_ref[...]
            kpos = i * qb + r0 - A_BAND_PREV + band_pos
            s = jnp.where((kpos >= -p_hist) & (kpos < l_valid), s, NEG)
            e = jnp.exp(s - jnp.max(s, axis=-1, keepdims=True))
            dens.append(jnp.sum(e, axis=-1, keepdims=True))
            es.append(e.astype(BF16))
        o_alls = [_dot(es[u], vb_ref[pl.ds(r0s[u], A_BAND), :]) * (1.0 / dens[u]) for u in range(group)]
        for u in range(group):
            o = jnp.zeros((CHUNK, GROUP_WIDTH), F32)
            for h in range(HEADS):
                o = o + jnp.where(lane_head == h, o_alls[u][h * CHUNK:(h + 1) * CHUNK, :], 0.0)
            o_ref[0, pl.ds(r0s[u], CHUNK), :] = o
        return carry

    lax.fori_loop(0, n_chunks // group, chunks, 0)


def _attention(proj3, k_cache, v_cache, bias, l_valid):
    b = proj3.shape[0]
    lq = _round_up(l_valid, CHUNK)
    qb = next((q for q in (ATTN_BLOCK, A_BAND_PREV) if lq % q == 0), lq)
    nblk = lq // qb
    col = lambda c: pl.BlockSpec((1, qb, GROUP_WIDTH), lambda bi, i: (bi, i, c // GROUP_WIDTH))
    if k_cache is None:
        assert qb % A_BAND_PREV == 0
        p_hist = 0
        per = qb // A_BAND_PREV
        prev = lambda c: pl.BlockSpec((1, A_BAND_PREV, GROUP_WIDTH),
                                      lambda bi, i: (bi, jnp.maximum(i * per - 1, 0), c // GROUP_WIDTH))
        kh, vh, kh_spec, vh_spec = proj3, proj3, prev(COL_AK), prev(COL_AV)
    else:
        assert nblk == 1 and k_cache.shape[1] == A_BAND_PREV
        p_hist = A_BAND_PREV
        kh_spec = vh_spec = pl.BlockSpec((1, A_BAND_PREV, GROUP_WIDTH), lambda bi, i: (bi, 0, 0))
        kh, vh = k_cache, v_cache
    return pl.pallas_call(
        functools.partial(_attn_kernel, qb=qb, p_hist=p_hist, l_valid=l_valid),
        out_shape=jax.ShapeDtypeStruct((b, lq, GROUP_WIDTH), F32),
        grid=(b, nblk),
        in_specs=[col(COL_AQ), kh_spec, col(COL_AK), vh_spec, col(COL_AV),
                  pl.BlockSpec((HEADS * CHUNK, A_BAND), lambda bi, i: (0, 0))],
        out_specs=pl.BlockSpec((1, qb, GROUP_WIDTH), lambda bi, i: (bi, i, 0)),
        scratch_shapes=[pltpu.VMEM((A_BAND_PREV + qb, GROUP_WIDTH), BF16),
                        pltpu.VMEM((A_BAND_PREV + qb, GROUP_WIDTH), BF16)],
        compiler_params=_cparams(("parallel", "arbitrary")),
        name="attention",
    )(proj3, kh, proj3, vh, proj3, bias)


def _attn_cached_kernel(q_ref, kn_ref, vn_ref, kt_ref, vt_ref, bias_ref, o_ref, kt_out_ref, vt_out_ref, *, l_new):
    lane_head = lax.broadcasted_iota(jnp.int32, (l_new, GROUP_WIDTH), 1) // HEAD_DIM
    q = q_ref[0] * (HEAD_DIM ** -0.5)
    qs = jnp.concatenate([jnp.where(lane_head == h, q, 0.0) for h in range(HEADS)], axis=0).astype(BF16)
    kn = kn_ref[0]
    vn = vn_ref[0]
    kt = kt_ref[0, 0]
    vt = vt_ref[0, 0]
    s_old = _dot(qs, kt.astype(BF16)) + bias_ref[:, 0:A_BAND_PREV]
    s_new = _dot_nt(qs, kn.astype(BF16)) + bias_ref[:, A_BAND_PREV:A_BAND_PREV + l_new]
    m = jnp.maximum(jnp.max(s_old, axis=-1, keepdims=True), jnp.max(s_new, axis=-1, keepdims=True))
    e_old = jnp.exp(s_old - m)
    e_new = jnp.exp(s_new - m)
    den = jnp.sum(e_old, axis=-1, keepdims=True) + jnp.sum(e_new, axis=-1, keepdims=True)
    o_all = (_dot_nt(e_old.astype(BF16), vt.astype(BF16)) + _dot(e_new.astype(BF16), vn.astype(BF16))) * (1.0 / den)
    o = jnp.zeros((l_new, GROUP_WIDTH), F32)
    for h in range(HEADS):
        o = o + jnp.where(lane_head == h, o_all[h * l_new:(h + 1) * l_new, :], 0.0)
    o_ref[0] = o

    keep = A_BAND_PREV - l_new
    lane = lax.broadcasted_iota(jnp.int32, (GROUP_WIDTH, A_BAND_PREV), 1)
    r = lax.broadcasted_iota(jnp.int32, (l_new, A_BAND_PREV), 0)
    c = lax.broadcasted_iota(jnp.int32, (l_new, A_BAND_PREV), 1)
    place = (c == r + keep).astype(BF16)

    def appended(old_t, new_rows):
        hi, mid, lo = _split3(new_rows)
        new_cols = _dot_tn(hi, place) + _dot_tn(mid, place) + _dot_tn(lo, place)
        return jnp.where(lane < keep, pltpu.roll(old_t, keep, 1), new_cols)

    kt_out_ref[0] = appended(kt, kn)
    vt_out_ref[0] = appended(vt, vn)


def _attention_cached(proj3, kt_cache, vt_cache, layer, bias, l_valid):
    b = proj3.shape[0]
    col = lambda c: pl.BlockSpec((1, l_valid, GROUP_WIDTH), lambda bi: (bi, 0, c // GROUP_WIDTH))
    cache_spec = pl.BlockSpec((1, 1, GROUP_WIDTH, A_BAND_PREV), lambda bi: (layer, bi, 0, 0))
    out_t = pl.BlockSpec((1, GROUP_WIDTH, A_BAND_PREV), lambda bi: (bi, 0, 0))
    return pl.pallas_call(
        functools.partial(_attn_cached_kernel, l_new=l_valid),
        out_shape=(jax.ShapeDtypeStruct((b, l_valid, GROUP_WIDTH), F32),
                   jax.ShapeDtypeStruct((b, GROUP_WIDTH, A_BAND_PREV), F32),
                   jax.ShapeDtypeStruct((b, GROUP_WIDTH, A_BAND_PREV), F32)),
        grid=(b,),
        in_specs=[col(COL_AQ), col(COL_AK), col(COL_AV), cache_spec, cache_spec,
                  pl.BlockSpec((HEADS * l_valid, A_BAND), lambda bi: (0, 0))],
        out_specs=(pl.BlockSpec((1, l_valid, GROUP_WIDTH), lambda bi: (bi, 0, 0)), out_t, out_t),
        compiler_params=_cparams(("parallel",)),
        name="attention_cached",
    )(proj3, proj3, proj3, kt_cache, vt_cache, bias)


CONV_PAD = 8
SSD_CHUNK = 128


def _ssd_kernel(xbc_ref, z_ref, dt_ref, conv0_ref, ssm0_ref, cw_ref, cb_ref, dtb_ref, alog_ref, d_ref, nrm_ref,
                o_ref, conv_out_ref, ssm_out_ref, xpad_ref, st_ref, *, t_blk, l_valid, n_blk):
    t = pl.program_id(1)
    hist = B_CONV - 1

    @pl.when(t == 0)
    def _():
        xpad_ref[CONV_PAD - hist:CONV_PAD, :] = conv0_ref[0]
        st_ref[...] = ssm0_ref[0].reshape(GROUP_WIDTH, B_STATE).T

    n_last = l_valid - (n_blk - 1) * t_blk
    conv_out_ref[0] = xbc_ref[0, n_last - hist:n_last, :]

    tc = SSD_CHUNK
    short = l_valid < t_blk

    def rows_of(ref, sl):
        if short:
            return jnp.concatenate([ref[0], jnp.zeros((tc - l_valid, ref.shape[2]), F32)], axis=0)
        return ref[0, sl, :]

    tril = _tril(tc)
    tril_b = tril.astype(BF16)
    lane = lax.broadcasted_iota(jnp.int32, (tc, B_STATE), 1)
    rows = lax.broadcasted_iota(jnp.int32, (tc, GROUP_WIDTH), 0)
    a_neg = -jnp.exp(alog_ref[...])
    conv_b = cb_ref[...]
    conv_w = cw_ref[...]
    dt_bias = dtb_ref[...]
    d_skip = d_ref[...]
    nrm = nrm_ref[...]

    def chunk(j, carry):
        r0 = pl.multiple_of(j * tc, tc)
        sl = pl.ds(r0, tc)
        xpad_ref[CONV_PAD:CONV_PAD + tc, :] = rows_of(xbc_ref, sl)
        y = conv_b
        for i in range(B_CONV):
            y = y + xpad_ref[CONV_PAD - hist + i:CONV_PAD - hist + i + tc, :] * conv_w[i:i + 1, :]
        xbc = _silu(y)
        xpad_ref[CONV_PAD - hist:CONV_PAD, :] = xpad_ref[CONV_PAD + tc - hist:CONV_PAD + tc, :]

        xs = xbc[:, 0:GROUP_WIDTH]
        bm = xbc[:, GROUP_WIDTH:2 * GROUP_WIDTH]
        cm = xbc[:, 2 * GROUP_WIDTH:3 * GROUP_WIDTH]
        dtr = rows_of(dt_ref, sl) + dt_bias
        dt = jnp.maximum(dtr, 0.0) + jnp.log1p(jnp.exp(-jnp.abs(dtr)))
        if l_valid < n_blk * t_blk:
            dt = jnp.where(t * t_blk + r0 + rows < l_valid, dt, 0.0)
        a = dt * a_neg
        u = xs * dt
        acs = _dot01_lhs(tril_b, a)
        acs_t = acs.T
        eacs = jnp.exp(acs)
        last = acs[tc - 1:tc, :]
        elast = jnp.exp(last)
        ub = u.astype(BF16)
        upb = (u * jnp.exp(last - acs)).astype(BF16)
        ys = []
        for g in range(2):
            gs = slice(g * B_STATE, (g + 1) * B_STATE)
            cg = cm[:, gs].astype(BF16)
            bg = bm[:, gs].astype(BF16)
            cb = _dot_nt(cg, bg)
            y_state = _dot(cg, st_ref[:, gs].astype(BF16)) * eacs[:, gs]
            parts = []
            for hh in range(2):
                c0 = (2 * g + hh) * HEAD_DIM
                col = jnp.broadcast_to(acs[:, c0:c0 + 1], (tc, tc))
                rowv = jnp.broadcast_to(acs_t[c0:c0 + 1, :], (tc, tc))
                dec = jnp.exp(jnp.where(tril, col - rowv, NEG))
                parts.append(_dot((cb * dec).astype(BF16), ub[:, gs]))
            ys.append(jnp.where(lane < HEAD_DIM, parts[0], parts[1]) + y_state)
            st_ref[:, gs] = st_ref[:, gs] * elast[:, gs] + _dot_tn(bg, upb[:, gs])
        yv = jnp.concatenate(ys, axis=1) + d_skip * xs
        res = _rms(yv * _silu(rows_of(z_ref, sl)), nrm)
        if short:
            o_ref[0] = res[0:l_valid, :]
        else:
            o_ref[0, sl, :] = res
        return carry

    lax.fori_loop(0, t_blk // tc, chunk, 0)

    @pl.when(t == n_blk - 1)
    def _():
        ssm_out_ref[0] = st_ref[...].T.reshape(HEADS, HEAD_DIM, B_STATE)


def _ssd(proj3, conv0, ssm0, cw, cb, dtb, alog, dvec, nrm, l_valid, t_blk):
    b, lrows, _ = proj3.shape
    n_blk = -(-l_valid // t_blk)
    assert t_blk % SSD_CHUNK == 0 and lrows == l_valid
    assert l_valid == n_blk * t_blk or (t_blk == SSD_CHUNK and B_CONV - 1 <= l_valid < t_blk and l_valid % 8 == 0)
    t_in = min(t_blk, l_valid)
    vec = lambda w: pl.BlockSpec((1, w), lambda bi, t: (0, 0))
    return pl.pallas_call(
        functools.partial(_ssd_kernel, t_blk=t_blk, l_valid=l_valid, n_blk=n_blk),
        out_shape=(jax.ShapeDtypeStruct((b, l_valid, GROUP_WIDTH), F32),
                   jax.ShapeDtypeStruct((b, B_CONV - 1, B_CONV_DIM), F32),
                   jax.ShapeDtypeStruct((b, HEADS, HEAD_DIM, B_STATE), F32)),
        grid=(b, n_blk),
        in_specs=[pl.BlockSpec((1, t_in, B_CONV_DIM), lambda bi, t: (bi, t, COL_XBC // B_CONV_DIM)),
                  pl.BlockSpec((1, t_in, GROUP_WIDTH), lambda bi, t: (bi, t, COL_BZ // GROUP_WIDTH)),
                  pl.BlockSpec((1, t_in, GROUP_WIDTH), lambda bi, t: (bi, t, COL_DT // GROUP_WIDTH)),
                  pl.BlockSpec((1, B_CONV - 1, B_CONV_DIM), lambda bi, t: (bi, 0, 0)),
                  pl.BlockSpec((1, HEADS, HEAD_DIM, B_STATE), lambda bi, t: (bi, 0, 0, 0)),
                  pl.BlockSpec((B_CONV, B_CONV_DIM), lambda bi, t: (0, 0)),
                  vec(B_CONV_DIM), vec(GROUP_WIDTH), vec(GROUP_WIDTH), vec(GROUP_WIDTH), vec(GROUP_WIDTH)],
        out_specs=(pl.BlockSpec((1, t_in, GROUP_WIDTH), lambda bi, t: (bi, t, 0)),
                   pl.BlockSpec((1, B_CONV - 1, B_CONV_DIM), lambda bi, t: (bi, 0, 0)),
                   pl.BlockSpec((1, HEADS, HEAD_DIM, B_STATE), lambda bi, t: (bi, 0, 0, 0))),
        scratch_shapes=[pltpu.VMEM((CONV_PAD + SSD_CHUNK, B_CONV_DIM), F32),
                        pltpu.VMEM((B_STATE, GROUP_WIDTH), F32)],
        compiler_params=_cparams(("parallel", "arbitrary")),
        name="ssd",
    )(proj3, proj3, proj3, conv0, ssm0, cw, cb, dtb, alog, dvec, nrm)


HGRN_INFLIGHT = 8
HGRN_SAFE_LOG2 = 64.0


def _hgrn_kernel(q_ref, f_ref, i_ref, g_ref, st0_ref, lb_ref, nrm_ref, o_ref, st_out_ref, st_ref, b_blk, c_blk,
                 y_scr, *, t_blk, l_valid, n_blk):
    t = pl.program_id(1)

    @pl.when(t == 0)
    def _():
        for h in range(HEADS):
            parts = [jnp.zeros((HEAD_DIM, HEAD_DIM * h), F32)] if h > 0 else []
            parts.append(st0_ref[0, h].T)
            if h < HEADS - 1:
                parts.append(jnp.zeros((HEAD_DIM, HEAD_DIM * (HEADS - 1 - h)), F32))
            st_ref[h * HEAD_DIM:(h + 1) * HEAD_DIM, :] = jnp.concatenate(parts, axis=1)

    lb = lb_ref[...]
    nrm = nrm_ref[...]
    head_eq = _head_ones()
    ones_bd = head_eq.astype(BF16)
    tril = _tril(CHUNK).astype(BF16)
    rows = lax.broadcasted_iota(jnp.int32, (CHUNK, GROUP_WIDTH), 0)
    row_in_sub = rows % SUB
    lane_head = lax.broadcasted_iota(jnp.int32, (SUB, GROUP_WIDTH), 1) // HEAD_DIM
    n_sub = CHUNK // SUB
    n_chunks = t_blk // CHUNK
    sc_row = lax.broadcasted_iota(jnp.int32, (HEADS * SUB, CHUNK), 0) % SUB
    sc_col = lax.broadcasted_iota(jnp.int32, (HEADS * SUB, CHUNK), 1)

    def gates(cc, gmax):
        r0 = pl.multiple_of(cc * CHUNK, CHUNK)
        sl = pl.ds(r0, CHUNK)
        f = lb + (1.0 - lb) * jax.nn.sigmoid(f_ref[0, sl, :])
        kk = jnp.maximum(1.0 - f, 0.0)
        lf2 = jnp.log(f) * LOG2E
        if l_valid < n_blk * t_blk:
            ok = t * t_blk + r0 + rows < l_valid
            kk = jnp.where(ok, kk, 0.0)
            lf2 = jnp.where(ok, lf2, 0.0)
        bcs2 = _dot01_lhs(tril, lf2)
        b_blk[sl, :] = bcs2
        c_blk[sl, :] = jnp.log(kk) * LOG2E - bcs2
        prev = jnp.zeros((1, GROUP_WIDTH), F32)
        for a in range(n_sub):
            end = bcs2[(a + 1) * SUB - 1:(a + 1) * SUB, :]
            gmax = jnp.maximum(gmax, prev - end)
            prev = end
        return gmax

    gmax = lax.fori_loop(0, n_chunks, gates, jnp.zeros((1, GROUP_WIDTH), F32),
                         unroll=HGRN_INFLIGHT if n_chunks % HGRN_INFLIGHT == 0 else 1)
    factored_ok = jnp.max(gmax) <= HGRN_SAFE_LOG2

    def stack_heads(x):
        return jnp.concatenate([jnp.where(lane_head == h, x, 0.0) for h in range(HEADS)], axis=0).astype(BF16)

    def unstack_heads(x_all):
        out = jnp.zeros((SUB, GROUP_WIDTH), F32)
        for h in range(HEADS):
            out = out + jnp.where(lane_head == h, x_all[h * SUB:(h + 1) * SUB, :], 0.0)
        return out

    def chunks(ccs, factored):
        n = len(ccs)
        r0s = [pl.multiple_of(cc * CHUNK, CHUNK) for cc in ccs]
        sls = [pl.ds(r0, CHUNK) for r0 in r0s]
        qs = [q_ref[0, sl, :] for sl in sls]
        vs = [i_ref[0, sl, :] for sl in sls]
        bs = [b_blk[sl, :] for sl in sls]
        cs = [c_blk[sl, :] for sl in sls]
        vbs = [v.astype(BF16) for v in vs]

        os_ = []
        for s in range(n):
            os_.append(_dot_nt((qs[s] * jnp.exp2(bs[s])).astype(BF16), st_ref[...].astype(BF16)))
            last = bs[s][CHUNK - 1:CHUNK, :]
            kt = jnp.exp2(last + cs[s]).astype(BF16)
            st_ref[...] = st_ref[...] * jnp.exp2(last) + jnp.where(head_eq, _dot_tn(vbs[s], kt), 0.0)

        pieces = [[] for _ in range(n)]
        for a in range(n_sub):
            s0 = a * SUB
            s_end = s0 + SUB if factored else s0
            if s_end == 0:
                for s in range(n):
                    pieces[s].append(jnp.zeros((SUB, GROUP_WIDTH), F32))
                continue
            scs = []
            for s in range(n):
                ref_row = bs[s][s0 - 1:s0, :] if a > 0 else jnp.zeros((1, GROUP_WIDTH), F32)
                qa = qs[s][s0:s0 + SUB, :] * jnp.exp2(bs[s][s0:s0 + SUB, :] - ref_row)
                ka = jnp.where(rows < s_end, jnp.exp2(ref_row + cs[s]), 0.0)
                sc = _dot_nt(stack_heads(qa), ka.astype(BF16))
                if factored:
                    sc = jnp.where(sc_col <= s0 + sc_row, sc, 0.0)
                scs.append(sc.astype(BF16))
            for s in range(n):
                pieces[s].append(unstack_heads(_dot(scs[s], vbs[s])))
        for s in range(n):
            os_[s] = os_[s] + jnp.concatenate(pieces[s], axis=0)

        if not factored:
            for s in range(n):
                def sub_rows(ref, jj):
                    return jnp.concatenate(
                        [jnp.broadcast_to(ref[pl.ds(r0s[s] + a * SUB + jj, 1), :], (SUB, GROUP_WIDTH))
                         for a in range(n_sub)], axis=0)

                for jj in range(SUB):
                    w = jnp.exp2(bs[s] + sub_rows(c_blk, jj))
                    y_scr[jj * CHUNK:(jj + 1) * CHUNK, :] = jnp.where(row_in_sub >= jj, qs[s] * w, 0.0).astype(BF16)
                att = _dot(y_scr[...], ones_bd)
                for jj in range(SUB):
                    os_[s] = os_[s] + att[jj * CHUNK:(jj + 1) * CHUNK, :] * sub_rows(i_ref.at[0], jj)

        mss = [_dot01_rhs(o * o, ones_bd) * (1.0 / HEAD_DIM) for o in os_]
        for s in range(n):
            o_ref[0, sls[s], :] = os_[s] * lax.rsqrt(mss[s] + EPS) * nrm * _silu(g_ref[0, sls[s], :])

    @pl.when(factored_ok)
    def _():
        inflight = HGRN_INFLIGHT if n_chunks % HGRN_INFLIGHT == 0 else 1

        def group(pp, carry):
            chunks([inflight * pp + s for s in range(inflight)], True)
            return carry

        lax.fori_loop(0, n_chunks // inflight, group, 0)

    @pl.when(jnp.logical_not(factored_ok))
    def _():
        def one(cc, carry):
            chunks([cc], False)
            return carry

        lax.fori_loop(0, n_chunks, one, 0)

    @pl.when(t == n_blk - 1)
    def _():
        for h in range(HEADS):
            hs = slice(h * HEAD_DIM, (h + 1) * HEAD_DIM)
            st_out_ref[0, h] = st_ref[hs, hs].T


def _hgrn(proj3, col0, st0, lb, nrm, l_valid, t_blk):
    b, lrows, _ = proj3.shape
    n_blk = -(-l_valid // t_blk)
    assert n_blk * t_blk <= lrows and t_blk % CHUNK == 0
    col = lambda c: pl.BlockSpec((1, t_blk, GROUP_WIDTH),
                                 lambda bi, t: (bi, t, (c - COL_CQ + col0) // GROUP_WIDTH))
    vec = pl.BlockSpec((1, GROUP_WIDTH), lambda bi, t: (0, 0))
    st_spec = pl.BlockSpec((1, HEADS, HEAD_DIM, HEAD_DIM), lambda bi, t: (bi, 0, 0, 0))
    return pl.pallas_call(
        functools.partial(_hgrn_kernel, t_blk=t_blk, l_valid=l_valid, n_blk=n_blk),
        out_shape=(jax.ShapeDtypeStruct((b, n_blk * t_blk, GROUP_WIDTH), F32),
                   jax.ShapeDtypeStruct((b, HEADS, HEAD_DIM, HEAD_DIM), F32)),
        grid=(b, n_blk),
        in_specs=[col(COL_CQ), col(COL_CF), col(COL_CI), col(COL_CG), st_spec, vec, vec],
        out_specs=(pl.BlockSpec((1, t_blk, GROUP_WIDTH), lambda bi, t: (bi, t, 0)), st_spec),
        scratch_shapes=[pltpu.VMEM((GROUP_WIDTH, GROUP_WIDTH), F32),
                        pltpu.VMEM((t_blk, GROUP_WIDTH), F32),
                        pltpu.VMEM((t_blk, GROUP_WIDTH), F32),
                        pltpu.VMEM((SUB * CHUNK, GROUP_WIDTH), BF16)],
        compiler_params=_cparams(("parallel", "arbitrary")),
        name="hgrn",
    )(proj3, proj3, proj3, proj3, st0, lb, nrm)


POOL_PAD = 24
POOL_FRONT = 8
D_GROUP = 64


def _pool_kernel(x_ref, buf_ref, w_ref, sc_ref, o_ref, buf_out_ref, xp_ref, s2_ref, s4_ref, *, t_blk):
    t = pl.program_id(1)
    hi = POOL_PAD + t_blk
    half = GROUP_WIDTH // 2

    @pl.when(t == 0)
    def _():
        zeros = jnp.zeros((POOL_PAD - D_BUF, GROUP_WIDTH), F32)
        xp_ref[0:POOL_PAD - D_BUF, :] = zeros
        s2_ref[0:POOL_FRONT, :] = zeros[0:POOL_FRONT]
        s4_ref[0:POOL_FRONT, :] = zeros[0:POOL_FRONT]
        xp_ref[POOL_PAD - D_BUF:POOL_PAD, :] = buf_ref[0]

    x = x_ref[0]
    xp_ref[POOL_PAD:hi, :] = x
    s2 = xp_ref[POOL_FRONT:hi, :] + xp_ref[POOL_FRONT - 1:hi - 1, :]
    s2_ref[POOL_FRONT:hi, :] = s2
    s4 = s2 + s2_ref[POOL_FRONT - 2:hi - 2, :]
    s4_ref[POOL_FRONT:hi, :] = s4
    s8 = s4[:, half:] + s4_ref[POOL_FRONT - 4:hi - 4, half:]
    n_ext = hi - POOL_FRONT
    s16 = s8[8:n_ext, :] + s8[0:n_ext - 8, :]
    lo_sum = s2[n_ext - t_blk:, :half]
    lane = lax.broadcasted_iota(jnp.int32, (t_blk, half), 1)
    pooled_lo = jnp.where(lane < D_GROUP, lo_sum * 0.5, s4[n_ext - t_blk:, :half] * 0.25)
    pooled_hi = jnp.where(lane < D_GROUP, s8[n_ext - t_blk:, :] * 0.125, s16[n_ext - 8 - t_blk:, :] * 0.0625)
    pooled = jnp.concatenate([pooled_lo, pooled_hi], axis=1) - x
    o_ref[0] = _dot(pooled.astype(BF16), w_ref[...]) * sc_ref[...]
    tail = xp_ref[hi - D_BUF:hi, :]
    buf_out_ref[0] = tail
    xp_ref[POOL_PAD - D_BUF:POOL_PAD, :] = tail


def _pool(proj3, buf, w_bd, scale, l_valid, t_blk):
    b = proj3.shape[0]
    n_blk = l_valid // t_blk
    assert n_blk * t_blk == l_valid and t_blk >= D_BUF
    buf_spec = pl.BlockSpec((1, D_BUF, GROUP_WIDTH), lambda bi, t: (bi, 0, 0))
    return pl.pallas_call(
        functools.partial(_pool_kernel, t_blk=t_blk),
        out_shape=(jax.ShapeDtypeStruct((b, l_valid, GROUP_WIDTH), F32),
                   jax.ShapeDtypeStruct((b, D_BUF, GROUP_WIDTH), F32)),
        grid=(b, n_blk),
        in_specs=[pl.BlockSpec((1, t_blk, GROUP_WIDTH), lambda bi, t: (bi, t, COL_DX // GROUP_WIDTH)),
                  buf_spec,
                  pl.BlockSpec((GROUP_WIDTH, GROUP_WIDTH), lambda bi, t: (0, 0)),
                  pl.BlockSpec((1, GROUP_WIDTH), lambda bi, t: (0, 0))],
        out_specs=(pl.BlockSpec((1, t_blk, GROUP_WIDTH), lambda bi, t: (bi, t, 0)), buf_spec),
        scratch_shapes=[pltpu.VMEM((POOL_PAD + t_blk, GROUP_WIDTH), F32)] * 3,
        compiler_params=_cparams(("parallel", "arbitrary")),
        name="pool",
    )(proj3, buf, w_bd, scale)


SSD_BLOCK = 1024
HGRN_BLOCK = 1024
POOL_BLOCK = 2048


def _prep_layer(layer, norm1, w_in, a_rel_bias, b_conv_w, b_conv_b, b_dt_bias, b_a_log, b_d, b_norm, lbs, c_norm,
                d_pool_w, d_pool_scale, w_out, norm2, w_gate_up, w_down):
    w_r = _w_in_prep(jnp.transpose(w_in, (0, 2, 1)), layer)
    tab = a_rel_bias[layer].T
    m = A_BAND + CHUNK
    far = tab[:, 2 * REL_CLIP:]
    n_far = A_BAND_PREV - REL_CLIP + 1
    r = jnp.concatenate([jnp.broadcast_to(far, (HEADS, n_far)),
                         tab[:, 2 * REL_CLIP - 1:CHUNK:-1],
                         jnp.broadcast_to(far, (HEADS, m - n_far - (2 * REL_CLIP - 1 - CHUNK)))], axis=1)
    bias = jnp.tile(r, (1, CHUNK))[:, :CHUNK * (m - 1)].reshape(HEADS, CHUNK, m - 1)[:, :, :A_BAND]
    bias = bias.reshape(HEADS * CHUNK, A_BAND)
    rep = lambda p: jnp.repeat(p[layer], HEAD_DIM)[None, :]
    pw = d_pool_w[layer]
    w_bd = jnp.zeros((GROUP_WIDTH, GROUP_WIDTH), F32)
    for g in range(4):
        w_bd = w_bd.at[g * 64:(g + 1) * 64, g * 64:(g + 1) * 64].set(pw[g])
    return dict(
        n1=norm1[layer][None, :], w_in=w_r, bias=bias,
        cw=b_conv_w[layer], cb=b_conv_b[layer][None, :], dtb=rep(b_dt_bias), alog=rep(b_a_log), dvec=rep(b_d),
        bnrm=b_norm[layer][None, :], lb=lbs[layer][None, :], cnrm=c_norm[layer].reshape(1, GROUP_WIDTH),
        w_bd=w_bd.astype(BF16), psc=d_pool_scale[layer][None, :], n2=norm2[layer][None, :], layer=layer,
        w_out=w_out.astype(BF16), w_gu=w_gate_up.astype(BF16), w_dn=w_down.astype(BF16))


def _round_up(n, m):
    return -(-n // m) * m


def _group_layer(x, st, p, nf, final, cache_t=None):
    k_prev, v_prev, conv0, ssm0, hgrn0, pool0 = st
    b, l, _ = x.shape
    p_hist = k_prev.shape[1]
    proj = _inproj(x.reshape(b * l, D_MODEL), p["n1"], p["w_in"]).reshape(b, l, PROJ_W)
    lq = _round_up(l, CHUNK)
    pad_rows = lambda a: a if lq == l else jnp.pad(a, ((0, 0), (0, lq - l), (0, 0)))

    keep = min(A_BAND_PREV, p_hist + l)
    n_new = min(l, keep)
    ka = proj[:, l - n_new:, COL_AK:COL_AK + GROUP_WIDTH].reshape(b, n_new, HEADS, HEAD_DIM)
    va = proj[:, l - n_new:, COL_AV:COL_AV + GROUP_WIDTH].reshape(b, n_new, HEADS, HEAD_DIM)
    if cache_t is not None:
        bias_s = p["bias"].reshape(HEADS, CHUNK, A_BAND)[:, :l].reshape(HEADS * l, A_BAND)
        oa, kt_new, vt_new = _attention_cached(proj, cache_t[0], cache_t[1], p["layer"], bias_s, l)
        untranspose = lambda a: a.reshape(b, HEADS, HEAD_DIM, A_BAND_PREV).transpose(0, 3, 1, 2)
        new_k, new_v = untranspose(kt_new), untranspose(vt_new)
    else:
        if p_hist == 0:
            k_cache = v_cache = None
        else:
            k_cache = k_prev.astype(BF16).reshape(b, p_hist, GROUP_WIDTH)
            v_cache = v_prev.astype(BF16).reshape(b, p_hist, GROUP_WIDTH)
        oa = _attention(pad_rows(proj), k_cache, v_cache, p["bias"], l)[:, :l]
        new_k = jnp.concatenate([k_prev[:, p_hist - (keep - n_new):], ka], axis=1)
        new_v = jnp.concatenate([v_prev[:, p_hist - (keep - n_new):], va], axis=1)

    ob, new_conv, new_ssm = _ssd(proj, conv0, ssm0, p["cw"], p["cb"], p["dtb"], p["alog"], p["dvec"], p["bnrm"],
                                 l, SSD_BLOCK if l % SSD_BLOCK == 0 else SSD_CHUNK)

    if l % HGRN_BLOCK == 0:
        oc, new_hgrn = _hgrn(proj, COL_CQ, hgrn0, p["lb"], p["cnrm"], l, HGRN_BLOCK)
    else:
        oc, new_hgrn = _hgrn(pad_rows(proj[:, :, COL_CQ:COL_CQ + 4 * GROUP_WIDTH]), 0, hgrn0, p["lb"], p["cnrm"], l, lq)
        oc = oc[:, :l]

    od, new_pool = _pool(proj, pool0, p["w_bd"], p["psc"], l, POOL_BLOCK if l % POOL_BLOCK == 0 else l)

    flat = lambda a: a.reshape(b * l, GROUP_WIDTH)
    y = _mlp(x.reshape(b * l, D_MODEL), (flat(oa), flat(ob), flat(oc), flat(od)), p["layer"],
             p["w_out"], p["n2"], p["w_gu"], p["w_dn"], nf, final)
    return y.reshape(b, l, D_MODEL), (new_k, new_v, new_conv, new_ssm, new_hgrn, new_pool)


def _trunk(x, states, params, nf):
    depth = len(params)
    new = [[] for _ in range(6)]
    b, l, _ = x.shape
    cache_t = None
    if states[0].shape[2] == A_BAND_PREV and l <= CHUNK and l % 8 == 0:
        as_t = lambda c: jnp.transpose(c, (0, 1, 3, 4, 2)).reshape(depth, b, GROUP_WIDTH, A_BAND_PREV)
        cache_t = (as_t(states[0]), as_t(states[1]))
    for layer in range(depth):
        st_l = tuple(s[layer] for s in states)
        x, ns = _group_layer(x, st_l, params[layer], nf, layer == depth - 1, cache_t)
        for i in range(6):
            new[i].append(ns[i])
    return x, tuple(jnp.stack(n, axis=0) for n in new)


def kernel(x_prompt, x_sample, cache_a_k, cache_a_v, state_b_conv, state_b_ssm, state_c_hgrn, state_d_pool, norm1,
           w_in, a_rel_bias, b_conv_w, b_conv_b, b_dt_bias, b_a_log, b_d, b_norm, c_lb_logits, c_norm, d_pool_w,
           d_pool_scale, w_out, norm2, w_gate_up, w_down, norm_f):
    depth = w_in.shape[0]
    lbs = jnp.cumsum(jax.nn.softmax(c_lb_logits.astype(F32), axis=0), axis=0)
    lbs = lbs - lbs[:1]
    params = [_prep_layer(layer, norm1, w_in, a_rel_bias, b_conv_w, b_conv_b, b_dt_bias, b_a_log, b_d, b_norm, lbs,
                          c_norm, d_pool_w, d_pool_scale, w_out, norm2, w_gate_up, w_down)
              for layer in range(depth)]
    nf = norm_f[None, :]
    bp = x_prompt.shape[0]
    prompt_states = (
        jnp.zeros((depth, bp, 0, HEADS, HEAD_DIM), F32),
        jnp.zeros((depth, bp, 0, HEADS, HEAD_DIM), F32),
        jnp.zeros((depth, bp, B_CONV - 1, B_CONV_DIM), F32),
        jnp.zeros((depth, bp, HEADS, HEAD_DIM, B_STATE), F32),
        jnp.zeros((depth, bp, HEADS, HEAD_DIM, HEAD_DIM), F32),
        jnp.zeros((depth, bp, D_BUF, GROUP_WIDTH), F32),
    )
    y_prompt, ps = _trunk(x_prompt, prompt_states, params, nf)
    sample_states = (cache_a_k, cache_a_v, state_b_conv, state_b_ssm, state_c_hgrn, state_d_pool)
    y_sample, ss = _trunk(x_sample, sample_states, params, nf)
    return (y_prompt, y_sample) + ps + ss
```

```python
import functools

import jax
import jax.numpy as jnp
from jax import lax
from jax.experimental import pallas as pl
from jax.experimental.pallas import tpu as pltpu

F32 = jnp.float32
BF16 = jnp.bfloat16

D_MODEL = 1024
GROUP_WIDTH = 256
HEADS = 4
HEAD_DIM = 64
CHUNK = 64
A_BAND_PREV = 512
A_BAND = A_BAND_PREV + CHUNK
REL_CLIP = 128
B_STATE = 128
B_CONV = 4
B_CONV_DIM = 768
D_BUF = 15
D_FF = 2816
EPS = 1e-6
NEG = -1e30
LOG2E = 1.4426950408889634
SUB = 16

COL_XBC = 0
COL_AQ = 768
COL_AK = 1024
COL_AV = 1280
COL_BZ = 1536
COL_CQ = 1792
COL_CF = 2048
COL_CI = 2304
COL_CG = 2560
COL_DX = 2816
COL_DT = 3072
PROJ_W = 3328

VMEM_LIMIT = 56 * 1024 * 1024


def _cparams(sem):
    return pltpu.CompilerParams(dimension_semantics=sem, vmem_limit_bytes=VMEM_LIMIT)


def _rms(x, w):
    return x * lax.rsqrt(jnp.mean(x * x, axis=-1, keepdims=True) + EPS) * w


def _silu(x):
    h = 0.5 * x
    return h + h * jnp.tanh(h)


def _dot(a, b):
    return jnp.dot(a, b, preferred_element_type=F32)


def _dot_nt(a, b):
    return lax.dot_general(a, b, (((1,), (1,)), ((), ())), preferred_element_type=F32)


def _dot_tn(a, b):
    return lax.dot_general(a, b, (((0,), (0,)), ((), ())), preferred_element_type=F32)


def _split3(x):
    hi = x.astype(BF16)
    r = x - hi.astype(F32)
    mid = r.astype(BF16)
    lo = (r - mid.astype(F32)).astype(BF16)
    return hi, mid, lo


def _dot01_lhs(m01, x):
    hi, mid, lo = _split3(x)
    return _dot(m01, hi) + _dot(m01, mid) + _dot(m01, lo)


def _dot01_rhs(x, m01):
    hi, mid, lo = _split3(x)
    return _dot(hi, m01) + _dot(mid, m01) + _dot(lo, m01)


def _tril(t):
    r = lax.broadcasted_iota(jnp.int32, (t, t), 0)
    c = lax.broadcasted_iota(jnp.int32, (t, t), 1)
    return r >= c


def _head_ones():
    r = lax.broadcasted_iota(jnp.int32, (GROUP_WIDTH, GROUP_WIDTH), 0) // HEAD_DIM
    c = lax.broadcasted_iota(jnp.int32, (GROUP_WIDTH, GROUP_WIDTH), 1) // HEAD_DIM
    return r == c


IN_WIDTH = 3076
ORIG_DT = 1792
W_PREP_COLS = 256


def _w_in_prep_kernel(wt_ref, o_ref):
    def put(c0, rows):
        o_ref[:, c0:c0 + W_PREP_COLS] = rows.T.astype(BF16)

    for j in range(3):
        put(COL_XBC + W_PREP_COLS * j, wt_ref[0, 1024 + W_PREP_COLS * j:1024 + W_PREP_COLS * (j + 1), :])
    for j in range(4):
        put(COL_AQ + W_PREP_COLS * j, wt_ref[0, W_PREP_COLS * j:W_PREP_COLS * (j + 1), :])
    for j in range(5):
        r0 = ORIG_DT + HEADS + W_PREP_COLS * j
        put(COL_CQ + W_PREP_COLS * j, wt_ref[0, r0:r0 + W_PREP_COLS, :])
    put(COL_DT, jnp.concatenate([jnp.broadcast_to(wt_ref[0, ORIG_DT + h:ORIG_DT + h + 1, :], (HEAD_DIM, D_MODEL))
                                 for h in range(HEADS)], axis=0))


def _w_in_prep(wt_all, layer):
    return pl.pallas_call(
        _w_in_prep_kernel,
        out_shape=jax.ShapeDtypeStruct((D_MODEL, PROJ_W), BF16),
        grid=(1,),
        in_specs=[pl.BlockSpec((1, IN_WIDTH, D_MODEL), lambda i: (layer, 0, 0), pipeline_mode=pl.Buffered(1))],
        out_specs=pl.BlockSpec((D_MODEL, PROJ_W), lambda i: (0, 0)),
        compiler_params=_cparams(("arbitrary",)),
        name="w_in_prep",
    )(wt_all)


def _inproj_kernel(x_ref, n_ref, w_ref, o_ref):
    hb = _rms(x_ref[...], n_ref[...]).astype(BF16)
    for c0 in range(0, PROJ_W, 256):
        o_ref[:, c0:c0 + 256] = _dot(hb, w_ref[:, c0:c0 + 256])


def _inproj(x2d, n1, w_in):
    rows = x2d.shape[0]
    tm = next((m for m in (1024, 512) if rows % m == 0), rows)
    return pl.pallas_call(
        _inproj_kernel,
        out_shape=jax.ShapeDtypeStruct((rows, PROJ_W), F32),
        grid=(rows // tm,),
        in_specs=[pl.BlockSpec((tm, D_MODEL), lambda i: (i, 0)),
                  pl.BlockSpec((1, D_MODEL), lambda i: (0, 0)),
                  pl.BlockSpec((D_MODEL, PROJ_W), lambda i: (0, 0), pipeline_mode=pl.Buffered(1))],
        out_specs=pl.BlockSpec((tm, PROJ_W), lambda i: (i, 0)),
        compiler_params=_cparams(("parallel",)),
        name="inproj",
    )(x2d, n1, w_in)


CONV_PAD = 8


def _causal_conv_silu(xpad_ref, rows, conv_w, conv_b):
    hist = B_CONV - 1
    y = conv_b
    for i in range(B_CONV):
        y = y + xpad_ref[CONV_PAD - hist + i:CONV_PAD - hist + i + rows, :] * conv_w[i:i + 1, :]
    return _silu(y)


def _softplus(x):
    return jnp.maximum(x, 0.0) + jnp.log1p(jnp.exp(-jnp.abs(x)))


FF_CHUNK = 256


def _mlp_kernel(x_ref, oa_ref, ob_ref, oc_ref, od_ref, wo_ref, n2_ref, wgu_ref, wdn_ref, nf_ref, o_ref, mix_ref,
                act_ref, *, final):
    for m, r in enumerate((oa_ref, ob_ref, oc_ref, od_ref)):
        mix_ref[:, m * GROUP_WIDTH:(m + 1) * GROUP_WIDTH] = r[...].astype(BF16)
    x1 = x_ref[...] + _dot(mix_ref[...], wo_ref[0])
    hb = _rms(x1, n2_ref[...]).astype(BF16)
    for c0 in range(0, D_FF, FF_CHUNK):
        gate = _dot(hb, wgu_ref[0, :, c0:c0 + FF_CHUNK])
        up = _dot(hb, wgu_ref[0, :, D_FF + c0:D_FF + c0 + FF_CHUNK])
        act_ref[:, c0:c0 + FF_CHUNK] = (_silu(gate) * up).astype(BF16)
    out = x1 + _dot(act_ref[...], wdn_ref[0])
    if final:
        out = _rms(out, nf_ref[...])
    o_ref[...] = out


def _mlp(x2d, mix, layer, w_out, n2, w_gu, w_dn, nf, final):
    rows = x2d.shape[0]
    tm = 512 if rows % 512 == 0 else rows
    row_spec = lambda w: pl.BlockSpec((tm, w), lambda i: (i, 0))
    vec = lambda a: pl.BlockSpec(a.shape, lambda i: (0, 0))
    wspec = lambda a: pl.BlockSpec((1,) + a.shape[1:], lambda i: (layer, 0, 0), pipeline_mode=pl.Buffered(1))
    return pl.pallas_call(
        functools.partial(_mlp_kernel, final=final),
        out_shape=jax.ShapeDtypeStruct((rows, D_MODEL), F32),
        grid=(rows // tm,),
        in_specs=[row_spec(D_MODEL)] + [row_spec(GROUP_WIDTH)] * 4
                 + [wspec(w_out), vec(n2), wspec(w_gu), wspec(w_dn), vec(nf)],
        out_specs=row_spec(D_MODEL),
        scratch_shapes=[pltpu.VMEM((tm, D_MODEL), BF16), pltpu.VMEM((tm, D_FF), BF16)],
        compiler_params=_cparams(("parallel",)),
        name="mlp",
    )(x2d, *mix, w_out, n2, w_gu, w_dn, nf)


ATTN_BLOCK = 1024
ATTN_UNROLL = 8


def _attn_kernel(q_ref, kh_ref, kc_ref, vh_ref, vc_ref, bias_ref, o_ref, kb_ref, vb_ref, *, qb, p_hist, l_valid):
    i = pl.program_id(1)
    if p_hist == 0:
        @pl.when(i == 0)
        def _():
            kb_ref[0:A_BAND_PREV, :] = jnp.zeros((A_BAND_PREV, GROUP_WIDTH), BF16)
            vb_ref[0:A_BAND_PREV, :] = jnp.zeros((A_BAND_PREV, GROUP_WIDTH), BF16)

        @pl.when(i > 0)
        def _():
            kb_ref[0:A_BAND_PREV, :] = kh_ref[0].astype(BF16)
            vb_ref[0:A_BAND_PREV, :] = vh_ref[0].astype(BF16)
    else:
        kb_ref[0:A_BAND_PREV, :] = kh_ref[0].astype(BF16)
        vb_ref[0:A_BAND_PREV, :] = vh_ref[0].astype(BF16)
    kb_ref[A_BAND_PREV:A_BAND_PREV + qb, :] = kc_ref[0].astype(BF16)
    vb_ref[A_BAND_PREV:A_BAND_PREV + qb, :] = vc_ref[0].astype(BF16)
    lane_head = lax.broadcasted_iota(jnp.int32, (CHUNK, GROUP_WIDTH), 1) // HEAD_DIM
    band_pos = lax.broadcasted_iota(jnp.int32, (1, A_BAND), 1)

    n_chunks = qb // CHUNK
    group = ATTN_UNROLL if n_chunks % ATTN_UNROLL == 0 else 1

    def chunks(gi, carry):
        r0s = [pl.multiple_of((gi * group + u) * CHUNK, CHUNK) for u in range(group)]
        es, dens = [], []
        for r0 in r0s:
            qc = q_ref[0, pl.ds(r0, CHUNK), :] * (HEAD_DIM ** -0.5)
            qs = jnp.concatenate([jnp.where(lane_head == h, qc, 0.0) for h in range(HEADS)], axis=0).astype(BF16)
            s = _dot_nt(qs, kb_ref[pl.ds(r0, A_BAND), :]) + bias_ref[...]
            kpos = i * qb + r0 - A_BAND_PREV + band_pos
            s = jnp.where((kpos >= -p_hist) & (kpos < l_valid), s, NEG)
            e = jnp.exp(s - jnp.max(s, axis=-1, keepdims=True))
            dens.append(jnp.sum(e, axis=-1, keepdims=True))
            es.append(e.astype(BF16))
        o_alls = [_dot(es[u], vb_ref[pl.ds(r0s[u], A_BAND), :]) * (1.0 / dens[u]) for u in range(group)]
        for u in range(group):
            o = jnp.zeros((CHUNK, GROUP_WIDTH), F32)
            for h in range(HEADS):
                o = o + jnp.where(lane_head == h, o_alls[u][h * CHUNK:(h + 1) * CHUNK, :], 0.0)
            o_ref[0, pl.ds(r0s[u], CHUNK), :] = o
        return carry

    lax.fori_loop(0, n_chunks // group, chunks, 0)


def _attention(proj3, k_cache, v_cache, bias, l_valid):
    b = proj3.shape[0]
    lq = _round_up(l_valid, CHUNK)
    qb = next((q for q in (ATTN_BLOCK, A_BAND_PREV) if lq % q == 0), lq)
    nblk = lq // qb
    col = lambda c: pl.BlockSpec((1, qb, GROUP_WIDTH), lambda bi, i: (bi, i, c // GROUP_WIDTH))
    if k_cache is None:
        assert qb % A_BAND_PREV == 0
        p_hist = 0
        per = qb // A_BAND_PREV
        prev = lambda c: pl.BlockSpec((1, A_BAND_PREV, GROUP_WIDTH),
                                      lambda bi, i: (bi, jnp.maximum(i * per - 1, 0), c // GROUP_WIDTH))
        kh, vh, kh_spec, vh_spec = proj3, proj3, prev(COL_AK), prev(COL_AV)
    else:
        assert nblk == 1 and k_cache.shape[1] == A_BAND_PREV
        p_hist = A_BAND_PREV
        kh_spec = vh_spec = pl.BlockSpec((1, A_BAND_PREV, GROUP_WIDTH), lambda bi, i: (bi, 0, 0))
        kh, vh = k_cache, v_cache
    return pl.pallas_call(
        functools.partial(_attn_kernel, qb=qb, p_hist=p_hist, l_valid=l_valid),
        out_shape=jax.ShapeDtypeStruct((b, lq, GROUP_WIDTH), F32),
        grid=(b, nblk),
        in_specs=[col(COL_AQ), kh_spec, col(COL_AK), vh_spec, col(COL_AV),
                  pl.BlockSpec((HEADS * CHUNK, A_BAND), lambda bi, i: (0, 0))],
        out_specs=pl.BlockSpec((1, qb, GROUP_WIDTH), lambda bi, i: (bi, i, 0)),
        scratch_shapes=[pltpu.VMEM((A_BAND_PREV + qb, GROUP_WIDTH), BF16),
                        pltpu.VMEM((A_BAND_PREV + qb, GROUP_WIDTH), BF16)],
        compiler_params=_cparams(("parallel", "arbitrary")),
        name="attention",
    )(proj3, kh, proj3, vh, proj3, bias)


def _attn_cached_kernel(q_ref, kn_ref, vn_ref, kt_ref, vt_ref, bias_ref, o_ref, kt_out_ref, vt_out_ref, *, l_new):
    lane_head = lax.broadcasted_iota(jnp.int32, (l_new, GROUP_WIDTH), 1) // HEAD_DIM
    q = q_ref[0] * (HEAD_DIM ** -0.5)
    qs = jnp.concatenate([jnp.where(lane_head == h, q, 0.0) for h in range(HEADS)], axis=0).astype(BF16)
    kn = kn_ref[0]
    vn = vn_ref[0]
    kt = kt_ref[0, 0]
    vt = vt_ref[0, 0]
    s_old = _dot(qs, kt.astype(BF16)) + bias_ref[:, 0:A_BAND_PREV]
    s_new = _dot_nt(qs, kn.astype(BF16)) + bias_ref[:, A_BAND_PREV:A_BAND_PREV + l_new]
    m = jnp.maximum(jnp.max(s_old, axis=-1, keepdims=True), jnp.max(s_new, axis=-1, keepdims=True))
    e_old = jnp.exp(s_old - m)
    e_new = jnp.exp(s_new - m)
    den = jnp.sum(e_old, axis=-1, keepdims=True) + jnp.sum(e_new, axis=-1, keepdims=True)
    o_all = (_dot_nt(e_old.astype(BF16), vt.astype(BF16)) + _dot(e_new.astype(BF16), vn.astype(BF16))) * (1.0 / den)
    o = jnp.zeros((l_new, GROUP_WIDTH), F32)
    for h in range(HEADS):
        o = o + jnp.where(lane_head == h, o_all[h * l_new:(h + 1) * l_new, :], 0.0)
    o_ref[0] = o

    keep = A_BAND_PREV - l_new
    lane = lax.broadcasted_iota(jnp.int32, (GROUP_WIDTH, A_BAND_PREV), 1)
    r = lax.broadcasted_iota(jnp.int32, (l_new, A_BAND_PREV), 0)
    c = lax.broadcasted_iota(jnp.int32, (l_new, A_BAND_PREV), 1)
    place = (c == r + keep).astype(BF16)

    def appended(old_t, new_rows):
        hi, mid, lo = _split3(new_rows)
        new_cols = _dot_tn(hi, place) + _dot_tn(mid, place) + _dot_tn(lo, place)
        return jnp.where(lane < keep, pltpu.roll(old_t, keep, 1), new_cols)

    kt_out_ref[0] = appended(kt, kn)
    vt_out_ref[0] = appended(vt, vn)


def _attention_cached(proj3, kt_cache, vt_cache, layer, bias, l_valid):
    b = proj3.shape[0]
    col = lambda c: pl.BlockSpec((1, l_valid, GROUP_WIDTH), lambda bi: (bi, 0, c // GROUP_WIDTH))
    cache_spec = pl.BlockSpec((1, 1, GROUP_WIDTH, A_BAND_PREV), lambda bi: (layer, bi, 0, 0))
    out_t = pl.BlockSpec((1, GROUP_WIDTH, A_BAND_PREV), lambda bi: (bi, 0, 0))
    return pl.pallas_call(
        functools.partial(_attn_cached_kernel, l_new=l_valid),
        out_shape=(jax.ShapeDtypeStruct((b, l_valid, GROUP_WIDTH), F32),
                   jax.ShapeDtypeStruct((b, GROUP_WIDTH, A_BAND_PREV), F32),
                   jax.ShapeDtypeStruct((b, GROUP_WIDTH, A_BAND_PREV), F32)),
        grid=(b,),
        in_specs=[col(COL_AQ), col(COL_AK), col(COL_AV), cache_spec, cache_spec,
                  pl.BlockSpec((HEADS * l_valid, A_BAND), lambda bi: (0, 0))],
        out_specs=(pl.BlockSpec((1, l_valid, GROUP_WIDTH), lambda bi: (bi, 0, 0)), out_t, out_t),
        compiler_params=_cparams(("parallel",)),
        name="attention_cached",
    )(proj3, proj3, proj3, kt_cache, vt_cache, bias)


SSD_CHUNK = 128
SSD_INFLIGHT = 4


def _ssd_kernel(xbc_ref, z_ref, dt_ref, conv0_ref, ssm0_ref, cw_ref, cb_ref, dtb_ref, alog_ref, d_ref, nrm_ref,
                o_ref, conv_out_ref, ssm_out_ref, xpad_ref, st_ref, *, t_blk, l_valid, n_blk):
    t = pl.program_id(1)
    hist = B_CONV - 1

    @pl.when(t == 0)
    def _():
        xpad_ref[CONV_PAD - hist:CONV_PAD, :] = conv0_ref[0]
        st_ref[...] = ssm0_ref[0].reshape(GROUP_WIDTH, B_STATE).T

    n_last = l_valid - (n_blk - 1) * t_blk
    conv_out_ref[0] = xbc_ref[0, n_last - hist:n_last, :]

    tc = SSD_CHUNK
    short = l_valid < t_blk

    def rows_of(ref, sl):
        if short:
            return jnp.concatenate([ref[0], jnp.zeros((tc - l_valid, ref.shape[2]), F32)], axis=0)
        return ref[0, sl, :]

    tril = _tril(tc)
    tril_b = tril.astype(BF16)
    lane = lax.broadcasted_iota(jnp.int32, (tc, B_STATE), 1)
    rows = lax.broadcasted_iota(jnp.int32, (tc, GROUP_WIDTH), 0)
    a_neg = -jnp.exp(alog_ref[...])
    conv_b = cb_ref[...]
    conv_w = cw_ref[...]
    dt_bias = dtb_ref[...]
    d_skip = d_ref[...]
    nrm = nrm_ref[...]

    n_chunks = t_blk // tc
    group = SSD_INFLIGHT if n_chunks % SSD_INFLIGHT == 0 else 1
    groups = (slice(0, B_STATE), slice(B_STATE, 2 * B_STATE))

    def chunks(gi, carry):
        rg = group * tc
        base = pl.multiple_of(gi * rg, rg)
        sl_all = pl.ds(base, rg)
        xpad_ref[CONV_PAD:CONV_PAD + rg, :] = rows_of(xbc_ref, sl_all)
        xbc_all = _causal_conv_silu(xpad_ref, rg, conv_w, conv_b)
        xpad_ref[CONV_PAD - hist:CONV_PAD, :] = xpad_ref[CONV_PAD + rg - hist:CONV_PAD + rg, :]
        dt_all = _softplus(rows_of(dt_ref, sl_all) + dt_bias)
        if l_valid < n_blk * t_blk:
            rows_all = lax.broadcasted_iota(jnp.int32, (rg, GROUP_WIDTH), 0)
            dt_all = jnp.where(t * t_blk + base + rows_all < l_valid, dt_all, 0.0)
        z_all = rows_of(z_ref, sl_all)
        xs_all = xbc_all[:, 0:GROUP_WIDTH]
        a_all = dt_all * a_neg
        u_all = xs_all * dt_all

        cut = lambda arr, s: arr[s * tc:(s + 1) * tc, :]
        acss = [_dot01_lhs(tril_b, cut(a_all, s)) for s in range(group)]
        cms = [cut(xbc_all[:, 2 * GROUP_WIDTH:3 * GROUP_WIDTH], s).astype(BF16) for s in range(group)]
        bms = [cut(xbc_all[:, GROUP_WIDTH:2 * GROUP_WIDTH], s).astype(BF16) for s in range(group)]
        cbs = [[_dot_nt(cms[s][:, gs], bms[s][:, gs]) for gs in groups] for s in range(group)]

        y_states = []
        for s in range(group):
            acs = acss[s]
            last = acs[tc - 1:tc, :]
            upb = (cut(u_all, s) * jnp.exp(last - acs)).astype(BF16)
            eacs = jnp.exp(acs)
            elast = jnp.exp(last)
            ys_s = []
            for gs in groups:
                ys_s.append(_dot(cms[s][:, gs], st_ref[:, gs].astype(BF16)) * eacs[:, gs])
                st_ref[:, gs] = st_ref[:, gs] * elast[:, gs] + _dot_tn(bms[s][:, gs], upb[:, gs])
            y_states.append(ys_s)

        for s in range(group):
            acs = acss[s]
            acs_t = acs.T
            ub = cut(u_all, s).astype(BF16)
            ys = []
            for g, gs in enumerate(groups):
                parts = []
                for hh in range(2):
                    c0 = (2 * g + hh) * HEAD_DIM
                    col = jnp.broadcast_to(acs[:, c0:c0 + 1], (tc, tc))
                    rowv = jnp.broadcast_to(acs_t[c0:c0 + 1, :], (tc, tc))
                    dec = jnp.exp(jnp.where(tril, col - rowv, NEG))
                    parts.append(_dot((cbs[s][g] * dec).astype(BF16), ub[:, gs]))
                ys.append(jnp.where(lane < HEAD_DIM, parts[0], parts[1]) + y_states[s][g])
            yv = jnp.concatenate(ys, axis=1) + d_skip * cut(xs_all, s)
            res = _rms(yv * _silu(cut(z_all, s)), nrm)
            if short:
                o_ref[0] = res[0:l_valid, :]
            else:
                o_ref[0, pl.ds(pl.multiple_of(base + s * tc, tc), tc), :] = res
        return carry

    lax.fori_loop(0, n_chunks // group, chunks, 0)

    @pl.when(t == n_blk - 1)
    def _():
        ssm_out_ref[0] = st_ref[...].T.reshape(HEADS, HEAD_DIM, B_STATE)


def _ssd(proj3, conv0, ssm0, cw, cb, dtb, alog, dvec, nrm, l_valid, t_blk):
    b, lrows, _ = proj3.shape
    n_blk = -(-l_valid // t_blk)
    assert t_blk % SSD_CHUNK == 0 and lrows == l_valid
    assert l_valid == n_blk * t_blk or (t_blk == SSD_CHUNK and B_CONV - 1 <= l_valid < t_blk and l_valid % 8 == 0)
    t_in = min(t_blk, l_valid)
    vec = lambda w: pl.BlockSpec((1, w), lambda bi, t: (0, 0))
    return pl.pallas_call(
        functools.partial(_ssd_kernel, t_blk=t_blk, l_valid=l_valid, n_blk=n_blk),
        out_shape=(jax.ShapeDtypeStruct((b, l_valid, GROUP_WIDTH), F32),
                   jax.ShapeDtypeStruct((b, B_CONV - 1, B_CONV_DIM), F32),
                   jax.ShapeDtypeStruct((b, HEADS, HEAD_DIM, B_STATE), F32)),
        grid=(b, n_blk),
        in_specs=[pl.BlockSpec((1, t_in, B_CONV_DIM), lambda bi, t: (bi, t, COL_XBC // B_CONV_DIM)),
                  pl.BlockSpec((1, t_in, GROUP_WIDTH), lambda bi, t: (bi, t, COL_BZ // GROUP_WIDTH)),
                  pl.BlockSpec((1, t_in, GROUP_WIDTH), lambda bi, t: (bi, t, COL_DT // GROUP_WIDTH)),
                  pl.BlockSpec((1, B_CONV - 1, B_CONV_DIM), lambda bi, t: (bi, 0, 0)),
                  pl.BlockSpec((1, HEADS, HEAD_DIM, B_STATE), lambda bi, t: (bi, 0, 0, 0)),
                  pl.BlockSpec((B_CONV, B_CONV_DIM), lambda bi, t: (0, 0)),
                  vec(B_CONV_DIM), vec(GROUP_WIDTH), vec(GROUP_WIDTH), vec(GROUP_WIDTH), vec(GROUP_WIDTH)],
        out_specs=(pl.BlockSpec((1, t_in, GROUP_WIDTH), lambda bi, t: (bi, t, 0)),
                   pl.BlockSpec((1, B_CONV - 1, B_CONV_DIM), lambda bi, t: (bi, 0, 0)),
                   pl.BlockSpec((1, HEADS, HEAD_DIM, B_STATE), lambda bi, t: (bi, 0, 0, 0))),
        scratch_shapes=[pltpu.VMEM((CONV_PAD + SSD_INFLIGHT * SSD_CHUNK, B_CONV_DIM), F32),
                        pltpu.VMEM((B_STATE, GROUP_WIDTH), F32)],
        compiler_params=_cparams(("parallel", "arbitrary")),
        name="ssd",
    )(proj3, proj3, proj3, conv0, ssm0, cw, cb, dtb, alog, dvec, nrm)


HGRN_INFLIGHT = 8
HGRN_SAFE_LOG2 = 64.0


def _hgrn_kernel(q_ref, f_ref, i_ref, g_ref, st0_ref, lb_ref, nrm_ref, o_ref, st_out_ref, st_ref, b_blk, c_blk,
                 y_scr, *, t_blk, l_valid, n_blk):
    t = pl.program_id(1)

    @pl.when(t == 0)
    def _():
        for h in range(HEADS):
            parts = [jnp.zeros((HEAD_DIM, HEAD_DIM * h), F32)] if h > 0 else []
            parts.append(st0_ref[0, h].T)
            if h < HEADS - 1:
                parts.append(jnp.zeros((HEAD_DIM, HEAD_DIM * (HEADS - 1 - h)), F32))
            st_ref[h * HEAD_DIM:(h + 1) * HEAD_DIM, :] = jnp.concatenate(parts, axis=1)

    lb = lb_ref[...]
    nrm = nrm_ref[...]
    head_eq = _head_ones()
    ones_bd = head_eq.astype(BF16)
    tril = _tril(CHUNK).astype(BF16)
    rows = lax.broadcasted_iota(jnp.int32, (CHUNK, GROUP_WIDTH), 0)
    row_in_sub = rows % SUB
    lane_head = lax.broadcasted_iota(jnp.int32, (SUB, GROUP_WIDTH), 1) // HEAD_DIM
    n_sub = CHUNK // SUB
    n_chunks = t_blk // CHUNK
    sc_row = lax.broadcasted_iota(jnp.int32, (HEADS * SUB, CHUNK), 0) % SUB
    sc_col = lax.broadcasted_iota(jnp.int32, (HEADS * SUB, CHUNK), 1)

    def gates(cc, gmax):
        r0 = pl.multiple_of(cc * CHUNK, CHUNK)
        sl = pl.ds(r0, CHUNK)
        f = lb + (1.0 - lb) * jax.nn.sigmoid(f_ref[0, sl, :])
        kk = jnp.maximum(1.0 - f, 0.0)
        lf2 = jnp.log(f) * LOG2E
        if l_valid < n_blk * t_blk:
            ok = t * t_blk + r0 + rows < l_valid
            kk = jnp.where(ok, kk, 0.0)
            lf2 = jnp.where(ok, lf2, 0.0)
        bcs2 = _dot01_lhs(tril, lf2)
        b_blk[sl, :] = bcs2
        c_blk[sl, :] = jnp.log(kk) * LOG2E - bcs2
        prev = jnp.zeros((1, GROUP_WIDTH), F32)
        for a in range(n_sub):
            end = bcs2[(a + 1) * SUB - 1:(a + 1) * SUB, :]
            gmax = jnp.maximum(gmax, prev - end)
            prev = end
        return gmax

    gmax = lax.fori_loop(0, n_chunks, gates, jnp.zeros((1, GROUP_WIDTH), F32),
                         unroll=HGRN_INFLIGHT if n_chunks % HGRN_INFLIGHT == 0 else 1)
    factored_ok = jnp.max(gmax) <= HGRN_SAFE_LOG2

    def stack_heads(x):
        return jnp.concatenate([jnp.where(lane_head == h, x, 0.0) for h in range(HEADS)], axis=0).astype(BF16)

    def unstack_heads(x_all):
        out = jnp.zeros((SUB, GROUP_WIDTH), F32)
        for h in range(HEADS):
            out = out + jnp.where(lane_head == h, x_all[h * SUB:(h + 1) * SUB, :], 0.0)
        return out

    def chunks(ccs, factored):
        n = len(ccs)
        r0s = [pl.multiple_of(cc * CHUNK, CHUNK) for cc in ccs]
        sls = [pl.ds(r0, CHUNK) for r0 in r0s]
        qs = [q_ref[0, sl, :] for sl in sls]
        vs = [i_ref[0, sl, :] for sl in sls]
        bs = [b_blk[sl, :] for sl in sls]
        cs = [c_blk[sl, :] for sl in sls]
        vbs = [v.astype(BF16) for v in vs]

        os_ = []
        for s in range(n):
            os_.append(_dot_nt((qs[s] * jnp.exp2(bs[s])).astype(BF16), st_ref[...].astype(BF16)))
            last = bs[s][CHUNK - 1:CHUNK, :]
            kt = jnp.exp2(last + cs[s]).astype(BF16)
            st_ref[...] = st_ref[...] * jnp.exp2(last) + jnp.where(head_eq, _dot_tn(vbs[s], kt), 0.0)

        pieces = [[] for _ in range(n)]
        for a in range(n_sub):
            s0 = a * SUB
            s_end = s0 + SUB if factored else s0
            if s_end == 0:
                for s in range(n):
                    pieces[s].append(jnp.zeros((SUB, GROUP_WIDTH), F32))
                continue
            scs = []
            for s in range(n):
                ref_row = bs[s][s0 - 1:s0, :] if a > 0 else jnp.zeros((1, GROUP_WIDTH), F32)
                qa = qs[s][s0:s0 + SUB, :] * jnp.exp2(bs[s][s0:s0 + SUB, :] - ref_row)
                ka = jnp.where(rows < s_end, jnp.exp2(ref_row + cs[s]), 0.0)
                sc = _dot_nt(stack_heads(qa), ka.astype(BF16))
                if factored:
                    sc = jnp.where(sc_col <= s0 + sc_row, sc, 0.0)
                scs.append(sc.astype(BF16))
            for s in range(n):
                pieces[s].append(unstack_heads(_dot(scs[s], vbs[s])))
        for s in range(n):
            os_[s] = os_[s] + jnp.concatenate(pieces[s], axis=0)

        if not factored:
            for s in range(n):
                def sub_rows(ref, jj):
                    return jnp.concatenate(
                        [jnp.broadcast_to(ref[pl.ds(r0s[s] + a * SUB + jj, 1), :], (SUB, GROUP_WIDTH))
                         for a in range(n_sub)], axis=0)

                for jj in range(SUB):
                    w = jnp.exp2(bs[s] + sub_rows(c_blk, jj))
                    y_scr[jj * CHUNK:(jj + 1) * CHUNK, :] = jnp.where(row_in_sub >= jj, qs[s] * w, 0.0).astype(BF16)
                att = _dot(y_scr[...], ones_bd)
                for jj in range(SUB):
                    os_[s] = os_[s] + att[jj * CHUNK:(jj + 1) * CHUNK, :] * sub_rows(i_ref.at[0], jj)

        mss = [_dot01_rhs(o * o, ones_bd) * (1.0 / HEAD_DIM) for o in os_]
        for s in range(n):
            o_ref[0, sls[s], :] = os_[s] * lax.rsqrt(mss[s] + EPS) * nrm * _silu(g_ref[0, sls[s], :])

    @pl.when(factored_ok)
    def _():
        inflight = HGRN_INFLIGHT if n_chunks % HGRN_INFLIGHT == 0 else 1

        def group(pp, carry):
            chunks([inflight * pp + s for s in range(inflight)], True)
            return carry

        lax.fori_loop(0, n_chunks // inflight, group, 0)

    @pl.when(jnp.logical_not(factored_ok))
    def _():
        def one(cc, carry):
            chunks([cc], False)
            return carry

        lax.fori_loop(0, n_chunks, one, 0)

    @pl.when(t == n_blk - 1)
    def _():
        for h in range(HEADS):
            hs = slice(h * HEAD_DIM, (h + 1) * HEAD_DIM)
            st_out_ref[0, h] = st_ref[hs, hs].T


def _hgrn(proj3, col0, st0, lb, nrm, l_valid, t_blk):
    b, lrows, _ = proj3.shape
    n_blk = -(-l_valid // t_blk)
    assert n_blk * t_blk <= lrows and t_blk % CHUNK == 0
    col = lambda c: pl.BlockSpec((1, t_blk, GROUP_WIDTH),
                                 lambda bi, t: (bi, t, (c - COL_CQ + col0) // GROUP_WIDTH))
    vec = pl.BlockSpec((1, GROUP_WIDTH), lambda bi, t: (0, 0))
    st_spec = pl.BlockSpec((1, HEADS, HEAD_DIM, HEAD_DIM), lambda bi, t: (bi, 0, 0, 0))
    return pl.pallas_call(
        functools.partial(_hgrn_kernel, t_blk=t_blk, l_valid=l_valid, n_blk=n_blk),
        out_shape=(jax.ShapeDtypeStruct((b, n_blk * t_blk, GROUP_WIDTH), F32),
                   jax.ShapeDtypeStruct((b, HEADS, HEAD_DIM, HEAD_DIM), F32)),
        grid=(b, n_blk),
        in_specs=[col(COL_CQ), col(COL_CF), col(COL_CI), col(COL_CG), st_spec, vec, vec],
        out_specs=(pl.BlockSpec((1, t_blk, GROUP_WIDTH), lambda bi, t: (bi, t, 0)), st_spec),
        scratch_shapes=[pltpu.VMEM((GROUP_WIDTH, GROUP_WIDTH), F32),
                        pltpu.VMEM((t_blk, GROUP_WIDTH), F32),
                        pltpu.VMEM((t_blk, GROUP_WIDTH), F32),
                        pltpu.VMEM((SUB * CHUNK, GROUP_WIDTH), BF16)],
        compiler_params=_cparams(("parallel", "arbitrary")),
        name="hgrn",
    )(proj3, proj3, proj3, proj3, st0, lb, nrm)


POOL_PAD = 24
POOL_FRONT = 8
D_GROUP = 64


def _pool_kernel(x_ref, buf_ref, w_ref, sc_ref, o_ref, buf_out_ref, xp_ref, s2_ref, s4_ref, *, t_blk):
    t = pl.program_id(1)
    hi = POOL_PAD + t_blk
    half = GROUP_WIDTH // 2

    @pl.when(t == 0)
    def _():
        zeros = jnp.zeros((POOL_PAD - D_BUF, GROUP_WIDTH), F32)
        xp_ref[0:POOL_PAD - D_BUF, :] = zeros
        s2_ref[0:POOL_FRONT, :] = zeros[0:POOL_FRONT]
        s4_ref[0:POOL_FRONT, :] = zeros[0:POOL_FRONT]
        xp_ref[POOL_PAD - D_BUF:POOL_PAD, :] = buf_ref[0]

    x = x_ref[0]
    xp_ref[POOL_PAD:hi, :] = x
    s2 = xp_ref[POOL_FRONT:hi, :] + xp_ref[POOL_FRONT - 1:hi - 1, :]
    s2_ref[POOL_FRONT:hi, :] = s2
    s4 = s2 + s2_ref[POOL_FRONT - 2:hi - 2, :]
    s4_ref[POOL_FRONT:hi, :] = s4
    s8 = s4[:, half:] + s4_ref[POOL_FRONT - 4:hi - 4, half:]
    n_ext = hi - POOL_FRONT
    s16 = s8[8:n_ext, :] + s8[0:n_ext - 8, :]
    lo_sum = s2[n_ext - t_blk:, :half]
    lane = lax.broadcasted_iota(jnp.int32, (t_blk, half), 1)
    pooled_lo = jnp.where(lane < D_GROUP, lo_sum * 0.5, s4[n_ext - t_blk:, :half] * 0.25)
    pooled_hi = jnp.where(lane < D_GROUP, s8[n_ext - t_blk:, :] * 0.125, s16[n_ext - 8 - t_blk:, :] * 0.0625)
    pooled = jnp.concatenate([pooled_lo, pooled_hi], axis=1) - x
    o_ref[0] = _dot(pooled.astype(BF16), w_ref[...]) * sc_ref[...]
    tail = xp_ref[hi - D_BUF:hi, :]
    buf_out_ref[0] = tail
    xp_ref[POOL_PAD - D_BUF:POOL_PAD, :] = tail


def _pool(proj3, buf, w_bd, scale, l_valid, t_blk):
    b = proj3.shape[0]
    n_blk = l_valid // t_blk
    assert n_blk * t_blk == l_valid and t_blk >= D_BUF
    buf_spec = pl.BlockSpec((1, D_BUF, GROUP_WIDTH), lambda bi, t: (bi, 0, 0))
    return pl.pallas_call(
        functools.partial(_pool_kernel, t_blk=t_blk),
        out_shape=(jax.ShapeDtypeStruct((b, l_valid, GROUP_WIDTH), F32),
                   jax.ShapeDtypeStruct((b, D_BUF, GROUP_WIDTH), F32)),
        grid=(b, n_blk),
        in_specs=[pl.BlockSpec((1, t_blk, GROUP_WIDTH), lambda bi, t: (bi, t, COL_DX // GROUP_WIDTH)),
                  buf_spec,
                  pl.BlockSpec((GROUP_WIDTH, GROUP_WIDTH), lambda bi, t: (0, 0)),
                  pl.BlockSpec((1, GROUP_WIDTH), lambda bi, t: (0, 0))],
        out_specs=(pl.BlockSpec((1, t_blk, GROUP_WIDTH), lambda bi, t: (bi, t, 0)), buf_spec),
        scratch_shapes=[pltpu.VMEM((POOL_PAD + t_blk, GROUP_WIDTH), F32)] * 3,
        compiler_params=_cparams(("parallel", "arbitrary")),
        name="pool",
    )(proj3, buf, w_bd, scale)


SSD_BLOCK = 1024
HGRN_BLOCK = 1024
POOL_BLOCK = 2048


def _prep_layer(layer, norm1, w_in, a_rel_bias, b_conv_w, b_conv_b, b_dt_bias, b_a_log, b_d, b_norm, lbs, c_norm,
                d_pool_w, d_pool_scale, w_out, norm2, w_gate_up, w_down):
    w_r = _w_in_prep(jnp.transpose(w_in, (0, 2, 1)), layer)
    tab = a_rel_bias[layer].T
    m = A_BAND + CHUNK
    far = tab[:, 2 * REL_CLIP:]
    n_far = A_BAND_PREV - REL_CLIP + 1
    r = jnp.concatenate([jnp.broadcast_to(far, (HEADS, n_far)),
                         tab[:, 2 * REL_CLIP - 1:CHUNK:-1],
                         jnp.broadcast_to(far, (HEADS, m - n_far - (2 * REL_CLIP - 1 - CHUNK)))], axis=1)
    bias = jnp.tile(r, (1, CHUNK))[:, :CHUNK * (m - 1)].reshape(HEADS, CHUNK, m - 1)[:, :, :A_BAND]
    bias = bias.reshape(HEADS * CHUNK, A_BAND)
    rep = lambda p: jnp.repeat(p[layer], HEAD_DIM)[None, :]
    pw = d_pool_w[layer]
    w_bd = jnp.zeros((GROUP_WIDTH, GROUP_WIDTH), F32)
    for g in range(4):
        w_bd = w_bd.at[g * 64:(g + 1) * 64, g * 64:(g + 1) * 64].set(pw[g])
    return dict(
        n1=norm1[layer][None, :], w_in=w_r, bias=bias,
        cw=b_conv_w[layer], cb=b_conv_b[layer][None, :], dtb=rep(b_dt_bias), alog=rep(b_a_log), dvec=rep(b_d),
        bnrm=b_norm[layer][None, :], lb=lbs[layer][None, :], cnrm=c_norm[layer].reshape(1, GROUP_WIDTH),
        w_bd=w_bd.astype(BF16), psc=d_pool_scale[layer][None, :], n2=norm2[layer][None, :], layer=layer,
        w_out=w_out.astype(BF16), w_gu=w_gate_up.astype(BF16), w_dn=w_down.astype(BF16))


def _round_up(n, m):
    return -(-n // m) * m


def _group_layer(x, st, p, nf, final, cache_t=None):
    k_prev, v_prev, conv0, ssm0, hgrn0, pool0 = st
    b, l, _ = x.shape
    p_hist = k_prev.shape[1]
    proj = _inproj(x.reshape(b * l, D_MODEL), p["n1"], p["w_in"]).reshape(b, l, PROJ_W)
    lq = _round_up(l, CHUNK)
    pad_rows = lambda a: a if lq == l else jnp.pad(a, ((0, 0), (0, lq - l), (0, 0)))

    keep = min(A_BAND_PREV, p_hist + l)
    n_new = min(l, keep)
    ka = proj[:, l - n_new:, COL_AK:COL_AK + GROUP_WIDTH].reshape(b, n_new, HEADS, HEAD_DIM)
    va = proj[:, l - n_new:, COL_AV:COL_AV + GROUP_WIDTH].reshape(b, n_new, HEADS, HEAD_DIM)
    if cache_t is not None:
        bias_s = p["bias"].reshape(HEADS, CHUNK, A_BAND)[:, :l].reshape(HEADS * l, A_BAND)
        oa, kt_new, vt_new = _attention_cached(proj, cache_t[0], cache_t[1], p["layer"], bias_s, l)
        untranspose = lambda a: a.reshape(b, HEADS, HEAD_DIM, A_BAND_PREV).transpose(0, 3, 1, 2)
        new_k, new_v = untranspose(kt_new), untranspose(vt_new)
    else:
        if p_hist == 0:
            k_cache = v_cache = None
        else:
            k_cache = k_prev.astype(BF16).reshape(b, p_hist, GROUP_WIDTH)
            v_cache = v_prev.astype(BF16).reshape(b, p_hist, GROUP_WIDTH)
        oa = _attention(pad_rows(proj), k_cache, v_cache, p["bias"], l)[:, :l]
        new_k = jnp.concatenate([k_prev[:, p_hist - (keep - n_new):], ka], axis=1)
        new_v = jnp.concatenate([v_prev[:, p_hist - (keep - n_new):], va], axis=1)

    ob, new_conv, new_ssm = _ssd(proj, conv0, ssm0, p["cw"], p["cb"], p["dtb"], p["alog"], p["dvec"], p["bnrm"],
                                 l, SSD_BLOCK if l % SSD_BLOCK == 0 else SSD_CHUNK)

    if l % HGRN_BLOCK == 0:
        oc, new_hgrn = _hgrn(proj, COL_CQ, hgrn0, p["lb"], p["cnrm"], l, HGRN_BLOCK)
    else:
        oc, new_hgrn = _hgrn(pad_rows(proj[:, :, COL_CQ:COL_CQ + 4 * GROUP_WIDTH]), 0, hgrn0, p["lb"], p["cnrm"], l, lq)
        oc = oc[:, :l]

    od, new_pool = _pool(proj, pool0, p["w_bd"], p["psc"], l, POOL_BLOCK if l % POOL_BLOCK == 0 else l)

    flat = lambda a: a.reshape(b * l, GROUP_WIDTH)
    y = _mlp(x.reshape(b * l, D_MODEL), (flat(oa), flat(ob), flat(oc), flat(od)), p["layer"],
             p["w_out"], p["n2"], p["w_gu"], p["w_dn"], nf, final)
    return y.reshape(b, l, D_MODEL), (new_k, new_v, new_conv, new_ssm, new_hgrn, new_pool)


def _trunk(x, states, params, nf):
    depth = len(params)
    new = [[] for _ in range(6)]
    b, l, _ = x.shape
    cache_t = None
    if states[0].shape[2] == A_BAND_PREV and l <= CHUNK and l % 8 == 0:
        as_t = lambda c: jnp.transpose(c, (0, 1, 3, 4, 2)).reshape(depth, b, GROUP_WIDTH, A_BAND_PREV)
        cache_t = (as_t(states[0]), as_t(states[1]))
    for layer in range(depth):
        st_l = tuple(s[layer] for s in states)
        x, ns = _group_layer(x, st_l, params[layer], nf, layer == depth - 1, cache_t)
        for i in range(6):
            new[i].append(ns[i])
    return x, tuple(jnp.stack(n, axis=0) for n in new)


def kernel(x_prompt, x_sample, cache_a_k, cache_a_v, state_b_conv, state_b_ssm, state_c_hgrn, state_d_pool, norm1,
           w_in, a_rel_bias, b_conv_w, b_conv_b, b_dt_bias, b_a_log, b_d, b_norm, c_lb_logits, c_norm, d_pool_w,
           d_pool_scale, w_out, norm2, w_gate_up, w_down, norm_f):
    depth = w_in.shape[0]
    lbs = jnp.cumsum(jax.nn.softmax(c_lb_logits.astype(F32), axis=0), axis=0)
    lbs = lbs - lbs[:1]
    params = [_prep_layer(layer, norm1, w_in, a_rel_bias, b_conv_w, b_conv_b, b_dt_bias, b_a_log, b_d, b_norm, lbs,
                          c_norm, d_pool_w, d_pool_scale, w_out, norm2, w_gate_up, w_down)
              for layer in range(depth)]
    nf = norm_f[None, :]
    bp = x_prompt.shape[0]
    prompt_states = (
        jnp.zeros((depth, bp, 0, HEADS, HEAD_DIM), F32),
        jnp.zeros((depth, bp, 0, HEADS, HEAD_DIM), F32),
        jnp.zeros((depth, bp, B_CONV - 1, B_CONV_DIM), F32),
        jnp.zeros((depth, bp, HEADS, HEAD_DIM, B_STATE), F32),
        jnp.zeros((depth, bp, HEADS, HEAD_DIM, HEAD_DIM), F32),
        jnp.zeros((depth, bp, D_BUF, GROUP_WIDTH), F32),
    )
    y_prompt, ps = _trunk(x_prompt, prompt_states, params, nf)
    sample_states = (cache_a_k, cache_a_v, state_b_conv, state_b_ssm, state_c_hgrn, state_d_pool)
    y_sample, ss = _trunk(x_sample, sample_states, params, nf)
    return (y_prompt, y_sample) + ps + ss
```

```python
import functools

import jax
import jax.numpy as jnp
from jax import lax
from jax.experimental import pallas as pl
from jax.experimental.pallas import tpu as pltpu

F32 = jnp.float32
BF16 = jnp.bfloat16
MIX_DTYPE = BF16

D_MODEL = 1024
GROUP_WIDTH = 256
HEADS = 4
HEAD_DIM = 64
CHUNK = 64
A_BAND_PREV = 512
A_BAND = A_BAND_PREV + CHUNK
REL_CLIP = 128
B_STATE = 128
B_CONV = 4
B_CONV_DIM = 768
D_BUF = 15
D_FF = 2816
EPS = 1e-6
NEG = -1e30
LOG2E = 1.4426950408889634
SUB = 16

COL_XBC = 0
COL_AQ = 768
COL_AK = 1024
COL_AV = 1280
COL_BZ = 1536
COL_CQ = 1792
COL_CF = 2048
COL_CI = 2304
COL_CG = 2560
COL_DX = 2816
COL_DT = 3072
PROJ_W = 3328

VMEM_LIMIT = 56 * 1024 * 1024


def _cparams(sem):
    return pltpu.CompilerParams(dimension_semantics=sem, vmem_limit_bytes=VMEM_LIMIT)


def _rms(x, w):
    return x * lax.rsqrt(jnp.mean(x * x, axis=-1, keepdims=True) + EPS) * w


def _silu(x):
    h = 0.5 * x
    return h + h * jnp.tanh(h)


def _dot(a, b):
    return jnp.dot(a, b, preferred_element_type=F32)


def _dot_nt(a, b):
    return lax.dot_general(a, b, (((1,), (1,)), ((), ())), preferred_element_type=F32)


def _dot_tn(a, b):
    return lax.dot_general(a, b, (((0,), (0,)), ((), ())), preferred_element_type=F32)


def _split3(x):
    hi = x.astype(BF16)
    r = x - hi.astype(F32)
    mid = r.astype(BF16)
    lo = (r - mid.astype(F32)).astype(BF16)
    return hi, mid, lo


def _dot01_lhs(m01, x):
    hi, mid, lo = _split3(x)
    return _dot(m01, hi) + _dot(m01, mid) + _dot(m01, lo)


def _tril(t):
    r = lax.broadcasted_iota(jnp.int32, (t, t), 0)
    c = lax.broadcasted_iota(jnp.int32, (t, t), 1)
    return r >= c


def _head_ones():
    r = lax.broadcasted_iota(jnp.int32, (GROUP_WIDTH, GROUP_WIDTH), 0) // HEAD_DIM
    c = lax.broadcasted_iota(jnp.int32, (GROUP_WIDTH, GROUP_WIDTH), 1) // HEAD_DIM
    return r == c


IN_WIDTH = 3076
ORIG_DT = 1792
W_PREP_COLS = 256


def _w_in_prep_kernel(wt_ref, o_ref):
    def put(c0, rows):
        o_ref[:, c0:c0 + W_PREP_COLS] = rows.T.astype(BF16)

    for j in range(3):
        put(COL_XBC + W_PREP_COLS * j, wt_ref[0, 1024 + W_PREP_COLS * j:1024 + W_PREP_COLS * (j + 1), :])
    for j in range(4):
        put(COL_AQ + W_PREP_COLS * j, wt_ref[0, W_PREP_COLS * j:W_PREP_COLS * (j + 1), :])
    for j in range(5):
        r0 = ORIG_DT + HEADS + W_PREP_COLS * j
        put(COL_CQ + W_PREP_COLS * j, wt_ref[0, r0:r0 + W_PREP_COLS, :])
    put(COL_DT, jnp.concatenate([jnp.broadcast_to(wt_ref[0, ORIG_DT + h:ORIG_DT + h + 1, :], (HEAD_DIM, D_MODEL))
                                 for h in range(HEADS)], axis=0))


def _w_in_prep(wt_all, layer):
    return pl.pallas_call(
        _w_in_prep_kernel,
        out_shape=jax.ShapeDtypeStruct((D_MODEL, PROJ_W), BF16),
        grid=(1,),
        in_specs=[pl.BlockSpec((1, IN_WIDTH, D_MODEL), lambda i: (layer, 0, 0), pipeline_mode=pl.Buffered(1))],
        out_specs=pl.BlockSpec((D_MODEL, PROJ_W), lambda i: (0, 0)),
        compiler_params=_cparams(("arbitrary",)),
        name="w_in_prep",
    )(wt_all)


def _inproj_kernel(x_ref, n_ref, w_ref, o_ref):
    hb = _rms(x_ref[...], n_ref[...]).astype(BF16)
    for c0 in range(0, PROJ_W, 256):
        o_ref[:, c0:c0 + 256] = _dot(hb, w_ref[:, c0:c0 + 256])


def _inproj(x2d, n1, w_in):
    rows = x2d.shape[0]
    tm = next((m for m in (1024, 512) if rows % m == 0), rows)
    return pl.pallas_call(
        _inproj_kernel,
        out_shape=jax.ShapeDtypeStruct((rows, PROJ_W), F32),
        grid=(rows // tm,),
        in_specs=[pl.BlockSpec((tm, D_MODEL), lambda i: (i, 0)),
                  pl.BlockSpec((1, D_MODEL), lambda i: (0, 0)),
                  pl.BlockSpec((D_MODEL, PROJ_W), lambda i: (0, 0), pipeline_mode=pl.Buffered(1))],
        out_specs=pl.BlockSpec((tm, PROJ_W), lambda i: (i, 0)),
        compiler_params=_cparams(("parallel",)),
        name="inproj",
    )(x2d, n1, w_in)


CONV_PAD = 8


def _causal_conv_silu(xpad_ref, rows, conv_w, conv_b):
    hist = B_CONV - 1
    y = conv_b
    for i in range(B_CONV):
        y = y + xpad_ref[CONV_PAD - hist + i:CONV_PAD - hist + i + rows, :] * conv_w[i:i + 1, :]
    return _silu(y)


def _softplus(x):
    return jnp.maximum(x, 0.0) + jnp.log1p(jnp.exp(-jnp.abs(x)))


FF_CHUNK = 256


def _mlp_kernel(x_ref, oa_ref, ob_ref, oc_ref, od_ref, wo_ref, n2_ref, wgu_ref, wdn_ref, nf_ref, o_ref, mix_ref,
                act_ref, *, final):
    for m, r in enumerate((oa_ref, ob_ref, oc_ref, od_ref)):
        mix_ref[:, m * GROUP_WIDTH:(m + 1) * GROUP_WIDTH] = r[...]
    x1 = x_ref[...] + _dot(mix_ref[...], wo_ref[0])
    hb = _rms(x1, n2_ref[...]).astype(BF16)
    for c0 in range(0, D_FF, FF_CHUNK):
        gate = _dot(hb, wgu_ref[0, :, c0:c0 + FF_CHUNK])
        up = _dot(hb, wgu_ref[0, :, D_FF + c0:D_FF + c0 + FF_CHUNK])
        act_ref[:, c0:c0 + FF_CHUNK] = (_silu(gate) * up).astype(BF16)
    out = x1 + _dot(act_ref[...], wdn_ref[0])
    if final:
        out = _rms(out, nf_ref[...])
    o_ref[...] = out


def _mlp(x2d, mix, layer, w_out, n2, w_gu, w_dn, nf, final):
    rows = x2d.shape[0]
    tm = 512 if rows % 512 == 0 else rows
    row_spec = lambda w: pl.BlockSpec((tm, w), lambda i: (i, 0))
    vec = lambda a: pl.BlockSpec(a.shape, lambda i: (0, 0))
    wspec = lambda a: pl.BlockSpec((1,) + a.shape[1:], lambda i: (layer, 0, 0), pipeline_mode=pl.Buffered(1))
    return pl.pallas_call(
        functools.partial(_mlp_kernel, final=final),
        out_shape=jax.ShapeDtypeStruct((rows, D_MODEL), F32),
        grid=(rows // tm,),
        in_specs=[row_spec(D_MODEL)] + [row_spec(GROUP_WIDTH)] * 4
                 + [wspec(w_out), vec(n2), wspec(w_gu), wspec(w_dn), vec(nf)],
        out_specs=row_spec(D_MODEL),
        scratch_shapes=[pltpu.VMEM((tm, D_MODEL), BF16), pltpu.VMEM((tm, D_FF), BF16)],
        compiler_params=_cparams(("parallel",)),
        name="mlp",
    )(x2d, *mix, w_out, n2, w_gu, w_dn, nf)


ATTN_BLOCK = 1024
ATTN_UNROLL = 8


def _attn_kernel(q_ref, kh_ref, kc_ref, vh_ref, vc_ref, bias_ref, o_ref, kb_ref, vb_ref, *, qb, p_hist, l_valid):
    i = pl.program_id(1)
    if p_hist == 0:
        @pl.when(i == 0)
        def _():
            kb_ref[0:A_BAND_PREV, :] = jnp.zeros((A_BAND_PREV, GROUP_WIDTH), BF16)
            vb_ref[0:A_BAND_PREV, :] = jnp.zeros((A_BAND_PREV, GROUP_WIDTH), BF16)

        @pl.when(i > 0)
        def _():
            kb_ref[0:A_BAND_PREV, :] = kh_ref[0].astype(BF16)
            vb_ref[0:A_BAND_PREV, :] = vh_ref[0].astype(BF16)
    else:
        kb_ref[0:A_BAND_PREV, :] = kh_ref[0].astype(BF16)
        vb_ref[0:A_BAND_PREV, :] = vh_ref[0].astype(BF16)
    kb_ref[A_BAND_PREV:A_BAND_PREV + qb, :] = kc_ref[0].astype(BF16)
    vb_ref[A_BAND_PREV:A_BAND_PREV + qb, :] = vc_ref[0].astype(BF16)
    lane_head = lax.broadcasted_iota(jnp.int32, (CHUNK, GROUP_WIDTH), 1) // HEAD_DIM
    band_pos = lax.broadcasted_iota(jnp.int32, (1, A_BAND), 1)

    n_chunks = qb // CHUNK
    group = ATTN_UNROLL if n_chunks % ATTN_UNROLL == 0 else 1

    def chunks(gi, carry):
        r0s = [pl.multiple_of((gi * group + u) * CHUNK, CHUNK) for u in range(group)]
        es, dens = [], []
        for r0 in r0s:
            qc = q_ref[0, pl.ds(r0, CHUNK), :] * (HEAD_DIM ** -0.5)
            qs = jnp.concatenate([jnp.where(lane_head == h, qc, 0.0) for h in range(HEADS)], axis=0).astype(BF16)
            s = _dot_nt(qs, kb_ref[pl.ds(r0, A_BAND), :]) + bias_ref[...]
            kpos = i * qb + r0 - A_BAND_PREV + band_pos
            s = jnp.where((kpos >= -p_hist) & (kpos < l_valid), s, NEG)
            e = jnp.exp(s - jnp.max(s, axis=-1, keepdims=True))
            dens.append(jnp.sum(e, axis=-1, keepdims=True))
            es.append(e.astype(BF16))
        o_alls = [_dot(es[u], vb_ref[pl.ds(r0s[u], A_BAND), :]) * (1.0 / dens[u]) for u in range(group)]
        for u in range(group):
            o = jnp.zeros((CHUNK, GROUP_WIDTH), F32)
            for h in range(HEADS):
                o = o + jnp.where(lane_head == h, o_alls[u][h * CHUNK:(h + 1) * CHUNK, :], 0.0)
            o_ref[0, pl.ds(r0s[u], CHUNK), :] = o.astype(MIX_DTYPE)
        return carry

    lax.fori_loop(0, n_chunks // group, chunks, 0)


def _attention(proj3, k_cache, v_cache, bias, l_valid):
    b = proj3.shape[0]
    lq = _round_up(l_valid, CHUNK)
    qb = next((q for q in (ATTN_BLOCK, A_BAND_PREV) if lq % q == 0), lq)
    nblk = lq // qb
    col = lambda c: pl.BlockSpec((1, qb, GROUP_WIDTH), lambda bi, i: (bi, i, c // GROUP_WIDTH))
    if k_cache is None:
        assert qb % A_BAND_PREV == 0
        p_hist = 0
        per = qb // A_BAND_PREV
        prev = lambda c: pl.BlockSpec((1, A_BAND_PREV, GROUP_WIDTH),
                                      lambda bi, i: (bi, jnp.maximum(i * per - 1, 0), c // GROUP_WIDTH))
        kh, vh, kh_spec, vh_spec = proj3, proj3, prev(COL_AK), prev(COL_AV)
    else:
        assert nblk == 1 and k_cache.shape[1] == A_BAND_PREV
        p_hist = A_BAND_PREV
        kh_spec = vh_spec = pl.BlockSpec((1, A_BAND_PREV, GROUP_WIDTH), lambda bi, i: (bi, 0, 0))
        kh, vh = k_cache, v_cache
    return pl.pallas_call(
        functools.partial(_attn_kernel, qb=qb, p_hist=p_hist, l_valid=l_valid),
        out_shape=jax.ShapeDtypeStruct((b, lq, GROUP_WIDTH), MIX_DTYPE),
        grid=(b, nblk),
        in_specs=[col(COL_AQ), kh_spec, col(COL_AK), vh_spec, col(COL_AV),
                  pl.BlockSpec((HEADS * CHUNK, A_BAND), lambda bi, i: (0, 0))],
        out_specs=pl.BlockSpec((1, qb, GROUP_WIDTH), lambda bi, i: (bi, i, 0)),
        scratch_shapes=[pltpu.VMEM((A_BAND_PREV + qb, GROUP_WIDTH), BF16),
                        pltpu.VMEM((A_BAND_PREV + qb, GROUP_WIDTH), BF16)],
        compiler_params=_cparams(("parallel", "arbitrary")),
        name="attention",
    )(proj3, kh, proj3, vh, proj3, bias)


def _attn_cached_kernel(q_ref, kn_ref, vn_ref, kt_ref, vt_ref, bias_ref, o_ref, kt_out_ref, vt_out_ref, *, l_new):
    lane_head = lax.broadcasted_iota(jnp.int32, (l_new, GROUP_WIDTH), 1) // HEAD_DIM
    q = q_ref[0] * (HEAD_DIM ** -0.5)
    qs = jnp.concatenate([jnp.where(lane_head == h, q, 0.0) for h in range(HEADS)], axis=0).astype(BF16)
    kn = kn_ref[0]
    vn = vn_ref[0]
    kt = kt_ref[0, 0]
    vt = vt_ref[0, 0]
    s_old = _dot(qs, kt.astype(BF16)) + bias_ref[:, 0:A_BAND_PREV]
    s_new = _dot_nt(qs, kn.astype(BF16)) + bias_ref[:, A_BAND_PREV:A_BAND_PREV + l_new]
    m = jnp.maximum(jnp.max(s_old, axis=-1, keepdims=True), jnp.max(s_new, axis=-1, keepdims=True))
    e_old = jnp.exp(s_old - m)
    e_new = jnp.exp(s_new - m)
    den = jnp.sum(e_old, axis=-1, keepdims=True) + jnp.sum(e_new, axis=-1, keepdims=True)
    o_all = (_dot_nt(e_old.astype(BF16), vt.astype(BF16)) + _dot(e_new.astype(BF16), vn.astype(BF16))) * (1.0 / den)
    o = jnp.zeros((l_new, GROUP_WIDTH), F32)
    for h in range(HEADS):
        o = o + jnp.where(lane_head == h, o_all[h * l_new:(h + 1) * l_new, :], 0.0)
    o_ref[0] = o.astype(MIX_DTYPE)

    keep = A_BAND_PREV - l_new
    lane = lax.broadcasted_iota(jnp.int32, (GROUP_WIDTH, A_BAND_PREV), 1)
    r = lax.broadcasted_iota(jnp.int32, (l_new, A_BAND_PREV), 0)
    c = lax.broadcasted_iota(jnp.int32, (l_new, A_BAND_PREV), 1)
    place = (c == r + keep).astype(BF16)

    def appended(old_t, new_rows):
        hi, mid, lo = _split3(new_rows)
        new_cols = _dot_tn(hi, place) + _dot_tn(mid, place) + _dot_tn(lo, place)
        return jnp.where(lane < keep, pltpu.roll(old_t, keep, 1), new_cols)

    kt_out_ref[0] = appended(kt, kn)
    vt_out_ref[0] = appended(vt, vn)


def _attention_cached(proj3, kt_cache, vt_cache, layer, bias, l_valid):
    b = proj3.shape[0]
    col = lambda c: pl.BlockSpec((1, l_valid, GROUP_WIDTH), lambda bi: (bi, 0, c // GROUP_WIDTH))
    cache_spec = pl.BlockSpec((1, 1, GROUP_WIDTH, A_BAND_PREV), lambda bi: (layer, bi, 0, 0))
    out_t = pl.BlockSpec((1, GROUP_WIDTH, A_BAND_PREV), lambda bi: (bi, 0, 0))
    return pl.pallas_call(
        functools.partial(_attn_cached_kernel, l_new=l_valid),
        out_shape=(jax.ShapeDtypeStruct((b, l_valid, GROUP_WIDTH), MIX_DTYPE),
                   jax.ShapeDtypeStruct((b, GROUP_WIDTH, A_BAND_PREV), F32),
                   jax.ShapeDtypeStruct((b, GROUP_WIDTH, A_BAND_PREV), F32)),
        grid=(b,),
        in_specs=[col(COL_AQ), col(COL_AK), col(COL_AV), cache_spec, cache_spec,
                  pl.BlockSpec((HEADS * l_valid, A_BAND), lambda bi: (0, 0))],
        out_specs=(pl.BlockSpec((1, l_valid, GROUP_WIDTH), lambda bi: (bi, 0, 0)), out_t, out_t),
        compiler_params=_cparams(("parallel",)),
        name="attention_cached",
    )(proj3, proj3, proj3, kt_cache, vt_cache, bias)


SSD_CHUNK = 128
SSD_INFLIGHT = 4


def _ssd_kernel(xbc_ref, z_ref, dt_ref, conv0_ref, ssm0_ref, cw_ref, cb_ref, dtb_ref, alog_ref, d_ref, nrm_ref,
                o_ref, conv_out_ref, ssm_out_ref, xpad_ref, st_ref, *, t_blk, l_valid, n_blk):
    t = pl.program_id(1)
    hist = B_CONV - 1

    @pl.when(t == 0)
    def _():
        xpad_ref[CONV_PAD - hist:CONV_PAD, :] = conv0_ref[0]
        st_ref[...] = ssm0_ref[0].reshape(GROUP_WIDTH, B_STATE).T

    n_last = l_valid - (n_blk - 1) * t_blk
    conv_out_ref[0] = xbc_ref[0, n_last - hist:n_last, :]

    tc = SSD_CHUNK
    short = l_valid < t_blk

    def rows_of(ref, sl):
        if short:
            return jnp.concatenate([ref[0], jnp.zeros((tc - l_valid, ref.shape[2]), F32)], axis=0)
        return ref[0, sl, :]

    tril = _tril(tc)
    tril_b = tril.astype(BF16)
    lane = lax.broadcasted_iota(jnp.int32, (tc, B_STATE), 1)
    rows = lax.broadcasted_iota(jnp.int32, (tc, GROUP_WIDTH), 0)
    a_neg = -jnp.exp(alog_ref[...])
    conv_b = cb_ref[...]
    conv_w = cw_ref[...]
    dt_bias = dtb_ref[...]
    d_skip = d_ref[...]
    nrm = nrm_ref[...]

    n_chunks = t_blk // tc
    group = SSD_INFLIGHT if n_chunks % SSD_INFLIGHT == 0 else 1
    groups = (slice(0, B_STATE), slice(B_STATE, 2 * B_STATE))

    def chunks(gi, carry):
        rg = group * tc
        base = pl.multiple_of(gi * rg, rg)
        sl_all = pl.ds(base, rg)
        xpad_ref[CONV_PAD:CONV_PAD + rg, :] = rows_of(xbc_ref, sl_all)
        xbc_all = _causal_conv_silu(xpad_ref, rg, conv_w, conv_b)
        xpad_ref[CONV_PAD - hist:CONV_PAD, :] = xpad_ref[CONV_PAD + rg - hist:CONV_PAD + rg, :]
        dt_all = _softplus(rows_of(dt_ref, sl_all) + dt_bias)
        if l_valid < n_blk * t_blk:
            rows_all = lax.broadcasted_iota(jnp.int32, (rg, GROUP_WIDTH), 0)
            dt_all = jnp.where(t * t_blk + base + rows_all < l_valid, dt_all, 0.0)
        z_all = rows_of(z_ref, sl_all)
        xs_all = xbc_all[:, 0:GROUP_WIDTH]
        a_all = dt_all * a_neg
        u_all = xs_all * dt_all

        cut = lambda arr, s: arr[s * tc:(s + 1) * tc, :]
        acss = [_dot01_lhs(tril_b, cut(a_all, s)) for s in range(group)]
        cms = [cut(xbc_all[:, 2 * GROUP_WIDTH:3 * GROUP_WIDTH], s).astype(BF16) for s in range(group)]
        bms = [cut(xbc_all[:, GROUP_WIDTH:2 * GROUP_WIDTH], s).astype(BF16) for s in range(group)]
        cbs = [[_dot_nt(cms[s][:, gs], bms[s][:, gs]) for gs in groups] for s in range(group)]

        y_states = []
        for s in range(group):
            acs = acss[s]
            last = acs[tc - 1:tc, :]
            upb = (cut(u_all, s) * jnp.exp(last - acs)).astype(BF16)
            eacs = jnp.exp(acs)
            elast = jnp.exp(last)
            ys_s = []
            for gs in groups:
                ys_s.append(_dot(cms[s][:, gs], st_ref[:, gs].astype(BF16)) * eacs[:, gs])
                st_ref[:, gs] = st_ref[:, gs] * elast[:, gs] + _dot_tn(bms[s][:, gs], upb[:, gs])
            y_states.append(ys_s)

        for s in range(group):
            acs = acss[s]
            acs_t = acs.T
            ub = cut(u_all, s).astype(BF16)
            ys = []
            for g, gs in enumerate(groups):
                parts = []
                for hh in range(2):
                    c0 = (2 * g + hh) * HEAD_DIM
                    col = jnp.broadcast_to(acs[:, c0:c0 + 1], (tc, tc))
                    rowv = jnp.broadcast_to(acs_t[c0:c0 + 1, :], (tc, tc))
                    dec = jnp.exp(jnp.where(tril, col - rowv, NEG))
                    parts.append(_dot((cbs[s][g] * dec).astype(BF16), ub[:, gs]))
                ys.append(jnp.where(lane < HEAD_DIM, parts[0], parts[1]) + y_states[s][g])
            yv = jnp.concatenate(ys, axis=1) + d_skip * cut(xs_all, s)
            res = _rms(yv * _silu(cut(z_all, s)), nrm).astype(MIX_DTYPE)
            if short:
                o_ref[0] = res[0:l_valid, :]
            else:
                o_ref[0, pl.ds(pl.multiple_of(base + s * tc, tc), tc), :] = res
        return carry

    lax.fori_loop(0, n_chunks // group, chunks, 0)

    @pl.when(t == n_blk - 1)
    def _():
        ssm_out_ref[0] = st_ref[...].T.reshape(HEADS, HEAD_DIM, B_STATE)


def _ssd(proj3, conv0, ssm0, cw, cb, dtb, alog, dvec, nrm, l_valid, t_blk):
    b, lrows, _ = proj3.shape
    n_blk = -(-l_valid // t_blk)
    assert t_blk % SSD_CHUNK == 0 and lrows == l_valid
    assert l_valid == n_blk * t_blk or (t_blk == SSD_CHUNK and B_CONV - 1 <= l_valid < t_blk and l_valid % 8 == 0)
    t_in = min(t_blk, l_valid)
    vec = lambda w: pl.BlockSpec((1, w), lambda bi, t: (0, 0))
    return pl.pallas_call(
        functools.partial(_ssd_kernel, t_blk=t_blk, l_valid=l_valid, n_blk=n_blk),
        out_shape=(jax.ShapeDtypeStruct((b, l_valid, GROUP_WIDTH), MIX_DTYPE),
                   jax.ShapeDtypeStruct((b, B_CONV - 1, B_CONV_DIM), F32),
                   jax.ShapeDtypeStruct((b, HEADS, HEAD_DIM, B_STATE), F32)),
        grid=(b, n_blk),
        in_specs=[pl.BlockSpec((1, t_in, B_CONV_DIM), lambda bi, t: (bi, t, COL_XBC // B_CONV_DIM)),
                  pl.BlockSpec((1, t_in, GROUP_WIDTH), lambda bi, t: (bi, t, COL_BZ // GROUP_WIDTH)),
                  pl.BlockSpec((1, t_in, GROUP_WIDTH), lambda bi, t: (bi, t, COL_DT // GROUP_WIDTH)),
                  pl.BlockSpec((1, B_CONV - 1, B_CONV_DIM), lambda bi, t: (bi, 0, 0)),
                  pl.BlockSpec((1, HEADS, HEAD_DIM, B_STATE), lambda bi, t: (bi, 0, 0, 0)),
                  pl.BlockSpec((B_CONV, B_CONV_DIM), lambda bi, t: (0, 0)),
                  vec(B_CONV_DIM), vec(GROUP_WIDTH), vec(GROUP_WIDTH), vec(GROUP_WIDTH), vec(GROUP_WIDTH)],
        out_specs=(pl.BlockSpec((1, t_in, GROUP_WIDTH), lambda bi, t: (bi, t, 0)),
                   pl.BlockSpec((1, B_CONV - 1, B_CONV_DIM), lambda bi, t: (bi, 0, 0)),
                   pl.BlockSpec((1, HEADS, HEAD_DIM, B_STATE), lambda bi, t: (bi, 0, 0, 0))),
        scratch_shapes=[pltpu.VMEM((CONV_PAD + SSD_INFLIGHT * SSD_CHUNK, B_CONV_DIM), F32),
                        pltpu.VMEM((B_STATE, GROUP_WIDTH), F32)],
        compiler_params=_cparams(("parallel", "arbitrary")),
        name="ssd",
    )(proj3, proj3, proj3, conv0, ssm0, cw, cb, dtb, alog, dvec, nrm)


HGRN_INFLIGHT = 8
HGRN_SAFE_LOG2 = 64.0


def _hgrn_kernel(q_ref, f_ref, i_ref, g_ref, st0_ref, lb_ref, nrm_ref, o_ref, st_out_ref, st_ref, b_blk, c_blk,
                 y_scr, *, t_blk, l_valid, n_blk):
    t = pl.program_id(1)

    @pl.when(t == 0)
    def _():
        for h in range(HEADS):
            parts = [jnp.zeros((HEAD_DIM, HEAD_DIM * h), F32)] if h > 0 else []
            parts.append(st0_ref[0, h].T)
            if h < HEADS - 1:
                parts.append(jnp.zeros((HEAD_DIM, HEAD_DIM * (HEADS - 1 - h)), F32))
            st_ref[h * HEAD_DIM:(h + 1) * HEAD_DIM, :] = jnp.concatenate(parts, axis=1)

    lb = lb_ref[...]
    nrm = nrm_ref[...]
    head_eq = _head_ones()
    ones_bd = head_eq.astype(BF16)
    tril = _tril(CHUNK).astype(BF16)
    rows = lax.broadcasted_iota(jnp.int32, (CHUNK, GROUP_WIDTH), 0)
    row_in_sub = rows % SUB
    lane_head = lax.broadcasted_iota(jnp.int32, (SUB, GROUP_WIDTH), 1) // HEAD_DIM
    n_sub = CHUNK // SUB
    n_chunks = t_blk // CHUNK
    sc_row = lax.broadcasted_iota(jnp.int32, (HEADS * SUB, CHUNK), 0) % SUB
    sc_col = lax.broadcasted_iota(jnp.int32, (HEADS * SUB, CHUNK), 1)

    def gates(cc, gmax):
        r0 = pl.multiple_of(cc * CHUNK, CHUNK)
        sl = pl.ds(r0, CHUNK)
        f = lb + (1.0 - lb) * jax.nn.sigmoid(f_ref[0, sl, :])
        kk = jnp.maximum(1.0 - f, 0.0)
        lf2 = jnp.log(f) * LOG2E
        if l_valid < n_blk * t_blk:
            ok = t * t_blk + r0 + rows < l_valid
            kk = jnp.where(ok, kk, 0.0)
            lf2 = jnp.where(ok, lf2, 0.0)
        bcs2 = _dot01_lhs(tril, lf2)
        b_blk[sl, :] = bcs2
        c_blk[sl, :] = jnp.log(kk) * LOG2E - bcs2
        prev = jnp.zeros((1, GROUP_WIDTH), F32)
        for a in range(n_sub):
            end = bcs2[(a + 1) * SUB - 1:(a + 1) * SUB, :]
            gmax = jnp.maximum(gmax, prev - end)
            prev = end
        return gmax

    gmax = lax.fori_loop(0, n_chunks, gates, jnp.zeros((1, GROUP_WIDTH), F32),
                         unroll=HGRN_INFLIGHT if n_chunks % HGRN_INFLIGHT == 0 else 1)
    factored_ok = jnp.max(gmax) <= HGRN_SAFE_LOG2

    def stack_heads(x):
        return jnp.concatenate([jnp.where(lane_head == h, x, 0.0) for h in range(HEADS)], axis=0).astype(BF16)

    def unstack_heads(x_all):
        out = jnp.zeros((SUB, GROUP_WIDTH), F32)
        for h in range(HEADS):
            out = out + jnp.where(lane_head == h, x_all[h * SUB:(h + 1) * SUB, :], 0.0)
        return out

    def chunks(ccs, factored):
        n = len(ccs)
        r0s = [pl.multiple_of(cc * CHUNK, CHUNK) for cc in ccs]
        sls = [pl.ds(r0, CHUNK) for r0 in r0s]
        qs = [q_ref[0, sl, :] for sl in sls]
        vs = [i_ref[0, sl, :] for sl in sls]
        bs = [b_blk[sl, :] for sl in sls]
        cs = [c_blk[sl, :] for sl in sls]
        vbs = [v.astype(BF16) for v in vs]

        os_ = []
        for s in range(n):
            os_.append(_dot_nt((qs[s] * jnp.exp2(bs[s])).astype(BF16), st_ref[...].astype(BF16)))
            last = bs[s][CHUNK - 1:CHUNK, :]
            kt = jnp.exp2(last + cs[s]).astype(BF16)
            st_ref[...] = st_ref[...] * jnp.exp2(last) + jnp.where(head_eq, _dot_tn(vbs[s], kt), 0.0)

        pieces = [[] for _ in range(n)]
        for a in range(n_sub):
            s0 = a * SUB
            s_end = s0 + SUB if factored else s0
            if s_end == 0:
                for s in range(n):
                    pieces[s].append(jnp.zeros((SUB, GROUP_WIDTH), F32))
                continue
            scs = []
            for s in range(n):
                ref_row = bs[s][s0 - 1:s0, :] if a > 0 else jnp.zeros((1, GROUP_WIDTH), F32)
                qa = qs[s][s0:s0 + SUB, :] * jnp.exp2(bs[s][s0:s0 + SUB, :] - ref_row)
                ka = jnp.where(rows < s_end, jnp.exp2(ref_row + cs[s]), 0.0)
                sc = _dot_nt(stack_heads(qa), ka.astype(BF16))
                if factored:
                    sc = jnp.where(sc_col <= s0 + sc_row, sc, 0.0)
                scs.append(sc.astype(BF16))
            for s in range(n):
                pieces[s].append(unstack_heads(_dot(scs[s], vbs[s])))
        for s in range(n):
            os_[s] = os_[s] + jnp.concatenate(pieces[s], axis=0)

        if not factored:
            for s in range(n):
                def sub_rows(ref, jj):
                    return jnp.concatenate(
                        [jnp.broadcast_to(ref[pl.ds(r0s[s] + a * SUB + jj, 1), :], (SUB, GROUP_WIDTH))
                         for a in range(n_sub)], axis=0)

                for jj in range(SUB):
                    w = jnp.exp2(bs[s] + sub_rows(c_blk, jj))
                    y_scr[jj * CHUNK:(jj + 1) * CHUNK, :] = jnp.where(row_in_sub >= jj, qs[s] * w, 0.0).astype(BF16)
                att = _dot(y_scr[...], ones_bd)
                for jj in range(SUB):
                    os_[s] = os_[s] + att[jj * CHUNK:(jj + 1) * CHUNK, :] * sub_rows(i_ref.at[0], jj)

        mss = [_dot((o * o).astype(BF16), ones_bd) * (1.0 / HEAD_DIM) for o in os_]
        for s in range(n):
            res = os_[s] * lax.rsqrt(mss[s] + EPS) * nrm * _silu(g_ref[0, sls[s], :])
            o_ref[0, sls[s], :] = res.astype(MIX_DTYPE)

    @pl.when(factored_ok)
    def _():
        inflight = HGRN_INFLIGHT if n_chunks % HGRN_INFLIGHT == 0 else 1

        def group(pp, carry):
            chunks([inflight * pp + s for s in range(inflight)], True)
            return carry

        lax.fori_loop(0, n_chunks // inflight, group, 0)

    @pl.when(jnp.logical_not(factored_ok))
    def _():
        def one(cc, carry):
            chunks([cc], False)
            return carry

        lax.fori_loop(0, n_chunks, one, 0)

    @pl.when(t == n_blk - 1)
    def _():
        for h in range(HEADS):
            hs = slice(h * HEAD_DIM, (h + 1) * HEAD_DIM)
            st_out_ref[0, h] = st_ref[hs, hs].T


def _hgrn(proj3, col0, st0, lb, nrm, l_valid, t_blk):
    b, lrows, _ = proj3.shape
    n_blk = -(-l_valid // t_blk)
    assert n_blk * t_blk <= lrows and t_blk % CHUNK == 0
    col = lambda c: pl.BlockSpec((1, t_blk, GROUP_WIDTH),
                                 lambda bi, t: (bi, t, (c - COL_CQ + col0) // GROUP_WIDTH))
    vec = pl.BlockSpec((1, GROUP_WIDTH), lambda bi, t: (0, 0))
    st_spec = pl.BlockSpec((1, HEADS, HEAD_DIM, HEAD_DIM), lambda bi, t: (bi, 0, 0, 0))
    return pl.pallas_call(
        functools.partial(_hgrn_kernel, t_blk=t_blk, l_valid=l_valid, n_blk=n_blk),
        out_shape=(jax.ShapeDtypeStruct((b, n_blk * t_blk, GROUP_WIDTH), MIX_DTYPE),
                   jax.ShapeDtypeStruct((b, HEADS, HEAD_DIM, HEAD_DIM), F32)),
        grid=(b, n_blk),
        in_specs=[col(COL_CQ), col(COL_CF), col(COL_CI), col(COL_CG), st_spec, vec, vec],
        out_specs=(pl.BlockSpec((1, t_blk, GROUP_WIDTH), lambda bi, t: (bi, t, 0)), st_spec),
        scratch_shapes=[pltpu.VMEM((GROUP_WIDTH, GROUP_WIDTH), F32),
                        pltpu.VMEM((t_blk, GROUP_WIDTH), F32),
                        pltpu.VMEM((t_blk, GROUP_WIDTH), F32),
                        pltpu.VMEM((SUB * CHUNK, GROUP_WIDTH), BF16)],
        compiler_params=_cparams(("parallel", "arbitrary")),
        name="hgrn",
    )(proj3, proj3, proj3, proj3, st0, lb, nrm)


POOL_PAD = 24
POOL_FRONT = 8
D_GROUP = 64


def _pool_kernel(x_ref, buf_ref, w_ref, sc_ref, o_ref, buf_out_ref, xp_ref, s2_ref, s4_ref, *, t_blk):
    t = pl.program_id(1)
    hi = POOL_PAD + t_blk
    half = GROUP_WIDTH // 2

    @pl.when(t == 0)
    def _():
        zeros = jnp.zeros((POOL_PAD - D_BUF, GROUP_WIDTH), F32)
        xp_ref[0:POOL_PAD - D_BUF, :] = zeros
        s2_ref[0:POOL_FRONT, :] = zeros[0:POOL_FRONT]
        s4_ref[0:POOL_FRONT, :] = zeros[0:POOL_FRONT]
        xp_ref[POOL_PAD - D_BUF:POOL_PAD, :] = buf_ref[0]

    x = x_ref[0]
    xp_ref[POOL_PAD:hi, :] = x
    s2 = xp_ref[POOL_FRONT:hi, :] + xp_ref[POOL_FRONT - 1:hi - 1, :]
    s2_ref[POOL_FRONT:hi, :] = s2
    s4 = s2 + s2_ref[POOL_FRONT - 2:hi - 2, :]
    s4_ref[POOL_FRONT:hi, :] = s4
    s8 = s4[:, half:] + s4_ref[POOL_FRONT - 4:hi - 4, half:]
    n_ext = hi - POOL_FRONT
    s16 = s8[8:n_ext, :] + s8[0:n_ext - 8, :]
    lo_sum = s2[n_ext - t_blk:, :half]
    lane = lax.broadcasted_iota(jnp.int32, (t_blk, half), 1)
    pooled_lo = jnp.where(lane < D_GROUP, lo_sum * 0.5, s4[n_ext - t_blk:, :half] * 0.25)
    pooled_hi = jnp.where(lane < D_GROUP, s8[n_ext - t_blk:, :] * 0.125, s16[n_ext - 8 - t_blk:, :] * 0.0625)
    pooled = jnp.concatenate([pooled_lo, pooled_hi], axis=1) - x
    o_ref[0] = (_dot(pooled.astype(BF16), w_ref[...]) * sc_ref[...]).astype(MIX_DTYPE)
    tail = xp_ref[hi - D_BUF:hi, :]
    buf_out_ref[0] = tail
    xp_ref[POOL_PAD - D_BUF:POOL_PAD, :] = tail


def _pool(proj3, buf, w_bd, scale, l_valid, t_blk):
    b = proj3.shape[0]
    n_blk = l_valid // t_blk
    assert n_blk * t_blk == l_valid and t_blk >= D_BUF
    buf_spec = pl.BlockSpec((1, D_BUF, GROUP_WIDTH), lambda bi, t: (bi, 0, 0))
    return pl.pallas_call(
        functools.partial(_pool_kernel, t_blk=t_blk),
        out_shape=(jax.ShapeDtypeStruct((b, l_valid, GROUP_WIDTH), MIX_DTYPE),
                   jax.ShapeDtypeStruct((b, D_BUF, GROUP_WIDTH), F32)),
        grid=(b, n_blk),
        in_specs=[pl.BlockSpec((1, t_blk, GROUP_WIDTH), lambda bi, t: (bi, t, COL_DX // GROUP_WIDTH)),
                  buf_spec,
                  pl.BlockSpec((GROUP_WIDTH, GROUP_WIDTH), lambda bi, t: (0, 0)),
                  pl.BlockSpec((1, GROUP_WIDTH), lambda bi, t: (0, 0))],
        out_specs=(pl.BlockSpec((1, t_blk, GROUP_WIDTH), lambda bi, t: (bi, t, 0)), buf_spec),
        scratch_shapes=[pltpu.VMEM((POOL_PAD + t_blk, GROUP_WIDTH), F32)] * 3,
        compiler_params=_cparams(("parallel", "arbitrary")),
        name="pool",
    )(proj3, buf, w_bd, scale)


SSD_BLOCK = 1024
HGRN_BLOCK = 1024
POOL_BLOCK = 2048


def _prep_layer(layer, norm1, w_in, a_rel_bias, b_conv_w, b_conv_b, b_dt_bias, b_a_log, b_d, b_norm, lbs, c_norm,
                d_pool_w, d_pool_scale, w_out, norm2, w_gate_up, w_down):
    w_r = _w_in_prep(jnp.transpose(w_in, (0, 2, 1)), layer)
    tab = a_rel_bias[layer].T
    m = A_BAND + CHUNK
    far = tab[:, 2 * REL_CLIP:]
    n_far = A_BAND_PREV - REL_CLIP + 1
    r = jnp.concatenate([jnp.broadcast_to(far, (HEADS, n_far)),
                         tab[:, 2 * REL_CLIP - 1:CHUNK:-1],
                         jnp.broadcast_to(far, (HEADS, m - n_far - (2 * REL_CLIP - 1 - CHUNK)))], axis=1)
    bias = jnp.tile(r, (1, CHUNK))[:, :CHUNK * (m - 1)].reshape(HEADS, CHUNK, m - 1)[:, :, :A_BAND]
    bias = bias.reshape(HEADS * CHUNK, A_BAND)
    rep = lambda p: jnp.repeat(p[layer], HEAD_DIM)[None, :]
    pw = d_pool_w[layer]
    w_bd = jnp.zeros((GROUP_WIDTH, GROUP_WIDTH), F32)
    for g in range(4):
        w_bd = w_bd.at[g * 64:(g + 1) * 64, g * 64:(g + 1) * 64].set(pw[g])
    return dict(
        n1=norm1[layer][None, :], w_in=w_r, bias=bias,
        cw=b_conv_w[layer], cb=b_conv_b[layer][None, :], dtb=rep(b_dt_bias), alog=rep(b_a_log), dvec=rep(b_d),
        bnrm=b_norm[layer][None, :], lb=lbs[layer][None, :], cnrm=c_norm[layer].reshape(1, GROUP_WIDTH),
        w_bd=w_bd.astype(BF16), psc=d_pool_scale[layer][None, :], n2=norm2[layer][None, :], layer=layer,
        w_out=w_out.astype(BF16), w_gu=w_gate_up.astype(BF16), w_dn=w_down.astype(BF16))


def _round_up(n, m):
    return -(-n // m) * m


def _group_layer(x, st, p, nf, final, cache_t=None):
    k_prev, v_prev, conv0, ssm0, hgrn0, pool0 = st
    b, l, _ = x.shape
    p_hist = k_prev.shape[1]
    proj = _inproj(x.reshape(b * l, D_MODEL), p["n1"], p["w_in"]).reshape(b, l, PROJ_W)
    lq = _round_up(l, CHUNK)
    pad_rows = lambda a: a if lq == l else jnp.pad(a, ((0, 0), (0, lq - l), (0, 0)))

    keep = min(A_BAND_PREV, p_hist + l)
    n_new = min(l, keep)
    ka = proj[:, l - n_new:, COL_AK:COL_AK + GROUP_WIDTH].reshape(b, n_new, HEADS, HEAD_DIM)
    va = proj[:, l - n_new:, COL_AV:COL_AV + GROUP_WIDTH].reshape(b, n_new, HEADS, HEAD_DIM)
    if cache_t is not None:
        bias_s = p["bias"].reshape(HEADS, CHUNK, A_BAND)[:, :l].reshape(HEADS * l, A_BAND)
        oa, kt_new, vt_new = _attention_cached(proj, cache_t[0], cache_t[1], p["layer"], bias_s, l)
        untranspose = lambda a: a.reshape(b, HEADS, HEAD_DIM, A_BAND_PREV).transpose(0, 3, 1, 2)
        new_k, new_v = untranspose(kt_new), untranspose(vt_new)
    else:
        if p_hist == 0:
            k_cache = v_cache = None
        else:
            k_cache = k_prev.astype(BF16).reshape(b, p_hist, GROUP_WIDTH)
            v_cache = v_prev.astype(BF16).reshape(b, p_hist, GROUP_WIDTH)
        oa = _attention(pad_rows(proj), k_cache, v_cache, p["bias"], l)[:, :l]
        new_k = jnp.concatenate([k_prev[:, p_hist - (keep - n_new):], ka], axis=1)
        new_v = jnp.concatenate([v_prev[:, p_hist - (keep - n_new):], va], axis=1)

    ob, new_conv, new_ssm = _ssd(proj, conv0, ssm0, p["cw"], p["cb"], p["dtb"], p["alog"], p["dvec"], p["bnrm"],
                                 l, SSD_BLOCK if l % SSD_BLOCK == 0 else SSD_CHUNK)

    if l % HGRN_BLOCK == 0:
        oc, new_hgrn = _hgrn(proj, COL_CQ, hgrn0, p["lb"], p["cnrm"], l, HGRN_BLOCK)
    else:
        oc, new_hgrn = _hgrn(pad_rows(proj[:, :, COL_CQ:COL_CQ + 4 * GROUP_WIDTH]), 0, hgrn0, p["lb"], p["cnrm"], l, lq)
        oc = oc[:, :l]

    od, new_pool = _pool(proj, pool0, p["w_bd"], p["psc"], l, POOL_BLOCK if l % POOL_BLOCK == 0 else l)

    flat = lambda a: a.reshape(b * l, GROUP_WIDTH)
    y = _mlp(x.reshape(b * l, D_MODEL), (flat(oa), flat(ob), flat(oc), flat(od)), p["layer"],
             p["w_out"], p["n2"], p["w_gu"], p["w_dn"], nf, final)
    return y.reshape(b, l, D_MODEL), (new_k, new_v, new_conv, new_ssm, new_hgrn, new_pool)


def _trunk(x, states, params, nf):
    depth = len(params)
    new = [[] for _ in range(6)]
    b, l, _ = x.shape
    cache_t = None
    if states[0].shape[2] == A_BAND_PREV and l <= CHUNK and l % 8 == 0:
        as_t = lambda c: jnp.transpose(c, (0, 1, 3, 4, 2)).reshape(depth, b, GROUP_WIDTH, A_BAND_PREV)
        cache_t = (as_t(states[0]), as_t(states[1]))
    for layer in range(depth):
        st_l = tuple(s[layer] for s in states)
        x, ns = _group_layer(x, st_l, params[layer], nf, layer == depth - 1, cache_t)
        for i in range(6):
            new[i].append(ns[i])
    return x, tuple(jnp.stack(n, axis=0) for n in new)


def kernel(x_prompt, x_sample, cache_a_k, cache_a_v, state_b_conv, state_b_ssm, state_c_hgrn, state_d_pool, norm1,
           w_in, a_rel_bias, b_conv_w, b_conv_b, b_dt_bias, b_a_log, b_d, b_norm, c_lb_logits, c_norm, d_pool_w,
           d_pool_scale, w_out, norm2, w_gate_up, w_down, norm_f):
    depth = w_in.shape[0]
    lbs = jnp.cumsum(jax.nn.softmax(c_lb_logits.astype(F32), axis=0), axis=0)
    lbs = lbs - lbs[:1]
    params = [_prep_layer(layer, norm1, w_in, a_rel_bias, b_conv_w, b_conv_b, b_dt_bias, b_a_log, b_d, b_norm, lbs,
                          c_norm, d_pool_w, d_pool_scale, w_out, norm2, w_gate_up, w_down)
              for layer in range(depth)]
    nf = norm_f[None, :]
    bp = x_prompt.shape[0]
    prompt_states = (
        jnp.zeros((depth, bp, 0, HEADS, HEAD_DIM), F32),
        jnp.zeros((depth, bp, 0, HEADS, HEAD_DIM), F32),
        jnp.zeros((depth, bp, B_CONV - 1, B_CONV_DIM), F32),
        jnp.zeros((depth, bp, HEADS, HEAD_DIM, B_STATE), F32),
        jnp.zeros((depth, bp, HEADS, HEAD_DIM, HEAD_DIM), F32),
        jnp.zeros((depth, bp, D_BUF, GROUP_WIDTH), F32),
    )
    y_prompt, ps = _trunk(x_prompt, prompt_states, params, nf)
    sample_states = (cache_a_k, cache_a_v, state_b_conv, state_b_ssm, state_c_hgrn, state_d_pool)
    y_sample, ss = _trunk(x_sample, sample_states, params, nf)
    return (y_prompt, y_sample) + ps + ss
```

```python
import functools

import jax
import jax.numpy as jnp
from jax import lax
from jax.experimental import pallas as pl
from jax.experimental.pallas import tpu as pltpu

F32 = jnp.float32
BF16 = jnp.bfloat16
MIX_DTYPE = BF16

D_MODEL = 1024
GROUP_WIDTH = 256
HEADS = 4
HEAD_DIM = 64
CHUNK = 64
A_BAND_PREV = 512
A_BAND = A_BAND_PREV + CHUNK
REL_CLIP = 128
B_STATE = 128
B_CONV = 4
B_CONV_DIM = 768
D_BUF = 15
D_FF = 2816
EPS = 1e-6
NEG = -1e30
LOG2E = 1.4426950408889634
SUB = 16

COL_XBC = 0
COL_AQ = 768
COL_AK = 1024
COL_AV = 1280
COL_BZ = 1536
COL_CQ = 1792
COL_CF = 2048
COL_CI = 2304
COL_CG = 2560
COL_DX = 2816
COL_DT = 3072
PROJ_W = 3328

VMEM_LIMIT = 56 * 1024 * 1024


def _cparams(sem):
    return pltpu.CompilerParams(dimension_semantics=sem, vmem_limit_bytes=VMEM_LIMIT)


def _rms(x, w):
    return x * lax.rsqrt(jnp.mean(x * x, axis=-1, keepdims=True) + EPS) * w


def _silu(x):
    h = 0.5 * x
    return h + h * jnp.tanh(h)


def _dot(a, b):
    return jnp.dot(a, b, preferred_element_type=F32)


def _dot_nt(a, b):
    return lax.dot_general(a, b, (((1,), (1,)), ((), ())), preferred_element_type=F32)


def _dot_tn(a, b):
    return lax.dot_general(a, b, (((0,), (0,)), ((), ())), preferred_element_type=F32)


def _split3(x):
    hi = x.astype(BF16)
    r = x - hi.astype(F32)
    mid = r.astype(BF16)
    lo = (r - mid.astype(F32)).astype(BF16)
    return hi, mid, lo


def _dot01_lhs(m01, x):
    hi, mid, lo = _split3(x)
    return _dot(m01, hi) + _dot(m01, mid) + _dot(m01, lo)


SEQ_BATCH = 8


def _seq_batch(b, n_blk):
    return SEQ_BATCH if n_blk == 1 and b % SEQ_BATCH == 0 else 1


def _per_sequence(kernel_fn, sb, seq_dim):
    if sb == 1:
        return kernel_fn

    def run(*refs):
        for s in range(sb):
            kernel_fn(*[r if d is None else r.at[(slice(None),) * d + (pl.ds(s, 1),)]
                        for r, d in zip(refs, seq_dim)])

    return run


def _tril(t):
    r = lax.broadcasted_iota(jnp.int32, (t, t), 0)
    c = lax.broadcasted_iota(jnp.int32, (t, t), 1)
    return r >= c


def _head_ones():
    r = lax.broadcasted_iota(jnp.int32, (GROUP_WIDTH, GROUP_WIDTH), 0) // HEAD_DIM
    c = lax.broadcasted_iota(jnp.int32, (GROUP_WIDTH, GROUP_WIDTH), 1) // HEAD_DIM
    return r == c


IN_WIDTH = 3076
ORIG_DT = 1792
W_PREP_COLS = 256


def _w_in_prep_kernel(wt_ref, o_ref):
    def put(c0, rows):
        o_ref[:, c0:c0 + W_PREP_COLS] = rows.T.astype(BF16)

    for j in range(3):
        put(COL_XBC + W_PREP_COLS * j, wt_ref[0, 1024 + W_PREP_COLS * j:1024 + W_PREP_COLS * (j + 1), :])
    for j in range(4):
        put(COL_AQ + W_PREP_COLS * j, wt_ref[0, W_PREP_COLS * j:W_PREP_COLS * (j + 1), :])
    for j in range(5):
        r0 = ORIG_DT + HEADS + W_PREP_COLS * j
        put(COL_CQ + W_PREP_COLS * j, wt_ref[0, r0:r0 + W_PREP_COLS, :])
    put(COL_DT, jnp.concatenate([jnp.broadcast_to(wt_ref[0, ORIG_DT + h:ORIG_DT + h + 1, :], (HEAD_DIM, D_MODEL))
                                 for h in range(HEADS)], axis=0))


def _w_in_prep(wt_all, layer):
    return pl.pallas_call(
        _w_in_prep_kernel,
        out_shape=jax.ShapeDtypeStruct((D_MODEL, PROJ_W), BF16),
        grid=(1,),
        in_specs=[pl.BlockSpec((1, IN_WIDTH, D_MODEL), lambda i: (layer, 0, 0), pipeline_mode=pl.Buffered(1))],
        out_specs=pl.BlockSpec((D_MODEL, PROJ_W), lambda i: (0, 0)),
        compiler_params=_cparams(("arbitrary",)),
        name="w_in_prep",
    )(wt_all)


def _inproj_kernel(x_ref, n_ref, w_ref, o_ref):
    hb = _rms(x_ref[...], n_ref[...]).astype(BF16)
    for c0 in range(0, PROJ_W, 256):
        o_ref[:, c0:c0 + 256] = _dot(hb, w_ref[:, c0:c0 + 256])


def _inproj(x2d, n1, w_in):
    rows = x2d.shape[0]
    tm = next((m for m in (1024, 512) if rows % m == 0), rows)
    return pl.pallas_call(
        _inproj_kernel,
        out_shape=jax.ShapeDtypeStruct((rows, PROJ_W), F32),
        grid=(rows // tm,),
        in_specs=[pl.BlockSpec((tm, D_MODEL), lambda i: (i, 0)),
                  pl.BlockSpec((1, D_MODEL), lambda i: (0, 0)),
                  pl.BlockSpec((D_MODEL, PROJ_W), lambda i: (0, 0), pipeline_mode=pl.Buffered(1))],
        out_specs=pl.BlockSpec((tm, PROJ_W), lambda i: (i, 0)),
        compiler_params=_cparams(("parallel",)),
        name="inproj",
    )(x2d, n1, w_in)


CONV_PAD = 8


def _causal_conv_silu(xpad_ref, rows, conv_w, conv_b):
    hist = B_CONV - 1
    y = conv_b
    for i in range(B_CONV):
        y = y + xpad_ref[CONV_PAD - hist + i:CONV_PAD - hist + i + rows, :] * conv_w[i:i + 1, :]
    return _silu(y)


def _softplus(x):
    return jnp.maximum(x, 0.0) + jnp.log1p(jnp.exp(-jnp.abs(x)))


FF_CHUNK = 256


def _mlp_kernel(x_ref, oa_ref, ob_ref, oc_ref, od_ref, wo_ref, n2_ref, wgu_ref, wdn_ref, nf_ref, o_ref, mix_ref,
                act_ref, *, final):
    for m, r in enumerate((oa_ref, ob_ref, oc_ref, od_ref)):
        mix_ref[:, m * GROUP_WIDTH:(m + 1) * GROUP_WIDTH] = r[...]
    x1 = x_ref[...] + _dot(mix_ref[...], wo_ref[0])
    hb = _rms(x1, n2_ref[...]).astype(BF16)
    for c0 in range(0, D_FF, FF_CHUNK):
        gate = _dot(hb, wgu_ref[0, :, c0:c0 + FF_CHUNK])
        up = _dot(hb, wgu_ref[0, :, D_FF + c0:D_FF + c0 + FF_CHUNK])
        act_ref[:, c0:c0 + FF_CHUNK] = (_silu(gate) * up).astype(BF16)
    out = x1 + _dot(act_ref[...], wdn_ref[0])
    if final:
        out = _rms(out, nf_ref[...])
    o_ref[...] = out


def _mlp(x2d, mix, layer, w_out, n2, w_gu, w_dn, nf, final):
    rows = x2d.shape[0]
    tm = 512 if rows % 512 == 0 else rows
    row_spec = lambda w: pl.BlockSpec((tm, w), lambda i: (i, 0))
    vec = lambda a: pl.BlockSpec(a.shape, lambda i: (0, 0))
    wspec = lambda a: pl.BlockSpec((1,) + a.shape[1:], lambda i: (layer, 0, 0), pipeline_mode=pl.Buffered(1))
    return pl.pallas_call(
        functools.partial(_mlp_kernel, final=final),
        out_shape=jax.ShapeDtypeStruct((rows, D_MODEL), F32),
        grid=(rows // tm,),
        in_specs=[row_spec(D_MODEL)] + [row_spec(GROUP_WIDTH)] * 4
                 + [wspec(w_out), vec(n2), wspec(w_gu), wspec(w_dn), vec(nf)],
        out_specs=row_spec(D_MODEL),
        scratch_shapes=[pltpu.VMEM((tm, D_MODEL), BF16), pltpu.VMEM((tm, D_FF), BF16)],
        compiler_params=_cparams(("parallel",)),
        name="mlp",
    )(x2d, *mix, w_out, n2, w_gu, w_dn, nf)


ATTN_BLOCK = 1024
ATTN_UNROLL = 8


def _attn_kernel(q_ref, kh_ref, kc_ref, vh_ref, vc_ref, bias_ref, o_ref, kb_ref, vb_ref, *, qb, p_hist, l_valid):
    i = pl.program_id(1)
    if p_hist == 0:
        @pl.when(i == 0)
        def _():
            kb_ref[0:A_BAND_PREV, :] = jnp.zeros((A_BAND_PREV, GROUP_WIDTH), BF16)
            vb_ref[0:A_BAND_PREV, :] = jnp.zeros((A_BAND_PREV, GROUP_WIDTH), BF16)

        @pl.when(i > 0)
        def _():
            kb_ref[0:A_BAND_PREV, :] = kh_ref[0].astype(BF16)
            vb_ref[0:A_BAND_PREV, :] = vh_ref[0].astype(BF16)
    else:
        kb_ref[0:A_BAND_PREV, :] = kh_ref[0].astype(BF16)
        vb_ref[0:A_BAND_PREV, :] = vh_ref[0].astype(BF16)
    kb_ref[A_BAND_PREV:A_BAND_PREV + qb, :] = kc_ref[0].astype(BF16)
    vb_ref[A_BAND_PREV:A_BAND_PREV + qb, :] = vc_ref[0].astype(BF16)
    lane_head = lax.broadcasted_iota(jnp.int32, (CHUNK, GROUP_WIDTH), 1) // HEAD_DIM
    band_pos = lax.broadcasted_iota(jnp.int32, (1, A_BAND), 1)

    n_chunks = qb // CHUNK
    group = ATTN_UNROLL if n_chunks % ATTN_UNROLL == 0 else 1

    def chunks(gi, carry):
        r0s = [pl.multiple_of((gi * group + u) * CHUNK, CHUNK) for u in range(group)]
        es, dens = [], []
        for r0 in r0s:
            qc = q_ref[0, pl.ds(r0, CHUNK), :] * (HEAD_DIM ** -0.5)
            qs = jnp.concatenate([jnp.where(lane_head == h, qc, 0.0) for h in range(HEADS)], axis=0).astype(BF16)
            s = _dot_nt(qs, kb_ref[pl.ds(r0, A_BAND), :]) + bias_ref[...]
            kpos = i * qb + r0 - A_BAND_PREV + band_pos
            s = jnp.where((kpos >= -p_hist) & (kpos < l_valid), s, NEG)
            e = jnp.exp(s - jnp.max(s, axis=-1, keepdims=True))
            dens.append(jnp.sum(e, axis=-1, keepdims=True))
            es.append(e.astype(BF16))
        o_alls = [_dot(es[u], vb_ref[pl.ds(r0s[u], A_BAND), :]) * (1.0 / dens[u]) for u in range(group)]
        for u in range(group):
            o = jnp.zeros((CHUNK, GROUP_WIDTH), F32)
            for h in range(HEADS):
                o = o + jnp.where(lane_head == h, o_alls[u][h * CHUNK:(h + 1) * CHUNK, :], 0.0)
            o_ref[0, pl.ds(r0s[u], CHUNK), :] = o.astype(MIX_DTYPE)
        return carry

    lax.fori_loop(0, n_chunks // group, chunks, 0)


def _attention(proj3, k_cache, v_cache, bias, l_valid):
    b = proj3.shape[0]
    lq = _round_up(l_valid, CHUNK)
    qb = next((q for q in (ATTN_BLOCK, A_BAND_PREV) if lq % q == 0), lq)
    nblk = lq // qb
    col = lambda c: pl.BlockSpec((1, qb, GROUP_WIDTH), lambda bi, i: (bi, i, c // GROUP_WIDTH))
    if k_cache is None:
        assert qb % A_BAND_PREV == 0
        p_hist = 0
        per = qb // A_BAND_PREV
        prev = lambda c: pl.BlockSpec((1, A_BAND_PREV, GROUP_WIDTH),
                                      lambda bi, i: (bi, jnp.maximum(i * per - 1, 0), c // GROUP_WIDTH))
        kh, vh, kh_spec, vh_spec = proj3, proj3, prev(COL_AK), prev(COL_AV)
    else:
        assert nblk == 1 and k_cache.shape[1] == A_BAND_PREV
        p_hist = A_BAND_PREV
        kh_spec = vh_spec = pl.BlockSpec((1, A_BAND_PREV, GROUP_WIDTH), lambda bi, i: (bi, 0, 0))
        kh, vh = k_cache, v_cache
    return pl.pallas_call(
        functools.partial(_attn_kernel, qb=qb, p_hist=p_hist, l_valid=l_valid),
        out_shape=jax.ShapeDtypeStruct((b, lq, GROUP_WIDTH), MIX_DTYPE),
        grid=(b, nblk),
        in_specs=[col(COL_AQ), kh_spec, col(COL_AK), vh_spec, col(COL_AV),
                  pl.BlockSpec((HEADS * CHUNK, A_BAND), lambda bi, i: (0, 0))],
        out_specs=pl.BlockSpec((1, qb, GROUP_WIDTH), lambda bi, i: (bi, i, 0)),
        scratch_shapes=[pltpu.VMEM((A_BAND_PREV + qb, GROUP_WIDTH), BF16),
                        pltpu.VMEM((A_BAND_PREV + qb, GROUP_WIDTH), BF16)],
        compiler_params=_cparams(("parallel", "arbitrary")),
        name="attention",
    )(proj3, kh, proj3, vh, proj3, bias)


def _attn_cached_kernel(q_ref, kn_ref, vn_ref, kt_ref, vt_ref, bias_ref, o_ref, kt_out_ref, vt_out_ref, *, l_new):
    lane_head = lax.broadcasted_iota(jnp.int32, (l_new, GROUP_WIDTH), 1) // HEAD_DIM
    q = q_ref[0] * (HEAD_DIM ** -0.5)
    qs = jnp.concatenate([jnp.where(lane_head == h, q, 0.0) for h in range(HEADS)], axis=0).astype(BF16)
    kn = kn_ref[0]
    vn = vn_ref[0]
    kt = kt_ref[0, 0]
    vt = vt_ref[0, 0]
    s_old = _dot(qs, kt.astype(BF16)) + bias_ref[:, 0:A_BAND_PREV]
    s_new = _dot_nt(qs, kn.astype(BF16)) + bias_ref[:, A_BAND_PREV:A_BAND_PREV + l_new]
    m = jnp.maximum(jnp.max(s_old, axis=-1, keepdims=True), jnp.max(s_new, axis=-1, keepdims=True))
    e_old = jnp.exp(s_old - m)
    e_new = jnp.exp(s_new - m)
    den = jnp.sum(e_old, axis=-1, keepdims=True) + jnp.sum(e_new, axis=-1, keepdims=True)
    o_all = (_dot_nt(e_old.astype(BF16), vt.astype(BF16)) + _dot(e_new.astype(BF16), vn.astype(BF16))) * (1.0 / den)
    o = jnp.zeros((l_new, GROUP_WIDTH), F32)
    for h in range(HEADS):
        o = o + jnp.where(lane_head == h, o_all[h * l_new:(h + 1) * l_new, :], 0.0)
    o_ref[0] = o.astype(MIX_DTYPE)

    keep = A_BAND_PREV - l_new
    lane = lax.broadcasted_iota(jnp.int32, (GROUP_WIDTH, A_BAND_PREV), 1)
    r = lax.broadcasted_iota(jnp.int32, (l_new, A_BAND_PREV), 0)
    c = lax.broadcasted_iota(jnp.int32, (l_new, A_BAND_PREV), 1)
    place = (c == r + keep).astype(BF16)

    def appended(old_t, new_rows):
        hi, mid, lo = _split3(new_rows)
        new_cols = _dot_tn(hi, place) + _dot_tn(mid, place) + _dot_tn(lo, place)
        return jnp.where(lane < keep, pltpu.roll(old_t, keep, 1), new_cols)

    kt_out_ref[0] = appended(kt, kn)
    vt_out_ref[0] = appended(vt, vn)


def _attention_cached(proj3, kt_cache, vt_cache, layer, bias, l_valid):
    b = proj3.shape[0]
    sb = _seq_batch(b, 1)
    col = lambda c: pl.BlockSpec((sb, l_valid, GROUP_WIDTH), lambda bi: (bi, 0, c // GROUP_WIDTH))
    cache_spec = pl.BlockSpec((1, sb, GROUP_WIDTH, A_BAND_PREV), lambda bi: (layer, bi, 0, 0))
    out_t = pl.BlockSpec((sb, GROUP_WIDTH, A_BAND_PREV), lambda bi: (bi, 0, 0))
    return pl.pallas_call(
        _per_sequence(functools.partial(_attn_cached_kernel, l_new=l_valid), sb, (0, 0, 0, 1, 1, None, 0, 0, 0)),
        out_shape=(jax.ShapeDtypeStruct((b, l_valid, GROUP_WIDTH), MIX_DTYPE),
                   jax.ShapeDtypeStruct((b, GROUP_WIDTH, A_BAND_PREV), F32),
                   jax.ShapeDtypeStruct((b, GROUP_WIDTH, A_BAND_PREV), F32)),
        grid=(b // sb,),
        in_specs=[col(COL_AQ), col(COL_AK), col(COL_AV), cache_spec, cache_spec,
                  pl.BlockSpec((HEADS * l_valid, A_BAND), lambda bi: (0, 0))],
        out_specs=(pl.BlockSpec((sb, l_valid, GROUP_WIDTH), lambda bi: (bi, 0, 0)), out_t, out_t),
        compiler_params=_cparams(("parallel",)),
        name="attention_cached",
    )(proj3, proj3, proj3, kt_cache, vt_cache, bias)


SSD_CHUNK = 128
SSD_INFLIGHT = 4


def _ssd_kernel(xbc_ref, z_ref, dt_ref, conv0_ref, ssm0_ref, cw_ref, cb_ref, dtb_ref, alog_ref, d_ref, nrm_ref,
                o_ref, conv_out_ref, ssm_out_ref, xpad_ref, st_ref, *, t_blk, l_valid, n_blk):
    t = pl.program_id(1)
    hist = B_CONV - 1

    @pl.when(t == 0)
    def _():
        xpad_ref[CONV_PAD - hist:CONV_PAD, :] = conv0_ref[0]
        st_ref[...] = ssm0_ref[0].reshape(GROUP_WIDTH, B_STATE).T

    n_last = l_valid - (n_blk - 1) * t_blk
    conv_out_ref[0] = xbc_ref[0, n_last - hist:n_last, :]

    tc = SSD_CHUNK
    short = l_valid < t_blk

    def rows_of(ref, sl):
        if short:
            return jnp.concatenate([ref[0], jnp.zeros((tc - l_valid, ref.shape[2]), F32)], axis=0)
        return ref[0, sl, :]

    tril = _tril(tc)
    tril_b = tril.astype(BF16)
    lane = lax.broadcasted_iota(jnp.int32, (tc, B_STATE), 1)
    rows = lax.broadcasted_iota(jnp.int32, (tc, GROUP_WIDTH), 0)
    a_neg = -jnp.exp(alog_ref[...])
    conv_b = cb_ref[...]
    conv_w = cw_ref[...]
    dt_bias = dtb_ref[...]
    d_skip = d_ref[...]
    nrm = nrm_ref[...]

    n_chunks = t_blk // tc
    group = SSD_INFLIGHT if n_chunks % SSD_INFLIGHT == 0 else 1
    groups = (slice(0, B_STATE), slice(B_STATE, 2 * B_STATE))

    def chunks(gi, carry):
        rg = group * tc
        base = pl.multiple_of(gi * rg, rg)
        sl_all = pl.ds(base, rg)
        xpad_ref[CONV_PAD:CONV_PAD + rg, :] = rows_of(xbc_ref, sl_all)
        xbc_all = _causal_conv_silu(xpad_ref, rg, conv_w, conv_b)
        xpad_ref[CONV_PAD - hist:CONV_PAD, :] = xpad_ref[CONV_PAD + rg - hist:CONV_PAD + rg, :]
        dt_all = _softplus(rows_of(dt_ref, sl_all) + dt_bias)
        if l_valid < n_blk * t_blk:
            rows_all = lax.broadcasted_iota(jnp.int32, (rg, GROUP_WIDTH), 0)
            dt_all = jnp.where(t * t_blk + base + rows_all < l_valid, dt_all, 0.0)
        z_all = rows_of(z_ref, sl_all)
        xs_all = xbc_all[:, 0:GROUP_WIDTH]
        a_all = dt_all * a_neg
        u_all = xs_all * dt_all

        cut = lambda arr, s: arr[s * tc:(s + 1) * tc, :]
        acss = [_dot01_lhs(tril_b, cut(a_all, s)) for s in range(group)]
        cms = [cut(xbc_all[:, 2 * GROUP_WIDTH:3 * GROUP_WIDTH], s).astype(BF16) for s in range(group)]
        bms = [cut(xbc_all[:, GROUP_WIDTH:2 * GROUP_WIDTH], s).astype(BF16) for s in range(group)]
        cbs = [[_dot_nt(cms[s][:, gs], bms[s][:, gs]) for gs in groups] for s in range(group)]

        y_states = []
        for s in range(group):
            acs = acss[s]
            last = acs[tc - 1:tc, :]
            upb = (cut(u_all, s) * jnp.exp(last - acs)).astype(BF16)
            eacs = jnp.exp(acs)
            elast = jnp.exp(last)
            ys_s = []
            for gs in groups:
                ys_s.append(_dot(cms[s][:, gs], st_ref[:, gs].astype(BF16)) * eacs[:, gs])
                st_ref[:, gs] = st_ref[:, gs] * elast[:, gs] + _dot_tn(bms[s][:, gs], upb[:, gs])
            y_states.append(ys_s)

        for s in range(group):
            acs = acss[s]
            acs_t = acs.T
            ub = cut(u_all, s).astype(BF16)
            ys = []
            for g, gs in enumerate(groups):
                parts = []
                for hh in range(2):
                    c0 = (2 * g + hh) * HEAD_DIM
                    col = jnp.broadcast_to(acs[:, c0:c0 + 1], (tc, tc))
                    rowv = jnp.broadcast_to(acs_t[c0:c0 + 1, :], (tc, tc))
                    dec = jnp.exp(jnp.where(tril, col - rowv, NEG))
                    parts.append(_dot((cbs[s][g] * dec).astype(BF16), ub[:, gs]))
                ys.append(jnp.where(lane < HEAD_DIM, parts[0], parts[1]) + y_states[s][g])
            yv = jnp.concatenate(ys, axis=1) + d_skip * cut(xs_all, s)
            res = _rms(yv * _silu(cut(z_all, s)), nrm).astype(MIX_DTYPE)
            if short:
                o_ref[0] = res[0:l_valid, :]
            else:
                o_ref[0, pl.ds(pl.multiple_of(base + s * tc, tc), tc), :] = res
        return carry

    lax.fori_loop(0, n_chunks // group, chunks, 0)

    @pl.when(t == n_blk - 1)
    def _():
        ssm_out_ref[0] = st_ref[...].T.reshape(HEADS, HEAD_DIM, B_STATE)


def _ssd(proj3, conv0, ssm0, cw, cb, dtb, alog, dvec, nrm, l_valid, t_blk):
    b, lrows, _ = proj3.shape
    n_blk = -(-l_valid // t_blk)
    assert t_blk % SSD_CHUNK == 0 and lrows == l_valid
    assert l_valid == n_blk * t_blk or (t_blk == SSD_CHUNK and B_CONV - 1 <= l_valid < t_blk and l_valid % 8 == 0)
    t_in = min(t_blk, l_valid)
    sb = _seq_batch(b, n_blk)
    vec = lambda w: pl.BlockSpec((1, w), lambda bi, t: (0, 0))
    conv_spec = pl.BlockSpec((sb, B_CONV - 1, B_CONV_DIM), lambda bi, t: (bi, 0, 0))
    ssm_spec = pl.BlockSpec((sb, HEADS, HEAD_DIM, B_STATE), lambda bi, t: (bi, 0, 0, 0))
    return pl.pallas_call(
        _per_sequence(functools.partial(_ssd_kernel, t_blk=t_blk, l_valid=l_valid, n_blk=n_blk), sb,
                      (0,) * 5 + (None,) * 6 + (0,) * 3 + (None,) * 2),
        out_shape=(jax.ShapeDtypeStruct((b, l_valid, GROUP_WIDTH), MIX_DTYPE),
                   jax.ShapeDtypeStruct((b, B_CONV - 1, B_CONV_DIM), F32),
                   jax.ShapeDtypeStruct((b, HEADS, HEAD_DIM, B_STATE), F32)),
        grid=(b // sb, n_blk),
        in_specs=[pl.BlockSpec((sb, t_in, B_CONV_DIM), lambda bi, t: (bi, t, COL_XBC // B_CONV_DIM)),
                  pl.BlockSpec((sb, t_in, GROUP_WIDTH), lambda bi, t: (bi, t, COL_BZ // GROUP_WIDTH)),
                  pl.BlockSpec((sb, t_in, GROUP_WIDTH), lambda bi, t: (bi, t, COL_DT // GROUP_WIDTH)),
                  conv_spec, ssm_spec,
                  pl.BlockSpec((B_CONV, B_CONV_DIM), lambda bi, t: (0, 0)),
                  vec(B_CONV_DIM), vec(GROUP_WIDTH), vec(GROUP_WIDTH), vec(GROUP_WIDTH), vec(GROUP_WIDTH)],
        out_specs=(pl.BlockSpec((sb, t_in, GROUP_WIDTH), lambda bi, t: (bi, t, 0)), conv_spec, ssm_spec),
        scratch_shapes=[pltpu.VMEM((CONV_PAD + SSD_INFLIGHT * SSD_CHUNK, B_CONV_DIM), F32),
                        pltpu.VMEM((B_STATE, GROUP_WIDTH), F32)],
        compiler_params=_cparams(("parallel", "arbitrary")),
        name="ssd",
    )(proj3, proj3, proj3, conv0, ssm0, cw, cb, dtb, alog, dvec, nrm)


HGRN_INFLIGHT = 8
HGRN_SAFE_LOG2 = 64.0


def _hgrn_kernel(q_ref, f_ref, i_ref, g_ref, st0_ref, lb_ref, nrm_ref, o_ref, st_out_ref, st_ref, b_blk, c_blk,
                 y_scr, *, t_blk, l_valid, n_blk):
    t = pl.program_id(1)

    @pl.when(t == 0)
    def _():
        for h in range(HEADS):
            parts = [jnp.zeros((HEAD_DIM, HEAD_DIM * h), F32)] if h > 0 else []
            parts.append(st0_ref[0, h].T)
            if h < HEADS - 1:
                parts.append(jnp.zeros((HEAD_DIM, HEAD_DIM * (HEADS - 1 - h)), F32))
            st_ref[h * HEAD_DIM:(h + 1) * HEAD_DIM, :] = jnp.concatenate(parts, axis=1)

    lb = lb_ref[...]
    nrm = nrm_ref[...]
    head_eq = _head_ones()
    ones_bd = head_eq.astype(BF16)
    tril = _tril(CHUNK).astype(BF16)
    rows = lax.broadcasted_iota(jnp.int32, (CHUNK, GROUP_WIDTH), 0)
    row_in_sub = rows % SUB
    lane_head = lax.broadcasted_iota(jnp.int32, (SUB, GROUP_WIDTH), 1) // HEAD_DIM
    n_sub = CHUNK // SUB
    n_chunks = t_blk // CHUNK
    sc_row = lax.broadcasted_iota(jnp.int32, (HEADS * SUB, CHUNK), 0) % SUB
    sc_col = lax.broadcasted_iota(jnp.int32, (HEADS * SUB, CHUNK), 1)

    def gates(cc, gmax):
        r0 = pl.multiple_of(cc * CHUNK, CHUNK)
        sl = pl.ds(r0, CHUNK)
        f = lb + (1.0 - lb) * jax.nn.sigmoid(f_ref[0, sl, :])
        kk = jnp.maximum(1.0 - f, 0.0)
        lf2 = jnp.log(f) * LOG2E
        if l_valid < n_blk * t_blk:
            ok = t * t_blk + r0 + rows < l_valid
            kk = jnp.where(ok, kk, 0.0)
            lf2 = jnp.where(ok, lf2, 0.0)
        bcs2 = _dot01_lhs(tril, lf2)
        b_blk[sl, :] = bcs2
        c_blk[sl, :] = jnp.log(kk) * LOG2E - bcs2
        prev = jnp.zeros((1, GROUP_WIDTH), F32)
        for a in range(n_sub):
            end = bcs2[(a + 1) * SUB - 1:(a + 1) * SUB, :]
            gmax = jnp.maximum(gmax, prev - end)
            prev = end
        return gmax

    gmax = lax.fori_loop(0, n_chunks, gates, jnp.zeros((1, GROUP_WIDTH), F32),
                         unroll=HGRN_INFLIGHT if n_chunks % HGRN_INFLIGHT == 0 else 1)
    factored_ok = jnp.max(gmax) <= HGRN_SAFE_LOG2

    def stack_heads(x):
        return jnp.concatenate([jnp.where(lane_head == h, x, 0.0) for h in range(HEADS)], axis=0).astype(BF16)

    def unstack_heads(x_all):
        out = jnp.zeros((SUB, GROUP_WIDTH), F32)
        for h in range(HEADS):
            out = out + jnp.where(lane_head == h, x_all[h * SUB:(h + 1) * SUB, :], 0.0)
        return out

    def chunks(ccs, factored):
        n = len(ccs)
        r0s = [pl.multiple_of(cc * CHUNK, CHUNK) for cc in ccs]
        sls = [pl.ds(r0, CHUNK) for r0 in r0s]
        qs = [q_ref[0, sl, :] for sl in sls]
        vs = [i_ref[0, sl, :] for sl in sls]
        bs = [b_blk[sl, :] for sl in sls]
        cs = [c_blk[sl, :] for sl in sls]
        vbs = [v.astype(BF16) for v in vs]

        os_ = []
        for s in range(n):
            os_.append(_dot_nt((qs[s] * jnp.exp2(bs[s])).astype(BF16), st_ref[...].astype(BF16)))
            last = bs[s][CHUNK - 1:CHUNK, :]
            kt = jnp.exp2(last + cs[s]).astype(BF16)
            st_ref[...] = st_ref[...] * jnp.exp2(last) + jnp.where(head_eq, _dot_tn(vbs[s], kt), 0.0)

        pieces = [[] for _ in range(n)]
        for a in range(n_sub):
            s0 = a * SUB
            s_end = s0 + SUB if factored else s0
            if s_end == 0:
                for s in range(n):
                    pieces[s].append(jnp.zeros((SUB, GROUP_WIDTH), F32))
                continue
            scs = []
            for s in range(n):
                ref_row = bs[s][s0 - 1:s0, :] if a > 0 else jnp.zeros((1, GROUP_WIDTH), F32)
                qa = qs[s][s0:s0 + SUB, :] * jnp.exp2(bs[s][s0:s0 + SUB, :] - ref_row)
                ka = jnp.where(rows < s_end, jnp.exp2(ref_row + cs[s]), 0.0)
                sc = _dot_nt(stack_heads(qa), ka.astype(BF16))
                if factored:
                    sc = jnp.where(sc_col <= s0 + sc_row, sc, 0.0)
                scs.append(sc.astype(BF16))
            for s in range(n):
                pieces[s].append(unstack_heads(_dot(scs[s], vbs[s])))
        for s in range(n):
            os_[s] = os_[s] + jnp.concatenate(pieces[s], axis=0)

        if not factored:
            for s in range(n):
                def sub_rows(ref, jj):
                    return jnp.concatenate(
                        [jnp.broadcast_to(ref[pl.ds(r0s[s] + a * SUB + jj, 1), :], (SUB, GROUP_WIDTH))
                         for a in range(n_sub)], axis=0)

                for jj in range(SUB):
                    w = jnp.exp2(bs[s] + sub_rows(c_blk, jj))
                    y_scr[jj * CHUNK:(jj + 1) * CHUNK, :] = jnp.where(row_in_sub >= jj, qs[s] * w, 0.0).astype(BF16)
                att = _dot(y_scr[...], ones_bd)
                for jj in range(SUB):
                    os_[s] = os_[s] + att[jj * CHUNK:(jj + 1) * CHUNK, :] * sub_rows(i_ref.at[0], jj)

        mss = [_dot((o * o).astype(BF16), ones_bd) * (1.0 / HEAD_DIM) for o in os_]
        for s in range(n):
            res = os_[s] * lax.rsqrt(mss[s] + EPS) * nrm * _silu(g_ref[0, sls[s], :])
            o_ref[0, sls[s], :] = res.astype(MIX_DTYPE)

    @pl.when(factored_ok)
    def _():
        inflight = HGRN_INFLIGHT if n_chunks % HGRN_INFLIGHT == 0 else 1

        def group(pp, carry):
            chunks([inflight * pp + s for s in range(inflight)], True)
            return carry

        lax.fori_loop(0, n_chunks // inflight, group, 0)

    @pl.when(jnp.logical_not(factored_ok))
    def _():
        def one(cc, carry):
            chunks([cc], False)
            return carry

        lax.fori_loop(0, n_chunks, one, 0)

    @pl.when(t == n_blk - 1)
    def _():
        for h in range(HEADS):
            hs = slice(h * HEAD_DIM, (h + 1) * HEAD_DIM)
            st_out_ref[0, h] = st_ref[hs, hs].T


def _hgrn(proj3, col0, st0, lb, nrm, l_valid, t_blk):
    b, lrows, _ = proj3.shape
    n_blk = -(-l_valid // t_blk)
    assert n_blk * t_blk <= lrows and t_blk % CHUNK == 0
    sb = _seq_batch(b, n_blk)
    col = lambda c: pl.BlockSpec((sb, t_blk, GROUP_WIDTH),
                                 lambda bi, t: (bi, t, (c - COL_CQ + col0) // GROUP_WIDTH))
    vec = pl.BlockSpec((1, GROUP_WIDTH), lambda bi, t: (0, 0))
    st_spec = pl.BlockSpec((sb, HEADS, HEAD_DIM, HEAD_DIM), lambda bi, t: (bi, 0, 0, 0))
    return pl.pallas_call(
        _per_sequence(functools.partial(_hgrn_kernel, t_blk=t_blk, l_valid=l_valid, n_blk=n_blk), sb,
                      (0,) * 5 + (None,) * 2 + (0,) * 2 + (None,) * 4),
        out_shape=(jax.ShapeDtypeStruct((b, n_blk * t_blk, GROUP_WIDTH), MIX_DTYPE),
                   jax.ShapeDtypeStruct((b, HEADS, HEAD_DIM, HEAD_DIM), F32)),
        grid=(b // sb, n_blk),
        in_specs=[col(COL_CQ), col(COL_CF), col(COL_CI), col(COL_CG), st_spec, vec, vec],
        out_specs=(pl.BlockSpec((sb, t_blk, GROUP_WIDTH), lambda bi, t: (bi, t, 0)), st_spec),
        scratch_shapes=[pltpu.VMEM((GROUP_WIDTH, GROUP_WIDTH), F32),
                        pltpu.VMEM((t_blk, GROUP_WIDTH), F32),
                        pltpu.VMEM((t_blk, GROUP_WIDTH), F32),
                        pltpu.VMEM((SUB * CHUNK, GROUP_WIDTH), BF16)],
        compiler_params=_cparams(("parallel", "arbitrary")),
        name="hgrn",
    )(proj3, proj3, proj3, proj3, st0, lb, nrm)


POOL_PAD = 24
POOL_FRONT = 8
D_GROUP = 64


def _pool_kernel(x_ref, buf_ref, w_ref, sc_ref, o_ref, buf_out_ref, xp_ref, s2_ref, s4_ref, *, t_blk):
    t = pl.program_id(1)
    hi = POOL_PAD + t_blk
    half = GROUP_WIDTH // 2

    @pl.when(t == 0)
    def _():
        zeros = jnp.zeros((POOL_PAD - D_BUF, GROUP_WIDTH), F32)
        xp_ref[0:POOL_PAD - D_BUF, :] = zeros
        s2_ref[0:POOL_FRONT, :] = zeros[0:POOL_FRONT]
        s4_ref[0:POOL_FRONT, :] = zeros[0:POOL_FRONT]
        xp_ref[POOL_PAD - D_BUF:POOL_PAD, :] = buf_ref[0]

    x = x_ref[0]
    xp_ref[POOL_PAD:hi, :] = x
    s2 = xp_ref[POOL_FRONT:hi, :] + xp_ref[POOL_FRONT - 1:hi - 1, :]
    s2_ref[POOL_FRONT:hi, :] = s2
    s4 = s2 + s2_ref[POOL_FRONT - 2:hi - 2, :]
    s4_ref[POOL_FRONT:hi, :] = s4
    s8 = s4[:, half:] + s4_ref[POOL_FRONT - 4:hi - 4, half:]
    n_ext = hi - POOL_FRONT
    s16 = s8[8:n_ext, :] + s8[0:n_ext - 8, :]
    lo_sum = s2[n_ext - t_blk:, :half]
    lane = lax.broadcasted_iota(jnp.int32, (t_blk, half), 1)
    pooled_lo = jnp.where(lane < D_GROUP, lo_sum * 0.5, s4[n_ext - t_blk:, :half] * 0.25)
    pooled_hi = jnp.where(lane < D_GROUP, s8[n_ext - t_blk:, :] * 0.125, s16[n_ext - 8 - t_blk:, :] * 0.0625)
    pooled = jnp.concatenate([pooled_lo, pooled_hi], axis=1) - x
    o_ref[0] = (_dot(pooled.astype(BF16), w_ref[...]) * sc_ref[...]).astype(MIX_DTYPE)
    tail = xp_ref[hi - D_BUF:hi, :]
    buf_out_ref[0] = tail
    xp_ref[POOL_PAD - D_BUF:POOL_PAD, :] = tail


def _pool(proj3, buf, w_bd, scale, l_valid, t_blk):
    b = proj3.shape[0]
    n_blk = l_valid // t_blk
    assert n_blk * t_blk == l_valid and t_blk >= D_BUF
    sb = _seq_batch(b, n_blk)
    buf_spec = pl.BlockSpec((sb, D_BUF, GROUP_WIDTH), lambda bi, t: (bi, 0, 0))
    return pl.pallas_call(
        _per_sequence(functools.partial(_pool_kernel, t_blk=t_blk), sb, (0, 0, None, None, 0, 0, None, None, None)),
        out_shape=(jax.ShapeDtypeStruct((b, l_valid, GROUP_WIDTH), MIX_DTYPE),
                   jax.ShapeDtypeStruct((b, D_BUF, GROUP_WIDTH), F32)),
        grid=(b // sb, n_blk),
        in_specs=[pl.BlockSpec((sb, t_blk, GROUP_WIDTH), lambda bi, t: (bi, t, COL_DX // GROUP_WIDTH)),
                  buf_spec,
                  pl.BlockSpec((GROUP_WIDTH, GROUP_WIDTH), lambda bi, t: (0, 0)),
                  pl.BlockSpec((1, GROUP_WIDTH), lambda bi, t: (0, 0))],
        out_specs=(pl.BlockSpec((sb, t_blk, GROUP_WIDTH), lambda bi, t: (bi, t, 0)), buf_spec),
        scratch_shapes=[pltpu.VMEM((POOL_PAD + t_blk, GROUP_WIDTH), F32)] * 3,
        compiler_params=_cparams(("parallel", "arbitrary")),
        name="pool",
    )(proj3, buf, w_bd, scale)


SSD_BLOCK = 1024
HGRN_BLOCK = 1024
POOL_BLOCK = 2048


def _prep_layer(layer, norm1, w_in, a_rel_bias, b_conv_w, b_conv_b, b_dt_bias, b_a_log, b_d, b_norm, lbs, c_norm,
                d_pool_w, d_pool_scale, w_out, norm2, w_gate_up, w_down):
    w_r = _w_in_prep(jnp.transpose(w_in, (0, 2, 1)), layer)
    tab = a_rel_bias[layer].T
    m = A_BAND + CHUNK
    far = tab[:, 2 * REL_CLIP:]
    n_far = A_BAND_PREV - REL_CLIP + 1
    r = jnp.concatenate([jnp.broadcast_to(far, (HEADS, n_far)),
                         tab[:, 2 * REL_CLIP - 1:CHUNK:-1],
                         jnp.broadcast_to(far, (HEADS, m - n_far - (2 * REL_CLIP - 1 - CHUNK)))], axis=1)
    bias = jnp.tile(r, (1, CHUNK))[:, :CHUNK * (m - 1)].reshape(HEADS, CHUNK, m - 1)[:, :, :A_BAND]
    bias = bias.reshape(HEADS * CHUNK, A_BAND)
    rep = lambda p: jnp.repeat(p[layer], HEAD_DIM)[None, :]
    pw = d_pool_w[layer]
    w_bd = jnp.zeros((GROUP_WIDTH, GROUP_WIDTH), F32)
    for g in range(4):
        w_bd = w_bd.at[g * 64:(g + 1) * 64, g * 64:(g + 1) * 64].set(pw[g])
    return dict(
        n1=norm1[layer][None, :], w_in=w_r, bias=bias,
        cw=b_conv_w[layer], cb=b_conv_b[layer][None, :], dtb=rep(b_dt_bias), alog=rep(b_a_log), dvec=rep(b_d),
        bnrm=b_norm[layer][None, :], lb=lbs[layer][None, :], cnrm=c_norm[layer].reshape(1, GROUP_WIDTH),
        w_bd=w_bd.astype(BF16), psc=d_pool_scale[layer][None, :], n2=norm2[layer][None, :], layer=layer,
        w_out=w_out.astype(BF16), w_gu=w_gate_up.astype(BF16), w_dn=w_down.astype(BF16))


def _round_up(n, m):
    return -(-n // m) * m


def _group_layer(x, st, p, nf, final, cache_t=None):
    k_prev, v_prev, conv0, ssm0, hgrn0, pool0 = st
    b, l, _ = x.shape
    p_hist = k_prev.shape[1]
    proj = _inproj(x.reshape(b * l, D_MODEL), p["n1"], p["w_in"]).reshape(b, l, PROJ_W)
    lq = _round_up(l, CHUNK)
    pad_rows = lambda a: a if lq == l else jnp.pad(a, ((0, 0), (0, lq - l), (0, 0)))

    keep = min(A_BAND_PREV, p_hist + l)
    n_new = min(l, keep)
    ka = proj[:, l - n_new:, COL_AK:COL_AK + GROUP_WIDTH].reshape(b, n_new, HEADS, HEAD_DIM)
    va = proj[:, l - n_new:, COL_AV:COL_AV + GROUP_WIDTH].reshape(b, n_new, HEADS, HEAD_DIM)
    if cache_t is not None:
        bias_s = p["bias"].reshape(HEADS, CHUNK, A_BAND)[:, :l].reshape(HEADS * l, A_BAND)
        oa, kt_new, vt_new = _attention_cached(proj, cache_t[0], cache_t[1], p["layer"], bias_s, l)
        untranspose = lambda a: a.reshape(b, HEADS, HEAD_DIM, A_BAND_PREV).transpose(0, 3, 1, 2)
        new_k, new_v = untranspose(kt_new), untranspose(vt_new)
    else:
        if p_hist == 0:
            k_cache = v_cache = None
        else:
            k_cache = k_prev.astype(BF16).reshape(b, p_hist, GROUP_WIDTH)
            v_cache = v_prev.astype(BF16).reshape(b, p_hist, GROUP_WIDTH)
        oa = _attention(pad_rows(proj), k_cache, v_cache, p["bias"], l)[:, :l]
        new_k = jnp.concatenate([k_prev[:, p_hist - (keep - n_new):], ka], axis=1)
        new_v = jnp.concatenate([v_prev[:, p_hist - (keep - n_new):], va], axis=1)

    ob, new_conv, new_ssm = _ssd(proj, conv0, ssm0, p["cw"], p["cb"], p["dtb"], p["alog"], p["dvec"], p["bnrm"],
                                 l, SSD_BLOCK if l % SSD_BLOCK == 0 else SSD_CHUNK)

    if l % HGRN_BLOCK == 0:
        oc, new_hgrn = _hgrn(proj, COL_CQ, hgrn0, p["lb"], p["cnrm"], l, HGRN_BLOCK)
    else:
        oc, new_hgrn = _hgrn(pad_rows(proj[:, :, COL_CQ:COL_CQ + 4 * GROUP_WIDTH]), 0, hgrn0, p["lb"], p["cnrm"], l, lq)
        oc = oc[:, :l]

    od, new_pool = _pool(proj, pool0, p["w_bd"], p["psc"], l, POOL_BLOCK if l % POOL_BLOCK == 0 else l)

    flat = lambda a: a.reshape(b * l, GROUP_WIDTH)
    y = _mlp(x.reshape(b * l, D_MODEL), (flat(oa), flat(ob), flat(oc), flat(od)), p["layer"],
             p["w_out"], p["n2"], p["w_gu"], p["w_dn"], nf, final)
    return y.reshape(b, l, D_MODEL), (new_k, new_v, new_conv, new_ssm, new_hgrn, new_pool)


def _trunk(x, states, params, nf):
    depth = len(params)
    new = [[] for _ in range(6)]
    b, l, _ = x.shape
    cache_t = None
    if states[0].shape[2] == A_BAND_PREV and l <= CHUNK and l % 8 == 0:
        as_t = lambda c: jnp.transpose(c, (0, 1, 3, 4, 2)).reshape(depth, b, GROUP_WIDTH, A_BAND_PREV)
        cache_t = (as_t(states[0]), as_t(states[1]))
    for layer in range(depth):
        st_l = tuple(s[layer] for s in states)
        x, ns = _group_layer(x, st_l, params[layer], nf, layer == depth - 1, cache_t)
        for i in range(6):
            new[i].append(ns[i])
    return x, tuple(jnp.stack(n, axis=0) for n in new)


def kernel(x_prompt, x_sample, cache_a_k, cache_a_v, state_b_conv, state_b_ssm, state_c_hgrn, state_d_pool, norm1,
           w_in, a_rel_bias, b_conv_w, b_conv_b, b_dt_bias, b_a_log, b_d, b_norm, c_lb_logits, c_norm, d_pool_w,
           d_pool_scale, w_out, norm2, w_gate_up, w_down, norm_f):
    depth = w_in.shape[0]
    lbs = jnp.cumsum(jax.nn.softmax(c_lb_logits.astype(F32), axis=0), axis=0)
    lbs = lbs - lbs[:1]
    params = [_prep_layer(layer, norm1, w_in, a_rel_bias, b_conv_w, b_conv_b, b_dt_bias, b_a_log, b_d, b_norm, lbs,
                          c_norm, d_pool_w, d_pool_scale, w_out, norm2, w_gate_up, w_down)
              for layer in range(depth)]
    nf = norm_f[None, :]
    bp = x_prompt.shape[0]
    prompt_states = (
        jnp.zeros((depth, bp, 0, HEADS, HEAD_DIM), F32),
        jnp.zeros((depth, bp, 0, HEADS, HEAD_DIM), F32),
        jnp.zeros((depth, bp, B_CONV - 1, B_CONV_DIM), F32),
        jnp.zeros((depth, bp, HEADS, HEAD_DIM, B_STATE), F32),
        jnp.zeros((depth, bp, HEADS, HEAD_DIM, HEAD_DIM), F32),
        jnp.zeros((depth, bp, D_BUF, GROUP_WIDTH), F32),
    )
    y_prompt, ps = _trunk(x_prompt, prompt_states, params, nf)
    sample_states = (cache_a_k, cache_a_v, state_b_conv, state_b_ssm, state_c_hgrn, state_d_pool)
    y_sample, ss = _trunk(x_sample, sample_states, params, nf)
    return (y_prompt, y_sample) + ps + ss
```

```python
import functools

import jax
import jax.numpy as jnp
from jax import lax
from jax.experimental import pallas as pl
from jax.experimental.pallas import tpu as pltpu

F32 = jnp.float32
BF16 = jnp.bfloat16
MIX_DTYPE = BF16

D_MODEL = 1024
GROUP_WIDTH = 256
HEADS = 4
HEAD_DIM = 64
CHUNK = 64
A_BAND_PREV = 512
A_BAND = A_BAND_PREV + CHUNK
REL_CLIP = 128
B_STATE = 128
B_CONV = 4
B_CONV_DIM = 768
D_BUF = 15
D_FF = 2816
EPS = 1e-6
NEG = -1e30
LOG2E = 1.4426950408889634
SUB = 16

COL_XBC = 0
COL_AQ = 768
COL_AK = 1024
COL_AV = 1280
COL_BZ = 1536
COL_CQ = 1792
COL_CF = 2048
COL_CI = 2304
COL_CG = 2560
COL_DX = 2816
COL_DT = 3072
PROJ_W = 3328

VMEM_LIMIT = 56 * 1024 * 1024


def _cparams(sem):
    return pltpu.CompilerParams(dimension_semantics=sem, vmem_limit_bytes=VMEM_LIMIT)


def _rms(x, w):
    return x * lax.rsqrt(jnp.mean(x * x, axis=-1, keepdims=True) + EPS) * w


def _silu(x):
    h = 0.5 * x
    return h + h * jnp.tanh(h)


def _dot(a, b):
    return jnp.dot(a, b, preferred_element_type=F32)


def _dot_nt(a, b):
    return lax.dot_general(a, b, (((1,), (1,)), ((), ())), preferred_element_type=F32)


def _dot_tn(a, b):
    return lax.dot_general(a, b, (((0,), (0,)), ((), ())), preferred_element_type=F32)


def _split3(x):
    hi = x.astype(BF16)
    r = x - hi.astype(F32)
    mid = r.astype(BF16)
    lo = (r - mid.astype(F32)).astype(BF16)
    return hi, mid, lo


def _dot01_lhs(m01, x):
    hi, mid, lo = _split3(x)
    return _dot(m01, hi) + _dot(m01, mid) + _dot(m01, lo)


SEQ_BATCH = 8


def _seq_batch(b, n_blk):
    return SEQ_BATCH if n_blk == 1 and b % SEQ_BATCH == 0 else 1


def _per_sequence(kernel_fn, sb, seq_dim):
    if sb == 1:
        return kernel_fn

    def run(*refs):
        for s in range(sb):
            kernel_fn(*[r if d is None else r.at[(slice(None),) * d + (pl.ds(s, 1),)]
                        for r, d in zip(refs, seq_dim)])

    return run


def _tril(t):
    r = lax.broadcasted_iota(jnp.int32, (t, t), 0)
    c = lax.broadcasted_iota(jnp.int32, (t, t), 1)
    return r >= c


def _head_ones():
    r = lax.broadcasted_iota(jnp.int32, (GROUP_WIDTH, GROUP_WIDTH), 0) // HEAD_DIM
    c = lax.broadcasted_iota(jnp.int32, (GROUP_WIDTH, GROUP_WIDTH), 1) // HEAD_DIM
    return r == c


IN_WIDTH = 3076
ORIG_DT = 1792
W_PREP_COLS = 256


def _w_in_prep_kernel(wt_ref, o_ref):
    def put(c0, rows):
        o_ref[:, c0:c0 + W_PREP_COLS] = rows.T.astype(BF16)

    for j in range(3):
        put(COL_XBC + W_PREP_COLS * j, wt_ref[0, 1024 + W_PREP_COLS * j:1024 + W_PREP_COLS * (j + 1), :])
    for j in range(4):
        put(COL_AQ + W_PREP_COLS * j, wt_ref[0, W_PREP_COLS * j:W_PREP_COLS * (j + 1), :])
    for j in range(5):
        r0 = ORIG_DT + HEADS + W_PREP_COLS * j
        put(COL_CQ + W_PREP_COLS * j, wt_ref[0, r0:r0 + W_PREP_COLS, :])
    put(COL_DT, jnp.concatenate([jnp.broadcast_to(wt_ref[0, ORIG_DT + h:ORIG_DT + h + 1, :], (HEAD_DIM, D_MODEL))
                                 for h in range(HEADS)], axis=0))


def _w_in_prep(wt_all, layer):
    return pl.pallas_call(
        _w_in_prep_kernel,
        out_shape=jax.ShapeDtypeStruct((D_MODEL, PROJ_W), BF16),
        grid=(1,),
        in_specs=[pl.BlockSpec((1, IN_WIDTH, D_MODEL), lambda i: (layer, 0, 0), pipeline_mode=pl.Buffered(1))],
        out_specs=pl.BlockSpec((D_MODEL, PROJ_W), lambda i: (0, 0)),
        compiler_params=_cparams(("arbitrary",)),
        name="w_in_prep",
    )(wt_all)


def _inproj_kernel(x_ref, n_ref, w_ref, o_ref):
    hb = _rms(x_ref[...], n_ref[...]).astype(BF16)
    for c0 in range(0, PROJ_W, 256):
        o_ref[:, c0:c0 + 256] = _dot(hb, w_ref[:, c0:c0 + 256])


def _inproj(x2d, n1, w_in):
    rows = x2d.shape[0]
    tm = next((m for m in (1024, 512) if rows % m == 0), rows)
    return pl.pallas_call(
        _inproj_kernel,
        out_shape=jax.ShapeDtypeStruct((rows, PROJ_W), F32),
        grid=(rows // tm,),
        in_specs=[pl.BlockSpec((tm, D_MODEL), lambda i: (i, 0)),
                  pl.BlockSpec((1, D_MODEL), lambda i: (0, 0)),
                  pl.BlockSpec((D_MODEL, PROJ_W), lambda i: (0, 0), pipeline_mode=pl.Buffered(1))],
        out_specs=pl.BlockSpec((tm, PROJ_W), lambda i: (i, 0)),
        compiler_params=_cparams(("parallel",)),
        name="inproj",
    )(x2d, n1, w_in)


CONV_PAD = 8


def _causal_conv_silu(xpad_ref, rows, conv_w, conv_b):
    hist = B_CONV - 1
    y = conv_b
    for i in range(B_CONV):
        y = y + xpad_ref[CONV_PAD - hist + i:CONV_PAD - hist + i + rows, :] * conv_w[i:i + 1, :]
    return _silu(y)


def _softplus(x):
    return jnp.maximum(x, 0.0) + jnp.log1p(jnp.exp(-jnp.abs(x)))


FF_CHUNK = 256


def _mlp_kernel(x_ref, oa_ref, ob_ref, oc_ref, od_ref, wo_ref, n2_ref, wgu_ref, wdn_ref, nf_ref, o_ref, mix_ref,
                act_ref, *, final):
    for m, r in enumerate((oa_ref, ob_ref, oc_ref, od_ref)):
        mix_ref[:, m * GROUP_WIDTH:(m + 1) * GROUP_WIDTH] = r[...]
    x1 = x_ref[...] + _dot(mix_ref[...], wo_ref[0])
    hb = _rms(x1, n2_ref[...]).astype(BF16)
    for c0 in range(0, D_FF, FF_CHUNK):
        gate = _dot(hb, wgu_ref[0, :, c0:c0 + FF_CHUNK])
        up = _dot(hb, wgu_ref[0, :, D_FF + c0:D_FF + c0 + FF_CHUNK])
        act_ref[:, c0:c0 + FF_CHUNK] = (_silu(gate) * up).astype(BF16)
    out = x1 + _dot(act_ref[...], wdn_ref[0])
    if final:
        out = _rms(out, nf_ref[...])
    o_ref[...] = out


def _mlp(x2d, mix, layer, w_out, n2, w_gu, w_dn, nf, final):
    rows = x2d.shape[0]
    tm = 512 if rows % 512 == 0 else rows
    row_spec = lambda w: pl.BlockSpec((tm, w), lambda i: (i, 0))
    vec = lambda a: pl.BlockSpec(a.shape, lambda i: (0, 0))
    wspec = lambda a: pl.BlockSpec((1,) + a.shape[1:], lambda i: (layer, 0, 0), pipeline_mode=pl.Buffered(1))
    return pl.pallas_call(
        functools.partial(_mlp_kernel, final=final),
        out_shape=jax.ShapeDtypeStruct((rows, D_MODEL), F32),
        grid=(rows // tm,),
        in_specs=[row_spec(D_MODEL)] + [row_spec(GROUP_WIDTH)] * 4
                 + [wspec(w_out), vec(n2), wspec(w_gu), wspec(w_dn), vec(nf)],
        out_specs=row_spec(D_MODEL),
        scratch_shapes=[pltpu.VMEM((tm, D_MODEL), BF16), pltpu.VMEM((tm, D_FF), BF16)],
        compiler_params=_cparams(("parallel",)),
        name="mlp",
    )(x2d, *mix, w_out, n2, w_gu, w_dn, nf)


ATTN_BLOCK = 1024
ATTN_UNROLL = 8


def _attn_kernel(q_ref, kh_ref, kc_ref, vh_ref, vc_ref, bias_ref, o_ref, kb_ref, vb_ref, *, qb, p_hist, l_valid):
    i = pl.program_id(1)
    if p_hist == 0:
        @pl.when(i == 0)
        def _():
            kb_ref[0:A_BAND_PREV, :] = jnp.zeros((A_BAND_PREV, GROUP_WIDTH), BF16)
            vb_ref[0:A_BAND_PREV, :] = jnp.zeros((A_BAND_PREV, GROUP_WIDTH), BF16)

        @pl.when(i > 0)
        def _():
            kb_ref[0:A_BAND_PREV, :] = kh_ref[0].astype(BF16)
            vb_ref[0:A_BAND_PREV, :] = vh_ref[0].astype(BF16)
    else:
        kb_ref[0:A_BAND_PREV, :] = kh_ref[0].astype(BF16)
        vb_ref[0:A_BAND_PREV, :] = vh_ref[0].astype(BF16)
    kb_ref[A_BAND_PREV:A_BAND_PREV + qb, :] = kc_ref[0].astype(BF16)
    vb_ref[A_BAND_PREV:A_BAND_PREV + qb, :] = vc_ref[0].astype(BF16)
    lane_head = lax.broadcasted_iota(jnp.int32, (CHUNK, GROUP_WIDTH), 1) // HEAD_DIM
    band_pos = lax.broadcasted_iota(jnp.int32, (1, A_BAND), 1)

    n_chunks = qb // CHUNK
    group = ATTN_UNROLL if n_chunks % ATTN_UNROLL == 0 else 1

    def chunks(gi, carry):
        r0s = [pl.multiple_of((gi * group + u) * CHUNK, CHUNK) for u in range(group)]
        es, dens = [], []
        for r0 in r0s:
            qc = q_ref[0, pl.ds(r0, CHUNK), :] * (HEAD_DIM ** -0.5)
            qs = jnp.concatenate([jnp.where(lane_head == h, qc, 0.0) for h in range(HEADS)], axis=0).astype(BF16)
            s = _dot_nt(qs, kb_ref[pl.ds(r0, A_BAND), :]) + bias_ref[...]
            kpos = i * qb + r0 - A_BAND_PREV + band_pos
            s = jnp.where((kpos >= -p_hist) & (kpos < l_valid), s, NEG)
            e = jnp.exp(s - jnp.max(s, axis=-1, keepdims=True))
            dens.append(jnp.sum(e, axis=-1, keepdims=True))
            es.append(e.astype(BF16))
        o_alls = [_dot(es[u], vb_ref[pl.ds(r0s[u], A_BAND), :]) * (1.0 / dens[u]) for u in range(group)]
        for u in range(group):
            o = jnp.zeros((CHUNK, GROUP_WIDTH), F32)
            for h in range(HEADS):
                o = o + jnp.where(lane_head == h, o_alls[u][h * CHUNK:(h + 1) * CHUNK, :], 0.0)
            o_ref[0, pl.ds(r0s[u], CHUNK), :] = o.astype(MIX_DTYPE)
        return carry

    lax.fori_loop(0, n_chunks // group, chunks, 0)


def _attention(proj3, k_cache, v_cache, bias, l_valid):
    b = proj3.shape[0]
    lq = _round_up(l_valid, CHUNK)
    qb = next((q for q in (ATTN_BLOCK, A_BAND_PREV) if lq % q == 0), lq)
    nblk = lq // qb
    col = lambda c: pl.BlockSpec((1, qb, GROUP_WIDTH), lambda bi, i: (bi, i, c // GROUP_WIDTH))
    if k_cache is None:
        assert qb % A_BAND_PREV == 0
        p_hist = 0
        per = qb // A_BAND_PREV
        prev = lambda c: pl.BlockSpec((1, A_BAND_PREV, GROUP_WIDTH),
                                      lambda bi, i: (bi, jnp.maximum(i * per - 1, 0), c // GROUP_WIDTH))
        kh, vh, kh_spec, vh_spec = proj3, proj3, prev(COL_AK), prev(COL_AV)
    else:
        assert nblk == 1 and k_cache.shape[1] == A_BAND_PREV
        p_hist = A_BAND_PREV
        kh_spec = vh_spec = pl.BlockSpec((1, A_BAND_PREV, GROUP_WIDTH), lambda bi, i: (bi, 0, 0))
        kh, vh = k_cache, v_cache
    return pl.pallas_call(
        functools.partial(_attn_kernel, qb=qb, p_hist=p_hist, l_valid=l_valid),
        out_shape=jax.ShapeDtypeStruct((b, lq, GROUP_WIDTH), MIX_DTYPE),
        grid=(b, nblk),
        in_specs=[col(COL_AQ), kh_spec, col(COL_AK), vh_spec, col(COL_AV),
                  pl.BlockSpec((HEADS * CHUNK, A_BAND), lambda bi, i: (0, 0))],
        out_specs=pl.BlockSpec((1, qb, GROUP_WIDTH), lambda bi, i: (bi, i, 0)),
        scratch_shapes=[pltpu.VMEM((A_BAND_PREV + qb, GROUP_WIDTH), BF16),
                        pltpu.VMEM((A_BAND_PREV + qb, GROUP_WIDTH), BF16)],
        compiler_params=_cparams(("parallel", "arbitrary")),
        name="attention",
    )(proj3, kh, proj3, vh, proj3, bias)


def _attn_cached_kernel(q_ref, kn_ref, vn_ref, kt_ref, vt_ref, bias_ref, o_ref, kt_out_ref, vt_out_ref, *, l_new):
    lane_head = lax.broadcasted_iota(jnp.int32, (l_new, GROUP_WIDTH), 1) // HEAD_DIM
    q = q_ref[0] * (HEAD_DIM ** -0.5)
    qs = jnp.concatenate([jnp.where(lane_head == h, q, 0.0) for h in range(HEADS)], axis=0).astype(BF16)
    kn = kn_ref[0]
    vn = vn_ref[0]
    kt = kt_ref[0, 0]
    vt = vt_ref[0, 0]
    s_old = _dot(qs, kt.astype(BF16)) + bias_ref[:, 0:A_BAND_PREV]
    s_new = _dot_nt(qs, kn.astype(BF16)) + bias_ref[:, A_BAND_PREV:A_BAND_PREV + l_new]
    m = jnp.maximum(jnp.max(s_old, axis=-1, keepdims=True), jnp.max(s_new, axis=-1, keepdims=True))
    e_old = jnp.exp(s_old - m)
    e_new = jnp.exp(s_new - m)
    den = jnp.sum(e_old, axis=-1, keepdims=True) + jnp.sum(e_new, axis=-1, keepdims=True)
    o_all = (_dot_nt(e_old.astype(BF16), vt.astype(BF16)) + _dot(e_new.astype(BF16), vn.astype(BF16))) * (1.0 / den)
    o = jnp.zeros((l_new, GROUP_WIDTH), F32)
    for h in range(HEADS):
        o = o + jnp.where(lane_head == h, o_all[h * l_new:(h + 1) * l_new, :], 0.0)
    o_ref[0] = o.astype(MIX_DTYPE)

    keep = A_BAND_PREV - l_new
    lane = lax.broadcasted_iota(jnp.int32, (GROUP_WIDTH, A_BAND_PREV), 1)
    r = lax.broadcasted_iota(jnp.int32, (l_new, A_BAND_PREV), 0)
    c = lax.broadcasted_iota(jnp.int32, (l_new, A_BAND_PREV), 1)
    place = (c == r + keep).astype(BF16)

    def appended(old_t, new_rows):
        hi, mid, lo = _split3(new_rows)
        new_cols = _dot_tn(hi, place) + _dot_tn(mid, place) + _dot_tn(lo, place)
        return jnp.where(lane < keep, pltpu.roll(old_t, keep, 1), new_cols)

    kt_out_ref[0] = appended(kt, kn)
    vt_out_ref[0] = appended(vt, vn)


def _attention_cached(proj3, kt_cache, vt_cache, layer, bias, l_valid):
    b = proj3.shape[0]
    sb = _seq_batch(b, 1)
    col = lambda c: pl.BlockSpec((sb, l_valid, GROUP_WIDTH), lambda bi: (bi, 0, c // GROUP_WIDTH))
    cache_spec = pl.BlockSpec((1, sb, GROUP_WIDTH, A_BAND_PREV), lambda bi: (layer, bi, 0, 0))
    out_t = pl.BlockSpec((sb, GROUP_WIDTH, A_BAND_PREV), lambda bi: (bi, 0, 0))
    return pl.pallas_call(
        _per_sequence(functools.partial(_attn_cached_kernel, l_new=l_valid), sb, (0, 0, 0, 1, 1, None, 0, 0, 0)),
        out_shape=(jax.ShapeDtypeStruct((b, l_valid, GROUP_WIDTH), MIX_DTYPE),
                   jax.ShapeDtypeStruct((b, GROUP_WIDTH, A_BAND_PREV), F32),
                   jax.ShapeDtypeStruct((b, GROUP_WIDTH, A_BAND_PREV), F32)),
        grid=(b // sb,),
        in_specs=[col(COL_AQ), col(COL_AK), col(COL_AV), cache_spec, cache_spec,
                  pl.BlockSpec((HEADS * l_valid, A_BAND), lambda bi: (0, 0))],
        out_specs=(pl.BlockSpec((sb, l_valid, GROUP_WIDTH), lambda bi: (bi, 0, 0)), out_t, out_t),
        compiler_params=_cparams(("parallel",)),
        name="attention_cached",
    )(proj3, proj3, proj3, kt_cache, vt_cache, bias)


SSD_CHUNK = 128
SSD_INFLIGHT = 4


def _ssd_kernel(xbc_ref, z_ref, dt_ref, conv0_ref, ssm0_ref, cw_ref, cb_ref, dtb_ref, alog_ref, d_ref, nrm_ref,
                o_ref, conv_out_ref, ssm_out_ref, xpad_ref, st_ref, *, t_blk, l_valid, n_blk):
    t = pl.program_id(1)
    hist = B_CONV - 1

    @pl.when(t == 0)
    def _():
        xpad_ref[CONV_PAD - hist:CONV_PAD, :] = conv0_ref[0]
        st_ref[...] = ssm0_ref[0].reshape(GROUP_WIDTH, B_STATE).T

    n_last = l_valid - (n_blk - 1) * t_blk
    conv_out_ref[0] = xbc_ref[0, n_last - hist:n_last, :]

    tc = SSD_CHUNK
    short = l_valid < t_blk

    def rows_of(ref, sl):
        if short:
            return jnp.concatenate([ref[0], jnp.zeros((tc - l_valid, ref.shape[2]), F32)], axis=0)
        return ref[0, sl, :]

    tril = _tril(tc)
    tril_b = tril.astype(BF16)
    lane = lax.broadcasted_iota(jnp.int32, (tc, B_STATE), 1)
    rows = lax.broadcasted_iota(jnp.int32, (tc, GROUP_WIDTH), 0)
    a_neg = -jnp.exp(alog_ref[...])
    conv_b = cb_ref[...]
    conv_w = cw_ref[...]
    dt_bias = dtb_ref[...]
    d_skip = d_ref[...]
    nrm = nrm_ref[...]

    n_chunks = t_blk // tc
    group = SSD_INFLIGHT if n_chunks % SSD_INFLIGHT == 0 else 1
    groups = (slice(0, B_STATE), slice(B_STATE, 2 * B_STATE))

    def chunks(gi, carry):
        rg = group * tc
        base = pl.multiple_of(gi * rg, rg)
        sl_all = pl.ds(base, rg)
        xpad_ref[CONV_PAD:CONV_PAD + rg, :] = rows_of(xbc_ref, sl_all)
        xbc_all = _causal_conv_silu(xpad_ref, rg, conv_w, conv_b)
        xpad_ref[CONV_PAD - hist:CONV_PAD, :] = xpad_ref[CONV_PAD + rg - hist:CONV_PAD + rg, :]
        dt_all = _softplus(rows_of(dt_ref, sl_all) + dt_bias)
        if l_valid < n_blk * t_blk:
            rows_all = lax.broadcasted_iota(jnp.int32, (rg, GROUP_WIDTH), 0)
            dt_all = jnp.where(t * t_blk + base + rows_all < l_valid, dt_all, 0.0)
        z_all = rows_of(z_ref, sl_all)
        xs_all = xbc_all[:, 0:GROUP_WIDTH]
        a_all = dt_all * a_neg
        u_all = xs_all * dt_all

        cut = lambda arr, s: arr[s * tc:(s + 1) * tc, :]
        acss = [_dot01_lhs(tril_b, cut(a_all, s)) for s in range(group)]
        cms = [cut(xbc_all[:, 2 * GROUP_WIDTH:3 * GROUP_WIDTH], s).astype(BF16) for s in range(group)]
        bms = [cut(xbc_all[:, GROUP_WIDTH:2 * GROUP_WIDTH], s).astype(BF16) for s in range(group)]
        cbs = [[_dot_nt(cms[s][:, gs], bms[s][:, gs]) for gs in groups] for s in range(group)]

        y_states = []
        for s in range(group):
            acs = acss[s]
            last = acs[tc - 1:tc, :]
            upb = (cut(u_all, s) * jnp.exp(last - acs)).astype(BF16)
            eacs = jnp.exp(acs)
            elast = jnp.exp(last)
            ys_s = []
            for gs in groups:
                ys_s.append(_dot(cms[s][:, gs], st_ref[:, gs].astype(BF16)) * eacs[:, gs])
                st_ref[:, gs] = st_ref[:, gs] * elast[:, gs] + _dot_tn(bms[s][:, gs], upb[:, gs])
            y_states.append(ys_s)

        for s in range(group):
            acs = acss[s]
            acs_t = acs.T
            ub = cut(u_all, s).astype(BF16)
            ys = []
            for g, gs in enumerate(groups):
                parts = []
                for hh in range(2):
                    c0 = (2 * g + hh) * HEAD_DIM
                    col = jnp.broadcast_to(acs[:, c0:c0 + 1], (tc, tc))
                    rowv = jnp.broadcast_to(acs_t[c0:c0 + 1, :], (tc, tc))
                    dec = jnp.exp(jnp.where(tril, col - rowv, NEG))
                    parts.append(_dot((cbs[s][g] * dec).astype(BF16), ub[:, gs]))
                ys.append(jnp.where(lane < HEAD_DIM, parts[0], parts[1]) + y_states[s][g])
            yv = jnp.concatenate(ys, axis=1) + d_skip * cut(xs_all, s)
            res = _rms(yv * _silu(cut(z_all, s)), nrm).astype(MIX_DTYPE)
            if short:
                o_ref[0] = res[0:l_valid, :]
            else:
                o_ref[0, pl.ds(pl.multiple_of(base + s * tc, tc), tc), :] = res
        return carry

    lax.fori_loop(0, n_chunks // group, chunks, 0)

    @pl.when(t == n_blk - 1)
    def _():
        ssm_out_ref[0] = st_ref[...].T.reshape(HEADS, HEAD_DIM, B_STATE)


def _ssd(proj3, conv0, ssm0, cw, cb, dtb, alog, dvec, nrm, l_valid, t_blk):
    b, lrows, _ = proj3.shape
    n_blk = -(-l_valid // t_blk)
    assert t_blk % SSD_CHUNK == 0 and lrows == l_valid
    assert l_valid == n_blk * t_blk or (t_blk == SSD_CHUNK and B_CONV - 1 <= l_valid < t_blk and l_valid % 8 == 0)
    t_in = min(t_blk, l_valid)
    sb = _seq_batch(b, n_blk)
    vec = lambda w: pl.BlockSpec((1, w), lambda bi, t: (0, 0))
    conv_spec = pl.BlockSpec((sb, B_CONV - 1, B_CONV_DIM), lambda bi, t: (bi, 0, 0))
    ssm_spec = pl.BlockSpec((sb, HEADS, HEAD_DIM, B_STATE), lambda bi, t: (bi, 0, 0, 0))
    return pl.pallas_call(
        _per_sequence(functools.partial(_ssd_kernel, t_blk=t_blk, l_valid=l_valid, n_blk=n_blk), sb,
                      (0,) * 5 + (None,) * 6 + (0,) * 3 + (None,) * 2),
        out_shape=(jax.ShapeDtypeStruct((b, l_valid, GROUP_WIDTH), MIX_DTYPE),
                   jax.ShapeDtypeStruct((b, B_CONV - 1, B_CONV_DIM), F32),
                   jax.ShapeDtypeStruct((b, HEADS, HEAD_DIM, B_STATE), F32)),
        grid=(b // sb, n_blk),
        in_specs=[pl.BlockSpec((sb, t_in, B_CONV_DIM), lambda bi, t: (bi, t, COL_XBC // B_CONV_DIM)),
                  pl.BlockSpec((sb, t_in, GROUP_WIDTH), lambda bi, t: (bi, t, COL_BZ // GROUP_WIDTH)),
                  pl.BlockSpec((sb, t_in, GROUP_WIDTH), lambda bi, t: (bi, t, COL_DT // GROUP_WIDTH)),
                  conv_spec, ssm_spec,
                  pl.BlockSpec((B_CONV, B_CONV_DIM), lambda bi, t: (0, 0)),
                  vec(B_CONV_DIM), vec(GROUP_WIDTH), vec(GROUP_WIDTH), vec(GROUP_WIDTH), vec(GROUP_WIDTH)],
        out_specs=(pl.BlockSpec((sb, t_in, GROUP_WIDTH), lambda bi, t: (bi, t, 0)), conv_spec, ssm_spec),
        scratch_shapes=[pltpu.VMEM((CONV_PAD + SSD_INFLIGHT * SSD_CHUNK, B_CONV_DIM), F32),
                        pltpu.VMEM((B_STATE, GROUP_WIDTH), F32)],
        compiler_params=_cparams(("parallel", "arbitrary")),
        name="ssd",
    )(proj3, proj3, proj3, conv0, ssm0, cw, cb, dtb, alog, dvec, nrm)


HGRN_INFLIGHT = 8
HGRN_SAFE_LOG2 = 64.0


def _hgrn_kernel(q_ref, f_ref, i_ref, g_ref, st0_ref, lb_ref, nrm_ref, o_ref, st_out_ref, st_ref, b_blk, c_blk,
                 y_scr, *, t_blk, l_valid, n_blk, sb):
    t = pl.program_id(1)

    @pl.when(t == 0)
    def _():
        for q in range(sb):
            for h in range(HEADS):
                parts = [jnp.zeros((HEAD_DIM, HEAD_DIM * h), F32)] if h > 0 else []
                parts.append(st0_ref[q, h].T)
                if h < HEADS - 1:
                    parts.append(jnp.zeros((HEAD_DIM, HEAD_DIM * (HEADS - 1 - h)), F32))
                st_ref[q, h * HEAD_DIM:(h + 1) * HEAD_DIM, :] = jnp.concatenate(parts, axis=1)

    lb = lb_ref[...]
    nrm = nrm_ref[...]
    head_eq = _head_ones()
    ones_bd = head_eq.astype(BF16)
    tril = _tril(CHUNK).astype(BF16)
    rows = lax.broadcasted_iota(jnp.int32, (CHUNK, GROUP_WIDTH), 0)
    row_in_sub = rows % SUB
    lane_head = lax.broadcasted_iota(jnp.int32, (SUB, GROUP_WIDTH), 1) // HEAD_DIM
    n_sub = CHUNK // SUB
    n_chunks = t_blk // CHUNK
    sc_row = lax.broadcasted_iota(jnp.int32, (HEADS * SUB, CHUNK), 0) % SUB
    sc_col = lax.broadcasted_iota(jnp.int32, (HEADS * SUB, CHUNK), 1)

    def gates(q, cc, gmax):
        r0 = pl.multiple_of(cc * CHUNK, CHUNK)
        sl = pl.ds(r0, CHUNK)
        f = lb + (1.0 - lb) * jax.nn.sigmoid(f_ref[q, sl, :])
        kk = jnp.maximum(1.0 - f, 0.0)
        lf2 = jnp.log(f) * LOG2E
        if l_valid < n_blk * t_blk:
            ok = t * t_blk + r0 + rows < l_valid
            kk = jnp.where(ok, kk, 0.0)
            lf2 = jnp.where(ok, lf2, 0.0)
        bcs2 = _dot01_lhs(tril, lf2)
        b_blk[q, sl, :] = bcs2
        c_blk[q, sl, :] = jnp.log(kk) * LOG2E - bcs2
        prev = jnp.zeros((1, GROUP_WIDTH), F32)
        for a in range(n_sub):
            end = bcs2[(a + 1) * SUB - 1:(a + 1) * SUB, :]
            gmax = jnp.maximum(gmax, prev - end)
            prev = end
        return gmax

    gmax = jnp.zeros((1, GROUP_WIDTH), F32)
    for q in range(sb):
        gmax = lax.fori_loop(0, n_chunks, functools.partial(gates, q), gmax,
                             unroll=HGRN_INFLIGHT if n_chunks % HGRN_INFLIGHT == 0 else 1)
    factored_ok = jnp.max(gmax) <= HGRN_SAFE_LOG2

    def stack_heads(x):
        return jnp.concatenate([jnp.where(lane_head == h, x, 0.0) for h in range(HEADS)], axis=0).astype(BF16)

    def unstack_heads(x_all):
        out = jnp.zeros((SUB, GROUP_WIDTH), F32)
        for h in range(HEADS):
            out = out + jnp.where(lane_head == h, x_all[h * SUB:(h + 1) * SUB, :], 0.0)
        return out

    def chunks(items, factored):
        n = len(items)
        seqs = [q for q, _ in items]
        r0s = [pl.multiple_of(cc * CHUNK, CHUNK) for _, cc in items]
        sls = [pl.ds(r0, CHUNK) for r0 in r0s]
        qs = [q_ref[seqs[s], sls[s], :] for s in range(n)]
        vs = [i_ref[seqs[s], sls[s], :] for s in range(n)]
        bs = [b_blk[seqs[s], sls[s], :] for s in range(n)]
        cs = [c_blk[seqs[s], sls[s], :] for s in range(n)]
        vbs = [v.astype(BF16) for v in vs]

        os_ = []
        for s in range(n):
            st = st_ref.at[seqs[s]]
            os_.append(_dot_nt((qs[s] * jnp.exp2(bs[s])).astype(BF16), st[...].astype(BF16)))
            last = bs[s][CHUNK - 1:CHUNK, :]
            kt = jnp.exp2(last + cs[s]).astype(BF16)
            st[...] = st[...] * jnp.exp2(last) + jnp.where(head_eq, _dot_tn(vbs[s], kt), 0.0)

        pieces = [[] for _ in range(n)]
        for a in range(n_sub):
            s0 = a * SUB
            s_end = s0 + SUB if factored else s0
            if s_end == 0:
                for s in range(n):
                    pieces[s].append(jnp.zeros((SUB, GROUP_WIDTH), F32))
                continue
            scs = []
            for s in range(n):
                ref_row = bs[s][s0 - 1:s0, :] if a > 0 else jnp.zeros((1, GROUP_WIDTH), F32)
                qa = qs[s][s0:s0 + SUB, :] * jnp.exp2(bs[s][s0:s0 + SUB, :] - ref_row)
                ka = jnp.where(rows < s_end, jnp.exp2(ref_row + cs[s]), 0.0)
                sc = _dot_nt(stack_heads(qa), ka.astype(BF16))
                if factored:
                    sc = jnp.where(sc_col <= s0 + sc_row, sc, 0.0)
                scs.append(sc.astype(BF16))
            for s in range(n):
                pieces[s].append(unstack_heads(_dot(scs[s], vbs[s])))
        for s in range(n):
            os_[s] = os_[s] + jnp.concatenate(pieces[s], axis=0)

        if not factored:
            for s in range(n):
                def sub_rows(ref, jj):
                    return jnp.concatenate(
                        [jnp.broadcast_to(ref[pl.ds(r0s[s] + a * SUB + jj, 1), :], (SUB, GROUP_WIDTH))
                         for a in range(n_sub)], axis=0)

                for jj in range(SUB):
                    w = jnp.exp2(bs[s] + sub_rows(c_blk.at[seqs[s]], jj))
                    y_scr[jj * CHUNK:(jj + 1) * CHUNK, :] = jnp.where(row_in_sub >= jj, qs[s] * w, 0.0).astype(BF16)
                att = _dot(y_scr[...], ones_bd)
                for jj in range(SUB):
                    os_[s] = os_[s] + att[jj * CHUNK:(jj + 1) * CHUNK, :] * sub_rows(i_ref.at[seqs[s]], jj)

        mss = [_dot((o * o).astype(BF16), ones_bd) * (1.0 / HEAD_DIM) for o in os_]
        for s in range(n):
            res = os_[s] * lax.rsqrt(mss[s] + EPS) * nrm * _silu(g_ref[seqs[s], sls[s], :])
            o_ref[seqs[s], sls[s], :] = res.astype(MIX_DTYPE)

    @pl.when(factored_ok)
    def _():
        if sb > 1:
            chunks([(q, 0) for q in range(sb)], True)
        else:
            inflight = HGRN_INFLIGHT if n_chunks % HGRN_INFLIGHT == 0 else 1

            def group(pp, carry):
                chunks([(0, inflight * pp + s) for s in range(inflight)], True)
                return carry

            lax.fori_loop(0, n_chunks // inflight, group, 0)

    @pl.when(jnp.logical_not(factored_ok))
    def _():
        for q in range(sb):
            def one(cc, carry):
                chunks([(q, cc)], False)
                return carry

            lax.fori_loop(0, n_chunks, one, 0)

    @pl.when(t == n_blk - 1)
    def _():
        for q in range(sb):
            for h in range(HEADS):
                hs = slice(h * HEAD_DIM, (h + 1) * HEAD_DIM)
                st_out_ref[q, h] = st_ref[q, hs, hs].T


def _hgrn(proj3, col0, st0, lb, nrm, l_valid, t_blk):
    b, lrows, _ = proj3.shape
    n_blk = -(-l_valid // t_blk)
    assert n_blk * t_blk <= lrows and t_blk % CHUNK == 0
    sb = _seq_batch(b, n_blk)
    col = lambda c: pl.BlockSpec((sb, t_blk, GROUP_WIDTH),
                                 lambda bi, t: (bi, t, (c - COL_CQ + col0) // GROUP_WIDTH))
    vec = pl.BlockSpec((1, GROUP_WIDTH), lambda bi, t: (0, 0))
    st_spec = pl.BlockSpec((sb, HEADS, HEAD_DIM, HEAD_DIM), lambda bi, t: (bi, 0, 0, 0))
    return pl.pallas_call(
        functools.partial(_hgrn_kernel, t_blk=t_blk, l_valid=l_valid, n_blk=n_blk, sb=sb),
        out_shape=(jax.ShapeDtypeStruct((b, n_blk * t_blk, GROUP_WIDTH), MIX_DTYPE),
                   jax.ShapeDtypeStruct((b, HEADS, HEAD_DIM, HEAD_DIM), F32)),
        grid=(b // sb, n_blk),
        in_specs=[col(COL_CQ), col(COL_CF), col(COL_CI), col(COL_CG), st_spec, vec, vec],
        out_specs=(pl.BlockSpec((sb, t_blk, GROUP_WIDTH), lambda bi, t: (bi, t, 0)), st_spec),
        scratch_shapes=[pltpu.VMEM((sb, GROUP_WIDTH, GROUP_WIDTH), F32),
                        pltpu.VMEM((sb, t_blk, GROUP_WIDTH), F32),
                        pltpu.VMEM((sb, t_blk, GROUP_WIDTH), F32),
                        pltpu.VMEM((SUB * CHUNK, GROUP_WIDTH), BF16)],
        compiler_params=_cparams(("parallel", "arbitrary")),
        name="hgrn",
    )(proj3, proj3, proj3, proj3, st0, lb, nrm)


POOL_PAD = 24
POOL_FRONT = 8
D_GROUP = 64


def _pool_kernel(x_ref, buf_ref, w_ref, sc_ref, o_ref, buf_out_ref, xp_ref, s2_ref, s4_ref, *, t_blk):
    t = pl.program_id(1)
    hi = POOL_PAD + t_blk
    half = GROUP_WIDTH // 2

    @pl.when(t == 0)
    def _():
        zeros = jnp.zeros((POOL_PAD - D_BUF, GROUP_WIDTH), F32)
        xp_ref[0:POOL_PAD - D_BUF, :] = zeros
        s2_ref[0:POOL_FRONT, :] = zeros[0:POOL_FRONT]
        s4_ref[0:POOL_FRONT, :] = zeros[0:POOL_FRONT]
        xp_ref[POOL_PAD - D_BUF:POOL_PAD, :] = buf_ref[0]

    x = x_ref[0]
    xp_ref[POOL_PAD:hi, :] = x
    s2 = xp_ref[POOL_FRONT:hi, :] + xp_ref[POOL_FRONT - 1:hi - 1, :]
    s2_ref[POOL_FRONT:hi, :] = s2
    s4 = s2 + s2_ref[POOL_FRONT - 2:hi - 2, :]
    s4_ref[POOL_FRONT:hi, :] = s4
    s8 = s4[:, half:] + s4_ref[POOL_FRONT - 4:hi - 4, half:]
    n_ext = hi - POOL_FRONT
    s16 = s8[8:n_ext, :] + s8[0:n_ext - 8, :]
    lo_sum = s2[n_ext - t_blk:, :half]
    lane = lax.broadcasted_iota(jnp.int32, (t_blk, half), 1)
    pooled_lo = jnp.where(lane < D_GROUP, lo_sum * 0.5, s4[n_ext - t_blk:, :half] * 0.25)
    pooled_hi = jnp.where(lane < D_GROUP, s8[n_ext - t_blk:, :] * 0.125, s16[n_ext - 8 - t_blk:, :] * 0.0625)
    pooled = jnp.concatenate([pooled_lo, pooled_hi], axis=1) - x
    o_ref[0] = (_dot(pooled.astype(BF16), w_ref[...]) * sc_ref[...]).astype(MIX_DTYPE)
    tail = xp_ref[hi - D_BUF:hi, :]
    buf_out_ref[0] = tail
    xp_ref[POOL_PAD - D_BUF:POOL_PAD, :] = tail


def _pool(proj3, buf, w_bd, scale, l_valid, t_blk):
    b = proj3.shape[0]
    n_blk = l_valid // t_blk
    assert n_blk * t_blk == l_valid and t_blk >= D_BUF
    sb = _seq_batch(b, n_blk)
    buf_spec = pl.BlockSpec((sb, D_BUF, GROUP_WIDTH), lambda bi, t: (bi, 0, 0))
    return pl.pallas_call(
        _per_sequence(functools.partial(_pool_kernel, t_blk=t_blk), sb, (0, 0, None, None, 0, 0, None, None, None)),
        out_shape=(jax.ShapeDtypeStruct((b, l_valid, GROUP_WIDTH), MIX_DTYPE),
                   jax.ShapeDtypeStruct((b, D_BUF, GROUP_WIDTH), F32)),
        grid=(b // sb, n_blk),
        in_specs=[pl.BlockSpec((sb, t_blk, GROUP_WIDTH), lambda bi, t: (bi, t, COL_DX // GROUP_WIDTH)),
                  buf_spec,
                  pl.BlockSpec((GROUP_WIDTH, GROUP_WIDTH), lambda bi, t: (0, 0)),
                  pl.BlockSpec((1, GROUP_WIDTH), lambda bi, t: (0, 0))],
        out_specs=(pl.BlockSpec((sb, t_blk, GROUP_WIDTH), lambda bi, t: (bi, t, 0)), buf_spec),
        scratch_shapes=[pltpu.VMEM((POOL_PAD + t_blk, GROUP_WIDTH), F32)] * 3,
        compiler_params=_cparams(("parallel", "arbitrary")),
        name="pool",
    )(proj3, buf, w_bd, scale)


SSD_BLOCK = 1024
HGRN_BLOCK = 1024
POOL_BLOCK = 2048


def _prep_layer(layer, norm1, w_in, a_rel_bias, b_conv_w, b_conv_b, b_dt_bias, b_a_log, b_d, b_norm, lbs, c_norm,
                d_pool_w, d_pool_scale, w_out, norm2, w_gate_up, w_down):
    w_r = _w_in_prep(jnp.transpose(w_in, (0, 2, 1)), layer)
    tab = a_rel_bias[layer].T
    m = A_BAND + CHUNK
    far = tab[:, 2 * REL_CLIP:]
    n_far = A_BAND_PREV - REL_CLIP + 1
    r = jnp.concatenate([jnp.broadcast_to(far, (HEADS, n_far)),
                         tab[:, 2 * REL_CLIP - 1:CHUNK:-1],
                         jnp.broadcast_to(far, (HEADS, m - n_far - (2 * REL_CLIP - 1 - CHUNK)))], axis=1)
    bias = jnp.tile(r, (1, CHUNK))[:, :CHUNK * (m - 1)].reshape(HEADS, CHUNK, m - 1)[:, :, :A_BAND]
    bias = bias.reshape(HEADS * CHUNK, A_BAND)
    rep = lambda p: jnp.repeat(p[layer], HEAD_DIM)[None, :]
    pw = d_pool_w[layer]
    w_bd = jnp.zeros((GROUP_WIDTH, GROUP_WIDTH), F32)
    for g in range(4):
        w_bd = w_bd.at[g * 64:(g + 1) * 64, g * 64:(g + 1) * 64].set(pw[g])
    return dict(
        n1=norm1[layer][None, :], w_in=w_r, bias=bias,
        cw=b_conv_w[layer], cb=b_conv_b[layer][None, :], dtb=rep(b_dt_bias), alog=rep(b_a_log), dvec=rep(b_d),
        bnrm=b_norm[layer][None, :], lb=lbs[layer][None, :], cnrm=c_norm[layer].reshape(1, GROUP_WIDTH),
        w_bd=w_bd.astype(BF16), psc=d_pool_scale[layer][None, :], n2=norm2[layer][None, :], layer=layer,
        w_out=w_out.astype(BF16), w_gu=w_gate_up.astype(BF16), w_dn=w_down.astype(BF16))


def _round_up(n, m):
    return -(-n // m) * m


def _group_layer(x, st, p, nf, final, cache_t=None):
    k_prev, v_prev, conv0, ssm0, hgrn0, pool0 = st
    b, l, _ = x.shape
    p_hist = k_prev.shape[1]
    proj = _inproj(x.reshape(b * l, D_MODEL), p["n1"], p["w_in"]).reshape(b, l, PROJ_W)
    lq = _round_up(l, CHUNK)
    pad_rows = lambda a: a if lq == l else jnp.pad(a, ((0, 0), (0, lq - l), (0, 0)))

    keep = min(A_BAND_PREV, p_hist + l)
    n_new = min(l, keep)
    ka = proj[:, l - n_new:, COL_AK:COL_AK + GROUP_WIDTH].reshape(b, n_new, HEADS, HEAD_DIM)
    va = proj[:, l - n_new:, COL_AV:COL_AV + GROUP_WIDTH].reshape(b, n_new, HEADS, HEAD_DIM)
    if cache_t is not None:
        bias_s = p["bias"].reshape(HEADS, CHUNK, A_BAND)[:, :l].reshape(HEADS * l, A_BAND)
        oa, kt_new, vt_new = _attention_cached(proj, cache_t[0], cache_t[1], p["layer"], bias_s, l)
        untranspose = lambda a: a.reshape(b, HEADS, HEAD_DIM, A_BAND_PREV).transpose(0, 3, 1, 2)
        new_k, new_v = untranspose(kt_new), untranspose(vt_new)
    else:
        if p_hist == 0:
            k_cache = v_cache = None
        else:
            k_cache = k_prev.astype(BF16).reshape(b, p_hist, GROUP_WIDTH)
            v_cache = v_prev.astype(BF16).reshape(b, p_hist, GROUP_WIDTH)
        oa = _attention(pad_rows(proj), k_cache, v_cache, p["bias"], l)[:, :l]
        new_k = jnp.concatenate([k_prev[:, p_hist - (keep - n_new):], ka], axis=1)
        new_v = jnp.concatenate([v_prev[:, p_hist - (keep - n_new):], va], axis=1)

    ob, new_conv, new_ssm = _ssd(proj, conv0, ssm0, p["cw"], p["cb"], p["dtb"], p["alog"], p["dvec"], p["bnrm"],
                                 l, SSD_BLOCK if l % SSD_BLOCK == 0 else SSD_CHUNK)

    if l % HGRN_BLOCK == 0:
        oc, new_hgrn = _hgrn(proj, COL_CQ, hgrn0, p["lb"], p["cnrm"], l, HGRN_BLOCK)
    else:
        oc, new_hgrn = _hgrn(pad_rows(proj[:, :, COL_CQ:COL_CQ + 4 * GROUP_WIDTH]), 0, hgrn0, p["lb"], p["cnrm"], l, lq)
        oc = oc[:, :l]

    od, new_pool = _pool(proj, pool0, p["w_bd"], p["psc"], l, POOL_BLOCK if l % POOL_BLOCK == 0 else l)

    flat = lambda a: a.reshape(b * l, GROUP_WIDTH)
    y = _mlp(x.reshape(b * l, D_MODEL), (flat(oa), flat(ob), flat(oc), flat(od)), p["layer"],
             p["w_out"], p["n2"], p["w_gu"], p["w_dn"], nf, final)
    return y.reshape(b, l, D_MODEL), (new_k, new_v, new_conv, new_ssm, new_hgrn, new_pool)


def _trunk(x, states, params, nf):
    depth = len(params)
    new = [[] for _ in range(6)]
    b, l, _ = x.shape
    cache_t = None
    if states[0].shape[2] == A_BAND_PREV and l <= CHUNK and l % 8 == 0:
        as_t = lambda c: jnp.transpose(c, (0, 1, 3, 4, 2)).reshape(depth, b, GROUP_WIDTH, A_BAND_PREV)
        cache_t = (as_t(states[0]), as_t(states[1]))
    for layer in range(depth):
        st_l = tuple(s[layer] for s in states)
        x, ns = _group_layer(x, st_l, params[layer], nf, layer == depth - 1, cache_t)
        for i in range(6):
            new[i].append(ns[i])
    return x, tuple(jnp.stack(n, axis=0) for n in new)


def kernel(x_prompt, x_sample, cache_a_k, cache_a_v, state_b_conv, state_b_ssm, state_c_hgrn, state_d_pool, norm1,
           w_in, a_rel_bias, b_conv_w, b_conv_b, b_dt_bias, b_a_log, b_d, b_norm, c_lb_logits, c_norm, d_pool_w,
           d_pool_scale, w_out, norm2, w_gate_up, w_down, norm_f):
    depth = w_in.shape[0]
    lbs = jnp.cumsum(jax.nn.softmax(c_lb_logits.astype(F32), axis=0), axis=0)
    lbs = lbs - lbs[:1]
    params = [_prep_layer(layer, norm1, w_in, a_rel_bias, b_conv_w, b_conv_b, b_dt_bias, b_a_log, b_d, b_norm, lbs,
                          c_norm, d_pool_w, d_pool_scale, w_out, norm2, w_gate_up, w_down)
              for layer in range(depth)]
    nf = norm_f[None, :]
    bp = x_prompt.shape[0]
    prompt_states = (
        jnp.zeros((depth, bp, 0, HEADS, HEAD_DIM), F32),
        jnp.zeros((depth, bp, 0, HEADS, HEAD_DIM), F32),
        jnp.zeros((depth, bp, B_CONV - 1, B_CONV_DIM), F32),
        jnp.zeros((depth, bp, HEADS, HEAD_DIM, B_STATE), F32),
        jnp.zeros((depth, bp, HEADS, HEAD_DIM, HEAD_DIM), F32),
        jnp.zeros((depth, bp, D_BUF, GROUP_WIDTH), F32),
    )
    y_prompt, ps = _trunk(x_prompt, prompt_states, params, nf)
    sample_states = (cache_a_k, cache_a_v, state_b_conv, state_b_ssm, state_c_hgrn, state_d_pool)
    y_sample, ss = _trunk(x_sample, sample_states, params, nf)
    return (y_prompt, y_sample) + ps + ss
```

```python
import functools

import jax
import jax.numpy as jnp
from jax import lax
from jax.experimental import pallas as pl
from jax.experimental.pallas import tpu as pltpu

F32 = jnp.float32
BF16 = jnp.bfloat16
MIX_DTYPE = BF16

D_MODEL = 1024
GROUP_WIDTH = 256
HEADS = 4
HEAD_DIM = 64
CHUNK = 64
A_BAND_PREV = 512
A_BAND = A_BAND_PREV + CHUNK
REL_CLIP = 128
B_STATE = 128
B_CONV = 4
B_CONV_DIM = 768
D_BUF = 15
D_FF = 2816
EPS = 1e-6
NEG = -1e30
LOG2E = 1.4426950408889634
SUB = 16

COL_XBC = 0
COL_AQ = 768
COL_AK = 1024
COL_AV = 1280
COL_BZ = 1536
COL_CQ = 1792
COL_CF = 2048
COL_CI = 2304
COL_CG = 2560
COL_DX = 2816
COL_DT = 3072
PROJ_W = 3328

VMEM_LIMIT = 56 * 1024 * 1024


def _cparams(sem):
    return pltpu.CompilerParams(dimension_semantics=sem, vmem_limit_bytes=VMEM_LIMIT)


def _rms(x, w):
    return x * lax.rsqrt(jnp.mean(x * x, axis=-1, keepdims=True) + EPS) * w


def _silu(x):
    h = 0.5 * x
    return h + h * jnp.tanh(h)


def _dot(a, b):
    return jnp.dot(a, b, preferred_element_type=F32)


def _dot_nt(a, b):
    return lax.dot_general(a, b, (((1,), (1,)), ((), ())), preferred_element_type=F32)


def _dot_tn(a, b):
    return lax.dot_general(a, b, (((0,), (0,)), ((), ())), preferred_element_type=F32)


def _split3(x):
    hi = x.astype(BF16)
    r = x - hi.astype(F32)
    mid = r.astype(BF16)
    lo = (r - mid.astype(F32)).astype(BF16)
    return hi, mid, lo


def _dot01_lhs(m01, x):
    hi, mid, lo = _split3(x)
    return _dot(m01, hi) + _dot(m01, mid) + _dot(m01, lo)


SEQ_BATCH = 8


def _seq_batch(b, n_blk):
    return SEQ_BATCH if n_blk == 1 and b % SEQ_BATCH == 0 else 1


def _per_sequence(kernel_fn, sb, seq_dim):
    if sb == 1:
        return kernel_fn

    def run(*refs):
        for s in range(sb):
            kernel_fn(*[r if d is None else r.at[(slice(None),) * d + (pl.ds(s, 1),)]
                        for r, d in zip(refs, seq_dim)])

    return run


def _tril(t):
    r = lax.broadcasted_iota(jnp.int32, (t, t), 0)
    c = lax.broadcasted_iota(jnp.int32, (t, t), 1)
    return r >= c


def _head_ones():
    r = lax.broadcasted_iota(jnp.int32, (GROUP_WIDTH, GROUP_WIDTH), 0) // HEAD_DIM
    c = lax.broadcasted_iota(jnp.int32, (GROUP_WIDTH, GROUP_WIDTH), 1) // HEAD_DIM
    return r == c


IN_WIDTH = 3076
ORIG_DT = 1792
W_PREP_COLS = 256


def _w_in_prep_kernel(wt_ref, o_ref):
    def put(c0, rows):
        o_ref[:, c0:c0 + W_PREP_COLS] = rows.T.astype(BF16)

    for j in range(3):
        put(COL_XBC + W_PREP_COLS * j, wt_ref[0, 1024 + W_PREP_COLS * j:1024 + W_PREP_COLS * (j + 1), :])
    for j in range(4):
        put(COL_AQ + W_PREP_COLS * j, wt_ref[0, W_PREP_COLS * j:W_PREP_COLS * (j + 1), :])
    for j in range(5):
        r0 = ORIG_DT + HEADS + W_PREP_COLS * j
        put(COL_CQ + W_PREP_COLS * j, wt_ref[0, r0:r0 + W_PREP_COLS, :])
    put(COL_DT, jnp.concatenate([jnp.broadcast_to(wt_ref[0, ORIG_DT + h:ORIG_DT + h + 1, :], (HEAD_DIM, D_MODEL))
                                 for h in range(HEADS)], axis=0))


def _w_in_prep(wt_all, layer):
    return pl.pallas_call(
        _w_in_prep_kernel,
        out_shape=jax.ShapeDtypeStruct((D_MODEL, PROJ_W), BF16),
        grid=(1,),
        in_specs=[pl.BlockSpec((1, IN_WIDTH, D_MODEL), lambda i: (layer, 0, 0), pipeline_mode=pl.Buffered(1))],
        out_specs=pl.BlockSpec((D_MODEL, PROJ_W), lambda i: (0, 0)),
        compiler_params=_cparams(("arbitrary",)),
        name="w_in_prep",
    )(wt_all)


def _is_last_step():
    return pl.program_id(0) == pl.num_programs(0) - 1


def _inproj_tile(x_ref, n_ref, w_ref, o_ref):
    hb = _rms(x_ref[...], n_ref[...]).astype(BF16)
    for c0 in range(0, PROJ_W, 256):
        o_ref[:, c0:c0 + 256] = _dot(hb, w_ref[:, c0:c0 + 256])


def _inproj_kernel(x_ref, n_ref, w_ref, *rest):
    if len(rest) == 1:
        _inproj_tile(x_ref, n_ref, w_ref, rest[0])
    else:
        xe_ref, o_ref, oe_ref = rest
        _inproj_tile(x_ref, n_ref, w_ref, o_ref)
        pl.when(_is_last_step())(lambda: _inproj_tile(xe_ref, n_ref, w_ref, oe_ref))


def _inproj(x2d, n1, w_in, extra=None):
    rows = x2d.shape[0]
    tm = next((m for m in (1024, 512) if rows % m == 0), rows)
    whole = lambda a: pl.BlockSpec(a.shape, lambda i: (0, 0))
    in_specs = [pl.BlockSpec((tm, D_MODEL), lambda i: (i, 0)),
                pl.BlockSpec((1, D_MODEL), lambda i: (0, 0)),
                pl.BlockSpec((D_MODEL, PROJ_W), lambda i: (0, 0), pipeline_mode=pl.Buffered(1))]
    out_shape = jax.ShapeDtypeStruct((rows, PROJ_W), F32)
    out_specs = pl.BlockSpec((tm, PROJ_W), lambda i: (i, 0))
    operands = (x2d, n1, w_in)
    if extra is not None:
        out_e = jax.ShapeDtypeStruct((extra.shape[0], PROJ_W), F32)
        in_specs, operands = in_specs + [whole(extra)], operands + (extra,)
        out_shape, out_specs = (out_shape, out_e), (out_specs, whole(out_e))
    return pl.pallas_call(
        _inproj_kernel,
        out_shape=out_shape,
        grid=(rows // tm,),
        in_specs=in_specs,
        out_specs=out_specs,
        compiler_params=_cparams(("arbitrary",)),
        name="inproj",
    )(*operands)


CONV_PAD = 8


def _causal_conv_silu(xpad_ref, rows, conv_w, conv_b):
    hist = B_CONV - 1
    y = conv_b
    for i in range(B_CONV):
        y = y + xpad_ref[CONV_PAD - hist + i:CONV_PAD - hist + i + rows, :] * conv_w[i:i + 1, :]
    return _silu(y)


def _softplus(x):
    return jnp.maximum(x, 0.0) + jnp.log1p(jnp.exp(-jnp.abs(x)))


FF_CHUNK = 256


def _mlp_tile(x_ref, mix_refs, wo_ref, n2_ref, wgu_ref, wdn_ref, nf_ref, o_ref, mix_ref, act_ref, final):
    for m, r in enumerate(mix_refs):
        mix_ref[:, m * GROUP_WIDTH:(m + 1) * GROUP_WIDTH] = r[...]
    x1 = x_ref[...] + _dot(mix_ref[...], wo_ref[0])
    hb = _rms(x1, n2_ref[...]).astype(BF16)
    for c0 in range(0, D_FF, FF_CHUNK):
        gate = _dot(hb, wgu_ref[0, :, c0:c0 + FF_CHUNK])
        up = _dot(hb, wgu_ref[0, :, D_FF + c0:D_FF + c0 + FF_CHUNK])
        act_ref[:, c0:c0 + FF_CHUNK] = (_silu(gate) * up).astype(BF16)
    out = x1 + _dot(act_ref[...], wdn_ref[0])
    if final:
        out = _rms(out, nf_ref[...])
    o_ref[...] = out


def _mlp_kernel(*refs, final, has_extra):
    n_in = 10
    x_ref, oa_ref, ob_ref, oc_ref, od_ref, wo_ref, n2_ref, wgu_ref, wdn_ref, nf_ref = refs[:n_in]
    weights = (wo_ref, n2_ref, wgu_ref, wdn_ref, nf_ref)
    if not has_extra:
        o_ref, mix_ref, act_ref = refs[n_in:]
        _mlp_tile(x_ref, (oa_ref, ob_ref, oc_ref, od_ref), *weights, o_ref, mix_ref, act_ref, final)
    else:
        xe_ref, ea_ref, eb_ref, ec_ref, ed_ref, o_ref, oe_ref, mix_ref, act_ref, mixe_ref, acte_ref = refs[n_in:]
        _mlp_tile(x_ref, (oa_ref, ob_ref, oc_ref, od_ref), *weights, o_ref, mix_ref, act_ref, final)
        pl.when(_is_last_step())(
            lambda: _mlp_tile(xe_ref, (ea_ref, eb_ref, ec_ref, ed_ref), *weights, oe_ref, mixe_ref, acte_ref, final))


def _mlp(x2d, mix, layer, w_out, n2, w_gu, w_dn, nf, final, extra=None):
    rows = x2d.shape[0]
    tm = 512 if rows % 512 == 0 else rows
    row_spec = lambda w: pl.BlockSpec((tm, w), lambda i: (i, 0))
    whole = lambda a: pl.BlockSpec(a.shape, lambda i: (0, 0))
    wspec = lambda a: pl.BlockSpec((1,) + a.shape[1:], lambda i: (layer, 0, 0), pipeline_mode=pl.Buffered(1))
    in_specs = ([row_spec(D_MODEL)] + [row_spec(GROUP_WIDTH)] * 4
                + [wspec(w_out), whole(n2), wspec(w_gu), wspec(w_dn), whole(nf)])
    operands = (x2d, *mix, w_out, n2, w_gu, w_dn, nf)
    out_shape = jax.ShapeDtypeStruct((rows, D_MODEL), F32)
    out_specs = row_spec(D_MODEL)
    scratch = [pltpu.VMEM((tm, D_MODEL), BF16), pltpu.VMEM((tm, D_FF), BF16)]
    if extra is not None:
        x_e, mix_e = extra
        out_e = jax.ShapeDtypeStruct(x_e.shape, F32)
        in_specs = in_specs + [whole(x_e)] + [whole(m) for m in mix_e]
        operands = operands + (x_e, *mix_e)
        out_shape, out_specs = (out_shape, out_e), (out_specs, whole(out_e))
        scratch = scratch + [pltpu.VMEM((x_e.shape[0], D_MODEL), BF16), pltpu.VMEM((x_e.shape[0], D_FF), BF16)]
    return pl.pallas_call(
        functools.partial(_mlp_kernel, final=final, has_extra=extra is not None),
        out_shape=out_shape,
        grid=(rows // tm,),
        in_specs=in_specs,
        out_specs=out_specs,
        scratch_shapes=scratch,
        compiler_params=_cparams(("arbitrary",)),
        name="mlp",
    )(*operands)


ATTN_BLOCK = 1024
ATTN_UNROLL = 8


def _attn_kernel(q_ref, kh_ref, kc_ref, vh_ref, vc_ref, bias_ref, o_ref, kb_ref, vb_ref, *, qb, p_hist, l_valid):
    i = pl.program_id(1)
    if p_hist == 0:
        @pl.when(i == 0)
        def _():
            kb_ref[0:A_BAND_PREV, :] = jnp.zeros((A_BAND_PREV, GROUP_WIDTH), BF16)
            vb_ref[0:A_BAND_PREV, :] = jnp.zeros((A_BAND_PREV, GROUP_WIDTH), BF16)

        @pl.when(i > 0)
        def _():
            kb_ref[0:A_BAND_PREV, :] = kh_ref[0].astype(BF16)
            vb_ref[0:A_BAND_PREV, :] = vh_ref[0].astype(BF16)
    else:
        kb_ref[0:A_BAND_PREV, :] = kh_ref[0].astype(BF16)
        vb_ref[0:A_BAND_PREV, :] = vh_ref[0].astype(BF16)
    kb_ref[A_BAND_PREV:A_BAND_PREV + qb, :] = kc_ref[0].astype(BF16)
    vb_ref[A_BAND_PREV:A_BAND_PREV + qb, :] = vc_ref[0].astype(BF16)
    lane_head = lax.broadcasted_iota(jnp.int32, (CHUNK, GROUP_WIDTH), 1) // HEAD_DIM
    band_pos = lax.broadcasted_iota(jnp.int32, (1, A_BAND), 1)

    n_chunks = qb // CHUNK
    group = ATTN_UNROLL if n_chunks % ATTN_UNROLL == 0 else 1

    def chunks(gi, carry):
        r0s = [pl.multiple_of((gi * group + u) * CHUNK, CHUNK) for u in range(group)]
        es, dens = [], []
        for r0 in r0s:
            qc = q_ref[0, pl.ds(r0, CHUNK), :] * (HEAD_DIM ** -0.5)
            qs = jnp.concatenate([jnp.where(lane_head == h, qc, 0.0) for h in range(HEADS)], axis=0).astype(BF16)
            s = _dot_nt(qs, kb_ref[pl.ds(r0, A_BAND), :]) + bias_ref[...]
            kpos = i * qb + r0 - A_BAND_PREV + band_pos
            s = jnp.where((kpos >= -p_hist) & (kpos < l_valid), s, NEG)
            e = jnp.exp(s - jnp.max(s, axis=-1, keepdims=True))
            dens.append(jnp.sum(e, axis=-1, keepdims=True))
            es.append(e.astype(BF16))
        o_alls = [_dot(es[u], vb_ref[pl.ds(r0s[u], A_BAND), :]) * (1.0 / dens[u]) for u in range(group)]
        for u in range(group):
            o = jnp.zeros((CHUNK, GROUP_WIDTH), F32)
            for h in range(HEADS):
                o = o + jnp.where(lane_head == h, o_alls[u][h * CHUNK:(h + 1) * CHUNK, :], 0.0)
            o_ref[0, pl.ds(r0s[u], CHUNK), :] = o.astype(MIX_DTYPE)
        return carry

    lax.fori_loop(0, n_chunks // group, chunks, 0)


def _attention(proj3, k_cache, v_cache, bias, l_valid):
    b = proj3.shape[0]
    lq = _round_up(l_valid, CHUNK)
    qb = next((q for q in (ATTN_BLOCK, A_BAND_PREV) if lq % q == 0), lq)
    nblk = lq // qb
    col = lambda c: pl.BlockSpec((1, qb, GROUP_WIDTH), lambda bi, i: (bi, i, c // GROUP_WIDTH))
    if k_cache is None:
        assert qb % A_BAND_PREV == 0
        p_hist = 0
        per = qb // A_BAND_PREV
        prev = lambda c: pl.BlockSpec((1, A_BAND_PREV, GROUP_WIDTH),
                                      lambda bi, i: (bi, jnp.maximum(i * per - 1, 0), c // GROUP_WIDTH))
        kh, vh, kh_spec, vh_spec = proj3, proj3, prev(COL_AK), prev(COL_AV)
    else:
        assert nblk == 1 and k_cache.shape[1] == A_BAND_PREV
        p_hist = A_BAND_PREV
        kh_spec = vh_spec = pl.BlockSpec((1, A_BAND_PREV, GROUP_WIDTH), lambda bi, i: (bi, 0, 0))
        kh, vh = k_cache, v_cache
    return pl.pallas_call(
        functools.partial(_attn_kernel, qb=qb, p_hist=p_hist, l_valid=l_valid),
        out_shape=jax.ShapeDtypeStruct((b, lq, GROUP_WIDTH), MIX_DTYPE),
        grid=(b, nblk),
        in_specs=[col(COL_AQ), kh_spec, col(COL_AK), vh_spec, col(COL_AV),
                  pl.BlockSpec((HEADS * CHUNK, A_BAND), lambda bi, i: (0, 0))],
        out_specs=pl.BlockSpec((1, qb, GROUP_WIDTH), lambda bi, i: (bi, i, 0)),
        scratch_shapes=[pltpu.VMEM((A_BAND_PREV + qb, GROUP_WIDTH), BF16),
                        pltpu.VMEM((A_BAND_PREV + qb, GROUP_WIDTH), BF16)],
        compiler_params=_cparams(("parallel", "arbitrary")),
        name="attention",
    )(proj3, kh, proj3, vh, proj3, bias)


def _attn_cached_kernel(q_ref, kn_ref, vn_ref, kt_ref, vt_ref, bias_ref, o_ref, kt_out_ref, vt_out_ref, *, l_new):
    lane_head = lax.broadcasted_iota(jnp.int32, (l_new, GROUP_WIDTH), 1) // HEAD_DIM
    q = q_ref[0] * (HEAD_DIM ** -0.5)
    qs = jnp.concatenate([jnp.where(lane_head == h, q, 0.0) for h in range(HEADS)], axis=0).astype(BF16)
    kn = kn_ref[0]
    vn = vn_ref[0]
    kt = kt_ref[0, 0]
    vt = vt_ref[0, 0]
    s_old = _dot(qs, kt.astype(BF16)) + bias_ref[:, 0:A_BAND_PREV]
    s_new = _dot_nt(qs, kn.astype(BF16)) + bias_ref[:, A_BAND_PREV:A_BAND_PREV + l_new]
    m = jnp.maximum(jnp.max(s_old, axis=-1, keepdims=True), jnp.max(s_new, axis=-1, keepdims=True))
    e_old = jnp.exp(s_old - m)
    e_new = jnp.exp(s_new - m)
    den = jnp.sum(e_old, axis=-1, keepdims=True) + jnp.sum(e_new, axis=-1, keepdims=True)
    o_all = (_dot_nt(e_old.astype(BF16), vt.astype(BF16)) + _dot(e_new.astype(BF16), vn.astype(BF16))) * (1.0 / den)
    o = jnp.zeros((l_new, GROUP_WIDTH), F32)
    for h in range(HEADS):
        o = o + jnp.where(lane_head == h, o_all[h * l_new:(h + 1) * l_new, :], 0.0)
    o_ref[0] = o.astype(MIX_DTYPE)

    keep = A_BAND_PREV - l_new
    lane = lax.broadcasted_iota(jnp.int32, (GROUP_WIDTH, A_BAND_PREV), 1)
    r = lax.broadcasted_iota(jnp.int32, (l_new, A_BAND_PREV), 0)
    c = lax.broadcasted_iota(jnp.int32, (l_new, A_BAND_PREV), 1)
    place = (c == r + keep).astype(BF16)

    def appended(old_t, new_rows):
        hi, mid, lo = _split3(new_rows)
        new_cols = _dot_tn(hi, place) + _dot_tn(mid, place) + _dot_tn(lo, place)
        return jnp.where(lane < keep, pltpu.roll(old_t, keep, 1), new_cols)

    kt_out_ref[0] = appended(kt, kn)
    vt_out_ref[0] = appended(vt, vn)


def _attention_cached(proj3, kt_cache, vt_cache, layer, bias, l_valid):
    b = proj3.shape[0]
    sb = _seq_batch(b, 1)
    col = lambda c: pl.BlockSpec((sb, l_valid, GROUP_WIDTH), lambda bi: (bi, 0, c // GROUP_WIDTH))
    cache_spec = pl.BlockSpec((1, sb, GROUP_WIDTH, A_BAND_PREV), lambda bi: (layer, bi, 0, 0))
    out_t = pl.BlockSpec((sb, GROUP_WIDTH, A_BAND_PREV), lambda bi: (bi, 0, 0))
    return pl.pallas_call(
        _per_sequence(functools.partial(_attn_cached_kernel, l_new=l_valid), sb, (0, 0, 0, 1, 1, None, 0, 0, 0)),
        out_shape=(jax.ShapeDtypeStruct((b, l_valid, GROUP_WIDTH), MIX_DTYPE),
                   jax.ShapeDtypeStruct((b, GROUP_WIDTH, A_BAND_PREV), F32),
                   jax.ShapeDtypeStruct((b, GROUP_WIDTH, A_BAND_PREV), F32)),
        grid=(b // sb,),
        in_specs=[col(COL_AQ), col(COL_AK), col(COL_AV), cache_spec, cache_spec,
                  pl.BlockSpec((HEADS * l_valid, A_BAND), lambda bi: (0, 0))],
        out_specs=(pl.BlockSpec((sb, l_valid, GROUP_WIDTH), lambda bi: (bi, 0, 0)), out_t, out_t),
        compiler_params=_cparams(("parallel",)),
        name="attention_cached",
    )(proj3, proj3, proj3, kt_cache, vt_cache, bias)


SSD_CHUNK = 128
SSD_INFLIGHT = 4


def _ssd_kernel(xbc_ref, z_ref, dt_ref, conv0_ref, ssm0_ref, cw_ref, cb_ref, dtb_ref, alog_ref, d_ref, nrm_ref,
                o_ref, conv_out_ref, ssm_out_ref, xpad_ref, st_ref, *, t_blk, l_valid, n_blk):
    t = pl.program_id(1)
    hist = B_CONV - 1

    @pl.when(t == 0)
    def _():
        xpad_ref[CONV_PAD - hist:CONV_PAD, :] = conv0_ref[0]
        st_ref[...] = ssm0_ref[0].reshape(GROUP_WIDTH, B_STATE).T

    n_last = l_valid - (n_blk - 1) * t_blk
    conv_out_ref[0] = xbc_ref[0, n_last - hist:n_last, :]

    tc = SSD_CHUNK
    short = l_valid < t_blk

    def rows_of(ref, sl):
        if short:
            return jnp.concatenate([ref[0], jnp.zeros((tc - l_valid, ref.shape[2]), F32)], axis=0)
        return ref[0, sl, :]

    tril = _tril(tc)
    tril_b = tril.astype(BF16)
    lane = lax.broadcasted_iota(jnp.int32, (tc, B_STATE), 1)
    rows = lax.broadcasted_iota(jnp.int32, (tc, GROUP_WIDTH), 0)
    a_neg = -jnp.exp(alog_ref[...])
    conv_b = cb_ref[...]
    conv_w = cw_ref[...]
    dt_bias = dtb_ref[...]
    d_skip = d_ref[...]
    nrm = nrm_ref[...]

    n_chunks = t_blk // tc
    group = SSD_INFLIGHT if n_chunks % SSD_INFLIGHT == 0 else 1
    groups = (slice(0, B_STATE), slice(B_STATE, 2 * B_STATE))

    def chunks(gi, carry):
        rg = group * tc
        base = pl.multiple_of(gi * rg, rg)
        sl_all = pl.ds(base, rg)
        xpad_ref[CONV_PAD:CONV_PAD + rg, :] = rows_of(xbc_ref, sl_all)
        xbc_all = _causal_conv_silu(xpad_ref, rg, conv_w, conv_b)
        xpad_ref[CONV_PAD - hist:CONV_PAD, :] = xpad_ref[CONV_PAD + rg - hist:CONV_PAD + rg, :]
        dt_all = _softplus(rows_of(dt_ref, sl_all) + dt_bias)
        if l_valid < n_blk * t_blk:
            rows_all = lax.broadcasted_iota(jnp.int32, (rg, GROUP_WIDTH), 0)
            dt_all = jnp.where(t * t_blk + base + rows_all < l_valid, dt_all, 0.0)
        z_all = rows_of(z_ref, sl_all)
        xs_all = xbc_all[:, 0:GROUP_WIDTH]
        a_all = dt_all * a_neg
        u_all = xs_all * dt_all

        cut = lambda arr, s: arr[s * tc:(s + 1) * tc, :]
        acss = [_dot01_lhs(tril_b, cut(a_all, s)) for s in range(group)]
        cms = [cut(xbc_all[:, 2 * GROUP_WIDTH:3 * GROUP_WIDTH], s).astype(BF16) for s in range(group)]
        bms = [cut(xbc_all[:, GROUP_WIDTH:2 * GROUP_WIDTH], s).astype(BF16) for s in range(group)]
        cbs = [[_dot_nt(cms[s][:, gs], bms[s][:, gs]) for gs in groups] for s in range(group)]

        y_states = []
        for s in range(group):
            acs = acss[s]
            last = acs[tc - 1:tc, :]
            upb = (cut(u_all, s) * jnp.exp(last - acs)).astype(BF16)
            eacs = jnp.exp(acs)
            elast = jnp.exp(last)
            ys_s = []
            for gs in groups:
                ys_s.append(_dot(cms[s][:, gs], st_ref[:, gs].astype(BF16)) * eacs[:, gs])
                st_ref[:, gs] = st_ref[:, gs] * elast[:, gs] + _dot_tn(bms[s][:, gs], upb[:, gs])
            y_states.append(ys_s)

        for s in range(group):
            acs = acss[s]
            acs_t = acs.T
            ub = cut(u_all, s).astype(BF16)
            ys = []
            for g, gs in enumerate(groups):
                parts = []
                for hh in range(2):
                    c0 = (2 * g + hh) * HEAD_DIM
                    col = jnp.broadcast_to(acs[:, c0:c0 + 1], (tc, tc))
                    rowv = jnp.broadcast_to(acs_t[c0:c0 + 1, :], (tc, tc))
                    dec = jnp.exp(jnp.where(tril, col - rowv, NEG))
                    parts.append(_dot((cbs[s][g] * dec).astype(BF16), ub[:, gs]))
                ys.append(jnp.where(lane < HEAD_DIM, parts[0], parts[1]) + y_states[s][g])
            yv = jnp.concatenate(ys, axis=1) + d_skip * cut(xs_all, s)
            res = _rms(yv * _silu(cut(z_all, s)), nrm).astype(MIX_DTYPE)
            if short:
                o_ref[0] = res[0:l_valid, :]
            else:
                o_ref[0, pl.ds(pl.multiple_of(base + s * tc, tc), tc), :] = res
        return carry

    lax.fori_loop(0, n_chunks // group, chunks, 0)

    @pl.when(t == n_blk - 1)
    def _():
        ssm_out_ref[0] = st_ref[...].T.reshape(HEADS, HEAD_DIM, B_STATE)


def _ssd(proj3, conv0, ssm0, cw, cb, dtb, alog, dvec, nrm, l_valid, t_blk):
    b, lrows, _ = proj3.shape
    n_blk = -(-l_valid // t_blk)
    assert t_blk % SSD_CHUNK == 0 and lrows == l_valid
    assert l_valid == n_blk * t_blk or (t_blk == SSD_CHUNK and B_CONV - 1 <= l_valid < t_blk and l_valid % 8 == 0)
    t_in = min(t_blk, l_valid)
    sb = _seq_batch(b, n_blk)
    vec = lambda w: pl.BlockSpec((1, w), lambda bi, t: (0, 0))
    conv_spec = pl.BlockSpec((sb, B_CONV - 1, B_CONV_DIM), lambda bi, t: (bi, 0, 0))
    ssm_spec = pl.BlockSpec((sb, HEADS, HEAD_DIM, B_STATE), lambda bi, t: (bi, 0, 0, 0))
    return pl.pallas_call(
        _per_sequence(functools.partial(_ssd_kernel, t_blk=t_blk, l_valid=l_valid, n_blk=n_blk), sb,
                      (0,) * 5 + (None,) * 6 + (0,) * 3 + (None,) * 2),
        out_shape=(jax.ShapeDtypeStruct((b, l_valid, GROUP_WIDTH), MIX_DTYPE),
                   jax.ShapeDtypeStruct((b, B_CONV - 1, B_CONV_DIM), F32),
                   jax.ShapeDtypeStruct((b, HEADS, HEAD_DIM, B_STATE), F32)),
        grid=(b // sb, n_blk),
        in_specs=[pl.BlockSpec((sb, t_in, B_CONV_DIM), lambda bi, t: (bi, t, COL_XBC // B_CONV_DIM)),
                  pl.BlockSpec((sb, t_in, GROUP_WIDTH), lambda bi, t: (bi, t, COL_BZ // GROUP_WIDTH)),
                  pl.BlockSpec((sb, t_in, GROUP_WIDTH), lambda bi, t: (bi, t, COL_DT // GROUP_WIDTH)),
                  conv_spec, ssm_spec,
                  pl.BlockSpec((B_CONV, B_CONV_DIM), lambda bi, t: (0, 0)),
                  vec(B_CONV_DIM), vec(GROUP_WIDTH), vec(GROUP_WIDTH), vec(GROUP_WIDTH), vec(GROUP_WIDTH)],
        out_specs=(pl.BlockSpec((sb, t_in, GROUP_WIDTH), lambda bi, t: (bi, t, 0)), conv_spec, ssm_spec),
        scratch_shapes=[pltpu.VMEM((CONV_PAD + SSD_INFLIGHT * SSD_CHUNK, B_CONV_DIM), F32),
                        pltpu.VMEM((B_STATE, GROUP_WIDTH), F32)],
        compiler_params=_cparams(("parallel", "arbitrary")),
        name="ssd",
    )(proj3, proj3, proj3, conv0, ssm0, cw, cb, dtb, alog, dvec, nrm)


HGRN_INFLIGHT = 8
HGRN_SAFE_LOG2 = 64.0


def _hgrn_kernel(q_ref, f_ref, i_ref, g_ref, st0_ref, lb_ref, nrm_ref, o_ref, st_out_ref, st_ref, b_blk, c_blk,
                 y_scr, *, t_blk, l_valid, n_blk, sb):
    t = pl.program_id(1)

    @pl.when(t == 0)
    def _():
        for q in range(sb):
            for h in range(HEADS):
                parts = [jnp.zeros((HEAD_DIM, HEAD_DIM * h), F32)] if h > 0 else []
                parts.append(st0_ref[q, h].T)
                if h < HEADS - 1:
                    parts.append(jnp.zeros((HEAD_DIM, HEAD_DIM * (HEADS - 1 - h)), F32))
                st_ref[q, h * HEAD_DIM:(h + 1) * HEAD_DIM, :] = jnp.concatenate(parts, axis=1)

    lb = lb_ref[...]
    nrm = nrm_ref[...]
    head_eq = _head_ones()
    ones_bd = head_eq.astype(BF16)
    tril = _tril(CHUNK).astype(BF16)
    rows = lax.broadcasted_iota(jnp.int32, (CHUNK, GROUP_WIDTH), 0)
    row_in_sub = rows % SUB
    lane_head = lax.broadcasted_iota(jnp.int32, (SUB, GROUP_WIDTH), 1) // HEAD_DIM
    n_sub = CHUNK // SUB
    n_chunks = t_blk // CHUNK
    sc_row = lax.broadcasted_iota(jnp.int32, (HEADS * SUB, CHUNK), 0) % SUB
    sc_col = lax.broadcasted_iota(jnp.int32, (HEADS * SUB, CHUNK), 1)

    def gates(q, cc, gmax):
        r0 = pl.multiple_of(cc * CHUNK, CHUNK)
        sl = pl.ds(r0, CHUNK)
        f = lb + (1.0 - lb) * jax.nn.sigmoid(f_ref[q, sl, :])
        kk = jnp.maximum(1.0 - f, 0.0)
        lf2 = jnp.log(f) * LOG2E
        if l_valid < n_blk * t_blk:
            ok = t * t_blk + r0 + rows < l_valid
            kk = jnp.where(ok, kk, 0.0)
            lf2 = jnp.where(ok, lf2, 0.0)
        bcs2 = _dot01_lhs(tril, lf2)
        b_blk[q, sl, :] = bcs2
        c_blk[q, sl, :] = jnp.log(kk) * LOG2E - bcs2
        prev = jnp.zeros((1, GROUP_WIDTH), F32)
        for a in range(n_sub):
            end = bcs2[(a + 1) * SUB - 1:(a + 1) * SUB, :]
            gmax = jnp.maximum(gmax, prev - end)
            prev = end
        return gmax

    gmax = jnp.zeros((1, GROUP_WIDTH), F32)
    for q in range(sb):
        gmax = lax.fori_loop(0, n_chunks, functools.partial(gates, q), gmax,
                             unroll=HGRN_INFLIGHT if n_chunks % HGRN_INFLIGHT == 0 else 1)
    factored_ok = jnp.max(gmax) <= HGRN_SAFE_LOG2

    def stack_heads(x):
        return jnp.concatenate([jnp.where(lane_head == h, x, 0.0) for h in range(HEADS)], axis=0).astype(BF16)

    def unstack_heads(x_all):
        out = jnp.zeros((SUB, GROUP_WIDTH), F32)
        for h in range(HEADS):
            out = out + jnp.where(lane_head == h, x_all[h * SUB:(h + 1) * SUB, :], 0.0)
        return out

    def chunks(items, factored):
        n = len(items)
        seqs = [q for q, _ in items]
        r0s = [pl.multiple_of(cc * CHUNK, CHUNK) for _, cc in items]
        sls = [pl.ds(r0, CHUNK) for r0 in r0s]
        qs = [q_ref[seqs[s], sls[s], :] for s in range(n)]
        vs = [i_ref[seqs[s], sls[s], :] for s in range(n)]
        bs = [b_blk[seqs[s], sls[s], :] for s in range(n)]
        cs = [c_blk[seqs[s], sls[s], :] for s in range(n)]
        vbs = [v.astype(BF16) for v in vs]

        os_ = []
        for s in range(n):
            st = st_ref.at[seqs[s]]
            os_.append(_dot_nt((qs[s] * jnp.exp2(bs[s])).astype(BF16), st[...].astype(BF16)))
            last = bs[s][CHUNK - 1:CHUNK, :]
            kt = jnp.exp2(last + cs[s]).astype(BF16)
            st[...] = st[...] * jnp.exp2(last) + jnp.where(head_eq, _dot_tn(vbs[s], kt), 0.0)

        pieces = [[] for _ in range(n)]
        for a in range(n_sub):
            s0 = a * SUB
            s_end = s0 + SUB if factored else s0
            if s_end == 0:
                for s in range(n):
                    pieces[s].append(jnp.zeros((SUB, GROUP_WIDTH), F32))
                continue
            scs = []
            for s in range(n):
                ref_row = bs[s][s0 - 1:s0, :] if a > 0 else jnp.zeros((1, GROUP_WIDTH), F32)
                qa = qs[s][s0:s0 + SUB, :] * jnp.exp2(bs[s][s0:s0 + SUB, :] - ref_row)
                ka = jnp.where(rows < s_end, jnp.exp2(ref_row + cs[s]), 0.0)
                sc = _dot_nt(stack_heads(qa), ka.astype(BF16))
                if factored:
                    sc = jnp.where(sc_col <= s0 + sc_row, sc, 0.0)
                scs.append(sc.astype(BF16))
            for s in range(n):
                pieces[s].append(unstack_heads(_dot(scs[s], vbs[s])))
        for s in range(n):
            os_[s] = os_[s] + jnp.concatenate(pieces[s], axis=0)

        if not factored:
            for s in range(n):
                def sub_rows(ref, jj):
                    return jnp.concatenate(
                        [jnp.broadcast_to(ref[pl.ds(r0s[s] + a * SUB + jj, 1), :], (SUB, GROUP_WIDTH))
                         for a in range(n_sub)], axis=0)

                for jj in range(SUB):
                    w = jnp.exp2(bs[s] + sub_rows(c_blk.at[seqs[s]], jj))
                    y_scr[jj * CHUNK:(jj + 1) * CHUNK, :] = jnp.where(row_in_sub >= jj, qs[s] * w, 0.0).astype(BF16)
                att = _dot(y_scr[...], ones_bd)
                for jj in range(SUB):
                    os_[s] = os_[s] + att[jj * CHUNK:(jj + 1) * CHUNK, :] * sub_rows(i_ref.at[seqs[s]], jj)

        mss = [_dot((o * o).astype(BF16), ones_bd) * (1.0 / HEAD_DIM) for o in os_]
        for s in range(n):
            res = os_[s] * lax.rsqrt(mss[s] + EPS) * nrm * _silu(g_ref[seqs[s], sls[s], :])
            o_ref[seqs[s], sls[s], :] = res.astype(MIX_DTYPE)

    @pl.when(factored_ok)
    def _():
        if sb > 1:
            chunks([(q, 0) for q in range(sb)], True)
        else:
            inflight = HGRN_INFLIGHT if n_chunks % HGRN_INFLIGHT == 0 else 1

            def group(pp, carry):
                chunks([(0, inflight * pp + s) for s in range(inflight)], True)
                return carry

            lax.fori_loop(0, n_chunks // inflight, group, 0)

    @pl.when(jnp.logical_not(factored_ok))
    def _():
        for q in range(sb):
            def one(cc, carry):
                chunks([(q, cc)], False)
                return carry

            lax.fori_loop(0, n_chunks, one, 0)

    @pl.when(t == n_blk - 1)
    def _():
        for q in range(sb):
            for h in range(HEADS):
                hs = slice(h * HEAD_DIM, (h + 1) * HEAD_DIM)
                st_out_ref[q, h] = st_ref[q, hs, hs].T


def _hgrn(proj3, col0, st0, lb, nrm, l_valid, t_blk):
    b, lrows, _ = proj3.shape
    n_blk = -(-l_valid // t_blk)
    assert n_blk * t_blk <= lrows and t_blk % CHUNK == 0
    sb = _seq_batch(b, n_blk)
    col = lambda c: pl.BlockSpec((sb, t_blk, GROUP_WIDTH),
                                 lambda bi, t: (bi, t, (c - COL_CQ + col0) // GROUP_WIDTH))
    vec = pl.BlockSpec((1, GROUP_WIDTH), lambda bi, t: (0, 0))
    st_spec = pl.BlockSpec((sb, HEADS, HEAD_DIM, HEAD_DIM), lambda bi, t: (bi, 0, 0, 0))
    return pl.pallas_call(
        functools.partial(_hgrn_kernel, t_blk=t_blk, l_valid=l_valid, n_blk=n_blk, sb=sb),
        out_shape=(jax.ShapeDtypeStruct((b, n_blk * t_blk, GROUP_WIDTH), MIX_DTYPE),
                   jax.ShapeDtypeStruct((b, HEADS, HEAD_DIM, HEAD_DIM), F32)),
        grid=(b // sb, n_blk),
        in_specs=[col(COL_CQ), col(COL_CF), col(COL_CI), col(COL_CG), st_spec, vec, vec],
        out_specs=(pl.BlockSpec((sb, t_blk, GROUP_WIDTH), lambda bi, t: (bi, t, 0)), st_spec),
        scratch_shapes=[pltpu.VMEM((sb, GROUP_WIDTH, GROUP_WIDTH), F32),
                        pltpu.VMEM((sb, t_blk, GROUP_WIDTH), F32),
                        pltpu.VMEM((sb, t_blk, GROUP_WIDTH), F32),
                        pltpu.VMEM((SUB * CHUNK, GROUP_WIDTH), BF16)],
        compiler_params=_cparams(("parallel", "arbitrary")),
        name="hgrn",
    )(proj3, proj3, proj3, proj3, st0, lb, nrm)


POOL_PAD = 24
POOL_FRONT = 8
D_GROUP = 64


def _pool_kernel(x_ref, buf_ref, w_ref, sc_ref, o_ref, buf_out_ref, xp_ref, s2_ref, s4_ref, *, t_blk):
    t = pl.program_id(1)
    hi = POOL_PAD + t_blk
    half = GROUP_WIDTH // 2

    @pl.when(t == 0)
    def _():
        zeros = jnp.zeros((POOL_PAD - D_BUF, GROUP_WIDTH), F32)
        xp_ref[0:POOL_PAD - D_BUF, :] = zeros
        s2_ref[0:POOL_FRONT, :] = zeros[0:POOL_FRONT]
        s4_ref[0:POOL_FRONT, :] = zeros[0:POOL_FRONT]
        xp_ref[POOL_PAD - D_BUF:POOL_PAD, :] = buf_ref[0]

    x = x_ref[0]
    xp_ref[POOL_PAD:hi, :] = x
    s2 = xp_ref[POOL_FRONT:hi, :] + xp_ref[POOL_FRONT - 1:hi - 1, :]
    s2_ref[POOL_FRONT:hi, :] = s2
    s4 = s2 + s2_ref[POOL_FRONT - 2:hi - 2, :]
    s4_ref[POOL_FRONT:hi, :] = s4
    s8 = s4[:, half:] + s4_ref[POOL_FRONT - 4:hi - 4, half:]
    n_ext = hi - POOL_FRONT
    s16 = s8[8:n_ext, :] + s8[0:n_ext - 8, :]
    lo_sum = s2[n_ext - t_blk:, :half]
    lane = lax.broadcasted_iota(jnp.int32, (t_blk, half), 1)
    pooled_lo = jnp.where(lane < D_GROUP, lo_sum * 0.5, s4[n_ext - t_blk:, :half] * 0.25)
    pooled_hi = jnp.where(lane < D_GROUP, s8[n_ext - t_blk:, :] * 0.125, s16[n_ext - 8 - t_blk:, :] * 0.0625)
    pooled = jnp.concatenate([pooled_lo, pooled_hi], axis=1) - x
    o_ref[0] = (_dot(pooled.astype(BF16), w_ref[...]) * sc_ref[...]).astype(MIX_DTYPE)
    tail = xp_ref[hi - D_BUF:hi, :]
    buf_out_ref[0] = tail
    xp_ref[POOL_PAD - D_BUF:POOL_PAD, :] = tail


def _pool(proj3, buf, w_bd, scale, l_valid, t_blk):
    b = proj3.shape[0]
    n_blk = l_valid // t_blk
    assert n_blk * t_blk == l_valid and t_blk >= D_BUF
    sb = _seq_batch(b, n_blk)
    buf_spec = pl.BlockSpec((sb, D_BUF, GROUP_WIDTH), lambda bi, t: (bi, 0, 0))
    return pl.pallas_call(
        _per_sequence(functools.partial(_pool_kernel, t_blk=t_blk), sb, (0, 0, None, None, 0, 0, None, None, None)),
        out_shape=(jax.ShapeDtypeStruct((b, l_valid, GROUP_WIDTH), MIX_DTYPE),
                   jax.ShapeDtypeStruct((b, D_BUF, GROUP_WIDTH), F32)),
        grid=(b // sb, n_blk),
        in_specs=[pl.BlockSpec((sb, t_blk, GROUP_WIDTH), lambda bi, t: (bi, t, COL_DX // GROUP_WIDTH)),
                  buf_spec,
                  pl.BlockSpec((GROUP_WIDTH, GROUP_WIDTH), lambda bi, t: (0, 0)),
                  pl.BlockSpec((1, GROUP_WIDTH), lambda bi, t: (0, 0))],
        out_specs=(pl.BlockSpec((sb, t_blk, GROUP_WIDTH), lambda bi, t: (bi, t, 0)), buf_spec),
        scratch_shapes=[pltpu.VMEM((POOL_PAD + t_blk, GROUP_WIDTH), F32)] * 3,
        compiler_params=_cparams(("parallel", "arbitrary")),
        name="pool",
    )(proj3, buf, w_bd, scale)


SSD_BLOCK = 1024
HGRN_BLOCK = 1024
POOL_BLOCK = 2048


def _prep_layer(layer, norm1, w_in, a_rel_bias, b_conv_w, b_conv_b, b_dt_bias, b_a_log, b_d, b_norm, lbs, c_norm,
                d_pool_w, d_pool_scale, w_out, norm2, w_gate_up, w_down):
    w_r = _w_in_prep(jnp.transpose(w_in, (0, 2, 1)), layer)
    tab = a_rel_bias[layer].T
    m = A_BAND + CHUNK
    far = tab[:, 2 * REL_CLIP:]
    n_far = A_BAND_PREV - REL_CLIP + 1
    r = jnp.concatenate([jnp.broadcast_to(far, (HEADS, n_far)),
                         tab[:, 2 * REL_CLIP - 1:CHUNK:-1],
                         jnp.broadcast_to(far, (HEADS, m - n_far - (2 * REL_CLIP - 1 - CHUNK)))], axis=1)
    bias = jnp.tile(r, (1, CHUNK))[:, :CHUNK * (m - 1)].reshape(HEADS, CHUNK, m - 1)[:, :, :A_BAND]
    bias = bias.reshape(HEADS * CHUNK, A_BAND)
    rep = lambda p: jnp.repeat(p[layer], HEAD_DIM)[None, :]
    pw = d_pool_w[layer]
    w_bd = jnp.zeros((GROUP_WIDTH, GROUP_WIDTH), F32)
    for g in range(4):
        w_bd = w_bd.at[g * 64:(g + 1) * 64, g * 64:(g + 1) * 64].set(pw[g])
    return dict(
        n1=norm1[layer][None, :], w_in=w_r, bias=bias,
        cw=b_conv_w[layer], cb=b_conv_b[layer][None, :], dtb=rep(b_dt_bias), alog=rep(b_a_log), dvec=rep(b_d),
        bnrm=b_norm[layer][None, :], lb=lbs[layer][None, :], cnrm=c_norm[layer].reshape(1, GROUP_WIDTH),
        w_bd=w_bd.astype(BF16), psc=d_pool_scale[layer][None, :], n2=norm2[layer][None, :], layer=layer,
        w_out=w_out.astype(BF16), w_gu=w_gate_up.astype(BF16), w_dn=w_down.astype(BF16))


def _round_up(n, m):
    return -(-n // m) * m


def _mixers(proj, st, p, cache_t):
    k_prev, v_prev, conv0, ssm0, hgrn0, pool0 = st
    b, l, _ = proj.shape
    p_hist = k_prev.shape[1]
    lq = _round_up(l, CHUNK)
    pad_rows = lambda a: a if lq == l else jnp.pad(a, ((0, 0), (0, lq - l), (0, 0)))

    keep = min(A_BAND_PREV, p_hist + l)
    n_new = min(l, keep)
    ka = proj[:, l - n_new:, COL_AK:COL_AK + GROUP_WIDTH].reshape(b, n_new, HEADS, HEAD_DIM)
    va = proj[:, l - n_new:, COL_AV:COL_AV + GROUP_WIDTH].reshape(b, n_new, HEADS, HEAD_DIM)
    if cache_t is not None:
        bias_s = p["bias"].reshape(HEADS, CHUNK, A_BAND)[:, :l].reshape(HEADS * l, A_BAND)
        oa, kt_new, vt_new = _attention_cached(proj, cache_t[0], cache_t[1], p["layer"], bias_s, l)
        untranspose = lambda a: a.reshape(b, HEADS, HEAD_DIM, A_BAND_PREV).transpose(0, 3, 1, 2)
        new_k, new_v = untranspose(kt_new), untranspose(vt_new)
    else:
        if p_hist == 0:
            k_cache = v_cache = None
        else:
            k_cache = k_prev.astype(BF16).reshape(b, p_hist, GROUP_WIDTH)
            v_cache = v_prev.astype(BF16).reshape(b, p_hist, GROUP_WIDTH)
        oa = _attention(pad_rows(proj), k_cache, v_cache, p["bias"], l)[:, :l]
        new_k = jnp.concatenate([k_prev[:, p_hist - (keep - n_new):], ka], axis=1)
        new_v = jnp.concatenate([v_prev[:, p_hist - (keep - n_new):], va], axis=1)

    ob, new_conv, new_ssm = _ssd(proj, conv0, ssm0, p["cw"], p["cb"], p["dtb"], p["alog"], p["dvec"], p["bnrm"],
                                 l, SSD_BLOCK if l % SSD_BLOCK == 0 else SSD_CHUNK)

    if l % HGRN_BLOCK == 0:
        oc, new_hgrn = _hgrn(proj, COL_CQ, hgrn0, p["lb"], p["cnrm"], l, HGRN_BLOCK)
    else:
        oc, new_hgrn = _hgrn(pad_rows(proj[:, :, COL_CQ:COL_CQ + 4 * GROUP_WIDTH]), 0, hgrn0, p["lb"], p["cnrm"], l, lq)
        oc = oc[:, :l]

    od, new_pool = _pool(proj, pool0, p["w_bd"], p["psc"], l, POOL_BLOCK if l % POOL_BLOCK == 0 else l)

    flat = lambda a: a.reshape(b * l, GROUP_WIDTH)
    return (flat(oa), flat(ob), flat(oc), flat(od)), (new_k, new_v, new_conv, new_ssm, new_hgrn, new_pool)


def _trunk_groups(groups, params, nf):
    assert len(groups) <= 2
    depth = len(params)
    xs = [x.reshape(-1, D_MODEL) for x, _ in groups]
    shapes = [x.shape for x, _ in groups]
    new = [[[] for _ in range(6)] for _ in groups]
    cache_ts = []
    for x, states in groups:
        b, l, _ = x.shape
        cache_t = None
        if states[0].shape[2] == A_BAND_PREV and l <= CHUNK and l % 8 == 0:
            as_t = lambda c: jnp.transpose(c, (0, 1, 3, 4, 2)).reshape(depth, b, GROUP_WIDTH, A_BAND_PREV)
            cache_t = (as_t(states[0]), as_t(states[1]))
        cache_ts.append(cache_t)
    for layer in range(depth):
        p = params[layer]
        projs = _inproj(xs[0], p["n1"], p["w_in"], *xs[1:])
        projs = projs if len(groups) > 1 else (projs,)
        mixes = []
        for g, (_, states) in enumerate(groups):
            b, l, _ = shapes[g]
            st_l = tuple(s[layer] for s in states)
            mix, ns = _mixers(projs[g].reshape(b, l, PROJ_W), st_l, p, cache_ts[g])
            mixes.append(mix)
            for i in range(6):
                new[g][i].append(ns[i])
        extra = (xs[1], mixes[1]) if len(groups) > 1 else None
        ys = _mlp(xs[0], mixes[0], p["layer"], p["w_out"], p["n2"], p["w_gu"], p["w_dn"], nf, layer == depth - 1, extra)
        xs = list(ys) if len(groups) > 1 else [ys]
    return [(xs[g].reshape(shapes[g]), tuple(jnp.stack(n, axis=0) for n in new[g])) for g in range(len(groups))]


def _trunk(x, states, params, nf):
    return _trunk_groups([(x, states)], params, nf)[0]


def kernel(x_prompt, x_sample, cache_a_k, cache_a_v, state_b_conv, state_b_ssm, state_c_hgrn, state_d_pool, norm1,
           w_in, a_rel_bias, b_conv_w, b_conv_b, b_dt_bias, b_a_log, b_d, b_norm, c_lb_logits, c_norm, d_pool_w,
           d_pool_scale, w_out, norm2, w_gate_up, w_down, norm_f):
    depth = w_in.shape[0]
    lbs = jnp.cumsum(jax.nn.softmax(c_lb_logits.astype(F32), axis=0), axis=0)
    lbs = lbs - lbs[:1]
    params = [_prep_layer(layer, norm1, w_in, a_rel_bias, b_conv_w, b_conv_b, b_dt_bias, b_a_log, b_d, b_norm, lbs,
                          c_norm, d_pool_w, d_pool_scale, w_out, norm2, w_gate_up, w_down)
              for layer in range(depth)]
    nf = norm_f[None, :]
    bp = x_prompt.shape[0]
    prompt_states = (
        jnp.zeros((depth, bp, 0, HEADS, HEAD_DIM), F32),
        jnp.zeros((depth, bp, 0, HEADS, HEAD_DIM), F32),
        jnp.zeros((depth, bp, B_CONV - 1, B_CONV_DIM), F32),
        jnp.zeros((depth, bp, HEADS, HEAD_DIM, B_STATE), F32),
        jnp.zeros((depth, bp, HEADS, HEAD_DIM, HEAD_DIM), F32),
        jnp.zeros((depth, bp, D_BUF, GROUP_WIDTH), F32),
    )
    sample_states = (cache_a_k, cache_a_v, state_b_conv, state_b_ssm, state_c_hgrn, state_d_pool)
    (y_prompt, ps), (y_sample, ss) = _trunk_groups([(x_prompt, prompt_states), (x_sample, sample_states)], params, nf)
    return (y_prompt, y_sample) + ps + ss
```

```python
import functools

import jax
import jax.numpy as jnp
from jax import lax
from jax.experimental import pallas as pl
from jax.experimental.pallas import tpu as pltpu

F32 = jnp.float32
BF16 = jnp.bfloat16
MIX_DTYPE = BF16

D_MODEL = 1024
GROUP_WIDTH = 256
HEADS = 4
HEAD_DIM = 64
CHUNK = 64
A_BAND_PREV = 512
A_BAND = A_BAND_PREV + CHUNK
REL_CLIP = 128
B_STATE = 128
B_CONV = 4
B_CONV_DIM = 768
D_BUF = 15
D_FF = 2816
EPS = 1e-6
NEG = -1e30
LOG2E = 1.4426950408889634
SUB = 16

COL_XBC = 0
COL_AQ = 768
COL_AK = 1024
COL_AV = 1280
COL_BZ = 1536
COL_CQ = 1792
COL_CF = 2048
COL_CI = 2304
COL_CG = 2560
COL_DX = 2816
COL_DT = 3072
PROJ_W = 3328

VMEM_LIMIT = 56 * 1024 * 1024


def _cparams(sem):
    return pltpu.CompilerParams(dimension_semantics=sem, vmem_limit_bytes=VMEM_LIMIT)


def _rms(x, w):
    return x * lax.rsqrt(jnp.mean(x * x, axis=-1, keepdims=True) + EPS) * w


def _silu(x):
    h = 0.5 * x
    return h + h * jnp.tanh(h)


def _dot(a, b):
    return jnp.dot(a, b, preferred_element_type=F32)


def _dot_nt(a, b):
    return lax.dot_general(a, b, (((1,), (1,)), ((), ())), preferred_element_type=F32)


def _dot_tn(a, b):
    return lax.dot_general(a, b, (((0,), (0,)), ((), ())), preferred_element_type=F32)


def _split3(x):
    hi = x.astype(BF16)
    r = x - hi.astype(F32)
    mid = r.astype(BF16)
    lo = (r - mid.astype(F32)).astype(BF16)
    return hi, mid, lo


def _dot01_lhs(m01, x):
    hi, mid, lo = _split3(x)
    return _dot(m01, hi) + _dot(m01, mid) + _dot(m01, lo)


SEQ_BATCH = 8


def _seq_batch(b, n_blk):
    return SEQ_BATCH if n_blk == 1 and b % SEQ_BATCH == 0 else 1


def _per_sequence(kernel_fn, sb, seq_dim):
    if sb == 1:
        return kernel_fn

    def run(*refs):
        for s in range(sb):
            kernel_fn(*[r if d is None else r.at[(slice(None),) * d + (pl.ds(s, 1),)]
                        for r, d in zip(refs, seq_dim)])

    return run


def _tril(t):
    r = lax.broadcasted_iota(jnp.int32, (t, t), 0)
    c = lax.broadcasted_iota(jnp.int32, (t, t), 1)
    return r >= c


def _head_ones():
    r = lax.broadcasted_iota(jnp.int32, (GROUP_WIDTH, GROUP_WIDTH), 0) // HEAD_DIM
    c = lax.broadcasted_iota(jnp.int32, (GROUP_WIDTH, GROUP_WIDTH), 1) // HEAD_DIM
    return r == c


IN_WIDTH = 3076
ORIG_XBC = 1024
ORIG_DT = 1792
W_PREP_COLS = 256


def _w_in_prep_kernel(w_ref, o_ref):
    def put(c0, cols):
        o_ref[:, c0:c0 + W_PREP_COLS] = cols.astype(BF16)

    for j in range(3):
        put(COL_XBC + W_PREP_COLS * j, w_ref[0, :, ORIG_XBC + W_PREP_COLS * j:ORIG_XBC + W_PREP_COLS * (j + 1)])
    for j in range(4):
        put(COL_AQ + W_PREP_COLS * j, w_ref[0, :, W_PREP_COLS * j:W_PREP_COLS * (j + 1)])
    for j in range(5):
        c0 = ORIG_DT + HEADS + W_PREP_COLS * j
        put(COL_CQ + W_PREP_COLS * j, w_ref[0, :, c0:c0 + W_PREP_COLS])
    put(COL_DT, jnp.concatenate([jnp.broadcast_to(w_ref[0, :, ORIG_DT + h:ORIG_DT + h + 1], (D_MODEL, HEAD_DIM))
                                 for h in range(HEADS)], axis=1))


def _w_in_prep(w_in, layer):
    return pl.pallas_call(
        _w_in_prep_kernel,
        out_shape=jax.ShapeDtypeStruct((D_MODEL, PROJ_W), BF16),
        grid=(1,),
        in_specs=[pl.BlockSpec((1, D_MODEL, IN_WIDTH), lambda i: (layer, 0, 0), pipeline_mode=pl.Buffered(1))],
        out_specs=pl.BlockSpec((D_MODEL, PROJ_W), lambda i: (0, 0)),
        compiler_params=_cparams(("arbitrary",)),
        name="w_in_prep",
    )(w_in)


def _is_last_step():
    return pl.program_id(0) == pl.num_programs(0) - 1


def _inproj_tile(x_ref, n_ref, w_ref, o_ref):
    hb = _rms(x_ref[...], n_ref[...]).astype(BF16)
    for c0 in range(0, PROJ_W, W_PREP_COLS):
        o_ref[:, c0:c0 + W_PREP_COLS] = _dot(hb, w_ref[:, c0:c0 + W_PREP_COLS])


def _inproj_kernel(x_ref, n_ref, w_ref, *rest):
    if len(rest) == 1:
        _inproj_tile(x_ref, n_ref, w_ref, rest[0])
    else:
        xe_ref, o_ref, oe_ref = rest
        _inproj_tile(x_ref, n_ref, w_ref, o_ref)
        pl.when(_is_last_step())(lambda: _inproj_tile(xe_ref, n_ref, w_ref, oe_ref))


def _inproj(x2d, n1, w_in, extra=None):
    rows = x2d.shape[0]
    tm = next((m for m in (1024, 512) if rows % m == 0), rows)
    whole = lambda a: pl.BlockSpec(a.shape, lambda i: (0, 0))
    in_specs = [pl.BlockSpec((tm, D_MODEL), lambda i: (i, 0)),
                pl.BlockSpec((1, D_MODEL), lambda i: (0, 0)),
                pl.BlockSpec((D_MODEL, PROJ_W), lambda i: (0, 0), pipeline_mode=pl.Buffered(1))]
    out_shape = jax.ShapeDtypeStruct((rows, PROJ_W), F32)
    out_specs = pl.BlockSpec((tm, PROJ_W), lambda i: (i, 0))
    operands = (x2d, n1, w_in)
    if extra is not None:
        out_e = jax.ShapeDtypeStruct((extra.shape[0], PROJ_W), F32)
        in_specs, operands = in_specs + [whole(extra)], operands + (extra,)
        out_shape, out_specs = (out_shape, out_e), (out_specs, whole(out_e))
    return pl.pallas_call(
        _inproj_kernel,
        out_shape=out_shape,
        grid=(rows // tm,),
        in_specs=in_specs,
        out_specs=out_specs,
        compiler_params=_cparams(("arbitrary",)),
        name="inproj",
    )(*operands)


CONV_PAD = 8


def _causal_conv_silu(xpad_ref, rows, conv_w, conv_b):
    hist = B_CONV - 1
    y = conv_b
    for i in range(B_CONV):
        y = y + xpad_ref[CONV_PAD - hist + i:CONV_PAD - hist + i + rows, :] * conv_w[i:i + 1, :]
    return _silu(y)


def _softplus(x):
    return jnp.maximum(x, 0.0) + jnp.log1p(jnp.exp(-jnp.abs(x)))


FF_CHUNK = 256


def _mlp_tile(x_ref, mix_refs, wo_ref, n2_ref, wgu_ref, wdn_ref, nf_ref, o_ref, mix_ref, act_ref, final):
    for m, r in enumerate(mix_refs):
        mix_ref[:, m * GROUP_WIDTH:(m + 1) * GROUP_WIDTH] = r[...]
    x1 = x_ref[...] + _dot(mix_ref[...], wo_ref[0])
    hb = _rms(x1, n2_ref[...]).astype(BF16)
    for c0 in range(0, D_FF, FF_CHUNK):
        gate = _dot(hb, wgu_ref[0, :, c0:c0 + FF_CHUNK])
        up = _dot(hb, wgu_ref[0, :, D_FF + c0:D_FF + c0 + FF_CHUNK])
        act_ref[:, c0:c0 + FF_CHUNK] = (_silu(gate) * up).astype(BF16)
    out = x1 + _dot(act_ref[...], wdn_ref[0])
    if final:
        out = _rms(out, nf_ref[...])
    o_ref[...] = out


def _mlp_kernel(*refs, final, has_extra):
    n_in = 10
    x_ref, oa_ref, ob_ref, oc_ref, od_ref, wo_ref, n2_ref, wgu_ref, wdn_ref, nf_ref = refs[:n_in]
    weights = (wo_ref, n2_ref, wgu_ref, wdn_ref, nf_ref)
    if not has_extra:
        o_ref, mix_ref, act_ref = refs[n_in:]
        _mlp_tile(x_ref, (oa_ref, ob_ref, oc_ref, od_ref), *weights, o_ref, mix_ref, act_ref, final)
    else:
        xe_ref, ea_ref, eb_ref, ec_ref, ed_ref, o_ref, oe_ref, mix_ref, act_ref, mixe_ref, acte_ref = refs[n_in:]
        _mlp_tile(x_ref, (oa_ref, ob_ref, oc_ref, od_ref), *weights, o_ref, mix_ref, act_ref, final)
        pl.when(_is_last_step())(
            lambda: _mlp_tile(xe_ref, (ea_ref, eb_ref, ec_ref, ed_ref), *weights, oe_ref, mixe_ref, acte_ref, final))


def _mlp(x2d, mix, layer, w_out, n2, w_gu, w_dn, nf, final, extra=None):
    rows = x2d.shape[0]
    tm = 512 if rows % 512 == 0 else rows
    row_spec = lambda w: pl.BlockSpec((tm, w), lambda i: (i, 0))
    whole = lambda a: pl.BlockSpec(a.shape, lambda i: (0, 0))
    wspec = lambda a: pl.BlockSpec((1,) + a.shape[1:], lambda i: (layer, 0, 0), pipeline_mode=pl.Buffered(1))
    in_specs = ([row_spec(D_MODEL)] + [row_spec(GROUP_WIDTH)] * 4
                + [wspec(w_out), whole(n2), wspec(w_gu), wspec(w_dn), whole(nf)])
    operands = (x2d, *mix, w_out, n2, w_gu, w_dn, nf)
    out_shape = jax.ShapeDtypeStruct((rows, D_MODEL), F32)
    out_specs = row_spec(D_MODEL)
    scratch = [pltpu.VMEM((tm, D_MODEL), BF16), pltpu.VMEM((tm, D_FF), BF16)]
    if extra is not None:
        x_e, mix_e = extra
        out_e = jax.ShapeDtypeStruct(x_e.shape, F32)
        in_specs = in_specs + [whole(x_e)] + [whole(m) for m in mix_e]
        operands = operands + (x_e, *mix_e)
        out_shape, out_specs = (out_shape, out_e), (out_specs, whole(out_e))
        scratch = scratch + [pltpu.VMEM((x_e.shape[0], D_MODEL), BF16), pltpu.VMEM((x_e.shape[0], D_FF), BF16)]
    return pl.pallas_call(
        functools.partial(_mlp_kernel, final=final, has_extra=extra is not None),
        out_shape=out_shape,
        grid=(rows // tm,),
        in_specs=in_specs,
        out_specs=out_specs,
        scratch_shapes=scratch,
        compiler_params=_cparams(("arbitrary",)),
        name="mlp",
    )(*operands)


ATTN_BLOCK = 1024
ATTN_UNROLL = 8


def _attn_kernel(q_ref, kh_ref, kc_ref, vh_ref, vc_ref, bias_ref, o_ref, kb_ref, vb_ref, *, qb, p_hist, l_valid):
    i = pl.program_id(1)
    if p_hist == 0:
        @pl.when(i == 0)
        def _():
            kb_ref[0:A_BAND_PREV, :] = jnp.zeros((A_BAND_PREV, GROUP_WIDTH), BF16)
            vb_ref[0:A_BAND_PREV, :] = jnp.zeros((A_BAND_PREV, GROUP_WIDTH), BF16)

        @pl.when(i > 0)
        def _():
            kb_ref[0:A_BAND_PREV, :] = kh_ref[0].astype(BF16)
            vb_ref[0:A_BAND_PREV, :] = vh_ref[0].astype(BF16)
    else:
        kb_ref[0:A_BAND_PREV, :] = kh_ref[0].astype(BF16)
        vb_ref[0:A_BAND_PREV, :] = vh_ref[0].astype(BF16)
    kb_ref[A_BAND_PREV:A_BAND_PREV + qb, :] = kc_ref[0].astype(BF16)
    vb_ref[A_BAND_PREV:A_BAND_PREV + qb, :] = vc_ref[0].astype(BF16)
    lane_head = lax.broadcasted_iota(jnp.int32, (CHUNK, GROUP_WIDTH), 1) // HEAD_DIM
    band_pos = lax.broadcasted_iota(jnp.int32, (1, A_BAND), 1)

    n_chunks = qb // CHUNK
    group = ATTN_UNROLL if n_chunks % ATTN_UNROLL == 0 else 1

    def chunks(masked, gi, carry):
        r0s = [pl.multiple_of((gi * group + u) * CHUNK, CHUNK) for u in range(group)]
        es, dens = [], []
        for r0 in r0s:
            qc = q_ref[0, pl.ds(r0, CHUNK), :] * (HEAD_DIM ** -0.5)
            qs = jnp.concatenate([jnp.where(lane_head == h, qc, 0.0) for h in range(HEADS)], axis=0).astype(BF16)
            s = _dot_nt(qs, kb_ref[pl.ds(r0, A_BAND), :]) + bias_ref[...]
            if masked:
                kpos = i * qb + r0 - A_BAND_PREV + band_pos
                s = jnp.where((kpos >= -p_hist) & (kpos < l_valid), s, NEG)
            e = jnp.exp(s - jnp.max(s, axis=-1, keepdims=True))
            dens.append(jnp.sum(e, axis=-1, keepdims=True))
            es.append(e.astype(BF16))
        o_alls = [_dot(es[u], vb_ref[pl.ds(r0s[u], A_BAND), :]) * (1.0 / dens[u]) for u in range(group)]
        for u in range(group):
            o = jnp.zeros((CHUNK, GROUP_WIDTH), F32)
            for h in range(HEADS):
                o = o + jnp.where(lane_head == h, o_alls[u][h * CHUNK:(h + 1) * CHUNK, :], 0.0)
            o_ref[0, pl.ds(r0s[u], CHUNK), :] = o.astype(MIX_DTYPE)
        return carry

    n_blocks = pl.num_programs(1)
    first_has_invalid = p_hist < A_BAND_PREV
    last_has_invalid = l_valid % qb != 0
    if not (first_has_invalid or last_has_invalid):
        lax.fori_loop(0, n_chunks // group, functools.partial(chunks, False), 0)
    else:
        edge = False
        if first_has_invalid:
            edge = edge | (i == 0)
        if last_has_invalid:
            edge = edge | (i == n_blocks - 1)
        @pl.when(edge)
        def _():
            lax.fori_loop(0, n_chunks // group, functools.partial(chunks, True), 0)

        @pl.when(jnp.logical_not(edge))
        def _():
            lax.fori_loop(0, n_chunks // group, functools.partial(chunks, False), 0)


def _attention(proj3, k_cache, v_cache, bias, l_valid):
    b = proj3.shape[0]
    lq = _round_up(l_valid, CHUNK)
    qb = next((q for q in (ATTN_BLOCK, A_BAND_PREV) if lq % q == 0), lq)
    nblk = lq // qb
    col = lambda c: pl.BlockSpec((1, qb, GROUP_WIDTH), lambda bi, i: (bi, i, c // GROUP_WIDTH))
    if k_cache is None:
        assert qb % A_BAND_PREV == 0
        p_hist = 0
        per = qb // A_BAND_PREV
        prev = lambda c: pl.BlockSpec((1, A_BAND_PREV, GROUP_WIDTH),
                                      lambda bi, i: (bi, jnp.maximum(i * per - 1, 0), c // GROUP_WIDTH))
        kh, vh, kh_spec, vh_spec = proj3, proj3, prev(COL_AK), prev(COL_AV)
    else:
        assert nblk == 1 and k_cache.shape[1] == A_BAND_PREV
        p_hist = A_BAND_PREV
        kh_spec = vh_spec = pl.BlockSpec((1, A_BAND_PREV, GROUP_WIDTH), lambda bi, i: (bi, 0, 0))
        kh, vh = k_cache, v_cache
    return pl.pallas_call(
        functools.partial(_attn_kernel, qb=qb, p_hist=p_hist, l_valid=l_valid),
        out_shape=jax.ShapeDtypeStruct((b, lq, GROUP_WIDTH), MIX_DTYPE),
        grid=(b, nblk),
        in_specs=[col(COL_AQ), kh_spec, col(COL_AK), vh_spec, col(COL_AV),
                  pl.BlockSpec((HEADS * CHUNK, A_BAND), lambda bi, i: (0, 0))],
        out_specs=pl.BlockSpec((1, qb, GROUP_WIDTH), lambda bi, i: (bi, i, 0)),
        scratch_shapes=[pltpu.VMEM((A_BAND_PREV + qb, GROUP_WIDTH), BF16),
                        pltpu.VMEM((A_BAND_PREV + qb, GROUP_WIDTH), BF16)],
        compiler_params=_cparams(("parallel", "arbitrary")),
        name="attention",
    )(proj3, kh, proj3, vh, proj3, bias)


def _attn_cached_kernel(q_ref, kn_ref, vn_ref, kt_ref, vt_ref, bias_ref, o_ref, kt_out_ref, vt_out_ref, *, l_new):
    lane_head = lax.broadcasted_iota(jnp.int32, (l_new, GROUP_WIDTH), 1) // HEAD_DIM
    q = q_ref[0] * (HEAD_DIM ** -0.5)
    qs = jnp.concatenate([jnp.where(lane_head == h, q, 0.0) for h in range(HEADS)], axis=0).astype(BF16)
    kn = kn_ref[0]
    vn = vn_ref[0]
    kt = kt_ref[0, 0]
    vt = vt_ref[0, 0]
    s_old = _dot(qs, kt.astype(BF16)) + bias_ref[:, 0:A_BAND_PREV]
    s_new = _dot_nt(qs, kn.astype(BF16)) + bias_ref[:, A_BAND_PREV:A_BAND_PREV + l_new]
    m = jnp.maximum(jnp.max(s_old, axis=-1, keepdims=True), jnp.max(s_new, axis=-1, keepdims=True))
    e_old = jnp.exp(s_old - m)
    e_new = jnp.exp(s_new - m)
    den = jnp.sum(e_old, axis=-1, keepdims=True) + jnp.sum(e_new, axis=-1, keepdims=True)
    o_all = (_dot_nt(e_old.astype(BF16), vt.astype(BF16)) + _dot(e_new.astype(BF16), vn.astype(BF16))) * (1.0 / den)
    o = jnp.zeros((l_new, GROUP_WIDTH), F32)
    for h in range(HEADS):
        o = o + jnp.where(lane_head == h, o_all[h * l_new:(h + 1) * l_new, :], 0.0)
    o_ref[0] = o.astype(MIX_DTYPE)

    keep = A_BAND_PREV - l_new
    lane = lax.broadcasted_iota(jnp.int32, (GROUP_WIDTH, A_BAND_PREV), 1)
    r = lax.broadcasted_iota(jnp.int32, (l_new, A_BAND_PREV), 0)
    c = lax.broadcasted_iota(jnp.int32, (l_new, A_BAND_PREV), 1)
    place = (c == r + keep).astype(BF16)

    def appended(old_t, new_rows):
        hi, mid, lo = _split3(new_rows)
        new_cols = _dot_tn(hi, place) + _dot_tn(mid, place) + _dot_tn(lo, place)
        return jnp.where(lane < keep, pltpu.roll(old_t, keep, 1), new_cols)

    kt_out_ref[0] = appended(kt, kn)
    vt_out_ref[0] = appended(vt, vn)


def _attention_cached(proj3, kt_cache, vt_cache, layer, bias, l_valid):
    b = proj3.shape[0]
    sb = _seq_batch(b, 1)
    col = lambda c: pl.BlockSpec((sb, l_valid, GROUP_WIDTH), lambda bi: (bi, 0, c // GROUP_WIDTH))
    cache_spec = pl.BlockSpec((1, sb, GROUP_WIDTH, A_BAND_PREV), lambda bi: (layer, bi, 0, 0))
    out_t = pl.BlockSpec((sb, GROUP_WIDTH, A_BAND_PREV), lambda bi: (bi, 0, 0))
    return pl.pallas_call(
        _per_sequence(functools.partial(_attn_cached_kernel, l_new=l_valid), sb, (0, 0, 0, 1, 1, None, 0, 0, 0)),
        out_shape=(jax.ShapeDtypeStruct((b, l_valid, GROUP_WIDTH), MIX_DTYPE),
                   jax.ShapeDtypeStruct((b, GROUP_WIDTH, A_BAND_PREV), F32),
                   jax.ShapeDtypeStruct((b, GROUP_WIDTH, A_BAND_PREV), F32)),
        grid=(b // sb,),
        in_specs=[col(COL_AQ), col(COL_AK), col(COL_AV), cache_spec, cache_spec,
                  pl.BlockSpec((HEADS * l_valid, A_BAND), lambda bi: (0, 0))],
        out_specs=(pl.BlockSpec((sb, l_valid, GROUP_WIDTH), lambda bi: (bi, 0, 0)), out_t, out_t),
        compiler_params=_cparams(("parallel",)),
        name="attention_cached",
    )(proj3, proj3, proj3, kt_cache, vt_cache, bias)


SSD_CHUNK = 128
SSD_INFLIGHT = 4


def _ssd_kernel(xbc_ref, z_ref, dt_ref, conv0_ref, ssm0_ref, cw_ref, cb_ref, dtb_ref, alog_ref, d_ref, nrm_ref,
                o_ref, conv_out_ref, ssm_out_ref, xpad_ref, st_ref, *, t_blk, l_valid, n_blk):
    t = pl.program_id(1)
    hist = B_CONV - 1

    @pl.when(t == 0)
    def _():
        xpad_ref[CONV_PAD - hist:CONV_PAD, :] = conv0_ref[0]
        st_ref[...] = ssm0_ref[0].reshape(GROUP_WIDTH, B_STATE).T

    n_last = l_valid - (n_blk - 1) * t_blk
    conv_out_ref[0] = xbc_ref[0, n_last - hist:n_last, :]

    tc = SSD_CHUNK
    short = l_valid < t_blk

    def rows_of(ref, sl):
        if short:
            return jnp.concatenate([ref[0], jnp.zeros((tc - l_valid, ref.shape[2]), F32)], axis=0)
        return ref[0, sl, :]

    tril = _tril(tc)
    tril_b = tril.astype(BF16)
    lane = lax.broadcasted_iota(jnp.int32, (tc, B_STATE), 1)
    rows = lax.broadcasted_iota(jnp.int32, (tc, GROUP_WIDTH), 0)
    a_neg = -jnp.exp(alog_ref[...])
    conv_b = cb_ref[...]
    conv_w = cw_ref[...]
    dt_bias = dtb_ref[...]
    d_skip = d_ref[...]
    nrm = nrm_ref[...]

    n_chunks = t_blk // tc
    group = SSD_INFLIGHT if n_chunks % SSD_INFLIGHT == 0 else 1
    groups = (slice(0, B_STATE), slice(B_STATE, 2 * B_STATE))

    def chunks(gi, carry):
        rg = group * tc
        base = pl.multiple_of(gi * rg, rg)
        sl_all = pl.ds(base, rg)
        xpad_ref[CONV_PAD:CONV_PAD + rg, :] = rows_of(xbc_ref, sl_all)
        xbc_all = _causal_conv_silu(xpad_ref, rg, conv_w, conv_b)
        xpad_ref[CONV_PAD - hist:CONV_PAD, :] = xpad_ref[CONV_PAD + rg - hist:CONV_PAD + rg, :]
        dt_all = _softplus(rows_of(dt_ref, sl_all) + dt_bias)
        if l_valid < n_blk * t_blk:
            rows_all = lax.broadcasted_iota(jnp.int32, (rg, GROUP_WIDTH), 0)
            dt_all = jnp.where(t * t_blk + base + rows_all < l_valid, dt_all, 0.0)
        z_all = rows_of(z_ref, sl_all)
        xs_all = xbc_all[:, 0:GROUP_WIDTH]
        a_all = dt_all * a_neg
        u_all = xs_all * dt_all

        cut = lambda arr, s: arr[s * tc:(s + 1) * tc, :]
        acss = [_dot01_lhs(tril_b, cut(a_all, s)) for s in range(group)]
        cms = [cut(xbc_all[:, 2 * GROUP_WIDTH:3 * GROUP_WIDTH], s).astype(BF16) for s in range(group)]
        bms = [cut(xbc_all[:, GROUP_WIDTH:2 * GROUP_WIDTH], s).astype(BF16) for s in range(group)]
        cbs = [[_dot_nt(cms[s][:, gs], bms[s][:, gs]) for gs in groups] for s in range(group)]

        y_states = []
        for s in range(group):
            acs = acss[s]
            last = acs[tc - 1:tc, :]
            upb = (cut(u_all, s) * jnp.exp(last - acs)).astype(BF16)
            eacs = jnp.exp(acs)
            elast = jnp.exp(last)
            ys_s = []
            for gs in groups:
                ys_s.append(_dot(cms[s][:, gs], st_ref[:, gs].astype(BF16)) * eacs[:, gs])
                st_ref[:, gs] = st_ref[:, gs] * elast[:, gs] + _dot_tn(bms[s][:, gs], upb[:, gs])
            y_states.append(ys_s)

        for s in range(group):
            acs = acss[s]
            acs_t = acs.T
            ub = cut(u_all, s).astype(BF16)
            ys = []
            for g, gs in enumerate(groups):
                parts = []
                for hh in range(2):
                    c0 = (2 * g + hh) * HEAD_DIM
                    col = jnp.broadcast_to(acs[:, c0:c0 + 1], (tc, tc))
                    rowv = jnp.broadcast_to(acs_t[c0:c0 + 1, :], (tc, tc))
                    dec = jnp.exp(jnp.where(tril, col - rowv, NEG))
                    parts.append(_dot((cbs[s][g] * dec).astype(BF16), ub[:, gs]))
                ys.append(jnp.where(lane < HEAD_DIM, parts[0], parts[1]) + y_states[s][g])
            yv = jnp.concatenate(ys, axis=1) + d_skip * cut(xs_all, s)
            res = _rms(yv * _silu(cut(z_all, s)), nrm).astype(MIX_DTYPE)
            if short:
                o_ref[0] = res[0:l_valid, :]
            else:
                o_ref[0, pl.ds(pl.multiple_of(base + s * tc, tc), tc), :] = res
        return carry

    lax.fori_loop(0, n_chunks // group, chunks, 0)

    @pl.when(t == n_blk - 1)
    def _():
        ssm_out_ref[0] = st_ref[...].T.reshape(HEADS, HEAD_DIM, B_STATE)


def _ssd(proj3, conv0, ssm0, cw, cb, dtb, alog, dvec, nrm, l_valid, t_blk):
    b, lrows, _ = proj3.shape
    n_blk = -(-l_valid // t_blk)
    assert t_blk % SSD_CHUNK == 0 and lrows == l_valid
    assert l_valid == n_blk * t_blk or (t_blk == SSD_CHUNK and B_CONV - 1 <= l_valid < t_blk and l_valid % 8 == 0)
    t_in = min(t_blk, l_valid)
    sb = _seq_batch(b, n_blk)
    vec = lambda w: pl.BlockSpec((1, w), lambda bi, t: (0, 0))
    conv_spec = pl.BlockSpec((sb, B_CONV - 1, B_CONV_DIM), lambda bi, t: (bi, 0, 0))
    ssm_spec = pl.BlockSpec((sb, HEADS, HEAD_DIM, B_STATE), lambda bi, t: (bi, 0, 0, 0))
    return pl.pallas_call(
        _per_sequence(functools.partial(_ssd_kernel, t_blk=t_blk, l_valid=l_valid, n_blk=n_blk), sb,
                      (0,) * 5 + (None,) * 6 + (0,) * 3 + (None,) * 2),
        out_shape=(jax.ShapeDtypeStruct((b, l_valid, GROUP_WIDTH), MIX_DTYPE),
                   jax.ShapeDtypeStruct((b, B_CONV - 1, B_CONV_DIM), F32),
                   jax.ShapeDtypeStruct((b, HEADS, HEAD_DIM, B_STATE), F32)),
        grid=(b // sb, n_blk),
        in_specs=[pl.BlockSpec((sb, t_in, B_CONV_DIM), lambda bi, t: (bi, t, COL_XBC // B_CONV_DIM)),
                  pl.BlockSpec((sb, t_in, GROUP_WIDTH), lambda bi, t: (bi, t, COL_BZ // GROUP_WIDTH)),
                  pl.BlockSpec((sb, t_in, GROUP_WIDTH), lambda bi, t: (bi, t, COL_DT // GROUP_WIDTH)),
                  conv_spec, ssm_spec,
                  pl.BlockSpec((B_CONV, B_CONV_DIM), lambda bi, t: (0, 0)),
                  vec(B_CONV_DIM), vec(GROUP_WIDTH), vec(GROUP_WIDTH), vec(GROUP_WIDTH), vec(GROUP_WIDTH)],
        out_specs=(pl.BlockSpec((sb, t_in, GROUP_WIDTH), lambda bi, t: (bi, t, 0)), conv_spec, ssm_spec),
        scratch_shapes=[pltpu.VMEM((CONV_PAD + SSD_INFLIGHT * SSD_CHUNK, B_CONV_DIM), F32),
                        pltpu.VMEM((B_STATE, GROUP_WIDTH), F32)],
        compiler_params=_cparams(("parallel", "arbitrary")),
        name="ssd",
    )(proj3, proj3, proj3, conv0, ssm0, cw, cb, dtb, alog, dvec, nrm)


HGRN_INFLIGHT = 8
HGRN_SAFE_LOG2 = 64.0


def _hgrn_kernel(q_ref, f_ref, i_ref, g_ref, st0_ref, lb_ref, nrm_ref, o_ref, st_out_ref, st_ref, b_blk, c_blk,
                 y_scr, *, t_blk, l_valid, n_blk, sb):
    t = pl.program_id(1)

    @pl.when(t == 0)
    def _():
        for q in range(sb):
            for h in range(HEADS):
                parts = [jnp.zeros((HEAD_DIM, HEAD_DIM * h), F32)] if h > 0 else []
                parts.append(st0_ref[q, h].T)
                if h < HEADS - 1:
                    parts.append(jnp.zeros((HEAD_DIM, HEAD_DIM * (HEADS - 1 - h)), F32))
                st_ref[q, h * HEAD_DIM:(h + 1) * HEAD_DIM, :] = jnp.concatenate(parts, axis=1)

    lb = lb_ref[...]
    nrm = nrm_ref[...]
    head_eq = _head_ones()
    ones_bd = head_eq.astype(BF16)
    tril = _tril(CHUNK).astype(BF16)
    rows = lax.broadcasted_iota(jnp.int32, (CHUNK, GROUP_WIDTH), 0)
    row_in_sub = rows % SUB
    lane_head = lax.broadcasted_iota(jnp.int32, (SUB, GROUP_WIDTH), 1) // HEAD_DIM
    n_sub = CHUNK // SUB
    n_chunks = t_blk // CHUNK
    sc_row = lax.broadcasted_iota(jnp.int32, (HEADS * SUB, CHUNK), 0) % SUB
    sc_col = lax.broadcasted_iota(jnp.int32, (HEADS * SUB, CHUNK), 1)

    def gates(q, cc, gmax):
        r0 = pl.multiple_of(cc * CHUNK, CHUNK)
        sl = pl.ds(r0, CHUNK)
        f = lb + (1.0 - lb) * jax.nn.sigmoid(f_ref[q, sl, :])
        kk = jnp.maximum(1.0 - f, 0.0)
        lf2 = jnp.log(f) * LOG2E
        if l_valid < n_blk * t_blk:
            ok = t * t_blk + r0 + rows < l_valid
            kk = jnp.where(ok, kk, 0.0)
            lf2 = jnp.where(ok, lf2, 0.0)
        bcs2 = _dot01_lhs(tril, lf2)
        b_blk[q, sl, :] = bcs2
        c_blk[q, sl, :] = jnp.log(kk) * LOG2E - bcs2
        prev = jnp.zeros((1, GROUP_WIDTH), F32)
        for a in range(n_sub):
            end = bcs2[(a + 1) * SUB - 1:(a + 1) * SUB, :]
            gmax = jnp.maximum(gmax, prev - end)
            prev = end
        return gmax

    gmax = jnp.zeros((1, GROUP_WIDTH), F32)
    for q in range(sb):
        gmax = lax.fori_loop(0, n_chunks, functools.partial(gates, q), gmax,
                             unroll=HGRN_INFLIGHT if n_chunks % HGRN_INFLIGHT == 0 else 1)
    factored_ok = jnp.max(gmax) <= HGRN_SAFE_LOG2

    def stack_heads(x):
        return jnp.concatenate([jnp.where(lane_head == h, x, 0.0) for h in range(HEADS)], axis=0).astype(BF16)

    def unstack_heads(x_all):
        out = jnp.zeros((SUB, GROUP_WIDTH), F32)
        for h in range(HEADS):
            out = out + jnp.where(lane_head == h, x_all[h * SUB:(h + 1) * SUB, :], 0.0)
        return out

    def chunks(items, factored):
        n = len(items)
        seqs = [q for q, _ in items]
        r0s = [pl.multiple_of(cc * CHUNK, CHUNK) for _, cc in items]
        sls = [pl.ds(r0, CHUNK) for r0 in r0s]
        qs = [q_ref[seqs[s], sls[s], :] for s in range(n)]
        vs = [i_ref[seqs[s], sls[s], :] for s in range(n)]
        bs = [b_blk[seqs[s], sls[s], :] for s in range(n)]
        cs = [c_blk[seqs[s], sls[s], :] for s in range(n)]
        vbs = [v.astype(BF16) for v in vs]

        os_ = []
        for s in range(n):
            st = st_ref.at[seqs[s]]
            os_.append(_dot_nt((qs[s] * jnp.exp2(bs[s])).astype(BF16), st[...].astype(BF16)))
            last = bs[s][CHUNK - 1:CHUNK, :]
            kt = jnp.exp2(last + cs[s]).astype(BF16)
            st[...] = st[...] * jnp.exp2(last) + jnp.where(head_eq, _dot_tn(vbs[s], kt), 0.0)

        pieces = [[] for _ in range(n)]
        for a in range(n_sub):
            s0 = a * SUB
            s_end = s0 + SUB if factored else s0
            if s_end == 0:
                for s in range(n):
                    pieces[s].append(jnp.zeros((SUB, GROUP_WIDTH), F32))
                continue
            scs = []
            for s in range(n):
                ref_row = bs[s][s0 - 1:s0, :] if a > 0 else jnp.zeros((1, GROUP_WIDTH), F32)
                qa = qs[s][s0:s0 + SUB, :] * jnp.exp2(bs[s][s0:s0 + SUB, :] - ref_row)
                ka = jnp.where(rows < s_end, jnp.exp2(ref_row + cs[s]), 0.0)
                sc = _dot_nt(stack_heads(qa), ka.astype(BF16))
                if factored:
                    sc = jnp.where(sc_col <= s0 + sc_row, sc, 0.0)
                scs.append(sc.astype(BF16))
            for s in range(n):
                pieces[s].append(unstack_heads(_dot(scs[s], vbs[s])))
        for s in range(n):
            os_[s] = os_[s] + jnp.concatenate(pieces[s], axis=0)

        if not factored:
            for s in range(n):
                def sub_rows(ref, jj):
                    return jnp.concatenate(
                        [jnp.broadcast_to(ref[pl.ds(r0s[s] + a * SUB + jj, 1), :], (SUB, GROUP_WIDTH))
                         for a in range(n_sub)], axis=0)

                for jj in range(SUB):
                    w = jnp.exp2(bs[s] + sub_rows(c_blk.at[seqs[s]], jj))
                    y_scr[jj * CHUNK:(jj + 1) * CHUNK, :] = jnp.where(row_in_sub >= jj, qs[s] * w, 0.0).astype(BF16)
                att = _dot(y_scr[...], ones_bd)
                for jj in range(SUB):
                    os_[s] = os_[s] + att[jj * CHUNK:(jj + 1) * CHUNK, :] * sub_rows(i_ref.at[seqs[s]], jj)

        mss = [_dot((o * o).astype(BF16), ones_bd) * (1.0 / HEAD_DIM) for o in os_]
        for s in range(n):
            res = os_[s] * lax.rsqrt(mss[s] + EPS) * nrm * _silu(g_ref[seqs[s], sls[s], :])
            o_ref[seqs[s], sls[s], :] = res.astype(MIX_DTYPE)

    @pl.when(factored_ok)
    def _():
        if sb > 1:
            chunks([(q, 0) for q in range(sb)], True)
        else:
            inflight = HGRN_INFLIGHT if n_chunks % HGRN_INFLIGHT == 0 else 1

            def group(pp, carry):
                chunks([(0, inflight * pp + s) for s in range(inflight)], True)
                return carry

            lax.fori_loop(0, n_chunks // inflight, group, 0)

    @pl.when(jnp.logical_not(factored_ok))
    def _():
        for q in range(sb):
            def one(cc, carry):
                chunks([(q, cc)], False)
                return carry

            lax.fori_loop(0, n_chunks, one, 0)

    @pl.when(t == n_blk - 1)
    def _():
        for q in range(sb):
            for h in range(HEADS):
                hs = slice(h * HEAD_DIM, (h + 1) * HEAD_DIM)
                st_out_ref[q, h] = st_ref[q, hs, hs].T


def _hgrn(proj3, col0, st0, lb, nrm, l_valid, t_blk):
    b, lrows, _ = proj3.shape
    n_blk = -(-l_valid // t_blk)
    assert n_blk * t_blk <= lrows and t_blk % CHUNK == 0
    sb = _seq_batch(b, n_blk)
    col = lambda c: pl.BlockSpec((sb, t_blk, GROUP_WIDTH),
                                 lambda bi, t: (bi, t, (c - COL_CQ + col0) // GROUP_WIDTH))
    vec = pl.BlockSpec((1, GROUP_WIDTH), lambda bi, t: (0, 0))
    st_spec = pl.BlockSpec((sb, HEADS, HEAD_DIM, HEAD_DIM), lambda bi, t: (bi, 0, 0, 0))
    return pl.pallas_call(
        functools.partial(_hgrn_kernel, t_blk=t_blk, l_valid=l_valid, n_blk=n_blk, sb=sb),
        out_shape=(jax.ShapeDtypeStruct((b, n_blk * t_blk, GROUP_WIDTH), MIX_DTYPE),
                   jax.ShapeDtypeStruct((b, HEADS, HEAD_DIM, HEAD_DIM), F32)),
        grid=(b // sb, n_blk),
        in_specs=[col(COL_CQ), col(COL_CF), col(COL_CI), col(COL_CG), st_spec, vec, vec],
        out_specs=(pl.BlockSpec((sb, t_blk, GROUP_WIDTH), lambda bi, t: (bi, t, 0)), st_spec),
        scratch_shapes=[pltpu.VMEM((sb, GROUP_WIDTH, GROUP_WIDTH), F32),
                        pltpu.VMEM((sb, t_blk, GROUP_WIDTH), F32),
                        pltpu.VMEM((sb, t_blk, GROUP_WIDTH), F32),
                        pltpu.VMEM((SUB * CHUNK, GROUP_WIDTH), BF16)],
        compiler_params=_cparams(("parallel", "arbitrary")),
        name="hgrn",
    )(proj3, proj3, proj3, proj3, st0, lb, nrm)


POOL_PAD = 24
POOL_FRONT = 8
D_GROUP = 64


def _pool_kernel(x_ref, buf_ref, w_ref, sc_ref, o_ref, buf_out_ref, xp_ref, s2_ref, s4_ref, *, t_blk):
    t = pl.program_id(1)
    hi = POOL_PAD + t_blk
    half = GROUP_WIDTH // 2

    @pl.when(t == 0)
    def _():
        zeros = jnp.zeros((POOL_PAD - D_BUF, GROUP_WIDTH), F32)
        xp_ref[0:POOL_PAD - D_BUF, :] = zeros
        s2_ref[0:POOL_FRONT, :] = zeros[0:POOL_FRONT]
        s4_ref[0:POOL_FRONT, :] = zeros[0:POOL_FRONT]
        xp_ref[POOL_PAD - D_BUF:POOL_PAD, :] = buf_ref[0]

    x = x_ref[0]
    xp_ref[POOL_PAD:hi, :] = x
    s2 = xp_ref[POOL_FRONT:hi, :] + xp_ref[POOL_FRONT - 1:hi - 1, :]
    s2_ref[POOL_FRONT:hi, :] = s2
    s4 = s2 + s2_ref[POOL_FRONT - 2:hi - 2, :]
    s4_ref[POOL_FRONT:hi, :] = s4
    s8 = s4[:, half:] + s4_ref[POOL_FRONT - 4:hi - 4, half:]
    n_ext = hi - POOL_FRONT
    s16 = s8[8:n_ext, :] + s8[0:n_ext - 8, :]
    lo_sum = s2[n_ext - t_blk:, :half]
    lane = lax.broadcasted_iota(jnp.int32, (t_blk, half), 1)
    pooled_lo = jnp.where(lane < D_GROUP, lo_sum * 0.5, s4[n_ext - t_blk:, :half] * 0.25)
    pooled_hi = jnp.where(lane < D_GROUP, s8[n_ext - t_blk:, :] * 0.125, s16[n_ext - 8 - t_blk:, :] * 0.0625)
    pooled = jnp.concatenate([pooled_lo, pooled_hi], axis=1) - x
    o_ref[0] = (_dot(pooled.astype(BF16), w_ref[...]) * sc_ref[...]).astype(MIX_DTYPE)
    tail = xp_ref[hi - D_BUF:hi, :]
    buf_out_ref[0] = tail
    xp_ref[POOL_PAD - D_BUF:POOL_PAD, :] = tail


def _pool(proj3, buf, w_bd, scale, l_valid, t_blk):
    b = proj3.shape[0]
    n_blk = l_valid // t_blk
    assert n_blk * t_blk == l_valid and t_blk >= D_BUF
    sb = _seq_batch(b, n_blk)
    buf_spec = pl.BlockSpec((sb, D_BUF, GROUP_WIDTH), lambda bi, t: (bi, 0, 0))
    return pl.pallas_call(
        _per_sequence(functools.partial(_pool_kernel, t_blk=t_blk), sb, (0, 0, None, None, 0, 0, None, None, None)),
        out_shape=(jax.ShapeDtypeStruct((b, l_valid, GROUP_WIDTH), MIX_DTYPE),
                   jax.ShapeDtypeStruct((b, D_BUF, GROUP_WIDTH), F32)),
        grid=(b // sb, n_blk),
        in_specs=[pl.BlockSpec((sb, t_blk, GROUP_WIDTH), lambda bi, t: (bi, t, COL_DX // GROUP_WIDTH)),
                  buf_spec,
                  pl.BlockSpec((GROUP_WIDTH, GROUP_WIDTH), lambda bi, t: (0, 0)),
                  pl.BlockSpec((1, GROUP_WIDTH), lambda bi, t: (0, 0))],
        out_specs=(pl.BlockSpec((sb, t_blk, GROUP_WIDTH), lambda bi, t: (bi, t, 0)), buf_spec),
        scratch_shapes=[pltpu.VMEM((POOL_PAD + t_blk, GROUP_WIDTH), F32)] * 3,
        compiler_params=_cparams(("parallel", "arbitrary")),
        name="pool",
    )(proj3, buf, w_bd, scale)


SSD_BLOCK = 1024
HGRN_BLOCK = 1024
POOL_BLOCK = 2048


def _prep_layer(layer, norm1, w_in, a_rel_bias, b_conv_w, b_conv_b, b_dt_bias, b_a_log, b_d, b_norm, lbs, c_norm,
                d_pool_w, d_pool_scale, w_out, norm2, w_gate_up, w_down):
    w_r = _w_in_prep(w_in, layer)
    tab = a_rel_bias[layer].T
    m = A_BAND + CHUNK
    far = tab[:, 2 * REL_CLIP:]
    n_far = A_BAND_PREV - REL_CLIP + 1
    r = jnp.concatenate([jnp.broadcast_to(far, (HEADS, n_far)),
                         tab[:, 2 * REL_CLIP - 1:CHUNK:-1],
                         jnp.broadcast_to(far, (HEADS, m - n_far - (2 * REL_CLIP - 1 - CHUNK)))], axis=1)
    bias = jnp.tile(r, (1, CHUNK))[:, :CHUNK * (m - 1)].reshape(HEADS, CHUNK, m - 1)[:, :, :A_BAND]
    bias = bias.reshape(HEADS * CHUNK, A_BAND)
    rep = lambda p: jnp.repeat(p[layer], HEAD_DIM)[None, :]
    pw = d_pool_w[layer]
    w_bd = jnp.zeros((GROUP_WIDTH, GROUP_WIDTH), F32)
    for g in range(4):
        w_bd = w_bd.at[g * D_GROUP:(g + 1) * D_GROUP, g * D_GROUP:(g + 1) * D_GROUP].set(pw[g])
    return dict(
        n1=norm1[layer][None, :], w_in=w_r, bias=bias,
        cw=b_conv_w[layer], cb=b_conv_b[layer][None, :], dtb=rep(b_dt_bias), alog=rep(b_a_log), dvec=rep(b_d),
        bnrm=b_norm[layer][None, :], lb=lbs[layer][None, :], cnrm=c_norm[layer].reshape(1, GROUP_WIDTH),
        w_bd=w_bd.astype(BF16), psc=d_pool_scale[layer][None, :], n2=norm2[layer][None, :], layer=layer,
        w_out=w_out.astype(BF16), w_gu=w_gate_up.astype(BF16), w_dn=w_down.astype(BF16))


def _round_up(n, m):
    return -(-n // m) * m


def _mixers(proj, st, p, cache_t):
    k_prev, v_prev, conv0, ssm0, hgrn0, pool0 = st
    b, l, _ = proj.shape
    p_hist = k_prev.shape[1]
    lq = _round_up(l, CHUNK)
    pad_rows = lambda a: a if lq == l else jnp.pad(a, ((0, 0), (0, lq - l), (0, 0)))

    keep = min(A_BAND_PREV, p_hist + l)
    n_new = min(l, keep)
    ka = proj[:, l - n_new:, COL_AK:COL_AK + GROUP_WIDTH].reshape(b, n_new, HEADS, HEAD_DIM)
    va = proj[:, l - n_new:, COL_AV:COL_AV + GROUP_WIDTH].reshape(b, n_new, HEADS, HEAD_DIM)
    if cache_t is not None:
        bias_s = p["bias"].reshape(HEADS, CHUNK, A_BAND)[:, :l].reshape(HEADS * l, A_BAND)
        oa, kt_new, vt_new = _attention_cached(proj, cache_t[0], cache_t[1], p["layer"], bias_s, l)
        untranspose = lambda a: a.reshape(b, HEADS, HEAD_DIM, A_BAND_PREV).transpose(0, 3, 1, 2)
        new_k, new_v = untranspose(kt_new), untranspose(vt_new)
    else:
        if p_hist == 0:
            k_cache = v_cache = None
        else:
            k_cache = k_prev.astype(BF16).reshape(b, p_hist, GROUP_WIDTH)
            v_cache = v_prev.astype(BF16).reshape(b, p_hist, GROUP_WIDTH)
        oa = _attention(pad_rows(proj), k_cache, v_cache, p["bias"], l)[:, :l]
        new_k = jnp.concatenate([k_prev[:, p_hist - (keep - n_new):], ka], axis=1)
        new_v = jnp.concatenate([v_prev[:, p_hist - (keep - n_new):], va], axis=1)

    ob, new_conv, new_ssm = _ssd(proj, conv0, ssm0, p["cw"], p["cb"], p["dtb"], p["alog"], p["dvec"], p["bnrm"],
                                 l, SSD_BLOCK if l % SSD_BLOCK == 0 else SSD_CHUNK)

    if l % HGRN_BLOCK == 0:
        oc, new_hgrn = _hgrn(proj, COL_CQ, hgrn0, p["lb"], p["cnrm"], l, HGRN_BLOCK)
    else:
        oc, new_hgrn = _hgrn(pad_rows(proj[:, :, COL_CQ:COL_CQ + 4 * GROUP_WIDTH]), 0, hgrn0, p["lb"], p["cnrm"], l, lq)
        oc = oc[:, :l]

    od, new_pool = _pool(proj, pool0, p["w_bd"], p["psc"], l, POOL_BLOCK if l % POOL_BLOCK == 0 else l)

    flat = lambda a: a.reshape(b * l, GROUP_WIDTH)
    return (flat(oa), flat(ob), flat(oc), flat(od)), (new_k, new_v, new_conv, new_ssm, new_hgrn, new_pool)


def _trunk_groups(groups, params, nf):
    assert len(groups) <= 2
    depth = len(params)
    xs = [x.reshape(-1, D_MODEL) for x, _ in groups]
    shapes = [x.shape for x, _ in groups]
    new = [[[] for _ in range(6)] for _ in groups]
    cache_ts = []
    for x, states in groups:
        b, l, _ = x.shape
        cache_t = None
        if states[0].shape[2] == A_BAND_PREV and l <= CHUNK and l % 8 == 0:
            as_t = lambda c: jnp.transpose(c, (0, 1, 3, 4, 2)).reshape(depth, b, GROUP_WIDTH, A_BAND_PREV)
            cache_t = (as_t(states[0]), as_t(states[1]))
        cache_ts.append(cache_t)
    for layer in range(depth):
        p = params[layer]
        projs = _inproj(xs[0], p["n1"], p["w_in"], *xs[1:])
        projs = projs if len(groups) > 1 else (projs,)
        mixes = []
        for g, (_, states) in enumerate(groups):
            b, l, _ = shapes[g]
            st_l = tuple(s[layer] for s in states)
            mix, ns = _mixers(projs[g].reshape(b, l, PROJ_W), st_l, p, cache_ts[g])
            mixes.append(mix)
            for i in range(6):
                new[g][i].append(ns[i])
        extra = (xs[1], mixes[1]) if len(groups) > 1 else None
        ys = _mlp(xs[0], mixes[0], p["layer"], p["w_out"], p["n2"], p["w_gu"], p["w_dn"], nf, layer == depth - 1, extra)
        xs = list(ys) if len(groups) > 1 else [ys]
    return [(xs[g].reshape(shapes[g]), tuple(jnp.stack(n, axis=0) for n in new[g])) for g in range(len(groups))]


def _trunk(x, states, params, nf):
    return _trunk_groups([(x, states)], params, nf)[0]


def kernel(x_prompt, x_sample, cache_a_k, cache_a_v, state_b_conv, state_b_ssm, state_c_hgrn, state_d_pool, norm1,
           w_in, a_rel_bias, b_conv_w, b_conv_b, b_dt_bias, b_a_log, b_d, b_norm, c_lb_logits, c_norm, d_pool_w,
           d_pool_scale, w_out, norm2, w_gate_up, w_down, norm_f):
    depth = w_in.shape[0]
    lbs = jnp.cumsum(jax.nn.softmax(c_lb_logits.astype(F32), axis=0), axis=0)
    lbs = lbs - lbs[:1]
    params = [_prep_layer(layer, norm1, w_in, a_rel_bias, b_conv_w, b_conv_b, b_dt_bias, b_a_log, b_d, b_norm, lbs,
                          c_norm, d_pool_w, d_pool_scale, w_out, norm2, w_gate_up, w_down)
              for layer in range(depth)]
    nf = norm_f[None, :]
    bp = x_prompt.shape[0]
    prompt_states = (
        jnp.zeros((depth, bp, 0, HEADS, HEAD_DIM), F32),
        jnp.zeros((depth, bp, 0, HEADS, HEAD_DIM), F32),
        jnp.zeros((depth, bp, B_CONV - 1, B_CONV_DIM), F32),
        jnp.zeros((depth, bp, HEADS, HEAD_DIM, B_STATE), F32),
        jnp.zeros((depth, bp, HEADS, HEAD_DIM, HEAD_DIM), F32),
        jnp.zeros((depth, bp, D_BUF, GROUP_WIDTH), F32),
    )
    sample_states = (cache_a_k, cache_a_v, state_b_conv, state_b_ssm, state_c_hgrn, state_d_pool)
    (y_prompt, ps), (y_sample, ss) = _trunk_groups([(x_prompt, prompt_states), (x_sample, sample_states)], params, nf)
    return (y_prompt, y_sample) + ps + ss
```

```python
import functools

import jax
import jax.numpy as jnp
from jax import lax
from jax.experimental import pallas as pl
from jax.experimental.pallas import tpu as pltpu

F32 = jnp.float32
BF16 = jnp.bfloat16
MIX_DTYPE = BF16

D_MODEL = 1024
GROUP_WIDTH = 256
HEADS = 4
HEAD_DIM = 64
CHUNK = 64
A_BAND_PREV = 512
A_BAND = A_BAND_PREV + CHUNK
REL_CLIP = 128
B_STATE = 128
B_CONV = 4
B_CONV_DIM = 768
D_BUF = 15
D_FF = 2816
EPS = 1e-6
NEG = -1e30
LOG2E = 1.4426950408889634
SUB = 16

COL_XBC = 0
COL_AQ = 768
COL_AK = 1024
COL_AV = 1280
COL_BZ = 1536
COL_CQ = 1792
COL_CF = 2048
COL_CI = 2304
COL_CG = 2560
COL_DX = 2816
COL_DT = 3072
PROJ_W = 3328

VMEM_LIMIT = 56 * 1024 * 1024


def _cparams(sem):
    return pltpu.CompilerParams(dimension_semantics=sem, vmem_limit_bytes=VMEM_LIMIT)


def _rms(x, w):
    return x * lax.rsqrt(jnp.mean(x * x, axis=-1, keepdims=True) + EPS) * w


def _silu(x):
    h = 0.5 * x
    return h + h * jnp.tanh(h)


def _dot(a, b):
    return jnp.dot(a, b, preferred_element_type=F32)


def _dot_nt(a, b):
    return lax.dot_general(a, b, (((1,), (1,)), ((), ())), preferred_element_type=F32)


def _dot_tn(a, b):
    return lax.dot_general(a, b, (((0,), (0,)), ((), ())), preferred_element_type=F32)


def _split3(x):
    hi = x.astype(BF16)
    r = x - hi.astype(F32)
    mid = r.astype(BF16)
    lo = (r - mid.astype(F32)).astype(BF16)
    return hi, mid, lo


def _dot01_lhs(m01, x):
    hi, mid, lo = _split3(x)
    return _dot(m01, hi) + _dot(m01, mid) + _dot(m01, lo)


SEQ_BATCH = 8


def _seq_batch(b, n_blk):
    return SEQ_BATCH if n_blk == 1 and b % SEQ_BATCH == 0 else 1


def _per_sequence(kernel_fn, sb, seq_dim):
    if sb == 1:
        return kernel_fn

    def run(*refs):
        for s in range(sb):
            kernel_fn(*[r if d is None else r.at[(slice(None),) * d + (pl.ds(s, 1),)]
                        for r, d in zip(refs, seq_dim)])

    return run


def _tril(t):
    r = lax.broadcasted_iota(jnp.int32, (t, t), 0)
    c = lax.broadcasted_iota(jnp.int32, (t, t), 1)
    return r >= c


def _head_ones():
    r = lax.broadcasted_iota(jnp.int32, (GROUP_WIDTH, GROUP_WIDTH), 0) // HEAD_DIM
    c = lax.broadcasted_iota(jnp.int32, (GROUP_WIDTH, GROUP_WIDTH), 1) // HEAD_DIM
    return r == c


IN_WIDTH = 3076
ORIG_XBC = 1024
ORIG_DT = 1792
W_PREP_COLS = 256


def _w_in_prep_kernel(wt_ref, o_ref):
    def put(c0, rows):
        o_ref[:, c0:c0 + W_PREP_COLS] = rows.T.astype(BF16)

    for j in range(3):
        put(COL_XBC + W_PREP_COLS * j, wt_ref[0, ORIG_XBC + W_PREP_COLS * j:ORIG_XBC + W_PREP_COLS * (j + 1), :])
    for j in range(4):
        put(COL_AQ + W_PREP_COLS * j, wt_ref[0, W_PREP_COLS * j:W_PREP_COLS * (j + 1), :])
    for j in range(5):
        r0 = ORIG_DT + HEADS + W_PREP_COLS * j
        put(COL_CQ + W_PREP_COLS * j, wt_ref[0, r0:r0 + W_PREP_COLS, :])
    put(COL_DT, jnp.concatenate([jnp.broadcast_to(wt_ref[0, ORIG_DT + h:ORIG_DT + h + 1, :], (HEAD_DIM, D_MODEL))
                                 for h in range(HEADS)], axis=0))


def _w_in_prep(wt_all, layer):
    return pl.pallas_call(
        _w_in_prep_kernel,
        out_shape=jax.ShapeDtypeStruct((D_MODEL, PROJ_W), BF16),
        grid=(1,),
        in_specs=[pl.BlockSpec((1, IN_WIDTH, D_MODEL), lambda i: (layer, 0, 0), pipeline_mode=pl.Buffered(1))],
        out_specs=pl.BlockSpec((D_MODEL, PROJ_W), lambda i: (0, 0)),
        compiler_params=_cparams(("arbitrary",)),
        name="w_in_prep",
    )(wt_all)


def _is_last_step():
    return pl.program_id(0) == pl.num_programs(0) - 1


def _inproj_tile(x_ref, n_ref, w_ref, o_ref):
    hb = _rms(x_ref[...], n_ref[...]).astype(BF16)
    for c0 in range(0, PROJ_W, W_PREP_COLS):
        o_ref[:, c0:c0 + W_PREP_COLS] = _dot(hb, w_ref[:, c0:c0 + W_PREP_COLS])


def _inproj_kernel(x_ref, n_ref, w_ref, *rest):
    if len(rest) == 1:
        _inproj_tile(x_ref, n_ref, w_ref, rest[0])
    else:
        xe_ref, o_ref, oe_ref = rest
        _inproj_tile(x_ref, n_ref, w_ref, o_ref)
        pl.when(_is_last_step())(lambda: _inproj_tile(xe_ref, n_ref, w_ref, oe_ref))


def _inproj(x2d, n1, w_in, extra=None):
    rows = x2d.shape[0]
    tm = next((m for m in (1024, 512) if rows % m == 0), rows)
    whole = lambda a: pl.BlockSpec(a.shape, lambda i: (0, 0))
    in_specs = [pl.BlockSpec((tm, D_MODEL), lambda i: (i, 0)),
                pl.BlockSpec((1, D_MODEL), lambda i: (0, 0)),
                pl.BlockSpec((D_MODEL, PROJ_W), lambda i: (0, 0), pipeline_mode=pl.Buffered(1))]
    out_shape = jax.ShapeDtypeStruct((rows, PROJ_W), F32)
    out_specs = pl.BlockSpec((tm, PROJ_W), lambda i: (i, 0))
    operands = (x2d, n1, w_in)
    if extra is not None:
        out_e = jax.ShapeDtypeStruct((extra.shape[0], PROJ_W), F32)
        in_specs, operands = in_specs + [whole(extra)], operands + (extra,)
        out_shape, out_specs = (out_shape, out_e), (out_specs, whole(out_e))
    return pl.pallas_call(
        _inproj_kernel,
        out_shape=out_shape,
        grid=(rows // tm,),
        in_specs=in_specs,
        out_specs=out_specs,
        compiler_params=_cparams(("arbitrary",)),
        name="inproj",
    )(*operands)


CONV_PAD = 8


def _causal_conv_silu(xpad_ref, rows, conv_w, conv_b):
    hist = B_CONV - 1
    y = conv_b
    for i in range(B_CONV):
        y = y + xpad_ref[CONV_PAD - hist + i:CONV_PAD - hist + i + rows, :] * conv_w[i:i + 1, :]
    return _silu(y)


def _softplus(x):
    return jnp.maximum(x, 0.0) + jnp.log1p(jnp.exp(-jnp.abs(x)))


FF_CHUNK = 256


def _mlp_tile(x_ref, mix_refs, wo_ref, n2_ref, wgu_ref, wdn_ref, nf_ref, o_ref, mix_ref, act_ref, final):
    for m, r in enumerate(mix_refs):
        mix_ref[:, m * GROUP_WIDTH:(m + 1) * GROUP_WIDTH] = r[...]
    x1 = x_ref[...] + _dot(mix_ref[...], wo_ref[0])
    hb = _rms(x1, n2_ref[...]).astype(BF16)
    for c0 in range(0, D_FF, FF_CHUNK):
        gate = _dot(hb, wgu_ref[0, :, c0:c0 + FF_CHUNK])
        up = _dot(hb, wgu_ref[0, :, D_FF + c0:D_FF + c0 + FF_CHUNK])
        act_ref[:, c0:c0 + FF_CHUNK] = (_silu(gate) * up).astype(BF16)
    out = x1 + _dot(act_ref[...], wdn_ref[0])
    if final:
        out = _rms(out, nf_ref[...])
    o_ref[...] = out


def _mlp_kernel(*refs, final, has_extra):
    n_in = 10
    x_ref, oa_ref, ob_ref, oc_ref, od_ref, wo_ref, n2_ref, wgu_ref, wdn_ref, nf_ref = refs[:n_in]
    weights = (wo_ref, n2_ref, wgu_ref, wdn_ref, nf_ref)
    if not has_extra:
        o_ref, mix_ref, act_ref = refs[n_in:]
        _mlp_tile(x_ref, (oa_ref, ob_ref, oc_ref, od_ref), *weights, o_ref, mix_ref, act_ref, final)
    else:
        xe_ref, ea_ref, eb_ref, ec_ref, ed_ref, o_ref, oe_ref, mix_ref, act_ref, mixe_ref, acte_ref = refs[n_in:]
        _mlp_tile(x_ref, (oa_ref, ob_ref, oc_ref, od_ref), *weights, o_ref, mix_ref, act_ref, final)
        pl.when(_is_last_step())(
            lambda: _mlp_tile(xe_ref, (ea_ref, eb_ref, ec_ref, ed_ref), *weights, oe_ref, mixe_ref, acte_ref, final))


def _mlp(x2d, mix, layer, w_out, n2, w_gu, w_dn, nf, final, extra=None):
    rows = x2d.shape[0]
    tm = 512 if rows % 512 == 0 else rows
    row_spec = lambda w: pl.BlockSpec((tm, w), lambda i: (i, 0))
    whole = lambda a: pl.BlockSpec(a.shape, lambda i: (0, 0))
    wspec = lambda a: pl.BlockSpec((1,) + a.shape[1:], lambda i: (layer, 0, 0), pipeline_mode=pl.Buffered(1))
    in_specs = ([row_spec(D_MODEL)] + [row_spec(GROUP_WIDTH)] * 4
                + [wspec(w_out), whole(n2), wspec(w_gu), wspec(w_dn), whole(nf)])
    operands = (x2d, *mix, w_out, n2, w_gu, w_dn, nf)
    out_shape = jax.ShapeDtypeStruct((rows, D_MODEL), F32)
    out_specs = row_spec(D_MODEL)
    scratch = [pltpu.VMEM((tm, D_MODEL), BF16), pltpu.VMEM((tm, D_FF), BF16)]
    if extra is not None:
        x_e, mix_e = extra
        out_e = jax.ShapeDtypeStruct(x_e.shape, F32)
        in_specs = in_specs + [whole(x_e)] + [whole(m) for m in mix_e]
        operands = operands + (x_e, *mix_e)
        out_shape, out_specs = (out_shape, out_e), (out_specs, whole(out_e))
        scratch = scratch + [pltpu.VMEM((x_e.shape[0], D_MODEL), BF16), pltpu.VMEM((x_e.shape[0], D_FF), BF16)]
    return pl.pallas_call(
        functools.partial(_mlp_kernel, final=final, has_extra=extra is not None),
        out_shape=out_shape,
        grid=(rows // tm,),
        in_specs=in_specs,
        out_specs=out_specs,
        scratch_shapes=scratch,
        compiler_params=_cparams(("arbitrary",)),
        name="mlp",
    )(*operands)


ATTN_BLOCK = 1024
ATTN_UNROLL = 8


def _attn_kernel(q_ref, kh_ref, kc_ref, vh_ref, vc_ref, bias_ref, o_ref, kb_ref, vb_ref, *, qb, p_hist, l_valid):
    i = pl.program_id(1)
    if p_hist == 0:
        @pl.when(i == 0)
        def _():
            kb_ref[0:A_BAND_PREV, :] = jnp.zeros((A_BAND_PREV, GROUP_WIDTH), BF16)
            vb_ref[0:A_BAND_PREV, :] = jnp.zeros((A_BAND_PREV, GROUP_WIDTH), BF16)

        @pl.when(i > 0)
        def _():
            kb_ref[0:A_BAND_PREV, :] = kh_ref[0].astype(BF16)
            vb_ref[0:A_BAND_PREV, :] = vh_ref[0].astype(BF16)
    else:
        kb_ref[0:A_BAND_PREV, :] = kh_ref[0].astype(BF16)
        vb_ref[0:A_BAND_PREV, :] = vh_ref[0].astype(BF16)
    kb_ref[A_BAND_PREV:A_BAND_PREV + qb, :] = kc_ref[0].astype(BF16)
    vb_ref[A_BAND_PREV:A_BAND_PREV + qb, :] = vc_ref[0].astype(BF16)
    lane_head = lax.broadcasted_iota(jnp.int32, (CHUNK, GROUP_WIDTH), 1) // HEAD_DIM
    band_pos = lax.broadcasted_iota(jnp.int32, (1, A_BAND), 1)

    n_chunks = qb // CHUNK
    group = ATTN_UNROLL if n_chunks % ATTN_UNROLL == 0 else 1

    def chunks(masked, gi, carry):
        r0s = [pl.multiple_of((gi * group + u) * CHUNK, CHUNK) for u in range(group)]
        es, dens = [], []
        for r0 in r0s:
            qc = q_ref[0, pl.ds(r0, CHUNK), :] * (HEAD_DIM ** -0.5)
            qs = jnp.concatenate([jnp.where(lane_head == h, qc, 0.0) for h in range(HEADS)], axis=0).astype(BF16)
            s = _dot_nt(qs, kb_ref[pl.ds(r0, A_BAND), :]) + bias_ref[...]
            if masked:
                kpos = i * qb + r0 - A_BAND_PREV + band_pos
                s = jnp.where((kpos >= -p_hist) & (kpos < l_valid), s, NEG)
            e = jnp.exp(s - jnp.max(s, axis=-1, keepdims=True))
            dens.append(jnp.sum(e, axis=-1, keepdims=True))
            es.append(e.astype(BF16))
        o_alls = [_dot(es[u], vb_ref[pl.ds(r0s[u], A_BAND), :]) * (1.0 / dens[u]) for u in range(group)]
        for u in range(group):
            o = jnp.zeros((CHUNK, GROUP_WIDTH), F32)
            for h in range(HEADS):
                o = o + jnp.where(lane_head == h, o_alls[u][h * CHUNK:(h + 1) * CHUNK, :], 0.0)
            o_ref[0, pl.ds(r0s[u], CHUNK), :] = o.astype(MIX_DTYPE)
        return carry

    n_blocks = pl.num_programs(1)
    first_has_invalid = p_hist < A_BAND_PREV
    last_has_invalid = l_valid % qb != 0
    if not (first_has_invalid or last_has_invalid):
        lax.fori_loop(0, n_chunks // group, functools.partial(chunks, False), 0)
    else:
        edge = False
        if first_has_invalid:
            edge = edge | (i == 0)
        if last_has_invalid:
            edge = edge | (i == n_blocks - 1)
        @pl.when(edge)
        def _():
            lax.fori_loop(0, n_chunks // group, functools.partial(chunks, True), 0)

        @pl.when(jnp.logical_not(edge))
        def _():
            lax.fori_loop(0, n_chunks // group, functools.partial(chunks, False), 0)


def _attention(proj3, k_cache, v_cache, bias, l_valid):
    b = proj3.shape[0]
    lq = _round_up(l_valid, CHUNK)
    qb = next((q for q in (ATTN_BLOCK, A_BAND_PREV) if lq % q == 0), lq)
    nblk = lq // qb
    col = lambda c: pl.BlockSpec((1, qb, GROUP_WIDTH), lambda bi, i: (bi, i, c // GROUP_WIDTH))
    if k_cache is None:
        assert qb % A_BAND_PREV == 0
        p_hist = 0
        per = qb // A_BAND_PREV
        prev = lambda c: pl.BlockSpec((1, A_BAND_PREV, GROUP_WIDTH),
                                      lambda bi, i: (bi, jnp.maximum(i * per - 1, 0), c // GROUP_WIDTH))
        kh, vh, kh_spec, vh_spec = proj3, proj3, prev(COL_AK), prev(COL_AV)
    else:
        assert nblk == 1 and k_cache.shape[1] == A_BAND_PREV
        p_hist = A_BAND_PREV
        kh_spec = vh_spec = pl.BlockSpec((1, A_BAND_PREV, GROUP_WIDTH), lambda bi, i: (bi, 0, 0))
        kh, vh = k_cache, v_cache
    return pl.pallas_call(
        functools.partial(_attn_kernel, qb=qb, p_hist=p_hist, l_valid=l_valid),
        out_shape=jax.ShapeDtypeStruct((b, lq, GROUP_WIDTH), MIX_DTYPE),
        grid=(b, nblk),
        in_specs=[col(COL_AQ), kh_spec, col(COL_AK), vh_spec, col(COL_AV),
                  pl.BlockSpec((HEADS * CHUNK, A_BAND), lambda bi, i: (0, 0))],
        out_specs=pl.BlockSpec((1, qb, GROUP_WIDTH), lambda bi, i: (bi, i, 0)),
        scratch_shapes=[pltpu.VMEM((A_BAND_PREV + qb, GROUP_WIDTH), BF16),
                        pltpu.VMEM((A_BAND_PREV + qb, GROUP_WIDTH), BF16)],
        compiler_params=_cparams(("parallel", "arbitrary")),
        name="attention",
    )(proj3, kh, proj3, vh, proj3, bias)


def _attn_cached_kernel(q_ref, kn_ref, vn_ref, kt_ref, vt_ref, bias_ref, o_ref, kt_out_ref, vt_out_ref, *, l_new):
    lane_head = lax.broadcasted_iota(jnp.int32, (l_new, GROUP_WIDTH), 1) // HEAD_DIM
    q = q_ref[0] * (HEAD_DIM ** -0.5)
    qs = jnp.concatenate([jnp.where(lane_head == h, q, 0.0) for h in range(HEADS)], axis=0).astype(BF16)
    kn = kn_ref[0]
    vn = vn_ref[0]
    kt = kt_ref[0, 0]
    vt = vt_ref[0, 0]
    s_old = _dot(qs, kt.astype(BF16)) + bias_ref[:, 0:A_BAND_PREV]
    s_new = _dot_nt(qs, kn.astype(BF16)) + bias_ref[:, A_BAND_PREV:A_BAND_PREV + l_new]
    m = jnp.maximum(jnp.max(s_old, axis=-1, keepdims=True), jnp.max(s_new, axis=-1, keepdims=True))
    e_old = jnp.exp(s_old - m)
    e_new = jnp.exp(s_new - m)
    den = jnp.sum(e_old, axis=-1, keepdims=True) + jnp.sum(e_new, axis=-1, keepdims=True)
    o_all = (_dot_nt(e_old.astype(BF16), vt.astype(BF16)) + _dot(e_new.astype(BF16), vn.astype(BF16))) * (1.0 / den)
    o = jnp.zeros((l_new, GROUP_WIDTH), F32)
    for h in range(HEADS):
        o = o + jnp.where(lane_head == h, o_all[h * l_new:(h + 1) * l_new, :], 0.0)
    o_ref[0] = o.astype(MIX_DTYPE)

    keep = A_BAND_PREV - l_new
    lane = lax.broadcasted_iota(jnp.int32, (GROUP_WIDTH, A_BAND_PREV), 1)
    r = lax.broadcasted_iota(jnp.int32, (l_new, A_BAND_PREV), 0)
    c = lax.broadcasted_iota(jnp.int32, (l_new, A_BAND_PREV), 1)
    place = (c == r + keep).astype(BF16)

    def appended(old_t, new_rows):
        hi, mid, lo = _split3(new_rows)
        new_cols = _dot_tn(hi, place) + _dot_tn(mid, place) + _dot_tn(lo, place)
        return jnp.where(lane < keep, pltpu.roll(old_t, keep, 1), new_cols)

    kt_out_ref[0] = appended(kt, kn)
    vt_out_ref[0] = appended(vt, vn)


def _attention_cached(proj3, kt_cache, vt_cache, layer, bias, l_valid):
    b = proj3.shape[0]
    sb = _seq_batch(b, 1)
    col = lambda c: pl.BlockSpec((sb, l_valid, GROUP_WIDTH), lambda bi: (bi, 0, c // GROUP_WIDTH))
    cache_spec = pl.BlockSpec((1, sb, GROUP_WIDTH, A_BAND_PREV), lambda bi: (layer, bi, 0, 0))
    out_t = pl.BlockSpec((sb, GROUP_WIDTH, A_BAND_PREV), lambda bi: (bi, 0, 0))
    return pl.pallas_call(
        _per_sequence(functools.partial(_attn_cached_kernel, l_new=l_valid), sb, (0, 0, 0, 1, 1, None, 0, 0, 0)),
        out_shape=(jax.ShapeDtypeStruct((b, l_valid, GROUP_WIDTH), MIX_DTYPE),
                   jax.ShapeDtypeStruct((b, GROUP_WIDTH, A_BAND_PREV), F32),
                   jax.ShapeDtypeStruct((b, GROUP_WIDTH, A_BAND_PREV), F32)),
        grid=(b // sb,),
        in_specs=[col(COL_AQ), col(COL_AK), col(COL_AV), cache_spec, cache_spec,
                  pl.BlockSpec((HEADS * l_valid, A_BAND), lambda bi: (0, 0))],
        out_specs=(pl.BlockSpec((sb, l_valid, GROUP_WIDTH), lambda bi: (bi, 0, 0)), out_t, out_t),
        compiler_params=_cparams(("parallel",)),
        name="attention_cached",
    )(proj3, proj3, proj3, kt_cache, vt_cache, bias)


SSD_CHUNK = 128
SSD_INFLIGHT = 4


def _ssd_kernel(xbc_ref, z_ref, dt_ref, conv0_ref, ssm0_ref, cw_ref, cb_ref, dtb_ref, alog_ref, d_ref, nrm_ref,
                o_ref, conv_out_ref, ssm_out_ref, xpad_ref, st_ref, *, t_blk, l_valid, n_blk):
    t = pl.program_id(1)
    hist = B_CONV - 1

    @pl.when(t == 0)
    def _():
        xpad_ref[CONV_PAD - hist:CONV_PAD, :] = conv0_ref[0]
        st_ref[...] = ssm0_ref[0].reshape(GROUP_WIDTH, B_STATE).T

    n_last = l_valid - (n_blk - 1) * t_blk
    conv_out_ref[0] = xbc_ref[0, n_last - hist:n_last, :]

    tc = SSD_CHUNK
    short = l_valid < t_blk

    def rows_of(ref, sl):
        if short:
            return jnp.concatenate([ref[0], jnp.zeros((tc - l_valid, ref.shape[2]), F32)], axis=0)
        return ref[0, sl, :]

    tril = _tril(tc)
    tril_b = tril.astype(BF16)
    lane = lax.broadcasted_iota(jnp.int32, (tc, B_STATE), 1)
    rows = lax.broadcasted_iota(jnp.int32, (tc, GROUP_WIDTH), 0)
    a_neg = -jnp.exp(alog_ref[...])
    conv_b = cb_ref[...]
    conv_w = cw_ref[...]
    dt_bias = dtb_ref[...]
    d_skip = d_ref[...]
    nrm = nrm_ref[...]

    n_chunks = t_blk // tc
    group = SSD_INFLIGHT if n_chunks % SSD_INFLIGHT == 0 else 1
    groups = (slice(0, B_STATE), slice(B_STATE, 2 * B_STATE))

    def chunks(gi, carry):
        rg = group * tc
        base = pl.multiple_of(gi * rg, rg)
        sl_all = pl.ds(base, rg)
        xpad_ref[CONV_PAD:CONV_PAD + rg, :] = rows_of(xbc_ref, sl_all)
        xbc_all = _causal_conv_silu(xpad_ref, rg, conv_w, conv_b)
        xpad_ref[CONV_PAD - hist:CONV_PAD, :] = xpad_ref[CONV_PAD + rg - hist:CONV_PAD + rg, :]
        dt_all = _softplus(rows_of(dt_ref, sl_all) + dt_bias)
        if l_valid < n_blk * t_blk:
            rows_all = lax.broadcasted_iota(jnp.int32, (rg, GROUP_WIDTH), 0)
            dt_all = jnp.where(t * t_blk + base + rows_all < l_valid, dt_all, 0.0)
        z_all = rows_of(z_ref, sl_all)
        xs_all = xbc_all[:, 0:GROUP_WIDTH]
        a_all = dt_all * a_neg
        u_all = xs_all * dt_all

        cut = lambda arr, s: arr[s * tc:(s + 1) * tc, :]
        acss = [_dot01_lhs(tril_b, cut(a_all, s)) for s in range(group)]
        cms = [cut(xbc_all[:, 2 * GROUP_WIDTH:3 * GROUP_WIDTH], s).astype(BF16) for s in range(group)]
        bms = [cut(xbc_all[:, GROUP_WIDTH:2 * GROUP_WIDTH], s).astype(BF16) for s in range(group)]
        cbs = [[_dot_nt(cms[s][:, gs], bms[s][:, gs]) for gs in groups] for s in range(group)]

        y_states = []
        for s in range(group):
            acs = acss[s]
            last = acs[tc - 1:tc, :]
            upb = (cut(u_all, s) * jnp.exp(last - acs)).astype(BF16)
            eacs = jnp.exp(acs)
            elast = jnp.exp(last)
            ys_s = []
            for gs in groups:
                ys_s.append(_dot(cms[s][:, gs], st_ref[:, gs].astype(BF16)) * eacs[:, gs])
                st_ref[:, gs] = st_ref[:, gs] * elast[:, gs] + _dot_tn(bms[s][:, gs], upb[:, gs])
            y_states.append(ys_s)

        for s in range(group):
            acs = acss[s]
            acs_t = acs.T
            ub = cut(u_all, s).astype(BF16)
            ys = []
            for g, gs in enumerate(groups):
                parts = []
                for hh in range(2):
                    c0 = (2 * g + hh) * HEAD_DIM
                    col = jnp.broadcast_to(acs[:, c0:c0 + 1], (tc, tc))
                    rowv = jnp.broadcast_to(acs_t[c0:c0 + 1, :], (tc, tc))
                    dec = jnp.exp(jnp.where(tril, col - rowv, NEG))
                    parts.append(_dot((cbs[s][g] * dec).astype(BF16), ub[:, gs]))
                ys.append(jnp.where(lane < HEAD_DIM, parts[0], parts[1]) + y_states[s][g])
            yv = jnp.concatenate(ys, axis=1) + d_skip * cut(xs_all, s)
            res = _rms(yv * _silu(cut(z_all, s)), nrm).astype(MIX_DTYPE)
            if short:
                o_ref[0] = res[0:l_valid, :]
            else:
                o_ref[0, pl.ds(pl.multiple_of(base + s * tc, tc), tc), :] = res
        return carry

    lax.fori_loop(0, n_chunks // group, chunks, 0)

    @pl.when(t == n_blk - 1)
    def _():
        ssm_out_ref[0] = st_ref[...].T.reshape(HEADS, HEAD_DIM, B_STATE)


def _ssd(proj3, conv0, ssm0, cw, cb, dtb, alog, dvec, nrm, l_valid, t_blk):
    b, lrows, _ = proj3.shape
    n_blk = -(-l_valid // t_blk)
    assert t_blk % SSD_CHUNK == 0 and lrows == l_valid
    assert l_valid == n_blk * t_blk or (t_blk == SSD_CHUNK and B_CONV - 1 <= l_valid < t_blk and l_valid % 8 == 0)
    t_in = min(t_blk, l_valid)
    sb = _seq_batch(b, n_blk)
    vec = lambda w: pl.BlockSpec((1, w), lambda bi, t: (0, 0))
    conv_spec = pl.BlockSpec((sb, B_CONV - 1, B_CONV_DIM), lambda bi, t: (bi, 0, 0))
    ssm_spec = pl.BlockSpec((sb, HEADS, HEAD_DIM, B_STATE), lambda bi, t: (bi, 0, 0, 0))
    return pl.pallas_call(
        _per_sequence(functools.partial(_ssd_kernel, t_blk=t_blk, l_valid=l_valid, n_blk=n_blk), sb,
                      (0,) * 5 + (None,) * 6 + (0,) * 3 + (None,) * 2),
        out_shape=(jax.ShapeDtypeStruct((b, l_valid, GROUP_WIDTH), MIX_DTYPE),
                   jax.ShapeDtypeStruct((b, B_CONV - 1, B_CONV_DIM), F32),
                   jax.ShapeDtypeStruct((b, HEADS, HEAD_DIM, B_STATE), F32)),
        grid=(b // sb, n_blk),
        in_specs=[pl.BlockSpec((sb, t_in, B_CONV_DIM), lambda bi, t: (bi, t, COL_XBC // B_CONV_DIM)),
                  pl.BlockSpec((sb, t_in, GROUP_WIDTH), lambda bi, t: (bi, t, COL_BZ // GROUP_WIDTH)),
                  pl.BlockSpec((sb, t_in, GROUP_WIDTH), lambda bi, t: (bi, t, COL_DT // GROUP_WIDTH)),
                  conv_spec, ssm_spec,
                  pl.BlockSpec((B_CONV, B_CONV_DIM), lambda bi, t: (0, 0)),
                  vec(B_CONV_DIM), vec(GROUP_WIDTH), vec(GROUP_WIDTH), vec(GROUP_WIDTH), vec(GROUP_WIDTH)],
        out_specs=(pl.BlockSpec((sb, t_in, GROUP_WIDTH), lambda bi, t: (bi, t, 0)), conv_spec, ssm_spec),
        scratch_shapes=[pltpu.VMEM((CONV_PAD + SSD_INFLIGHT * SSD_CHUNK, B_CONV_DIM), F32),
                        pltpu.VMEM((B_STATE, GROUP_WIDTH), F32)],
        compiler_params=_cparams(("parallel", "arbitrary")),
        name="ssd",
    )(proj3, proj3, proj3, conv0, ssm0, cw, cb, dtb, alog, dvec, nrm)


HGRN_INFLIGHT = 8
HGRN_SAFE_LOG2 = 64.0


def _hgrn_kernel(q_ref, f_ref, i_ref, g_ref, st0_ref, lb_ref, nrm_ref, o_ref, st_out_ref, st_ref, b_blk, c_blk,
                 y_scr, *, t_blk, l_valid, n_blk, sb):
    t = pl.program_id(1)

    @pl.when(t == 0)
    def _():
        for q in range(sb):
            for h in range(HEADS):
                parts = [jnp.zeros((HEAD_DIM, HEAD_DIM * h), F32)] if h > 0 else []
                parts.append(st0_ref[q, h].T)
                if h < HEADS - 1:
                    parts.append(jnp.zeros((HEAD_DIM, HEAD_DIM * (HEADS - 1 - h)), F32))
                st_ref[q, h * HEAD_DIM:(h + 1) * HEAD_DIM, :] = jnp.concatenate(parts, axis=1)

    lb = lb_ref[...]
    nrm = nrm_ref[...]
    head_eq = _head_ones()
    ones_bd = head_eq.astype(BF16)
    tril = _tril(CHUNK).astype(BF16)
    rows = lax.broadcasted_iota(jnp.int32, (CHUNK, GROUP_WIDTH), 0)
    row_in_sub = rows % SUB
    lane_head = lax.broadcasted_iota(jnp.int32, (SUB, GROUP_WIDTH), 1) // HEAD_DIM
    n_sub = CHUNK // SUB
    n_chunks = t_blk // CHUNK
    sc_row = lax.broadcasted_iota(jnp.int32, (HEADS * SUB, CHUNK), 0) % SUB
    sc_col = lax.broadcasted_iota(jnp.int32, (HEADS * SUB, CHUNK), 1)

    def gates(q, cc, gmax):
        r0 = pl.multiple_of(cc * CHUNK, CHUNK)
        sl = pl.ds(r0, CHUNK)
        f = lb + (1.0 - lb) * jax.nn.sigmoid(f_ref[q, sl, :])
        kk = jnp.maximum(1.0 - f, 0.0)
        lf2 = jnp.log(f) * LOG2E
        if l_valid < n_blk * t_blk:
            ok = t * t_blk + r0 + rows < l_valid
            kk = jnp.where(ok, kk, 0.0)
            lf2 = jnp.where(ok, lf2, 0.0)
        bcs2 = _dot01_lhs(tril, lf2)
        b_blk[q, sl, :] = bcs2
        c_blk[q, sl, :] = jnp.log(kk) * LOG2E - bcs2
        prev = jnp.zeros((1, GROUP_WIDTH), F32)
        for a in range(n_sub):
            end = bcs2[(a + 1) * SUB - 1:(a + 1) * SUB, :]
            gmax = jnp.maximum(gmax, prev - end)
            prev = end
        return gmax

    gmax = jnp.zeros((1, GROUP_WIDTH), F32)
    for q in range(sb):
        gmax = lax.fori_loop(0, n_chunks, functools.partial(gates, q), gmax,
                             unroll=HGRN_INFLIGHT if n_chunks % HGRN_INFLIGHT == 0 else 1)
    factored_ok = jnp.max(gmax) <= HGRN_SAFE_LOG2

    def stack_heads(x):
        return jnp.concatenate([jnp.where(lane_head == h, x, 0.0) for h in range(HEADS)], axis=0).astype(BF16)

    def unstack_heads(x_all):
        out = jnp.zeros((SUB, GROUP_WIDTH), F32)
        for h in range(HEADS):
            out = out + jnp.where(lane_head == h, x_all[h * SUB:(h + 1) * SUB, :], 0.0)
        return out

    def chunks(items, factored):
        n = len(items)
        seqs = [q for q, _ in items]
        r0s = [pl.multiple_of(cc * CHUNK, CHUNK) for _, cc in items]
        sls = [pl.ds(r0, CHUNK) for r0 in r0s]
        qs = [q_ref[seqs[s], sls[s], :] for s in range(n)]
        vs = [i_ref[seqs[s], sls[s], :] for s in range(n)]
        bs = [b_blk[seqs[s], sls[s], :] for s in range(n)]
        cs = [c_blk[seqs[s], sls[s], :] for s in range(n)]
        vbs = [v.astype(BF16) for v in vs]

        os_ = []
        for s in range(n):
            st = st_ref.at[seqs[s]]
            os_.append(_dot_nt((qs[s] * jnp.exp2(bs[s])).astype(BF16), st[...].astype(BF16)))
            last = bs[s][CHUNK - 1:CHUNK, :]
            kt = jnp.exp2(last + cs[s]).astype(BF16)
            st[...] = st[...] * jnp.exp2(last) + jnp.where(head_eq, _dot_tn(vbs[s], kt), 0.0)

        pieces = [[] for _ in range(n)]
        for a in range(n_sub):
            s0 = a * SUB
            s_end = s0 + SUB if factored else s0
            if s_end == 0:
                for s in range(n):
                    pieces[s].append(jnp.zeros((SUB, GROUP_WIDTH), F32))
                continue
            scs = []
            for s in range(n):
                ref_row = bs[s][s0 - 1:s0, :] if a > 0 else jnp.zeros((1, GROUP_WIDTH), F32)
                qa = qs[s][s0:s0 + SUB, :] * jnp.exp2(bs[s][s0:s0 + SUB, :] - ref_row)
                ka = jnp.where(rows < s_end, jnp.exp2(ref_row + cs[s]), 0.0)
                sc = _dot_nt(stack_heads(qa), ka.astype(BF16))
                if factored:
                    sc = jnp.where(sc_col <= s0 + sc_row, sc, 0.0)
                scs.append(sc.astype(BF16))
            for s in range(n):
                pieces[s].append(unstack_heads(_dot(scs[s], vbs[s])))
        for s in range(n):
            os_[s] = os_[s] + jnp.concatenate(pieces[s], axis=0)

        if not factored:
            for s in range(n):
                def sub_rows(ref, jj):
                    return jnp.concatenate(
                        [jnp.broadcast_to(ref[pl.ds(r0s[s] + a * SUB + jj, 1), :], (SUB, GROUP_WIDTH))
                         for a in range(n_sub)], axis=0)

                for jj in range(SUB):
                    w = jnp.exp2(bs[s] + sub_rows(c_blk.at[seqs[s]], jj))
                    y_scr[jj * CHUNK:(jj + 1) * CHUNK, :] = jnp.where(row_in_sub >= jj, qs[s] * w, 0.0).astype(BF16)
                att = _dot(y_scr[...], ones_bd)
                for jj in range(SUB):
                    os_[s] = os_[s] + att[jj * CHUNK:(jj + 1) * CHUNK, :] * sub_rows(i_ref.at[seqs[s]], jj)

        mss = [_dot((o * o).astype(BF16), ones_bd) * (1.0 / HEAD_DIM) for o in os_]
        for s in range(n):
            res = os_[s] * lax.rsqrt(mss[s] + EPS) * nrm * _silu(g_ref[seqs[s], sls[s], :])
            o_ref[seqs[s], sls[s], :] = res.astype(MIX_DTYPE)

    @pl.when(factored_ok)
    def _():
        if sb > 1:
            chunks([(q, 0) for q in range(sb)], True)
        else:
            inflight = HGRN_INFLIGHT if n_chunks % HGRN_INFLIGHT == 0 else 1

            def group(pp, carry):
                chunks([(0, inflight * pp + s) for s in range(inflight)], True)
                return carry

            lax.fori_loop(0, n_chunks // inflight, group, 0)

    @pl.when(jnp.logical_not(factored_ok))
    def _():
        for q in range(sb):
            def one(cc, carry):
                chunks([(q, cc)], False)
                return carry

            lax.fori_loop(0, n_chunks, one, 0)

    @pl.when(t == n_blk - 1)
    def _():
        for q in range(sb):
            for h in range(HEADS):
                hs = slice(h * HEAD_DIM, (h + 1) * HEAD_DIM)
                st_out_ref[q, h] = st_ref[q, hs, hs].T


def _hgrn(proj3, col0, st0, lb, nrm, l_valid, t_blk):
    b, lrows, _ = proj3.shape
    n_blk = -(-l_valid // t_blk)
    assert n_blk * t_blk <= lrows and t_blk % CHUNK == 0
    sb = _seq_batch(b, n_blk)
    col = lambda c: pl.BlockSpec((sb, t_blk, GROUP_WIDTH),
                                 lambda bi, t: (bi, t, (c - COL_CQ + col0) // GROUP_WIDTH))
    vec = pl.BlockSpec((1, GROUP_WIDTH), lambda bi, t: (0, 0))
    st_spec = pl.BlockSpec((sb, HEADS, HEAD_DIM, HEAD_DIM), lambda bi, t: (bi, 0, 0, 0))
    return pl.pallas_call(
        functools.partial(_hgrn_kernel, t_blk=t_blk, l_valid=l_valid, n_blk=n_blk, sb=sb),
        out_shape=(jax.ShapeDtypeStruct((b, n_blk * t_blk, GROUP_WIDTH), MIX_DTYPE),
                   jax.ShapeDtypeStruct((b, HEADS, HEAD_DIM, HEAD_DIM), F32)),
        grid=(b // sb, n_blk),
        in_specs=[col(COL_CQ), col(COL_CF), col(COL_CI), col(COL_CG), st_spec, vec, vec],
        out_specs=(pl.BlockSpec((sb, t_blk, GROUP_WIDTH), lambda bi, t: (bi, t, 0)), st_spec),
        scratch_shapes=[pltpu.VMEM((sb, GROUP_WIDTH, GROUP_WIDTH), F32),
                        pltpu.VMEM((sb, t_blk, GROUP_WIDTH), F32),
                        pltpu.VMEM((sb, t_blk, GROUP_WIDTH), F32),
                        pltpu.VMEM((SUB * CHUNK, GROUP_WIDTH), BF16)],
        compiler_params=_cparams(("parallel", "arbitrary")),
        name="hgrn",
    )(proj3, proj3, proj3, proj3, st0, lb, nrm)


POOL_PAD = 24
POOL_FRONT = 8
D_GROUP = 64


def _pool_kernel(x_ref, buf_ref, w_ref, sc_ref, o_ref, buf_out_ref, xp_ref, s2_ref, s4_ref, *, t_blk):
    t = pl.program_id(1)
    hi = POOL_PAD + t_blk
    half = GROUP_WIDTH // 2

    @pl.when(t == 0)
    def _():
        zeros = jnp.zeros((POOL_PAD - D_BUF, GROUP_WIDTH), F32)
        xp_ref[0:POOL_PAD - D_BUF, :] = zeros
        s2_ref[0:POOL_FRONT, :] = zeros[0:POOL_FRONT]
        s4_ref[0:POOL_FRONT, :] = zeros[0:POOL_FRONT]
        xp_ref[POOL_PAD - D_BUF:POOL_PAD, :] = buf_ref[0]

    x = x_ref[0]
    xp_ref[POOL_PAD:hi, :] = x
    s2 = xp_ref[POOL_FRONT:hi, :] + xp_ref[POOL_FRONT - 1:hi - 1, :]
    s2_ref[POOL_FRONT:hi, :] = s2
    s4 = s2 + s2_ref[POOL_FRONT - 2:hi - 2, :]
    s4_ref[POOL_FRONT:hi, :] = s4
    s8 = s4[:, half:] + s4_ref[POOL_FRONT - 4:hi - 4, half:]
    n_ext = hi - POOL_FRONT
    s16 = s8[8:n_ext, :] + s8[0:n_ext - 8, :]
    lo_sum = s2[n_ext - t_blk:, :half]
    lane = lax.broadcasted_iota(jnp.int32, (t_blk, half), 1)
    pooled_lo = jnp.where(lane < D_GROUP, lo_sum * 0.5, s4[n_ext - t_blk:, :half] * 0.25)
    pooled_hi = jnp.where(lane < D_GROUP, s8[n_ext - t_blk:, :] * 0.125, s16[n_ext - 8 - t_blk:, :] * 0.0625)
    pooled = jnp.concatenate([pooled_lo, pooled_hi], axis=1) - x
    o_ref[0] = (_dot(pooled.astype(BF16), w_ref[...]) * sc_ref[...]).astype(MIX_DTYPE)
    tail = xp_ref[hi - D_BUF:hi, :]
    buf_out_ref[0] = tail
    xp_ref[POOL_PAD - D_BUF:POOL_PAD, :] = tail


def _pool(proj3, buf, w_bd, scale, l_valid, t_blk):
    b = proj3.shape[0]
    n_blk = l_valid // t_blk
    assert n_blk * t_blk == l_valid and t_blk >= D_BUF
    sb = _seq_batch(b, n_blk)
    buf_spec = pl.BlockSpec((sb, D_BUF, GROUP_WIDTH), lambda bi, t: (bi, 0, 0))
    return pl.pallas_call(
        _per_sequence(functools.partial(_pool_kernel, t_blk=t_blk), sb, (0, 0, None, None, 0, 0, None, None, None)),
        out_shape=(jax.ShapeDtypeStruct((b, l_valid, GROUP_WIDTH), MIX_DTYPE),
                   jax.ShapeDtypeStruct((b, D_BUF, GROUP_WIDTH), F32)),
        grid=(b // sb, n_blk),
        in_specs=[pl.BlockSpec((sb, t_blk, GROUP_WIDTH), lambda bi, t: (bi, t, COL_DX // GROUP_WIDTH)),
                  buf_spec,
                  pl.BlockSpec((GROUP_WIDTH, GROUP_WIDTH), lambda bi, t: (0, 0)),
                  pl.BlockSpec((1, GROUP_WIDTH), lambda bi, t: (0, 0))],
        out_specs=(pl.BlockSpec((sb, t_blk, GROUP_WIDTH), lambda bi, t: (bi, t, 0)), buf_spec),
        scratch_shapes=[pltpu.VMEM((POOL_PAD + t_blk, GROUP_WIDTH), F32)] * 3,
        compiler_params=_cparams(("parallel", "arbitrary")),
        name="pool",
    )(proj3, buf, w_bd, scale)


SSD_BLOCK = 1024
HGRN_BLOCK = 1024
POOL_BLOCK = 2048


def _prep_layer(layer, norm1, w_in, a_rel_bias, b_conv_w, b_conv_b, b_dt_bias, b_a_log, b_d, b_norm, lbs, c_norm,
                d_pool_w, d_pool_scale, w_out, norm2, w_gate_up, w_down):
    w_r = _w_in_prep(jnp.transpose(w_in, (0, 2, 1)), layer)
    tab = a_rel_bias[layer].T
    m = A_BAND + CHUNK
    far = tab[:, 2 * REL_CLIP:]
    n_far = A_BAND_PREV - REL_CLIP + 1
    r = jnp.concatenate([jnp.broadcast_to(far, (HEADS, n_far)),
                         tab[:, 2 * REL_CLIP - 1:CHUNK:-1],
                         jnp.broadcast_to(far, (HEADS, m - n_far - (2 * REL_CLIP - 1 - CHUNK)))], axis=1)
    bias = jnp.tile(r, (1, CHUNK))[:, :CHUNK * (m - 1)].reshape(HEADS, CHUNK, m - 1)[:, :, :A_BAND]
    bias = bias.reshape(HEADS * CHUNK, A_BAND)
    rep = lambda p: jnp.repeat(p[layer], HEAD_DIM)[None, :]
    pw = d_pool_w[layer]
    w_bd = jnp.zeros((GROUP_WIDTH, GROUP_WIDTH), F32)
    for g in range(4):
        w_bd = w_bd.at[g * D_GROUP:(g + 1) * D_GROUP, g * D_GROUP:(g + 1) * D_GROUP].set(pw[g])
    return dict(
        n1=norm1[layer][None, :], w_in=w_r, bias=bias,
        cw=b_conv_w[layer], cb=b_conv_b[layer][None, :], dtb=rep(b_dt_bias), alog=rep(b_a_log), dvec=rep(b_d),
        bnrm=b_norm[layer][None, :], lb=lbs[layer][None, :], cnrm=c_norm[layer].reshape(1, GROUP_WIDTH),
        w_bd=w_bd.astype(BF16), psc=d_pool_scale[layer][None, :], n2=norm2[layer][None, :], layer=layer,
        w_out=w_out.astype(BF16), w_gu=w_gate_up.astype(BF16), w_dn=w_down.astype(BF16))


def _round_up(n, m):
    return -(-n // m) * m


def _mixers(proj, st, p, cache_t):
    k_prev, v_prev, conv0, ssm0, hgrn0, pool0 = st
    b, l, _ = proj.shape
    p_hist = k_prev.shape[1]
    lq = _round_up(l, CHUNK)
    pad_rows = lambda a: a if lq == l else jnp.pad(a, ((0, 0), (0, lq - l), (0, 0)))

    keep = min(A_BAND_PREV, p_hist + l)
    n_new = min(l, keep)
    ka = proj[:, l - n_new:, COL_AK:COL_AK + GROUP_WIDTH].reshape(b, n_new, HEADS, HEAD_DIM)
    va = proj[:, l - n_new:, COL_AV:COL_AV + GROUP_WIDTH].reshape(b, n_new, HEADS, HEAD_DIM)
    if cache_t is not None:
        bias_s = p["bias"].reshape(HEADS, CHUNK, A_BAND)[:, :l].reshape(HEADS * l, A_BAND)
        oa, kt_new, vt_new = _attention_cached(proj, cache_t[0], cache_t[1], p["layer"], bias_s, l)
        untranspose = lambda a: a.reshape(b, HEADS, HEAD_DIM, A_BAND_PREV).transpose(0, 3, 1, 2)
        new_k, new_v = untranspose(kt_new), untranspose(vt_new)
    else:
        if p_hist == 0:
            k_cache = v_cache = None
        else:
            k_cache = k_prev.astype(BF16).reshape(b, p_hist, GROUP_WIDTH)
            v_cache = v_prev.astype(BF16).reshape(b, p_hist, GROUP_WIDTH)
        oa = _attention(pad_rows(proj), k_cache, v_cache, p["bias"], l)[:, :l]
        new_k = jnp.concatenate([k_prev[:, p_hist - (keep - n_new):], ka], axis=1)
        new_v = jnp.concatenate([v_prev[:, p_hist - (keep - n_new):], va], axis=1)

    ob, new_conv, new_ssm = _ssd(proj, conv0, ssm0, p["cw"], p["cb"], p["dtb"], p["alog"], p["dvec"], p["bnrm"],
                                 l, SSD_BLOCK if l % SSD_BLOCK == 0 else SSD_CHUNK)

    if l % HGRN_BLOCK == 0:
        oc, new_hgrn = _hgrn(proj, COL_CQ, hgrn0, p["lb"], p["cnrm"], l, HGRN_BLOCK)
    else:
        oc, new_hgrn = _hgrn(pad_rows(proj[:, :, COL_CQ:COL_CQ + 4 * GROUP_WIDTH]), 0, hgrn0, p["lb"], p["cnrm"], l, lq)
        oc = oc[:, :l]

    od, new_pool = _pool(proj, pool0, p["w_bd"], p["psc"], l, POOL_BLOCK if l % POOL_BLOCK == 0 else l)

    flat = lambda a: a.reshape(b * l, GROUP_WIDTH)
    return (flat(oa), flat(ob), flat(oc), flat(od)), (new_k, new_v, new_conv, new_ssm, new_hgrn, new_pool)


def _trunk_groups(groups, params, nf):
    assert len(groups) <= 2
    depth = len(params)
    xs = [x.reshape(-1, D_MODEL) for x, _ in groups]
    shapes = [x.shape for x, _ in groups]
    new = [[[] for _ in range(6)] for _ in groups]
    cache_ts = []
    for x, states in groups:
        b, l, _ = x.shape
        cache_t = None
        if states[0].shape[2] == A_BAND_PREV and l <= CHUNK and l % 8 == 0:
            as_t = lambda c: jnp.transpose(c, (0, 1, 3, 4, 2)).reshape(depth, b, GROUP_WIDTH, A_BAND_PREV)
            cache_t = (as_t(states[0]), as_t(states[1]))
        cache_ts.append(cache_t)
    for layer in range(depth):
        p = params[layer]
        projs = _inproj(xs[0], p["n1"], p["w_in"], *xs[1:])
        projs = projs if len(groups) > 1 else (projs,)
        mixes = []
        for g, (_, states) in enumerate(groups):
            b, l, _ = shapes[g]
            st_l = tuple(s[layer] for s in states)
            mix, ns = _mixers(projs[g].reshape(b, l, PROJ_W), st_l, p, cache_ts[g])
            mixes.append(mix)
            for i in range(6):
                new[g][i].append(ns[i])
        extra = (xs[1], mixes[1]) if len(groups) > 1 else None
        ys = _mlp(xs[0], mixes[0], p["layer"], p["w_out"], p["n2"], p["w_gu"], p["w_dn"], nf, layer == depth - 1, extra)
        xs = list(ys) if len(groups) > 1 else [ys]
    return [(xs[g].reshape(shapes[g]), tuple(jnp.stack(n, axis=0) for n in new[g])) for g in range(len(groups))]


def _trunk(x, states, params, nf):
    return _trunk_groups([(x, states)], params, nf)[0]


def kernel(x_prompt, x_sample, cache_a_k, cache_a_v, state_b_conv, state_b_ssm, state_c_hgrn, state_d_pool, norm1,
           w_in, a_rel_bias, b_conv_w, b_conv_b, b_dt_bias, b_a_log, b_d, b_norm, c_lb_logits, c_norm, d_pool_w,
           d_pool_scale, w_out, norm2, w_gate_up, w_down, norm_f):
    depth = w_in.shape[0]
    lbs = jnp.cumsum(jax.nn.softmax(c_lb_logits.astype(F32), axis=0), axis=0)
    lbs = lbs - lbs[:1]
    params = [_prep_layer(layer, norm1, w_in, a_rel_bias, b_conv_w, b_conv_b, b_dt_bias, b_a_log, b_d, b_norm, lbs,
                          c_norm, d_pool_w, d_pool_scale, w_out, norm2, w_gate_up, w_down)
              for layer in range(depth)]
    nf = norm_f[None, :]
    bp = x_prompt.shape[0]
    prompt_states = (
        jnp.zeros((depth, bp, 0, HEADS, HEAD_DIM), F32),
        jnp.zeros((depth, bp, 0, HEADS, HEAD_DIM), F32),
        jnp.zeros((depth, bp, B_CONV - 1, B_CONV_DIM), F32),
        jnp.zeros((depth, bp, HEADS, HEAD_DIM, B_STATE), F32),
        jnp.zeros((depth, bp, HEADS, HEAD_DIM, HEAD_DIM), F32),
        jnp.zeros((depth, bp, D_BUF, GROUP_WIDTH), F32),
    )
    sample_states = (cache_a_k, cache_a_v, state_b_conv, state_b_ssm, state_c_hgrn, state_d_pool)
    (y_prompt, ps), (y_sample, ss) = _trunk_groups([(x_prompt, prompt_states), (x_sample, sample_states)], params, nf)
    return (y_prompt, y_sample) + ps + ss
```

```python
import functools

import jax
import jax.numpy as jnp
from jax import lax
from jax.experimental import pallas as pl
from jax.experimental.pallas import tpu as pltpu

F32 = jnp.float32
BF16 = jnp.bfloat16
MIX_DTYPE = BF16

D_MODEL = 1024
GROUP_WIDTH = 256
HEADS = 4
HEAD_DIM = 64
CHUNK = 64
A_BAND_PREV = 512
A_BAND = A_BAND_PREV + CHUNK
REL_CLIP = 128
B_STATE = 128
B_CONV = 4
B_CONV_DIM = 768
D_BUF = 15
D_FF = 2816
EPS = 1e-6
NEG = -1e30
LOG2E = 1.4426950408889634
SUB = 16

COL_XBC = 0
COL_AQ = 768
COL_AK = 1024
COL_AV = 1280
COL_BZ = 1536
COL_CQ = 1792
COL_CF = 2048
COL_CI = 2304
COL_CG = 2560
COL_DX = 2816
COL_DT = 3072
PROJ_W = 3328

VMEM_LIMIT = 56 * 1024 * 1024


def _cparams(sem):
    return pltpu.CompilerParams(dimension_semantics=sem, vmem_limit_bytes=VMEM_LIMIT)


def _rms(x, w):
    return x * lax.rsqrt(jnp.mean(x * x, axis=-1, keepdims=True) + EPS) * w


def _silu(x):
    h = 0.5 * x
    return h + h * jnp.tanh(h)


def _dot(a, b):
    return jnp.dot(a, b, preferred_element_type=F32)


def _dot_nt(a, b):
    return lax.dot_general(a, b, (((1,), (1,)), ((), ())), preferred_element_type=F32)


def _dot_tn(a, b):
    return lax.dot_general(a, b, (((0,), (0,)), ((), ())), preferred_element_type=F32)


def _split3(x):
    hi = x.astype(BF16)
    r = x - hi.astype(F32)
    mid = r.astype(BF16)
    lo = (r - mid.astype(F32)).astype(BF16)
    return hi, mid, lo


def _dot01_lhs(m01, x):
    hi, mid, lo = _split3(x)
    return _dot(m01, hi) + _dot(m01, mid) + _dot(m01, lo)


SEQ_BATCH = 8


def _seq_batch(b, n_blk):
    return SEQ_BATCH if n_blk == 1 and b % SEQ_BATCH == 0 else 1


def _per_sequence(kernel_fn, sb, seq_dim):
    if sb == 1:
        return kernel_fn

    def run(*refs):
        for s in range(sb):
            kernel_fn(*[r if d is None else r.at[(slice(None),) * d + (pl.ds(s, 1),)]
                        for r, d in zip(refs, seq_dim)])

    return run


def _tril(t):
    r = lax.broadcasted_iota(jnp.int32, (t, t), 0)
    c = lax.broadcasted_iota(jnp.int32, (t, t), 1)
    return r >= c


def _head_ones():
    r = lax.broadcasted_iota(jnp.int32, (GROUP_WIDTH, GROUP_WIDTH), 0) // HEAD_DIM
    c = lax.broadcasted_iota(jnp.int32, (GROUP_WIDTH, GROUP_WIDTH), 1) // HEAD_DIM
    return r == c


IN_WIDTH = 3076
ORIG_XBC = 1024
ORIG_DT = 1792
W_PREP_COLS = 256


def _w_in_prep_kernel(wt_ref, o_ref):
    def put(c0, rows):
        o_ref[:, c0:c0 + W_PREP_COLS] = rows.T.astype(BF16)

    for j in range(3):
        put(COL_XBC + W_PREP_COLS * j, wt_ref[0, ORIG_XBC + W_PREP_COLS * j:ORIG_XBC + W_PREP_COLS * (j + 1), :])
    for j in range(4):
        put(COL_AQ + W_PREP_COLS * j, wt_ref[0, W_PREP_COLS * j:W_PREP_COLS * (j + 1), :])
    for j in range(5):
        r0 = ORIG_DT + HEADS + W_PREP_COLS * j
        put(COL_CQ + W_PREP_COLS * j, wt_ref[0, r0:r0 + W_PREP_COLS, :])
    put(COL_DT, jnp.concatenate([jnp.broadcast_to(wt_ref[0, ORIG_DT + h:ORIG_DT + h + 1, :], (HEAD_DIM, D_MODEL))
                                 for h in range(HEADS)], axis=0))


def _w_in_prep(wt_all, layer):
    return pl.pallas_call(
        _w_in_prep_kernel,
        out_shape=jax.ShapeDtypeStruct((D_MODEL, PROJ_W), BF16),
        grid=(1,),
        in_specs=[pl.BlockSpec((1, IN_WIDTH, D_MODEL), lambda i: (layer, 0, 0), pipeline_mode=pl.Buffered(1))],
        out_specs=pl.BlockSpec((D_MODEL, PROJ_W), lambda i: (0, 0)),
        compiler_params=_cparams(("arbitrary",)),
        name="w_in_prep",
    )(wt_all)


def _is_last_step():
    return pl.program_id(0) == pl.num_programs(0) - 1


def _inproj_tile(x_ref, n_ref, w_ref, o_ref):
    hb = _rms(x_ref[...], n_ref[...]).astype(BF16)
    for c0 in range(0, PROJ_W, W_PREP_COLS):
        o_ref[:, c0:c0 + W_PREP_COLS] = _dot(hb, w_ref[:, c0:c0 + W_PREP_COLS])


def _inproj_kernel(x_ref, n_ref, w_ref, *rest):
    if len(rest) == 1:
        _inproj_tile(x_ref, n_ref, w_ref, rest[0])
    else:
        xe_ref, o_ref, oe_ref = rest
        _inproj_tile(x_ref, n_ref, w_ref, o_ref)
        pl.when(_is_last_step())(lambda: _inproj_tile(xe_ref, n_ref, w_ref, oe_ref))


def _inproj(x2d, n1, w_in, extra=None):
    rows = x2d.shape[0]
    tm = next((m for m in (1024, 512) if rows % m == 0), rows)
    whole = lambda a: pl.BlockSpec(a.shape, lambda i: (0, 0))
    in_specs = [pl.BlockSpec((tm, D_MODEL), lambda i: (i, 0)),
                pl.BlockSpec((1, D_MODEL), lambda i: (0, 0)),
                pl.BlockSpec((D_MODEL, PROJ_W), lambda i: (0, 0), pipeline_mode=pl.Buffered(1))]
    out_shape = jax.ShapeDtypeStruct((rows, PROJ_W), F32)
    out_specs = pl.BlockSpec((tm, PROJ_W), lambda i: (i, 0))
    operands = (x2d, n1, w_in)
    if extra is not None:
        out_e = jax.ShapeDtypeStruct((extra.shape[0], PROJ_W), F32)
        in_specs, operands = in_specs + [whole(extra)], operands + (extra,)
        out_shape, out_specs = (out_shape, out_e), (out_specs, whole(out_e))
    return pl.pallas_call(
        _inproj_kernel,
        out_shape=out_shape,
        grid=(rows // tm,),
        in_specs=in_specs,
        out_specs=out_specs,
        compiler_params=_cparams(("arbitrary",)),
        name="inproj",
    )(*operands)


CONV_PAD = 8


def _causal_conv_silu(xpad_ref, rows, conv_w, conv_b):
    hist = B_CONV - 1
    y = conv_b
    for i in range(B_CONV):
        y = y + xpad_ref[CONV_PAD - hist + i:CONV_PAD - hist + i + rows, :] * conv_w[i:i + 1, :]
    return _silu(y)


def _softplus(x):
    return jnp.maximum(x, 0.0) + jnp.log1p(jnp.exp(-jnp.abs(x)))


FF_CHUNK = 256


def _mlp_tile(x_ref, mix_refs, wo_ref, n2_ref, wgu_ref, wdn_ref, nf_ref, o_ref, mix_ref, act_ref, final):
    for m, r in enumerate(mix_refs):
        mix_ref[:, m * GROUP_WIDTH:(m + 1) * GROUP_WIDTH] = r[...]
    x1 = x_ref[...] + _dot(mix_ref[...], wo_ref[0])
    hb = _rms(x1, n2_ref[...]).astype(BF16)
    for c0 in range(0, D_FF, FF_CHUNK):
        gate = _dot(hb, wgu_ref[0, :, c0:c0 + FF_CHUNK])
        up = _dot(hb, wgu_ref[0, :, D_FF + c0:D_FF + c0 + FF_CHUNK])
        act_ref[:, c0:c0 + FF_CHUNK] = (_silu(gate) * up).astype(BF16)
    out = x1 + _dot(act_ref[...], wdn_ref[0])
    if final:
        out = _rms(out, nf_ref[...])
    o_ref[...] = out


def _mlp_kernel(*refs, final, has_extra):
    n_in = 10
    x_ref, oa_ref, ob_ref, oc_ref, od_ref, wo_ref, n2_ref, wgu_ref, wdn_ref, nf_ref = refs[:n_in]
    weights = (wo_ref, n2_ref, wgu_ref, wdn_ref, nf_ref)
    if not has_extra:
        o_ref, mix_ref, act_ref = refs[n_in:]
        _mlp_tile(x_ref, (oa_ref, ob_ref, oc_ref, od_ref), *weights, o_ref, mix_ref, act_ref, final)
    else:
        xe_ref, ea_ref, eb_ref, ec_ref, ed_ref, o_ref, oe_ref, mix_ref, act_ref, mixe_ref, acte_ref = refs[n_in:]
        _mlp_tile(x_ref, (oa_ref, ob_ref, oc_ref, od_ref), *weights, o_ref, mix_ref, act_ref, final)
        pl.when(_is_last_step())(
            lambda: _mlp_tile(xe_ref, (ea_ref, eb_ref, ec_ref, ed_ref), *weights, oe_ref, mixe_ref, acte_ref, final))


def _mlp(x2d, mix, layer, w_out, n2, w_gu, w_dn, nf, final, extra=None):
    rows = x2d.shape[0]
    tm = 512 if rows % 512 == 0 else rows
    row_spec = lambda w: pl.BlockSpec((tm, w), lambda i: (i, 0))
    whole = lambda a: pl.BlockSpec(a.shape, lambda i: (0, 0))
    wspec = lambda a: pl.BlockSpec((1,) + a.shape[1:], lambda i: (layer, 0, 0), pipeline_mode=pl.Buffered(1))
    in_specs = ([row_spec(D_MODEL)] + [row_spec(GROUP_WIDTH)] * 4
                + [wspec(w_out), whole(n2), wspec(w_gu), wspec(w_dn), whole(nf)])
    operands = (x2d, *mix, w_out, n2, w_gu, w_dn, nf)
    out_shape = jax.ShapeDtypeStruct((rows, D_MODEL), F32)
    out_specs = row_spec(D_MODEL)
    scratch = [pltpu.VMEM((tm, D_MODEL), BF16), pltpu.VMEM((tm, D_FF), BF16)]
    if extra is not None:
        x_e, mix_e = extra
        out_e = jax.ShapeDtypeStruct(x_e.shape, F32)
        in_specs = in_specs + [whole(x_e)] + [whole(m) for m in mix_e]
        operands = operands + (x_e, *mix_e)
        out_shape, out_specs = (out_shape, out_e), (out_specs, whole(out_e))
        scratch = scratch + [pltpu.VMEM((x_e.shape[0], D_MODEL), BF16), pltpu.VMEM((x_e.shape[0], D_FF), BF16)]
    return pl.pallas_call(
        functools.partial(_mlp_kernel, final=final, has_extra=extra is not None),
        out_shape=out_shape,
        grid=(rows // tm,),
        in_specs=in_specs,
        out_specs=out_specs,
        scratch_shapes=scratch,
        compiler_params=_cparams(("arbitrary",)),
        name="mlp",
    )(*operands)


ATTN_BLOCK = 2048
ATTN_UNROLL = 8


def _attn_kernel(q_ref, kh_ref, kc_ref, vh_ref, vc_ref, bias_ref, o_ref, kb_ref, vb_ref, *, qb, p_hist, l_valid):
    i = pl.program_id(1)
    if p_hist == 0:
        @pl.when(i == 0)
        def _():
            kb_ref[0:A_BAND_PREV, :] = jnp.zeros((A_BAND_PREV, GROUP_WIDTH), BF16)
            vb_ref[0:A_BAND_PREV, :] = jnp.zeros((A_BAND_PREV, GROUP_WIDTH), BF16)

        @pl.when(i > 0)
        def _():
            kb_ref[0:A_BAND_PREV, :] = kh_ref[0].astype(BF16)
            vb_ref[0:A_BAND_PREV, :] = vh_ref[0].astype(BF16)
    else:
        kb_ref[0:A_BAND_PREV, :] = kh_ref[0].astype(BF16)
        vb_ref[0:A_BAND_PREV, :] = vh_ref[0].astype(BF16)
    kb_ref[A_BAND_PREV:A_BAND_PREV + qb, :] = kc_ref[0].astype(BF16)
    vb_ref[A_BAND_PREV:A_BAND_PREV + qb, :] = vc_ref[0].astype(BF16)
    lane_head = lax.broadcasted_iota(jnp.int32, (CHUNK, GROUP_WIDTH), 1) // HEAD_DIM
    band_pos = lax.broadcasted_iota(jnp.int32, (1, A_BAND), 1)

    n_chunks = qb // CHUNK
    group = ATTN_UNROLL if n_chunks % ATTN_UNROLL == 0 else 1

    def chunks(masked, gi, carry):
        r0s = [pl.multiple_of((gi * group + u) * CHUNK, CHUNK) for u in range(group)]
        es, dens = [], []
        for r0 in r0s:
            qc = q_ref[0, pl.ds(r0, CHUNK), :] * (HEAD_DIM ** -0.5)
            qs = jnp.concatenate([jnp.where(lane_head == h, qc, 0.0) for h in range(HEADS)], axis=0).astype(BF16)
            s = _dot_nt(qs, kb_ref[pl.ds(r0, A_BAND), :]) + bias_ref[...]
            if masked:
                kpos = i * qb + r0 - A_BAND_PREV + band_pos
                s = jnp.where((kpos >= -p_hist) & (kpos < l_valid), s, NEG)
            e = jnp.exp(s - jnp.max(s, axis=-1, keepdims=True))
            dens.append(jnp.sum(e, axis=-1, keepdims=True))
            es.append(e.astype(BF16))
        o_alls = [_dot(es[u], vb_ref[pl.ds(r0s[u], A_BAND), :]) * (1.0 / dens[u]) for u in range(group)]
        for u in range(group):
            o = jnp.zeros((CHUNK, GROUP_WIDTH), F32)
            for h in range(HEADS):
                o = o + jnp.where(lane_head == h, o_alls[u][h * CHUNK:(h + 1) * CHUNK, :], 0.0)
            o_ref[0, pl.ds(r0s[u], CHUNK), :] = o.astype(MIX_DTYPE)
        return carry

    n_blocks = pl.num_programs(1)
    first_has_invalid = p_hist < A_BAND_PREV
    last_has_invalid = l_valid % qb != 0
    if not (first_has_invalid or last_has_invalid):
        lax.fori_loop(0, n_chunks // group, functools.partial(chunks, False), 0)
    else:
        edge = False
        if first_has_invalid:
            edge = edge | (i == 0)
        if last_has_invalid:
            edge = edge | (i == n_blocks - 1)
        @pl.when(edge)
        def _():
            lax.fori_loop(0, n_chunks // group, functools.partial(chunks, True), 0)

        @pl.when(jnp.logical_not(edge))
        def _():
            lax.fori_loop(0, n_chunks // group, functools.partial(chunks, False), 0)


def _attention(proj3, k_cache, v_cache, bias, l_valid):
    b = proj3.shape[0]
    lq = _round_up(l_valid, CHUNK)
    qb = next((q for q in (ATTN_BLOCK, A_BAND_PREV) if lq % q == 0), lq)
    nblk = lq // qb
    col = lambda c: pl.BlockSpec((1, qb, GROUP_WIDTH), lambda bi, i: (bi, i, c // GROUP_WIDTH))
    if k_cache is None:
        assert qb % A_BAND_PREV == 0
        p_hist = 0
        per = qb // A_BAND_PREV
        prev = lambda c: pl.BlockSpec((1, A_BAND_PREV, GROUP_WIDTH),
                                      lambda bi, i: (bi, jnp.maximum(i * per - 1, 0), c // GROUP_WIDTH))
        kh, vh, kh_spec, vh_spec = proj3, proj3, prev(COL_AK), prev(COL_AV)
    else:
        assert nblk == 1 and k_cache.shape[1] == A_BAND_PREV
        p_hist = A_BAND_PREV
        kh_spec = vh_spec = pl.BlockSpec((1, A_BAND_PREV, GROUP_WIDTH), lambda bi, i: (bi, 0, 0))
        kh, vh = k_cache, v_cache
    return pl.pallas_call(
        functools.partial(_attn_kernel, qb=qb, p_hist=p_hist, l_valid=l_valid),
        out_shape=jax.ShapeDtypeStruct((b, lq, GROUP_WIDTH), MIX_DTYPE),
        grid=(b, nblk),
        in_specs=[col(COL_AQ), kh_spec, col(COL_AK), vh_spec, col(COL_AV),
                  pl.BlockSpec((HEADS * CHUNK, A_BAND), lambda bi, i: (0, 0))],
        out_specs=pl.BlockSpec((1, qb, GROUP_WIDTH), lambda bi, i: (bi, i, 0)),
        scratch_shapes=[pltpu.VMEM((A_BAND_PREV + qb, GROUP_WIDTH), BF16),
                        pltpu.VMEM((A_BAND_PREV + qb, GROUP_WIDTH), BF16)],
        compiler_params=_cparams(("parallel", "arbitrary")),
        name="attention",
    )(proj3, kh, proj3, vh, proj3, bias)


def _attn_cached_kernel(q_ref, kn_ref, vn_ref, kt_ref, vt_ref, bias_ref, o_ref, kt_out_ref, vt_out_ref, *, l_new):
    lane_head = lax.broadcasted_iota(jnp.int32, (l_new, GROUP_WIDTH), 1) // HEAD_DIM
    q = q_ref[0] * (HEAD_DIM ** -0.5)
    qs = jnp.concatenate([jnp.where(lane_head == h, q, 0.0) for h in range(HEADS)], axis=0).astype(BF16)
    kn = kn_ref[0]
    vn = vn_ref[0]
    kt = kt_ref[0, 0]
    vt = vt_ref[0, 0]
    s_old = _dot(qs, kt.astype(BF16)) + bias_ref[:, 0:A_BAND_PREV]
    s_new = _dot_nt(qs, kn.astype(BF16)) + bias_ref[:, A_BAND_PREV:A_BAND_PREV + l_new]
    m = jnp.maximum(jnp.max(s_old, axis=-1, keepdims=True), jnp.max(s_new, axis=-1, keepdims=True))
    e_old = jnp.exp(s_old - m)
    e_new = jnp.exp(s_new - m)
    den = jnp.sum(e_old, axis=-1, keepdims=True) + jnp.sum(e_new, axis=-1, keepdims=True)
    o_all = (_dot_nt(e_old.astype(BF16), vt.astype(BF16)) + _dot(e_new.astype(BF16), vn.astype(BF16))) * (1.0 / den)
    o = jnp.zeros((l_new, GROUP_WIDTH), F32)
    for h in range(HEADS):
        o = o + jnp.where(lane_head == h, o_all[h * l_new:(h + 1) * l_new, :], 0.0)
    o_ref[0] = o.astype(MIX_DTYPE)

    keep = A_BAND_PREV - l_new
    lane = lax.broadcasted_iota(jnp.int32, (GROUP_WIDTH, A_BAND_PREV), 1)
    r = lax.broadcasted_iota(jnp.int32, (l_new, A_BAND_PREV), 0)
    c = lax.broadcasted_iota(jnp.int32, (l_new, A_BAND_PREV), 1)
    place = (c == r + keep).astype(BF16)

    def appended(old_t, new_rows):
        hi, mid, lo = _split3(new_rows)
        new_cols = _dot_tn(hi, place) + _dot_tn(mid, place) + _dot_tn(lo, place)
        return jnp.where(lane < keep, pltpu.roll(old_t, keep, 1), new_cols)

    kt_out_ref[0] = appended(kt, kn)
    vt_out_ref[0] = appended(vt, vn)


def _attention_cached(proj3, kt_cache, vt_cache, layer, bias, l_valid):
    b = proj3.shape[0]
    sb = _seq_batch(b, 1)
    col = lambda c: pl.BlockSpec((sb, l_valid, GROUP_WIDTH), lambda bi: (bi, 0, c // GROUP_WIDTH))
    cache_spec = pl.BlockSpec((1, sb, GROUP_WIDTH, A_BAND_PREV), lambda bi: (layer, bi, 0, 0))
    out_t = pl.BlockSpec((sb, GROUP_WIDTH, A_BAND_PREV), lambda bi: (bi, 0, 0))
    return pl.pallas_call(
        _per_sequence(functools.partial(_attn_cached_kernel, l_new=l_valid), sb, (0, 0, 0, 1, 1, None, 0, 0, 0)),
        out_shape=(jax.ShapeDtypeStruct((b, l_valid, GROUP_WIDTH), MIX_DTYPE),
                   jax.ShapeDtypeStruct((b, GROUP_WIDTH, A_BAND_PREV), F32),
                   jax.ShapeDtypeStruct((b, GROUP_WIDTH, A_BAND_PREV), F32)),
        grid=(b // sb,),
        in_specs=[col(COL_AQ), col(COL_AK), col(COL_AV), cache_spec, cache_spec,
                  pl.BlockSpec((HEADS * l_valid, A_BAND), lambda bi: (0, 0))],
        out_specs=(pl.BlockSpec((sb, l_valid, GROUP_WIDTH), lambda bi: (bi, 0, 0)), out_t, out_t),
        compiler_params=_cparams(("parallel",)),
        name="attention_cached",
    )(proj3, proj3, proj3, kt_cache, vt_cache, bias)


SSD_CHUNK = 128
SSD_INFLIGHT = 4


def _ssd_kernel(xbc_ref, z_ref, dt_ref, conv0_ref, ssm0_ref, cw_ref, cb_ref, dtb_ref, alog_ref, d_ref, nrm_ref,
                o_ref, conv_out_ref, ssm_out_ref, xpad_ref, st_ref, *, t_blk, l_valid, n_blk):
    t = pl.program_id(1)
    hist = B_CONV - 1

    @pl.when(t == 0)
    def _():
        xpad_ref[CONV_PAD - hist:CONV_PAD, :] = conv0_ref[0]
        st_ref[...] = ssm0_ref[0].reshape(GROUP_WIDTH, B_STATE).T

    n_last = l_valid - (n_blk - 1) * t_blk
    conv_out_ref[0] = xbc_ref[0, n_last - hist:n_last, :]

    tc = SSD_CHUNK
    short = l_valid < t_blk

    def rows_of(ref, sl):
        if short:
            return jnp.concatenate([ref[0], jnp.zeros((tc - l_valid, ref.shape[2]), F32)], axis=0)
        return ref[0, sl, :]

    tril = _tril(tc)
    tril_b = tril.astype(BF16)
    lane = lax.broadcasted_iota(jnp.int32, (tc, B_STATE), 1)
    rows = lax.broadcasted_iota(jnp.int32, (tc, GROUP_WIDTH), 0)
    a_neg = -jnp.exp(alog_ref[...])
    conv_b = cb_ref[...]
    conv_w = cw_ref[...]
    dt_bias = dtb_ref[...]
    d_skip = d_ref[...]
    nrm = nrm_ref[...]

    n_chunks = t_blk // tc
    group = SSD_INFLIGHT if n_chunks % SSD_INFLIGHT == 0 else 1
    groups = (slice(0, B_STATE), slice(B_STATE, 2 * B_STATE))

    def chunks(gi, carry):
        rg = group * tc
        base = pl.multiple_of(gi * rg, rg)
        sl_all = pl.ds(base, rg)
        xpad_ref[CONV_PAD:CONV_PAD + rg, :] = rows_of(xbc_ref, sl_all)
        xbc_all = _causal_conv_silu(xpad_ref, rg, conv_w, conv_b)
        xpad_ref[CONV_PAD - hist:CONV_PAD, :] = xpad_ref[CONV_PAD + rg - hist:CONV_PAD + rg, :]
        dt_all = _softplus(rows_of(dt_ref, sl_all) + dt_bias)
        if l_valid < n_blk * t_blk:
            rows_all = lax.broadcasted_iota(jnp.int32, (rg, GROUP_WIDTH), 0)
            dt_all = jnp.where(t * t_blk + base + rows_all < l_valid, dt_all, 0.0)
        z_all = rows_of(z_ref, sl_all)
        xs_all = xbc_all[:, 0:GROUP_WIDTH]
        a_all = dt_all * a_neg
        u_all = xs_all * dt_all

        cut = lambda arr, s: arr[s * tc:(s + 1) * tc, :]
        acss = [_dot01_lhs(tril_b, cut(a_all, s)) for s in range(group)]
        cms = [cut(xbc_all[:, 2 * GROUP_WIDTH:3 * GROUP_WIDTH], s).astype(BF16) for s in range(group)]
        bms = [cut(xbc_all[:, GROUP_WIDTH:2 * GROUP_WIDTH], s).astype(BF16) for s in range(group)]
        cbs = [[_dot_nt(cms[s][:, gs], bms[s][:, gs]) for gs in groups] for s in range(group)]

        y_states = []
        for s in range(group):
            acs = acss[s]
            last = acs[tc - 1:tc, :]
            upb = (cut(u_all, s) * jnp.exp(last - acs)).astype(BF16)
            eacs = jnp.exp(acs)
            elast = jnp.exp(last)
            ys_s = []
            for gs in groups:
                ys_s.append(_dot(cms[s][:, gs], st_ref[:, gs].astype(BF16)) * eacs[:, gs])
                st_ref[:, gs] = st_ref[:, gs] * elast[:, gs] + _dot_tn(bms[s][:, gs], upb[:, gs])
            y_states.append(ys_s)

        for s in range(group):
            acs = acss[s]
            acs_t = acs.T
            ub = cut(u_all, s).astype(BF16)
            ys = []
            for g, gs in enumerate(groups):
                parts = []
                for hh in range(2):
                    c0 = (2 * g + hh) * HEAD_DIM
                    col = jnp.broadcast_to(acs[:, c0:c0 + 1], (tc, tc))
                    rowv = jnp.broadcast_to(acs_t[c0:c0 + 1, :], (tc, tc))
                    dec = jnp.exp(jnp.where(tril, col - rowv, NEG))
                    parts.append(_dot((cbs[s][g] * dec).astype(BF16), ub[:, gs]))
                ys.append(jnp.where(lane < HEAD_DIM, parts[0], parts[1]) + y_states[s][g])
            yv = jnp.concatenate(ys, axis=1) + d_skip * cut(xs_all, s)
            res = _rms(yv * _silu(cut(z_all, s)), nrm).astype(MIX_DTYPE)
            if short:
                o_ref[0] = res[0:l_valid, :]
            else:
                o_ref[0, pl.ds(pl.multiple_of(base + s * tc, tc), tc), :] = res
        return carry

    lax.fori_loop(0, n_chunks // group, chunks, 0)

    @pl.when(t == n_blk - 1)
    def _():
        ssm_out_ref[0] = st_ref[...].T.reshape(HEADS, HEAD_DIM, B_STATE)


def _ssd(proj3, conv0, ssm0, cw, cb, dtb, alog, dvec, nrm, l_valid, t_blk):
    b, lrows, _ = proj3.shape
    n_blk = -(-l_valid // t_blk)
    assert t_blk % SSD_CHUNK == 0 and lrows == l_valid
    assert l_valid == n_blk * t_blk or (t_blk == SSD_CHUNK and B_CONV - 1 <= l_valid < t_blk and l_valid % 8 == 0)
    t_in = min(t_blk, l_valid)
    sb = _seq_batch(b, n_blk)
    vec = lambda w: pl.BlockSpec((1, w), lambda bi, t: (0, 0))
    conv_spec = pl.BlockSpec((sb, B_CONV - 1, B_CONV_DIM), lambda bi, t: (bi, 0, 0))
    ssm_spec = pl.BlockSpec((sb, HEADS, HEAD_DIM, B_STATE), lambda bi, t: (bi, 0, 0, 0))
    return pl.pallas_call(
        _per_sequence(functools.partial(_ssd_kernel, t_blk=t_blk, l_valid=l_valid, n_blk=n_blk), sb,
                      (0,) * 5 + (None,) * 6 + (0,) * 3 + (None,) * 2),
        out_shape=(jax.ShapeDtypeStruct((b, l_valid, GROUP_WIDTH), MIX_DTYPE),
                   jax.ShapeDtypeStruct((b, B_CONV - 1, B_CONV_DIM), F32),
                   jax.ShapeDtypeStruct((b, HEADS, HEAD_DIM, B_STATE), F32)),
        grid=(b // sb, n_blk),
        in_specs=[pl.BlockSpec((sb, t_in, B_CONV_DIM), lambda bi, t: (bi, t, COL_XBC // B_CONV_DIM)),
                  pl.BlockSpec((sb, t_in, GROUP_WIDTH), lambda bi, t: (bi, t, COL_BZ // GROUP_WIDTH)),
                  pl.BlockSpec((sb, t_in, GROUP_WIDTH), lambda bi, t: (bi, t, COL_DT // GROUP_WIDTH)),
                  conv_spec, ssm_spec,
                  pl.BlockSpec((B_CONV, B_CONV_DIM), lambda bi, t: (0, 0)),
                  vec(B_CONV_DIM), vec(GROUP_WIDTH), vec(GROUP_WIDTH), vec(GROUP_WIDTH), vec(GROUP_WIDTH)],
        out_specs=(pl.BlockSpec((sb, t_in, GROUP_WIDTH), lambda bi, t: (bi, t, 0)), conv_spec, ssm_spec),
        scratch_shapes=[pltpu.VMEM((CONV_PAD + SSD_INFLIGHT * SSD_CHUNK, B_CONV_DIM), F32),
                        pltpu.VMEM((B_STATE, GROUP_WIDTH), F32)],
        compiler_params=_cparams(("parallel", "arbitrary")),
        name="ssd",
    )(proj3, proj3, proj3, conv0, ssm0, cw, cb, dtb, alog, dvec, nrm)


HGRN_INFLIGHT = 8
HGRN_SAFE_LOG2 = 64.0


def _hgrn_kernel(q_ref, f_ref, i_ref, g_ref, st0_ref, lb_ref, nrm_ref, o_ref, st_out_ref, st_ref, b_blk, c_blk,
                 y_scr, *, t_blk, l_valid, n_blk, sb):
    t = pl.program_id(1)

    @pl.when(t == 0)
    def _():
        for q in range(sb):
            for h in range(HEADS):
                parts = [jnp.zeros((HEAD_DIM, HEAD_DIM * h), F32)] if h > 0 else []
                parts.append(st0_ref[q, h].T)
                if h < HEADS - 1:
                    parts.append(jnp.zeros((HEAD_DIM, HEAD_DIM * (HEADS - 1 - h)), F32))
                st_ref[q, h * HEAD_DIM:(h + 1) * HEAD_DIM, :] = jnp.concatenate(parts, axis=1)

    lb = lb_ref[...]
    nrm = nrm_ref[...]
    head_eq = _head_ones()
    ones_bd = head_eq.astype(BF16)
    tril = _tril(CHUNK).astype(BF16)
    rows = lax.broadcasted_iota(jnp.int32, (CHUNK, GROUP_WIDTH), 0)
    row_in_sub = rows % SUB
    lane_head = lax.broadcasted_iota(jnp.int32, (SUB, GROUP_WIDTH), 1) // HEAD_DIM
    n_sub = CHUNK // SUB
    n_chunks = t_blk // CHUNK
    sc_row = lax.broadcasted_iota(jnp.int32, (HEADS * SUB, CHUNK), 0) % SUB
    sc_col = lax.broadcasted_iota(jnp.int32, (HEADS * SUB, CHUNK), 1)

    def gates(q, cc, gmax):
        r0 = pl.multiple_of(cc * CHUNK, CHUNK)
        sl = pl.ds(r0, CHUNK)
        f = lb + (1.0 - lb) * jax.nn.sigmoid(f_ref[q, sl, :])
        kk = jnp.maximum(1.0 - f, 0.0)
        lf2 = jnp.log(f) * LOG2E
        if l_valid < n_blk * t_blk:
            ok = t * t_blk + r0 + rows < l_valid
            kk = jnp.where(ok, kk, 0.0)
            lf2 = jnp.where(ok, lf2, 0.0)
        bcs2 = _dot01_lhs(tril, lf2)
        b_blk[q, sl, :] = bcs2
        c_blk[q, sl, :] = jnp.log(kk) * LOG2E - bcs2
        prev = jnp.zeros((1, GROUP_WIDTH), F32)
        for a in range(n_sub):
            end = bcs2[(a + 1) * SUB - 1:(a + 1) * SUB, :]
            gmax = jnp.maximum(gmax, prev - end)
            prev = end
        return gmax

    gmax = jnp.zeros((1, GROUP_WIDTH), F32)
    for q in range(sb):
        gmax = lax.fori_loop(0, n_chunks, functools.partial(gates, q), gmax,
                             unroll=HGRN_INFLIGHT if n_chunks % HGRN_INFLIGHT == 0 else 1)
    factored_ok = jnp.max(gmax) <= HGRN_SAFE_LOG2

    def stack_heads(x):
        return jnp.concatenate([jnp.where(lane_head == h, x, 0.0) for h in range(HEADS)], axis=0).astype(BF16)

    def unstack_heads(x_all):
        out = jnp.zeros((SUB, GROUP_WIDTH), F32)
        for h in range(HEADS):
            out = out + jnp.where(lane_head == h, x_all[h * SUB:(h + 1) * SUB, :], 0.0)
        return out

    def chunks(items, factored):
        n = len(items)
        seqs = [q for q, _ in items]
        r0s = [pl.multiple_of(cc * CHUNK, CHUNK) for _, cc in items]
        sls = [pl.ds(r0, CHUNK) for r0 in r0s]
        qs = [q_ref[seqs[s], sls[s], :] for s in range(n)]
        vs = [i_ref[seqs[s], sls[s], :] for s in range(n)]
        bs = [b_blk[seqs[s], sls[s], :] for s in range(n)]
        cs = [c_blk[seqs[s], sls[s], :] for s in range(n)]
        vbs = [v.astype(BF16) for v in vs]

        os_ = []
        for s in range(n):
            st = st_ref.at[seqs[s]]
            os_.append(_dot_nt((qs[s] * jnp.exp2(bs[s])).astype(BF16), st[...].astype(BF16)))
            last = bs[s][CHUNK - 1:CHUNK, :]
            kt = jnp.exp2(last + cs[s]).astype(BF16)
            st[...] = st[...] * jnp.exp2(last) + jnp.where(head_eq, _dot_tn(vbs[s], kt), 0.0)

        pieces = [[] for _ in range(n)]
        for a in range(n_sub):
            s0 = a * SUB
            s_end = s0 + SUB if factored else s0
            if s_end == 0:
                for s in range(n):
                    pieces[s].append(jnp.zeros((SUB, GROUP_WIDTH), F32))
                continue
            scs = []
            for s in range(n):
                ref_row = bs[s][s0 - 1:s0, :] if a > 0 else jnp.zeros((1, GROUP_WIDTH), F32)
                qa = qs[s][s0:s0 + SUB, :] * jnp.exp2(bs[s][s0:s0 + SUB, :] - ref_row)
                ka = jnp.where(rows < s_end, jnp.exp2(ref_row + cs[s]), 0.0)
                sc = _dot_nt(stack_heads(qa), ka.astype(BF16))
                if factored:
                    sc = jnp.where(sc_col <= s0 + sc_row, sc, 0.0)
                scs.append(sc.astype(BF16))
            for s in range(n):
                pieces[s].append(unstack_heads(_dot(scs[s], vbs[s])))
        for s in range(n):
            os_[s] = os_[s] + jnp.concatenate(pieces[s], axis=0)

        if not factored:
            for s in range(n):
                def sub_rows(ref, jj):
                    return jnp.concatenate(
                        [jnp.broadcast_to(ref[pl.ds(r0s[s] + a * SUB + jj, 1), :], (SUB, GROUP_WIDTH))
                         for a in range(n_sub)], axis=0)

                for jj in range(SUB):
                    w = jnp.exp2(bs[s] + sub_rows(c_blk.at[seqs[s]], jj))
                    y_scr[jj * CHUNK:(jj + 1) * CHUNK, :] = jnp.where(row_in_sub >= jj, qs[s] * w, 0.0).astype(BF16)
                att = _dot(y_scr[...], ones_bd)
                for jj in range(SUB):
                    os_[s] = os_[s] + att[jj * CHUNK:(jj + 1) * CHUNK, :] * sub_rows(i_ref.at[seqs[s]], jj)

        mss = [_dot((o * o).astype(BF16), ones_bd) * (1.0 / HEAD_DIM) for o in os_]
        for s in range(n):
            res = os_[s] * lax.rsqrt(mss[s] + EPS) * nrm * _silu(g_ref[seqs[s], sls[s], :])
            o_ref[seqs[s], sls[s], :] = res.astype(MIX_DTYPE)

    @pl.when(factored_ok)
    def _():
        if sb > 1:
            chunks([(q, 0) for q in range(sb)], True)
        else:
            inflight = HGRN_INFLIGHT if n_chunks % HGRN_INFLIGHT == 0 else 1

            def group(pp, carry):
                chunks([(0, inflight * pp + s) for s in range(inflight)], True)
                return carry

            lax.fori_loop(0, n_chunks // inflight, group, 0)

    @pl.when(jnp.logical_not(factored_ok))
    def _():
        for q in range(sb):
            def one(cc, carry):
                chunks([(q, cc)], False)
                return carry

            lax.fori_loop(0, n_chunks, one, 0)

    @pl.when(t == n_blk - 1)
    def _():
        for q in range(sb):
            for h in range(HEADS):
                hs = slice(h * HEAD_DIM, (h + 1) * HEAD_DIM)
                st_out_ref[q, h] = st_ref[q, hs, hs].T


def _hgrn(proj3, col0, st0, lb, nrm, l_valid, t_blk):
    b, lrows, _ = proj3.shape
    n_blk = -(-l_valid // t_blk)
    assert n_blk * t_blk <= lrows and t_blk % CHUNK == 0
    sb = _seq_batch(b, n_blk)
    col = lambda c: pl.BlockSpec((sb, t_blk, GROUP_WIDTH),
                                 lambda bi, t: (bi, t, (c - COL_CQ + col0) // GROUP_WIDTH))
    vec = pl.BlockSpec((1, GROUP_WIDTH), lambda bi, t: (0, 0))
    st_spec = pl.BlockSpec((sb, HEADS, HEAD_DIM, HEAD_DIM), lambda bi, t: (bi, 0, 0, 0))
    return pl.pallas_call(
        functools.partial(_hgrn_kernel, t_blk=t_blk, l_valid=l_valid, n_blk=n_blk, sb=sb),
        out_shape=(jax.ShapeDtypeStruct((b, n_blk * t_blk, GROUP_WIDTH), MIX_DTYPE),
                   jax.ShapeDtypeStruct((b, HEADS, HEAD_DIM, HEAD_DIM), F32)),
        grid=(b // sb, n_blk),
        in_specs=[col(COL_CQ), col(COL_CF), col(COL_CI), col(COL_CG), st_spec, vec, vec],
        out_specs=(pl.BlockSpec((sb, t_blk, GROUP_WIDTH), lambda bi, t: (bi, t, 0)), st_spec),
        scratch_shapes=[pltpu.VMEM((sb, GROUP_WIDTH, GROUP_WIDTH), F32),
                        pltpu.VMEM((sb, t_blk, GROUP_WIDTH), F32),
                        pltpu.VMEM((sb, t_blk, GROUP_WIDTH), F32),
                        pltpu.VMEM((SUB * CHUNK, GROUP_WIDTH), BF16)],
        compiler_params=_cparams(("parallel", "arbitrary")),
        name="hgrn",
    )(proj3, proj3, proj3, proj3, st0, lb, nrm)


POOL_PAD = 24
POOL_FRONT = 8
D_GROUP = 64


def _pool_kernel(x_ref, buf_ref, w_ref, sc_ref, o_ref, buf_out_ref, xp_ref, s2_ref, s4_ref, *, t_blk):
    t = pl.program_id(1)
    hi = POOL_PAD + t_blk
    half = GROUP_WIDTH // 2

    @pl.when(t == 0)
    def _():
        zeros = jnp.zeros((POOL_PAD - D_BUF, GROUP_WIDTH), F32)
        xp_ref[0:POOL_PAD - D_BUF, :] = zeros
        s2_ref[0:POOL_FRONT, :] = zeros[0:POOL_FRONT]
        s4_ref[0:POOL_FRONT, :] = zeros[0:POOL_FRONT]
        xp_ref[POOL_PAD - D_BUF:POOL_PAD, :] = buf_ref[0]

    x = x_ref[0]
    xp_ref[POOL_PAD:hi, :] = x
    s2 = xp_ref[POOL_FRONT:hi, :] + xp_ref[POOL_FRONT - 1:hi - 1, :]
    s2_ref[POOL_FRONT:hi, :] = s2
    s4 = s2 + s2_ref[POOL_FRONT - 2:hi - 2, :]
    s4_ref[POOL_FRONT:hi, :] = s4
    s8 = s4[:, half:] + s4_ref[POOL_FRONT - 4:hi - 4, half:]
    n_ext = hi - POOL_FRONT
    s16 = s8[8:n_ext, :] + s8[0:n_ext - 8, :]
    lo_sum = s2[n_ext - t_blk:, :half]
    lane = lax.broadcasted_iota(jnp.int32, (t_blk, half), 1)
    pooled_lo = jnp.where(lane < D_GROUP, lo_sum * 0.5, s4[n_ext - t_blk:, :half] * 0.25)
    pooled_hi = jnp.where(lane < D_GROUP, s8[n_ext - t_blk:, :] * 0.125, s16[n_ext - 8 - t_blk:, :] * 0.0625)
    pooled = jnp.concatenate([pooled_lo, pooled_hi], axis=1) - x
    o_ref[0] = (_dot(pooled.astype(BF16), w_ref[...]) * sc_ref[...]).astype(MIX_DTYPE)
    tail = xp_ref[hi - D_BUF:hi, :]
    buf_out_ref[0] = tail
    xp_ref[POOL_PAD - D_BUF:POOL_PAD, :] = tail


def _pool(proj3, buf, w_bd, scale, l_valid, t_blk):
    b = proj3.shape[0]
    n_blk = l_valid // t_blk
    assert n_blk * t_blk == l_valid and t_blk >= D_BUF
    sb = _seq_batch(b, n_blk)
    buf_spec = pl.BlockSpec((sb, D_BUF, GROUP_WIDTH), lambda bi, t: (bi, 0, 0))
    return pl.pallas_call(
        _per_sequence(functools.partial(_pool_kernel, t_blk=t_blk), sb, (0, 0, None, None, 0, 0, None, None, None)),
        out_shape=(jax.ShapeDtypeStruct((b, l_valid, GROUP_WIDTH), MIX_DTYPE),
                   jax.ShapeDtypeStruct((b, D_BUF, GROUP_WIDTH), F32)),
        grid=(b // sb, n_blk),
        in_specs=[pl.BlockSpec((sb, t_blk, GROUP_WIDTH), lambda bi, t: (bi, t, COL_DX // GROUP_WIDTH)),
                  buf_spec,
                  pl.BlockSpec((GROUP_WIDTH, GROUP_WIDTH), lambda bi, t: (0, 0)),
                  pl.BlockSpec((1, GROUP_WIDTH), lambda bi, t: (0, 0))],
        out_specs=(pl.BlockSpec((sb, t_blk, GROUP_WIDTH), lambda bi, t: (bi, t, 0)), buf_spec),
        scratch_shapes=[pltpu.VMEM((POOL_PAD + t_blk, GROUP_WIDTH), F32)] * 3,
        compiler_params=_cparams(("parallel", "arbitrary")),
        name="pool",
    )(proj3, buf, w_bd, scale)


SSD_BLOCK = 2048
HGRN_BLOCK = 2048
POOL_BLOCK = 2048


def _prep_layer(layer, norm1, w_in, a_rel_bias, b_conv_w, b_conv_b, b_dt_bias, b_a_log, b_d, b_norm, lbs, c_norm,
                d_pool_w, d_pool_scale, w_out, norm2, w_gate_up, w_down):
    w_r = _w_in_prep(jnp.transpose(w_in, (0, 2, 1)), layer)
    tab = a_rel_bias[layer].T
    m = A_BAND + CHUNK
    far = tab[:, 2 * REL_CLIP:]
    n_far = A_BAND_PREV - REL_CLIP + 1
    r = jnp.concatenate([jnp.broadcast_to(far, (HEADS, n_far)),
                         tab[:, 2 * REL_CLIP - 1:CHUNK:-1],
                         jnp.broadcast_to(far, (HEADS, m - n_far - (2 * REL_CLIP - 1 - CHUNK)))], axis=1)
    bias = jnp.tile(r, (1, CHUNK))[:, :CHUNK * (m - 1)].reshape(HEADS, CHUNK, m - 1)[:, :, :A_BAND]
    bias = bias.reshape(HEADS * CHUNK, A_BAND)
    rep = lambda p: jnp.repeat(p[layer], HEAD_DIM)[None, :]
    pw = d_pool_w[layer]
    w_bd = jnp.zeros((GROUP_WIDTH, GROUP_WIDTH), F32)
    for g in range(4):
        w_bd = w_bd.at[g * D_GROUP:(g + 1) * D_GROUP, g * D_GROUP:(g + 1) * D_GROUP].set(pw[g])
    return dict(
        n1=norm1[layer][None, :], w_in=w_r, bias=bias,
        cw=b_conv_w[layer], cb=b_conv_b[layer][None, :], dtb=rep(b_dt_bias), alog=rep(b_a_log), dvec=rep(b_d),
        bnrm=b_norm[layer][None, :], lb=lbs[layer][None, :], cnrm=c_norm[layer].reshape(1, GROUP_WIDTH),
        w_bd=w_bd.astype(BF16), psc=d_pool_scale[layer][None, :], n2=norm2[layer][None, :], layer=layer,
        w_out=w_out.astype(BF16), w_gu=w_gate_up.astype(BF16), w_dn=w_down.astype(BF16))


def _round_up(n, m):
    return -(-n // m) * m


def _mixers(proj, st, p, cache_t):
    k_prev, v_prev, conv0, ssm0, hgrn0, pool0 = st
    b, l, _ = proj.shape
    p_hist = k_prev.shape[1]
    lq = _round_up(l, CHUNK)
    pad_rows = lambda a: a if lq == l else jnp.pad(a, ((0, 0), (0, lq - l), (0, 0)))

    keep = min(A_BAND_PREV, p_hist + l)
    n_new = min(l, keep)
    ka = proj[:, l - n_new:, COL_AK:COL_AK + GROUP_WIDTH].reshape(b, n_new, HEADS, HEAD_DIM)
    va = proj[:, l - n_new:, COL_AV:COL_AV + GROUP_WIDTH].reshape(b, n_new, HEADS, HEAD_DIM)
    if cache_t is not None:
        bias_s = p["bias"].reshape(HEADS, CHUNK, A_BAND)[:, :l].reshape(HEADS * l, A_BAND)
        oa, kt_new, vt_new = _attention_cached(proj, cache_t[0], cache_t[1], p["layer"], bias_s, l)
        untranspose = lambda a: a.reshape(b, HEADS, HEAD_DIM, A_BAND_PREV).transpose(0, 3, 1, 2)
        new_k, new_v = untranspose(kt_new), untranspose(vt_new)
    else:
        if p_hist == 0:
            k_cache = v_cache = None
        else:
            k_cache = k_prev.astype(BF16).reshape(b, p_hist, GROUP_WIDTH)
            v_cache = v_prev.astype(BF16).reshape(b, p_hist, GROUP_WIDTH)
        oa = _attention(pad_rows(proj), k_cache, v_cache, p["bias"], l)[:, :l]
        new_k = jnp.concatenate([k_prev[:, p_hist - (keep - n_new):], ka], axis=1)
        new_v = jnp.concatenate([v_prev[:, p_hist - (keep - n_new):], va], axis=1)

    ob, new_conv, new_ssm = _ssd(proj, conv0, ssm0, p["cw"], p["cb"], p["dtb"], p["alog"], p["dvec"], p["bnrm"],
                                 l, SSD_BLOCK if l % SSD_BLOCK == 0 else SSD_CHUNK)

    if l % HGRN_BLOCK == 0:
        oc, new_hgrn = _hgrn(proj, COL_CQ, hgrn0, p["lb"], p["cnrm"], l, HGRN_BLOCK)
    else:
        oc, new_hgrn = _hgrn(pad_rows(proj[:, :, COL_CQ:COL_CQ + 4 * GROUP_WIDTH]), 0, hgrn0, p["lb"], p["cnrm"], l, lq)
        oc = oc[:, :l]

    od, new_pool = _pool(proj, pool0, p["w_bd"], p["psc"], l, POOL_BLOCK if l % POOL_BLOCK == 0 else l)

    flat = lambda a: a.reshape(b * l, GROUP_WIDTH)
    return (flat(oa), flat(ob), flat(oc), flat(od)), (new_k, new_v, new_conv, new_ssm, new_hgrn, new_pool)


def _trunk_groups(groups, params, nf):
    assert len(groups) <= 2
    depth = len(params)
    xs = [x.reshape(-1, D_MODEL) for x, _ in groups]
    shapes = [x.shape for x, _ in groups]
    new = [[[] for _ in range(6)] for _ in groups]
    cache_ts = []
    for x, states in groups:
        b, l, _ = x.shape
        cache_t = None
        if states[0].shape[2] == A_BAND_PREV and l <= CHUNK and l % 8 == 0:
            as_t = lambda c: jnp.transpose(c, (0, 1, 3, 4, 2)).reshape(depth, b, GROUP_WIDTH, A_BAND_PREV)
            cache_t = (as_t(states[0]), as_t(states[1]))
        cache_ts.append(cache_t)
    for layer in range(depth):
        p = params[layer]
        projs = _inproj(xs[0], p["n1"], p["w_in"], *xs[1:])
        projs = projs if len(groups) > 1 else (projs,)
        mixes = []
        for g, (_, states) in enumerate(groups):
            b, l, _ = shapes[g]
            st_l = tuple(s[layer] for s in states)
            mix, ns = _mixers(projs[g].reshape(b, l, PROJ_W), st_l, p, cache_ts[g])
            mixes.append(mix)
            for i in range(6):
                new[g][i].append(ns[i])
        extra = (xs[1], mixes[1]) if len(groups) > 1 else None
        ys = _mlp(xs[0], mixes[0], p["layer"], p["w_out"], p["n2"], p["w_gu"], p["w_dn"], nf, layer == depth - 1, extra)
        xs = list(ys) if len(groups) > 1 else [ys]
    return [(xs[g].reshape(shapes[g]), tuple(jnp.stack(n, axis=0) for n in new[g])) for g in range(len(groups))]


def _trunk(x, states, params, nf):
    return _trunk_groups([(x, states)], params, nf)[0]


def kernel(x_prompt, x_sample, cache_a_k, cache_a_v, state_b_conv, state_b_ssm, state_c_hgrn, state_d_pool, norm1,
           w_in, a_rel_bias, b_conv_w, b_conv_b, b_dt_bias, b_a_log, b_d, b_norm, c_lb_logits, c_norm, d_pool_w,
           d_pool_scale, w_out, norm2, w_gate_up, w_down, norm_f):
    depth = w_in.shape[0]
    lbs = jnp.cumsum(jax.nn.softmax(c_lb_logits.astype(F32), axis=0), axis=0)
    lbs = lbs - lbs[:1]
    params = [_prep_layer(layer, norm1, w_in, a_rel_bias, b_conv_w, b_conv_b, b_dt_bias, b_a_log, b_d, b_norm, lbs,
                          c_norm, d_pool_w, d_pool_scale, w_out, norm2, w_gate_up, w_down)
              for layer in range(depth)]
    nf = norm_f[None, :]
    bp = x_prompt.shape[0]
    prompt_states = (
        jnp.zeros((depth, bp, 0, HEADS, HEAD_DIM), F32),
        jnp.zeros((depth, bp, 0, HEADS, HEAD_DIM), F32),
        jnp.zeros((depth, bp, B_CONV - 1, B_CONV_DIM), F32),
        jnp.zeros((depth, bp, HEADS, HEAD_DIM, B_STATE), F32),
        jnp.zeros((depth, bp, HEADS, HEAD_DIM, HEAD_DIM), F32),
        jnp.zeros((depth, bp, D_BUF, GROUP_WIDTH), F32),
    )
    sample_states = (cache_a_k, cache_a_v, state_b_conv, state_b_ssm, state_c_hgrn, state_d_pool)
    (y_prompt, ps), (y_sample, ss) = _trunk_groups([(x_prompt, prompt_states), (x_sample, sample_states)], params, nf)
    return (y_prompt, y_sample) + ps + ss
```

```python
import functools

import jax
import jax.numpy as jnp
from jax import lax
from jax.experimental import pallas as pl
from jax.experimental.pallas import tpu as pltpu

F32 = jnp.float32
BF16 = jnp.bfloat16
MIX_DTYPE = BF16

D_MODEL = 1024
GROUP_WIDTH = 256
HEADS = 4
HEAD_DIM = 64
CHUNK = 64
A_BAND_PREV = 512
A_BAND = A_BAND_PREV + CHUNK
REL_CLIP = 128
B_STATE = 128
B_CONV = 4
B_CONV_DIM = 768
D_BUF = 15
D_FF = 2816
EPS = 1e-6
NEG = -1e30
LOG2E = 1.4426950408889634
SUB = 32

COL_XBC = 0
COL_AQ = 768
COL_AK = 1024
COL_AV = 1280
COL_BZ = 1536
COL_CQ = 1792
COL_CF = 2048
COL_CI = 2304
COL_CG = 2560
COL_DX = 2816
COL_DT = 3072
PROJ_W = 3328

VMEM_LIMIT = 56 * 1024 * 1024


def _cparams(sem):
    return pltpu.CompilerParams(dimension_semantics=sem, vmem_limit_bytes=VMEM_LIMIT)


def _rms(x, w):
    return x * lax.rsqrt(jnp.mean(x * x, axis=-1, keepdims=True) + EPS) * w


def _silu(x):
    h = 0.5 * x
    return h + h * jnp.tanh(h)


def _dot(a, b):
    return jnp.dot(a, b, preferred_element_type=F32)


def _dot_nt(a, b):
    return lax.dot_general(a, b, (((1,), (1,)), ((), ())), preferred_element_type=F32)


def _dot_tn(a, b):
    return lax.dot_general(a, b, (((0,), (0,)), ((), ())), preferred_element_type=F32)


def _split3(x):
    hi = x.astype(BF16)
    r = x - hi.astype(F32)
    mid = r.astype(BF16)
    lo = (r - mid.astype(F32)).astype(BF16)
    return hi, mid, lo


def _dot01_lhs(m01, x):
    hi, mid, lo = _split3(x)
    return _dot(m01, hi) + _dot(m01, mid) + _dot(m01, lo)


SEQ_BATCH = 8


def _seq_batch(b, n_blk):
    return SEQ_BATCH if n_blk == 1 and b % SEQ_BATCH == 0 else 1


def _per_sequence(kernel_fn, sb, seq_dim):
    if sb == 1:
        return kernel_fn

    def run(*refs):
        for s in range(sb):
            kernel_fn(*[r if d is None else r.at[(slice(None),) * d + (pl.ds(s, 1),)]
                        for r, d in zip(refs, seq_dim)])

    return run


def _tril(t):
    r = lax.broadcasted_iota(jnp.int32, (t, t), 0)
    c = lax.broadcasted_iota(jnp.int32, (t, t), 1)
    return r >= c


def _head_ones():
    r = lax.broadcasted_iota(jnp.int32, (GROUP_WIDTH, GROUP_WIDTH), 0) // HEAD_DIM
    c = lax.broadcasted_iota(jnp.int32, (GROUP_WIDTH, GROUP_WIDTH), 1) // HEAD_DIM
    return r == c


IN_WIDTH = 3076
ORIG_XBC = 1024
ORIG_DT = 1792
W_PREP_COLS = 256


def _w_in_prep_kernel(wt_ref, o_ref):
    def put(c0, rows):
        o_ref[:, c0:c0 + W_PREP_COLS] = rows.T.astype(BF16)

    for j in range(3):
        put(COL_XBC + W_PREP_COLS * j, wt_ref[0, ORIG_XBC + W_PREP_COLS * j:ORIG_XBC + W_PREP_COLS * (j + 1), :])
    for j in range(4):
        put(COL_AQ + W_PREP_COLS * j, wt_ref[0, W_PREP_COLS * j:W_PREP_COLS * (j + 1), :])
    for j in range(5):
        r0 = ORIG_DT + HEADS + W_PREP_COLS * j
        put(COL_CQ + W_PREP_COLS * j, wt_ref[0, r0:r0 + W_PREP_COLS, :])
    put(COL_DT, jnp.concatenate([jnp.broadcast_to(wt_ref[0, ORIG_DT + h:ORIG_DT + h + 1, :], (HEAD_DIM, D_MODEL))
                                 for h in range(HEADS)], axis=0))


def _w_in_prep(wt_all, layer):
    return pl.pallas_call(
        _w_in_prep_kernel,
        out_shape=jax.ShapeDtypeStruct((D_MODEL, PROJ_W), BF16),
        grid=(1,),
        in_specs=[pl.BlockSpec((1, IN_WIDTH, D_MODEL), lambda i: (layer, 0, 0), pipeline_mode=pl.Buffered(1))],
        out_specs=pl.BlockSpec((D_MODEL, PROJ_W), lambda i: (0, 0)),
        compiler_params=_cparams(("arbitrary",)),
        name="w_in_prep",
    )(wt_all)


def _is_last_step():
    return pl.program_id(0) == pl.num_programs(0) - 1


def _inproj_tile(x_ref, n_ref, w_ref, o_ref):
    hb = _rms(x_ref[...], n_ref[...]).astype(BF16)
    for c0 in range(0, PROJ_W, W_PREP_COLS):
        o_ref[:, c0:c0 + W_PREP_COLS] = _dot(hb, w_ref[:, c0:c0 + W_PREP_COLS])


def _inproj_kernel(x_ref, n_ref, w_ref, *rest):
    if len(rest) == 1:
        _inproj_tile(x_ref, n_ref, w_ref, rest[0])
    else:
        xe_ref, o_ref, oe_ref = rest
        _inproj_tile(x_ref, n_ref, w_ref, o_ref)
        pl.when(_is_last_step())(lambda: _inproj_tile(xe_ref, n_ref, w_ref, oe_ref))


def _inproj(x2d, n1, w_in, extra=None):
    rows = x2d.shape[0]
    tm = next((m for m in (1024, 512) if rows % m == 0), rows)
    whole = lambda a: pl.BlockSpec(a.shape, lambda i: (0, 0))
    in_specs = [pl.BlockSpec((tm, D_MODEL), lambda i: (i, 0)),
                pl.BlockSpec((1, D_MODEL), lambda i: (0, 0)),
                pl.BlockSpec((D_MODEL, PROJ_W), lambda i: (0, 0), pipeline_mode=pl.Buffered(1))]
    out_shape = jax.ShapeDtypeStruct((rows, PROJ_W), F32)
    out_specs = pl.BlockSpec((tm, PROJ_W), lambda i: (i, 0))
    operands = (x2d, n1, w_in)
    if extra is not None:
        out_e = jax.ShapeDtypeStruct((extra.shape[0], PROJ_W), F32)
        in_specs, operands = in_specs + [whole(extra)], operands + (extra,)
        out_shape, out_specs = (out_shape, out_e), (out_specs, whole(out_e))
    return pl.pallas_call(
        _inproj_kernel,
        out_shape=out_shape,
        grid=(rows // tm,),
        in_specs=in_specs,
        out_specs=out_specs,
        compiler_params=_cparams(("arbitrary",)),
        name="inproj",
    )(*operands)


CONV_PAD = 8


def _causal_conv_silu(xpad_ref, rows, conv_w, conv_b):
    hist = B_CONV - 1
    y = conv_b
    for i in range(B_CONV):
        y = y + xpad_ref[CONV_PAD - hist + i:CONV_PAD - hist + i + rows, :] * conv_w[i:i + 1, :]
    return _silu(y)


def _softplus(x):
    return jnp.maximum(x, 0.0) + jnp.log1p(jnp.exp(-jnp.abs(x)))


FF_CHUNK = 256


def _mlp_tile(x_ref, mix_refs, wo_ref, n2_ref, wgu_ref, wdn_ref, nf_ref, o_ref, mix_ref, act_ref, final):
    for m, r in enumerate(mix_refs):
        mix_ref[:, m * GROUP_WIDTH:(m + 1) * GROUP_WIDTH] = r[...]
    x1 = x_ref[...] + _dot(mix_ref[...], wo_ref[0])
    hb = _rms(x1, n2_ref[...]).astype(BF16)
    for c0 in range(0, D_FF, FF_CHUNK):
        gate = _dot(hb, wgu_ref[0, :, c0:c0 + FF_CHUNK])
        up = _dot(hb, wgu_ref[0, :, D_FF + c0:D_FF + c0 + FF_CHUNK])
        act_ref[:, c0:c0 + FF_CHUNK] = (_silu(gate) * up).astype(BF16)
    out = x1 + _dot(act_ref[...], wdn_ref[0])
    if final:
        out = _rms(out, nf_ref[...])
    o_ref[...] = out


def _mlp_kernel(*refs, final, has_extra):
    n_in = 10
    x_ref, oa_ref, ob_ref, oc_ref, od_ref, wo_ref, n2_ref, wgu_ref, wdn_ref, nf_ref = refs[:n_in]
    weights = (wo_ref, n2_ref, wgu_ref, wdn_ref, nf_ref)
    if not has_extra:
        o_ref, mix_ref, act_ref = refs[n_in:]
        _mlp_tile(x_ref, (oa_ref, ob_ref, oc_ref, od_ref), *weights, o_ref, mix_ref, act_ref, final)
    else:
        xe_ref, ea_ref, eb_ref, ec_ref, ed_ref, o_ref, oe_ref, mix_ref, act_ref, mixe_ref, acte_ref = refs[n_in:]
        _mlp_tile(x_ref, (oa_ref, ob_ref, oc_ref, od_ref), *weights, o_ref, mix_ref, act_ref, final)
        pl.when(_is_last_step())(
            lambda: _mlp_tile(xe_ref, (ea_ref, eb_ref, ec_ref, ed_ref), *weights, oe_ref, mixe_ref, acte_ref, final))


def _mlp(x2d, mix, layer, w_out, n2, w_gu, w_dn, nf, final, extra=None):
    rows = x2d.shape[0]
    tm = 512 if rows % 512 == 0 else rows
    row_spec = lambda w: pl.BlockSpec((tm, w), lambda i: (i, 0))
    whole = lambda a: pl.BlockSpec(a.shape, lambda i: (0, 0))
    wspec = lambda a: pl.BlockSpec((1,) + a.shape[1:], lambda i: (layer, 0, 0), pipeline_mode=pl.Buffered(1))
    in_specs = ([row_spec(D_MODEL)] + [row_spec(GROUP_WIDTH)] * 4
                + [wspec(w_out), whole(n2), wspec(w_gu), wspec(w_dn), whole(nf)])
    operands = (x2d, *mix, w_out, n2, w_gu, w_dn, nf)
    out_shape = jax.ShapeDtypeStruct((rows, D_MODEL), F32)
    out_specs = row_spec(D_MODEL)
    scratch = [pltpu.VMEM((tm, D_MODEL), BF16), pltpu.VMEM((tm, D_FF), BF16)]
    if extra is not None:
        x_e, mix_e = extra
        out_e = jax.ShapeDtypeStruct(x_e.shape, F32)
        in_specs = in_specs + [whole(x_e)] + [whole(m) for m in mix_e]
        operands = operands + (x_e, *mix_e)
        out_shape, out_specs = (out_shape, out_e), (out_specs, whole(out_e))
        scratch = scratch + [pltpu.VMEM((x_e.shape[0], D_MODEL), BF16), pltpu.VMEM((x_e.shape[0], D_FF), BF16)]
    return pl.pallas_call(
        functools.partial(_mlp_kernel, final=final, has_extra=extra is not None),
        out_shape=out_shape,
        grid=(rows // tm,),
        in_specs=in_specs,
        out_specs=out_specs,
        scratch_shapes=scratch,
        compiler_params=_cparams(("arbitrary",)),
        name="mlp",
    )(*operands)


ATTN_BLOCK = 1024
ATTN_UNROLL = 8


def _attn_kernel(q_ref, kh_ref, kc_ref, vh_ref, vc_ref, bias_ref, o_ref, kb_ref, vb_ref, *, qb, p_hist, l_valid):
    i = pl.program_id(1)
    if p_hist == 0:
        @pl.when(i == 0)
        def _():
            kb_ref[0:A_BAND_PREV, :] = jnp.zeros((A_BAND_PREV, GROUP_WIDTH), BF16)
            vb_ref[0:A_BAND_PREV, :] = jnp.zeros((A_BAND_PREV, GROUP_WIDTH), BF16)

        @pl.when(i > 0)
        def _():
            kb_ref[0:A_BAND_PREV, :] = kh_ref[0].astype(BF16)
            vb_ref[0:A_BAND_PREV, :] = vh_ref[0].astype(BF16)
    else:
        kb_ref[0:A_BAND_PREV, :] = kh_ref[0].astype(BF16)
        vb_ref[0:A_BAND_PREV, :] = vh_ref[0].astype(BF16)
    kb_ref[A_BAND_PREV:A_BAND_PREV + qb, :] = kc_ref[0].astype(BF16)
    vb_ref[A_BAND_PREV:A_BAND_PREV + qb, :] = vc_ref[0].astype(BF16)
    lane_head = lax.broadcasted_iota(jnp.int32, (CHUNK, GROUP_WIDTH), 1) // HEAD_DIM
    band_pos = lax.broadcasted_iota(jnp.int32, (1, A_BAND), 1)

    n_chunks = qb // CHUNK
    group = ATTN_UNROLL if n_chunks % ATTN_UNROLL == 0 else 1

    def chunks(masked, gi, carry):
        r0s = [pl.multiple_of((gi * group + u) * CHUNK, CHUNK) for u in range(group)]
        es, dens = [], []
        for r0 in r0s:
            qc = q_ref[0, pl.ds(r0, CHUNK), :] * (HEAD_DIM ** -0.5)
            qs = jnp.concatenate([jnp.where(lane_head == h, qc, 0.0) for h in range(HEADS)], axis=0).astype(BF16)
            s = _dot_nt(qs, kb_ref[pl.ds(r0, A_BAND), :]) + bias_ref[...]
            if masked:
                kpos = i * qb + r0 - A_BAND_PREV + band_pos
                s = jnp.where((kpos >= -p_hist) & (kpos < l_valid), s, NEG)
            e = jnp.exp(s - jnp.max(s, axis=-1, keepdims=True))
            dens.append(jnp.sum(e, axis=-1, keepdims=True))
            es.append(e.astype(BF16))
        o_alls = [_dot(es[u], vb_ref[pl.ds(r0s[u], A_BAND), :]) * (1.0 / dens[u]) for u in range(group)]
        for u in range(group):
            o = jnp.zeros((CHUNK, GROUP_WIDTH), F32)
            for h in range(HEADS):
                o = o + jnp.where(lane_head == h, o_alls[u][h * CHUNK:(h + 1) * CHUNK, :], 0.0)
            o_ref[0, pl.ds(r0s[u], CHUNK), :] = o.astype(MIX_DTYPE)
        return carry

    n_blocks = pl.num_programs(1)
    first_has_invalid = p_hist < A_BAND_PREV
    last_has_invalid = l_valid % qb != 0
    if not (first_has_invalid or last_has_invalid):
        lax.fori_loop(0, n_chunks // group, functools.partial(chunks, False), 0)
    else:
        edge = False
        if first_has_invalid:
            edge = edge | (i == 0)
        if last_has_invalid:
            edge = edge | (i == n_blocks - 1)
        @pl.when(edge)
        def _():
            lax.fori_loop(0, n_chunks // group, functools.partial(chunks, True), 0)

        @pl.when(jnp.logical_not(edge))
        def _():
            lax.fori_loop(0, n_chunks // group, functools.partial(chunks, False), 0)


def _attention(proj3, k_cache, v_cache, bias, l_valid):
    b = proj3.shape[0]
    lq = _round_up(l_valid, CHUNK)
    qb = next((q for q in (ATTN_BLOCK, A_BAND_PREV) if lq % q == 0), lq)
    nblk = lq // qb
    col = lambda c: pl.BlockSpec((1, qb, GROUP_WIDTH), lambda bi, i: (bi, i, c // GROUP_WIDTH))
    if k_cache is None:
        assert qb % A_BAND_PREV == 0
        p_hist = 0
        per = qb // A_BAND_PREV
        prev = lambda c: pl.BlockSpec((1, A_BAND_PREV, GROUP_WIDTH),
                                      lambda bi, i: (bi, jnp.maximum(i * per - 1, 0), c // GROUP_WIDTH))
        kh, vh, kh_spec, vh_spec = proj3, proj3, prev(COL_AK), prev(COL_AV)
    else:
        assert nblk == 1 and k_cache.shape[1] == A_BAND_PREV
        p_hist = A_BAND_PREV
        kh_spec = vh_spec = pl.BlockSpec((1, A_BAND_PREV, GROUP_WIDTH), lambda bi, i: (bi, 0, 0))
        kh, vh = k_cache, v_cache
    return pl.pallas_call(
        functools.partial(_attn_kernel, qb=qb, p_hist=p_hist, l_valid=l_valid),
        out_shape=jax.ShapeDtypeStruct((b, lq, GROUP_WIDTH), MIX_DTYPE),
        grid=(b, nblk),
        in_specs=[col(COL_AQ), kh_spec, col(COL_AK), vh_spec, col(COL_AV),
                  pl.BlockSpec((HEADS * CHUNK, A_BAND), lambda bi, i: (0, 0))],
        out_specs=pl.BlockSpec((1, qb, GROUP_WIDTH), lambda bi, i: (bi, i, 0)),
        scratch_shapes=[pltpu.VMEM((A_BAND_PREV + qb, GROUP_WIDTH), BF16),
                        pltpu.VMEM((A_BAND_PREV + qb, GROUP_WIDTH), BF16)],
        compiler_params=_cparams(("parallel", "arbitrary")),
        name="attention",
    )(proj3, kh, proj3, vh, proj3, bias)


def _attn_cached_kernel(q_ref, kn_ref, vn_ref, kt_ref, vt_ref, bias_ref, o_ref, kt_out_ref, vt_out_ref, *, l_new):
    lane_head = lax.broadcasted_iota(jnp.int32, (l_new, GROUP_WIDTH), 1) // HEAD_DIM
    q = q_ref[0] * (HEAD_DIM ** -0.5)
    qs = jnp.concatenate([jnp.where(lane_head == h, q, 0.0) for h in range(HEADS)], axis=0).astype(BF16)
    kn = kn_ref[0]
    vn = vn_ref[0]
    kt = kt_ref[0, 0]
    vt = vt_ref[0, 0]
    s_old = _dot(qs, kt.astype(BF16)) + bias_ref[:, 0:A_BAND_PREV]
    s_new = _dot_nt(qs, kn.astype(BF16)) + bias_ref[:, A_BAND_PREV:A_BAND_PREV + l_new]
    m = jnp.maximum(jnp.max(s_old, axis=-1, keepdims=True), jnp.max(s_new, axis=-1, keepdims=True))
    e_old = jnp.exp(s_old - m)
    e_new = jnp.exp(s_new - m)
    den = jnp.sum(e_old, axis=-1, keepdims=True) + jnp.sum(e_new, axis=-1, keepdims=True)
    o_all = (_dot_nt(e_old.astype(BF16), vt.astype(BF16)) + _dot(e_new.astype(BF16), vn.astype(BF16))) * (1.0 / den)
    o = jnp.zeros((l_new, GROUP_WIDTH), F32)
    for h in range(HEADS):
        o = o + jnp.where(lane_head == h, o_all[h * l_new:(h + 1) * l_new, :], 0.0)
    o_ref[0] = o.astype(MIX_DTYPE)

    keep = A_BAND_PREV - l_new
    lane = lax.broadcasted_iota(jnp.int32, (GROUP_WIDTH, A_BAND_PREV), 1)
    r = lax.broadcasted_iota(jnp.int32, (l_new, A_BAND_PREV), 0)
    c = lax.broadcasted_iota(jnp.int32, (l_new, A_BAND_PREV), 1)
    place = (c == r + keep).astype(BF16)

    def appended(old_t, new_rows):
        hi, mid, lo = _split3(new_rows)
        new_cols = _dot_tn(hi, place) + _dot_tn(mid, place) + _dot_tn(lo, place)
        return jnp.where(lane < keep, pltpu.roll(old_t, keep, 1), new_cols)

    kt_out_ref[0] = appended(kt, kn)
    vt_out_ref[0] = appended(vt, vn)


def _attention_cached(proj3, kt_cache, vt_cache, layer, bias, l_valid):
    b = proj3.shape[0]
    sb = _seq_batch(b, 1)
    col = lambda c: pl.BlockSpec((sb, l_valid, GROUP_WIDTH), lambda bi: (bi, 0, c // GROUP_WIDTH))
    cache_spec = pl.BlockSpec((1, sb, GROUP_WIDTH, A_BAND_PREV), lambda bi: (layer, bi, 0, 0))
    out_t = pl.BlockSpec((sb, GROUP_WIDTH, A_BAND_PREV), lambda bi: (bi, 0, 0))
    return pl.pallas_call(
        _per_sequence(functools.partial(_attn_cached_kernel, l_new=l_valid), sb, (0, 0, 0, 1, 1, None, 0, 0, 0)),
        out_shape=(jax.ShapeDtypeStruct((b, l_valid, GROUP_WIDTH), MIX_DTYPE),
                   jax.ShapeDtypeStruct((b, GROUP_WIDTH, A_BAND_PREV), F32),
                   jax.ShapeDtypeStruct((b, GROUP_WIDTH, A_BAND_PREV), F32)),
        grid=(b // sb,),
        in_specs=[col(COL_AQ), col(COL_AK), col(COL_AV), cache_spec, cache_spec,
                  pl.BlockSpec((HEADS * l_valid, A_BAND), lambda bi: (0, 0))],
        out_specs=(pl.BlockSpec((sb, l_valid, GROUP_WIDTH), lambda bi: (bi, 0, 0)), out_t, out_t),
        compiler_params=_cparams(("parallel",)),
        name="attention_cached",
    )(proj3, proj3, proj3, kt_cache, vt_cache, bias)


SSD_CHUNK = 128
SSD_INFLIGHT = 4


def _ssd_kernel(xbc_ref, z_ref, dt_ref, conv0_ref, ssm0_ref, cw_ref, cb_ref, dtb_ref, alog_ref, d_ref, nrm_ref,
                o_ref, conv_out_ref, ssm_out_ref, xpad_ref, st_ref, *, t_blk, l_valid, n_blk):
    t = pl.program_id(1)
    hist = B_CONV - 1

    @pl.when(t == 0)
    def _():
        xpad_ref[CONV_PAD - hist:CONV_PAD, :] = conv0_ref[0]
        st_ref[...] = ssm0_ref[0].reshape(GROUP_WIDTH, B_STATE).T

    n_last = l_valid - (n_blk - 1) * t_blk
    conv_out_ref[0] = xbc_ref[0, n_last - hist:n_last, :]

    tc = SSD_CHUNK
    short = l_valid < t_blk

    def rows_of(ref, sl):
        if short:
            return jnp.concatenate([ref[0], jnp.zeros((tc - l_valid, ref.shape[2]), F32)], axis=0)
        return ref[0, sl, :]

    tril = _tril(tc)
    tril_b = tril.astype(BF16)
    lane = lax.broadcasted_iota(jnp.int32, (tc, B_STATE), 1)
    rows = lax.broadcasted_iota(jnp.int32, (tc, GROUP_WIDTH), 0)
    a_neg = -jnp.exp(alog_ref[...])
    conv_b = cb_ref[...]
    conv_w = cw_ref[...]
    dt_bias = dtb_ref[...]
    d_skip = d_ref[...]
    nrm = nrm_ref[...]

    n_chunks = t_blk // tc
    group = SSD_INFLIGHT if n_chunks % SSD_INFLIGHT == 0 else 1
    groups = (slice(0, B_STATE), slice(B_STATE, 2 * B_STATE))

    def chunks(gi, carry):
        rg = group * tc
        base = pl.multiple_of(gi * rg, rg)
        sl_all = pl.ds(base, rg)
        xpad_ref[CONV_PAD:CONV_PAD + rg, :] = rows_of(xbc_ref, sl_all)
        xbc_all = _causal_conv_silu(xpad_ref, rg, conv_w, conv_b)
        xpad_ref[CONV_PAD - hist:CONV_PAD, :] = xpad_ref[CONV_PAD + rg - hist:CONV_PAD + rg, :]
        dt_all = _softplus(rows_of(dt_ref, sl_all) + dt_bias)
        if l_valid < n_blk * t_blk:
            rows_all = lax.broadcasted_iota(jnp.int32, (rg, GROUP_WIDTH), 0)
            dt_all = jnp.where(t * t_blk + base + rows_all < l_valid, dt_all, 0.0)
        z_all = rows_of(z_ref, sl_all)
        xs_all = xbc_all[:, 0:GROUP_WIDTH]
        a_all = dt_all * a_neg
        u_all = xs_all * dt_all

        cut = lambda arr, s: arr[s * tc:(s + 1) * tc, :]
        acss = [_dot01_lhs(tril_b, cut(a_all, s)) for s in range(group)]
        cms = [cut(xbc_all[:, 2 * GROUP_WIDTH:3 * GROUP_WIDTH], s).astype(BF16) for s in range(group)]
        bms = [cut(xbc_all[:, GROUP_WIDTH:2 * GROUP_WIDTH], s).astype(BF16) for s in range(group)]
        cbs = [[_dot_nt(cms[s][:, gs], bms[s][:, gs]) for gs in groups] for s in range(group)]

        y_states = []
        for s in range(group):
            acs = acss[s]
            last = acs[tc - 1:tc, :]
            upb = (cut(u_all, s) * jnp.exp(last - acs)).astype(BF16)
            eacs = jnp.exp(acs)
            elast = jnp.exp(last)
            ys_s = []
            for gs in groups:
                ys_s.append(_dot(cms[s][:, gs], st_ref[:, gs].astype(BF16)) * eacs[:, gs])
                st_ref[:, gs] = st_ref[:, gs] * elast[:, gs] + _dot_tn(bms[s][:, gs], upb[:, gs])
            y_states.append(ys_s)

        for s in range(group):
            acs = acss[s]
            acs_t = acs.T
            ub = cut(u_all, s).astype(BF16)
            ys = []
            for g, gs in enumerate(groups):
                parts = []
                for hh in range(2):
                    c0 = (2 * g + hh) * HEAD_DIM
                    col = jnp.broadcast_to(acs[:, c0:c0 + 1], (tc, tc))
                    rowv = jnp.broadcast_to(acs_t[c0:c0 + 1, :], (tc, tc))
                    dec = jnp.exp(jnp.where(tril, col - rowv, NEG))
                    parts.append(_dot((cbs[s][g] * dec).astype(BF16), ub[:, gs]))
                ys.append(jnp.where(lane < HEAD_DIM, parts[0], parts[1]) + y_states[s][g])
            yv = jnp.concatenate(ys, axis=1) + d_skip * cut(xs_all, s)
            res = _rms(yv * _silu(cut(z_all, s)), nrm).astype(MIX_DTYPE)
            if short:
                o_ref[0] = res[0:l_valid, :]
            else:
                o_ref[0, pl.ds(pl.multiple_of(base + s * tc, tc), tc), :] = res
        return carry

    lax.fori_loop(0, n_chunks // group, chunks, 0)

    @pl.when(t == n_blk - 1)
    def _():
        ssm_out_ref[0] = st_ref[...].T.reshape(HEADS, HEAD_DIM, B_STATE)


def _ssd(proj3, conv0, ssm0, cw, cb, dtb, alog, dvec, nrm, l_valid, t_blk):
    b, lrows, _ = proj3.shape
    n_blk = -(-l_valid // t_blk)
    assert t_blk % SSD_CHUNK == 0 and lrows == l_valid
    assert l_valid == n_blk * t_blk or (t_blk == SSD_CHUNK and B_CONV - 1 <= l_valid < t_blk and l_valid % 8 == 0)
    t_in = min(t_blk, l_valid)
    sb = _seq_batch(b, n_blk)
    vec = lambda w: pl.BlockSpec((1, w), lambda bi, t: (0, 0))
    conv_spec = pl.BlockSpec((sb, B_CONV - 1, B_CONV_DIM), lambda bi, t: (bi, 0, 0))
    ssm_spec = pl.BlockSpec((sb, HEADS, HEAD_DIM, B_STATE), lambda bi, t: (bi, 0, 0, 0))
    return pl.pallas_call(
        _per_sequence(functools.partial(_ssd_kernel, t_blk=t_blk, l_valid=l_valid, n_blk=n_blk), sb,
                      (0,) * 5 + (None,) * 6 + (0,) * 3 + (None,) * 2),
        out_shape=(jax.ShapeDtypeStruct((b, l_valid, GROUP_WIDTH), MIX_DTYPE),
                   jax.ShapeDtypeStruct((b, B_CONV - 1, B_CONV_DIM), F32),
                   jax.ShapeDtypeStruct((b, HEADS, HEAD_DIM, B_STATE), F32)),
        grid=(b // sb, n_blk),
        in_specs=[pl.BlockSpec((sb, t_in, B_CONV_DIM), lambda bi, t: (bi, t, COL_XBC // B_CONV_DIM)),
                  pl.BlockSpec((sb, t_in, GROUP_WIDTH), lambda bi, t: (bi, t, COL_BZ // GROUP_WIDTH)),
                  pl.BlockSpec((sb, t_in, GROUP_WIDTH), lambda bi, t: (bi, t, COL_DT // GROUP_WIDTH)),
                  conv_spec, ssm_spec,
                  pl.BlockSpec((B_CONV, B_CONV_DIM), lambda bi, t: (0, 0)),
                  vec(B_CONV_DIM), vec(GROUP_WIDTH), vec(GROUP_WIDTH), vec(GROUP_WIDTH), vec(GROUP_WIDTH)],
        out_specs=(pl.BlockSpec((sb, t_in, GROUP_WIDTH), lambda bi, t: (bi, t, 0)), conv_spec, ssm_spec),
        scratch_shapes=[pltpu.VMEM((CONV_PAD + SSD_INFLIGHT * SSD_CHUNK, B_CONV_DIM), F32),
                        pltpu.VMEM((B_STATE, GROUP_WIDTH), F32)],
        compiler_params=_cparams(("parallel", "arbitrary")),
        name="ssd",
    )(proj3, proj3, proj3, conv0, ssm0, cw, cb, dtb, alog, dvec, nrm)


HGRN_INFLIGHT = 8
HGRN_SAFE_LOG2 = 96.0


def _hgrn_kernel(q_ref, f_ref, i_ref, g_ref, st0_ref, lb_ref, nrm_ref, o_ref, st_out_ref, st_ref, b_blk, c_blk,
                 y_scr, *, t_blk, l_valid, n_blk, sb):
    t = pl.program_id(1)

    @pl.when(t == 0)
    def _():
        for q in range(sb):
            for h in range(HEADS):
                parts = [jnp.zeros((HEAD_DIM, HEAD_DIM * h), F32)] if h > 0 else []
                parts.append(st0_ref[q, h].T)
                if h < HEADS - 1:
                    parts.append(jnp.zeros((HEAD_DIM, HEAD_DIM * (HEADS - 1 - h)), F32))
                st_ref[q, h * HEAD_DIM:(h + 1) * HEAD_DIM, :] = jnp.concatenate(parts, axis=1)

    lb = lb_ref[...]
    nrm = nrm_ref[...]
    head_eq = _head_ones()
    ones_bd = head_eq.astype(BF16)
    tril = _tril(CHUNK).astype(BF16)
    rows = lax.broadcasted_iota(jnp.int32, (CHUNK, GROUP_WIDTH), 0)
    row_in_sub = rows % SUB
    lane_head = lax.broadcasted_iota(jnp.int32, (SUB, GROUP_WIDTH), 1) // HEAD_DIM
    n_sub = CHUNK // SUB
    n_chunks = t_blk // CHUNK
    sc_row = lax.broadcasted_iota(jnp.int32, (HEADS * SUB, CHUNK), 0) % SUB
    sc_col = lax.broadcasted_iota(jnp.int32, (HEADS * SUB, CHUNK), 1)

    def gates(q, cc, gmax):
        r0 = pl.multiple_of(cc * CHUNK, CHUNK)
        sl = pl.ds(r0, CHUNK)
        f = lb + (1.0 - lb) * jax.nn.sigmoid(f_ref[q, sl, :])
        kk = jnp.maximum(1.0 - f, 0.0)
        lf2 = jnp.log(f) * LOG2E
        if l_valid < n_blk * t_blk:
            ok = t * t_blk + r0 + rows < l_valid
            kk = jnp.where(ok, kk, 0.0)
            lf2 = jnp.where(ok, lf2, 0.0)
        bcs2 = _dot01_lhs(tril, lf2)
        b_blk[q, sl, :] = bcs2
        c_blk[q, sl, :] = jnp.log(kk) * LOG2E - bcs2
        prev = jnp.zeros((1, GROUP_WIDTH), F32)
        for a in range(n_sub):
            end = bcs2[(a + 1) * SUB - 1:(a + 1) * SUB, :]
            gmax = jnp.maximum(gmax, prev - end)
            prev = end
        return gmax

    gmax = jnp.zeros((1, GROUP_WIDTH), F32)
    for q in range(sb):
        gmax = lax.fori_loop(0, n_chunks, functools.partial(gates, q), gmax,
                             unroll=HGRN_INFLIGHT if n_chunks % HGRN_INFLIGHT == 0 else 1)
    factored_ok = jnp.max(gmax) <= HGRN_SAFE_LOG2

    def stack_heads(x):
        return jnp.concatenate([jnp.where(lane_head == h, x, 0.0) for h in range(HEADS)], axis=0).astype(BF16)

    def unstack_heads(x_all):
        out = jnp.zeros((SUB, GROUP_WIDTH), F32)
        for h in range(HEADS):
            out = out + jnp.where(lane_head == h, x_all[h * SUB:(h + 1) * SUB, :], 0.0)
        return out

    def chunks(items, factored):
        n = len(items)
        seqs = [q for q, _ in items]
        r0s = [pl.multiple_of(cc * CHUNK, CHUNK) for _, cc in items]
        sls = [pl.ds(r0, CHUNK) for r0 in r0s]
        qs = [q_ref[seqs[s], sls[s], :] for s in range(n)]
        vs = [i_ref[seqs[s], sls[s], :] for s in range(n)]
        bs = [b_blk[seqs[s], sls[s], :] for s in range(n)]
        cs = [c_blk[seqs[s], sls[s], :] for s in range(n)]
        vbs = [v.astype(BF16) for v in vs]

        os_ = []
        for s in range(n):
            st = st_ref.at[seqs[s]]
            os_.append(_dot_nt((qs[s] * jnp.exp2(bs[s])).astype(BF16), st[...].astype(BF16)))
            last = bs[s][CHUNK - 1:CHUNK, :]
            kt = jnp.exp2(last + cs[s]).astype(BF16)
            st[...] = st[...] * jnp.exp2(last) + jnp.where(head_eq, _dot_tn(vbs[s], kt), 0.0)

        pieces = [[] for _ in range(n)]
        for a in range(n_sub):
            s0 = a * SUB
            s_end = s0 + SUB if factored else s0
            if s_end == 0:
                for s in range(n):
                    pieces[s].append(jnp.zeros((SUB, GROUP_WIDTH), F32))
                continue
            scs = []
            for s in range(n):
                ref_row = bs[s][s0 - 1:s0, :] if a > 0 else jnp.zeros((1, GROUP_WIDTH), F32)
                qa = qs[s][s0:s0 + SUB, :] * jnp.exp2(bs[s][s0:s0 + SUB, :] - ref_row)
                ka = jnp.where(rows < s_end, jnp.exp2(ref_row + cs[s]), 0.0)
                sc = _dot_nt(stack_heads(qa), ka.astype(BF16))
                if factored:
                    sc = jnp.where(sc_col <= s0 + sc_row, sc, 0.0)
                scs.append(sc.astype(BF16))
            for s in range(n):
                pieces[s].append(unstack_heads(_dot(scs[s], vbs[s])))
        for s in range(n):
            os_[s] = os_[s] + jnp.concatenate(pieces[s], axis=0)

        if not factored:
            for s in range(n):
                def sub_rows(ref, jj):
                    return jnp.concatenate(
                        [jnp.broadcast_to(ref[pl.ds(r0s[s] + a * SUB + jj, 1), :], (SUB, GROUP_WIDTH))
                         for a in range(n_sub)], axis=0)

                for jj in range(SUB):
                    w = jnp.exp2(bs[s] + sub_rows(c_blk.at[seqs[s]], jj))
                    y_scr[jj * CHUNK:(jj + 1) * CHUNK, :] = jnp.where(row_in_sub >= jj, qs[s] * w, 0.0).astype(BF16)
                att = _dot(y_scr[...], ones_bd)
                for jj in range(SUB):
                    os_[s] = os_[s] + att[jj * CHUNK:(jj + 1) * CHUNK, :] * sub_rows(i_ref.at[seqs[s]], jj)

        mss = [_dot((o * o).astype(BF16), ones_bd) * (1.0 / HEAD_DIM) for o in os_]
        for s in range(n):
            res = os_[s] * lax.rsqrt(mss[s] + EPS) * nrm * _silu(g_ref[seqs[s], sls[s], :])
            o_ref[seqs[s], sls[s], :] = res.astype(MIX_DTYPE)

    @pl.when(factored_ok)
    def _():
        if sb > 1:
            chunks([(q, 0) for q in range(sb)], True)
        else:
            inflight = HGRN_INFLIGHT if n_chunks % HGRN_INFLIGHT == 0 else 1

            def group(pp, carry):
                chunks([(0, inflight * pp + s) for s in range(inflight)], True)
                return carry

            lax.fori_loop(0, n_chunks // inflight, group, 0)

    @pl.when(jnp.logical_not(factored_ok))
    def _():
        for q in range(sb):
            def one(cc, carry):
                chunks([(q, cc)], False)
                return carry

            lax.fori_loop(0, n_chunks, one, 0)

    @pl.when(t == n_blk - 1)
    def _():
        for q in range(sb):
            for h in range(HEADS):
                hs = slice(h * HEAD_DIM, (h + 1) * HEAD_DIM)
                st_out_ref[q, h] = st_ref[q, hs, hs].T


def _hgrn(proj3, col0, st0, lb, nrm, l_valid, t_blk):
    b, lrows, _ = proj3.shape
    n_blk = -(-l_valid // t_blk)
    assert n_blk * t_blk <= lrows and t_blk % CHUNK == 0
    sb = _seq_batch(b, n_blk)
    col = lambda c: pl.BlockSpec((sb, t_blk, GROUP_WIDTH),
                                 lambda bi, t: (bi, t, (c - COL_CQ + col0) // GROUP_WIDTH))
    vec = pl.BlockSpec((1, GROUP_WIDTH), lambda bi, t: (0, 0))
    st_spec = pl.BlockSpec((sb, HEADS, HEAD_DIM, HEAD_DIM), lambda bi, t: (bi, 0, 0, 0))
    return pl.pallas_call(
        functools.partial(_hgrn_kernel, t_blk=t_blk, l_valid=l_valid, n_blk=n_blk, sb=sb),
        out_shape=(jax.ShapeDtypeStruct((b, n_blk * t_blk, GROUP_WIDTH), MIX_DTYPE),
                   jax.ShapeDtypeStruct((b, HEADS, HEAD_DIM, HEAD_DIM), F32)),
        grid=(b // sb, n_blk),
        in_specs=[col(COL_CQ), col(COL_CF), col(COL_CI), col(COL_CG), st_spec, vec, vec],
        out_specs=(pl.BlockSpec((sb, t_blk, GROUP_WIDTH), lambda bi, t: (bi, t, 0)), st_spec),
        scratch_shapes=[pltpu.VMEM((sb, GROUP_WIDTH, GROUP_WIDTH), F32),
                        pltpu.VMEM((sb, t_blk, GROUP_WIDTH), F32),
                        pltpu.VMEM((sb, t_blk, GROUP_WIDTH), F32),
                        pltpu.VMEM((SUB * CHUNK, GROUP_WIDTH), BF16)],
        compiler_params=_cparams(("parallel", "arbitrary")),
        name="hgrn",
    )(proj3, proj3, proj3, proj3, st0, lb, nrm)


POOL_PAD = 24
POOL_FRONT = 8
D_GROUP = 64


def _pool_kernel(x_ref, buf_ref, w_ref, sc_ref, o_ref, buf_out_ref, xp_ref, s2_ref, s4_ref, *, t_blk):
    t = pl.program_id(1)
    hi = POOL_PAD + t_blk
    half = GROUP_WIDTH // 2

    @pl.when(t == 0)
    def _():
        zeros = jnp.zeros((POOL_PAD - D_BUF, GROUP_WIDTH), F32)
        xp_ref[0:POOL_PAD - D_BUF, :] = zeros
        s2_ref[0:POOL_FRONT, :] = zeros[0:POOL_FRONT]
        s4_ref[0:POOL_FRONT, :] = zeros[0:POOL_FRONT]
        xp_ref[POOL_PAD - D_BUF:POOL_PAD, :] = buf_ref[0]

    x = x_ref[0]
    xp_ref[POOL_PAD:hi, :] = x
    s2 = xp_ref[POOL_FRONT:hi, :] + xp_ref[POOL_FRONT - 1:hi - 1, :]
    s2_ref[POOL_FRONT:hi, :] = s2
    s4 = s2 + s2_ref[POOL_FRONT - 2:hi - 2, :]
    s4_ref[POOL_FRONT:hi, :] = s4
    s8 = s4[:, half:] + s4_ref[POOL_FRONT - 4:hi - 4, half:]
    n_ext = hi - POOL_FRONT
    s16 = s8[8:n_ext, :] + s8[0:n_ext - 8, :]
    lo_sum = s2[n_ext - t_blk:, :half]
    lane = lax.broadcasted_iota(jnp.int32, (t_blk, half), 1)
    pooled_lo = jnp.where(lane < D_GROUP, lo_sum * 0.5, s4[n_ext - t_blk:, :half] * 0.25)
    pooled_hi = jnp.where(lane < D_GROUP, s8[n_ext - t_blk:, :] * 0.125, s16[n_ext - 8 - t_blk:, :] * 0.0625)
    pooled = jnp.concatenate([pooled_lo, pooled_hi], axis=1) - x
    o_ref[0] = (_dot(pooled.astype(BF16), w_ref[...]) * sc_ref[...]).astype(MIX_DTYPE)
    tail = xp_ref[hi - D_BUF:hi, :]
    buf_out_ref[0] = tail
    xp_ref[POOL_PAD - D_BUF:POOL_PAD, :] = tail


def _pool(proj3, buf, w_bd, scale, l_valid, t_blk):
    b = proj3.shape[0]
    n_blk = l_valid // t_blk
    assert n_blk * t_blk == l_valid and t_blk >= D_BUF
    sb = _seq_batch(b, n_blk)
    buf_spec = pl.BlockSpec((sb, D_BUF, GROUP_WIDTH), lambda bi, t: (bi, 0, 0))
    return pl.pallas_call(
        _per_sequence(functools.partial(_pool_kernel, t_blk=t_blk), sb, (0, 0, None, None, 0, 0, None, None, None)),
        out_shape=(jax.ShapeDtypeStruct((b, l_valid, GROUP_WIDTH), MIX_DTYPE),
                   jax.ShapeDtypeStruct((b, D_BUF, GROUP_WIDTH), F32)),
        grid=(b // sb, n_blk),
        in_specs=[pl.BlockSpec((sb, t_blk, GROUP_WIDTH), lambda bi, t: (bi, t, COL_DX // GROUP_WIDTH)),
                  buf_spec,
                  pl.BlockSpec((GROUP_WIDTH, GROUP_WIDTH), lambda bi, t: (0, 0)),
                  pl.BlockSpec((1, GROUP_WIDTH), lambda bi, t: (0, 0))],
        out_specs=(pl.BlockSpec((sb, t_blk, GROUP_WIDTH), lambda bi, t: (bi, t, 0)), buf_spec),
        scratch_shapes=[pltpu.VMEM((POOL_PAD + t_blk, GROUP_WIDTH), F32)] * 3,
        compiler_params=_cparams(("parallel", "arbitrary")),
        name="pool",
    )(proj3, buf, w_bd, scale)


SSD_BLOCK = 1024
HGRN_BLOCK = 1024
POOL_BLOCK = 2048


def _prep_layer(layer, norm1, w_in, a_rel_bias, b_conv_w, b_conv_b, b_dt_bias, b_a_log, b_d, b_norm, lbs, c_norm,
                d_pool_w, d_pool_scale, w_out, norm2, w_gate_up, w_down):
    w_r = _w_in_prep(jnp.transpose(w_in, (0, 2, 1)), layer)
    tab = a_rel_bias[layer].T
    m = A_BAND + CHUNK
    far = tab[:, 2 * REL_CLIP:]
    n_far = A_BAND_PREV - REL_CLIP + 1
    r = jnp.concatenate([jnp.broadcast_to(far, (HEADS, n_far)),
                         tab[:, 2 * REL_CLIP - 1:CHUNK:-1],
                         jnp.broadcast_to(far, (HEADS, m - n_far - (2 * REL_CLIP - 1 - CHUNK)))], axis=1)
    bias = jnp.tile(r, (1, CHUNK))[:, :CHUNK * (m - 1)].reshape(HEADS, CHUNK, m - 1)[:, :, :A_BAND]
    bias = bias.reshape(HEADS * CHUNK, A_BAND)
    rep = lambda p: jnp.repeat(p[layer], HEAD_DIM)[None, :]
    pw = d_pool_w[layer]
    w_bd = jnp.zeros((GROUP_WIDTH, GROUP_WIDTH), F32)
    for g in range(4):
        w_bd = w_bd.at[g * D_GROUP:(g + 1) * D_GROUP, g * D_GROUP:(g + 1) * D_GROUP].set(pw[g])
    return dict(
        n1=norm1[layer][None, :], w_in=w_r, bias=bias,
        cw=b_conv_w[layer], cb=b_conv_b[layer][None, :], dtb=rep(b_dt_bias), alog=rep(b_a_log), dvec=rep(b_d),
        bnrm=b_norm[layer][None, :], lb=lbs[layer][None, :], cnrm=c_norm[layer].reshape(1, GROUP_WIDTH),
        w_bd=w_bd.astype(BF16), psc=d_pool_scale[layer][None, :], n2=norm2[layer][None, :], layer=layer,
        w_out=w_out.astype(BF16), w_gu=w_gate_up.astype(BF16), w_dn=w_down.astype(BF16))


def _round_up(n, m):
    return -(-n // m) * m


def _mixers(proj, st, p, cache_t):
    k_prev, v_prev, conv0, ssm0, hgrn0, pool0 = st
    b, l, _ = proj.shape
    p_hist = k_prev.shape[1]
    lq = _round_up(l, CHUNK)
    pad_rows = lambda a: a if lq == l else jnp.pad(a, ((0, 0), (0, lq - l), (0, 0)))

    keep = min(A_BAND_PREV, p_hist + l)
    n_new = min(l, keep)
    ka = proj[:, l - n_new:, COL_AK:COL_AK + GROUP_WIDTH].reshape(b, n_new, HEADS, HEAD_DIM)
    va = proj[:, l - n_new:, COL_AV:COL_AV + GROUP_WIDTH].reshape(b, n_new, HEADS, HEAD_DIM)
    if cache_t is not None:
        bias_s = p["bias"].reshape(HEADS, CHUNK, A_BAND)[:, :l].reshape(HEADS * l, A_BAND)
        oa, kt_new, vt_new = _attention_cached(proj, cache_t[0], cache_t[1], p["layer"], bias_s, l)
        untranspose = lambda a: a.reshape(b, HEADS, HEAD_DIM, A_BAND_PREV).transpose(0, 3, 1, 2)
        new_k, new_v = untranspose(kt_new), untranspose(vt_new)
    else:
        if p_hist == 0:
            k_cache = v_cache = None
        else:
            k_cache = k_prev.astype(BF16).reshape(b, p_hist, GROUP_WIDTH)
            v_cache = v_prev.astype(BF16).reshape(b, p_hist, GROUP_WIDTH)
        oa = _attention(pad_rows(proj), k_cache, v_cache, p["bias"], l)[:, :l]
        new_k = jnp.concatenate([k_prev[:, p_hist - (keep - n_new):], ka], axis=1)
        new_v = jnp.concatenate([v_prev[:, p_hist - (keep - n_new):], va], axis=1)

    ob, new_conv, new_ssm = _ssd(proj, conv0, ssm0, p["cw"], p["cb"], p["dtb"], p["alog"], p["dvec"], p["bnrm"],
                                 l, SSD_BLOCK if l % SSD_BLOCK == 0 else SSD_CHUNK)

    if l % HGRN_BLOCK == 0:
        oc, new_hgrn = _hgrn(proj, COL_CQ, hgrn0, p["lb"], p["cnrm"], l, HGRN_BLOCK)
    else:
        oc, new_hgrn = _hgrn(pad_rows(proj[:, :, COL_CQ:COL_CQ + 4 * GROUP_WIDTH]), 0, hgrn0, p["lb"], p["cnrm"], l, lq)
        oc = oc[:, :l]

    od, new_pool = _pool(proj, pool0, p["w_bd"], p["psc"], l, POOL_BLOCK if l % POOL_BLOCK == 0 else l)

    flat = lambda a: a.reshape(b * l, GROUP_WIDTH)
    return (flat(oa), flat(ob), flat(oc), flat(od)), (new_k, new_v, new_conv, new_ssm, new_hgrn, new_pool)


def _trunk_groups(groups, params, nf):
    assert len(groups) <= 2
    depth = len(params)
    xs = [x.reshape(-1, D_MODEL) for x, _ in groups]
    shapes = [x.shape for x, _ in groups]
    new = [[[] for _ in range(6)] for _ in groups]
    cache_ts = []
    for x, states in groups:
        b, l, _ = x.shape
        cache_t = None
        if states[0].shape[2] == A_BAND_PREV and l <= CHUNK and l % 8 == 0:
            as_t = lambda c: jnp.transpose(c, (0, 1, 3, 4, 2)).reshape(depth, b, GROUP_WIDTH, A_BAND_PREV)
            cache_t = (as_t(states[0]), as_t(states[1]))
        cache_ts.append(cache_t)
    for layer in range(depth):
        p = params[layer]
        projs = _inproj(xs[0], p["n1"], p["w_in"], *xs[1:])
        projs = projs if len(groups) > 1 else (projs,)
        mixes = []
        for g, (_, states) in enumerate(groups):
            b, l, _ = shapes[g]
            st_l = tuple(s[layer] for s in states)
            mix, ns = _mixers(projs[g].reshape(b, l, PROJ_W), st_l, p, cache_ts[g])
            mixes.append(mix)
            for i in range(6):
                new[g][i].append(ns[i])
        extra = (xs[1], mixes[1]) if len(groups) > 1 else None
        ys = _mlp(xs[0], mixes[0], p["layer"], p["w_out"], p["n2"], p["w_gu"], p["w_dn"], nf, layer == depth - 1, extra)
        xs = list(ys) if len(groups) > 1 else [ys]
    return [(xs[g].reshape(shapes[g]), tuple(jnp.stack(n, axis=0) for n in new[g])) for g in range(len(groups))]


def _trunk(x, states, params, nf):
    return _trunk_groups([(x, states)], params, nf)[0]


def kernel(x_prompt, x_sample, cache_a_k, cache_a_v, state_b_conv, state_b_ssm, state_c_hgrn, state_d_pool, norm1,
           w_in, a_rel_bias, b_conv_w, b_conv_b, b_dt_bias, b_a_log, b_d, b_norm, c_lb_logits, c_norm, d_pool_w,
           d_pool_scale, w_out, norm2, w_gate_up, w_down, norm_f):
    depth = w_in.shape[0]
    lbs = jnp.cumsum(jax.nn.softmax(c_lb_logits.astype(F32), axis=0), axis=0)
    lbs = lbs - lbs[:1]
    params = [_prep_layer(layer, norm1, w_in, a_rel_bias, b_conv_w, b_conv_b, b_dt_bias, b_a_log, b_d, b_norm, lbs,
                          c_norm, d_pool_w, d_pool_scale, w_out, norm2, w_gate_up, w_down)
              for layer in range(depth)]
    nf = norm_f[None, :]
    bp = x_prompt.shape[0]
    prompt_states = (
        jnp.zeros((depth, bp, 0, HEADS, HEAD_DIM), F32),
        jnp.zeros((depth, bp, 0, HEADS, HEAD_DIM), F32),
        jnp.zeros((depth, bp, B_CONV - 1, B_CONV_DIM), F32),
        jnp.zeros((depth, bp, HEADS, HEAD_DIM, B_STATE), F32),
        jnp.zeros((depth, bp, HEADS, HEAD_DIM, HEAD_DIM), F32),
        jnp.zeros((depth, bp, D_BUF, GROUP_WIDTH), F32),
    )
    sample_states = (cache_a_k, cache_a_v, state_b_conv, state_b_ssm, state_c_hgrn, state_d_pool)
    (y_prompt, ps), (y_sample, ss) = _trunk_groups([(x_prompt, prompt_states), (x_sample, sample_states)], params, nf)
    return (y_prompt, y_sample) + ps + ss
```

```python
import functools

import jax
import jax.numpy as jnp
from jax import lax
from jax.experimental import pallas as pl
from jax.experimental.pallas import tpu as pltpu

F32 = jnp.float32
BF16 = jnp.bfloat16
MIX_DTYPE = BF16

D_MODEL = 1024
GROUP_WIDTH = 256
HEADS = 4
HEAD_DIM = 64
CHUNK = 64
A_BAND_PREV = 512
A_BAND = A_BAND_PREV + CHUNK
REL_CLIP = 128
B_STATE = 128
B_CONV = 4
B_CONV_DIM = 768
D_BUF = 15
D_FF = 2816
EPS = 1e-6
NEG = -1e30
LOG2E = 1.4426950408889634
SUB = 32

COL_XBC = 0
COL_AQ = 768
COL_AK = 1024
COL_AV = 1280
COL_BZ = 1536
COL_CQ = 1792
COL_CF = 2048
COL_CI = 2304
COL_CG = 2560
COL_DX = 2816
COL_DT = 3072
PROJ_W = 3328

VMEM_LIMIT = 56 * 1024 * 1024


def _cparams(sem):
    return pltpu.CompilerParams(dimension_semantics=sem, vmem_limit_bytes=VMEM_LIMIT)


def _rms(x, w):
    return x * lax.rsqrt(jnp.mean(x * x, axis=-1, keepdims=True) + EPS) * w


def _silu(x):
    h = 0.5 * x
    return h + h * jnp.tanh(h)


def _dot(a, b):
    return jnp.dot(a, b, preferred_element_type=F32)


def _dot_nt(a, b):
    return lax.dot_general(a, b, (((1,), (1,)), ((), ())), preferred_element_type=F32)


def _dot_tn(a, b):
    return lax.dot_general(a, b, (((0,), (0,)), ((), ())), preferred_element_type=F32)


def _split3(x):
    hi = x.astype(BF16)
    r = x - hi.astype(F32)
    mid = r.astype(BF16)
    lo = (r - mid.astype(F32)).astype(BF16)
    return hi, mid, lo


def _dot01_lhs(m01, x):
    hi, mid, lo = _split3(x)
    return _dot(m01, hi) + _dot(m01, mid) + _dot(m01, lo)


SEQ_BATCH = 8


def _seq_batch(b, n_blk):
    return SEQ_BATCH if n_blk == 1 and b % SEQ_BATCH == 0 else 1


def _per_sequence(kernel_fn, sb, seq_dim):
    if sb == 1:
        return kernel_fn

    def run(*refs):
        for s in range(sb):
            kernel_fn(*[r if d is None else r.at[(slice(None),) * d + (pl.ds(s, 1),)]
                        for r, d in zip(refs, seq_dim)])

    return run


def _tril(t):
    r = lax.broadcasted_iota(jnp.int32, (t, t), 0)
    c = lax.broadcasted_iota(jnp.int32, (t, t), 1)
    return r >= c


def _head_ones():
    r = lax.broadcasted_iota(jnp.int32, (GROUP_WIDTH, GROUP_WIDTH), 0) // HEAD_DIM
    c = lax.broadcasted_iota(jnp.int32, (GROUP_WIDTH, GROUP_WIDTH), 1) // HEAD_DIM
    return r == c


IN_WIDTH = 3076
ORIG_XBC = 1024
ORIG_DT = 1792
W_PREP_COLS = 256


def _w_in_prep_kernel(wt_ref, o_ref):
    def put(c0, rows):
        o_ref[:, c0:c0 + W_PREP_COLS] = rows.T.astype(BF16)

    for j in range(3):
        put(COL_XBC + W_PREP_COLS * j, wt_ref[0, ORIG_XBC + W_PREP_COLS * j:ORIG_XBC + W_PREP_COLS * (j + 1), :])
    for j in range(4):
        put(COL_AQ + W_PREP_COLS * j, wt_ref[0, W_PREP_COLS * j:W_PREP_COLS * (j + 1), :])
    for j in range(5):
        r0 = ORIG_DT + HEADS + W_PREP_COLS * j
        put(COL_CQ + W_PREP_COLS * j, wt_ref[0, r0:r0 + W_PREP_COLS, :])
    put(COL_DT, jnp.concatenate([jnp.broadcast_to(wt_ref[0, ORIG_DT + h:ORIG_DT + h + 1, :], (HEAD_DIM, D_MODEL))
                                 for h in range(HEADS)], axis=0))


def _w_in_prep(wt_all, layer):
    return pl.pallas_call(
        _w_in_prep_kernel,
        out_shape=jax.ShapeDtypeStruct((D_MODEL, PROJ_W), BF16),
        grid=(1,),
        in_specs=[pl.BlockSpec((1, IN_WIDTH, D_MODEL), lambda i: (layer, 0, 0), pipeline_mode=pl.Buffered(1))],
        out_specs=pl.BlockSpec((D_MODEL, PROJ_W), lambda i: (0, 0)),
        compiler_params=_cparams(("arbitrary",)),
        name="w_in_prep",
    )(wt_all)


def _is_last_step():
    return pl.program_id(0) == pl.num_programs(0) - 1


def _inproj_tile(x_ref, n_ref, w_ref, o_ref):
    hb = _rms(x_ref[...], n_ref[...]).astype(BF16)
    for c0 in range(0, PROJ_W, W_PREP_COLS):
        o_ref[:, c0:c0 + W_PREP_COLS] = _dot(hb, w_ref[:, c0:c0 + W_PREP_COLS])


def _inproj_kernel(x_ref, n_ref, w_ref, *rest):
    if len(rest) == 1:
        _inproj_tile(x_ref, n_ref, w_ref, rest[0])
    else:
        xe_ref, o_ref, oe_ref = rest
        _inproj_tile(x_ref, n_ref, w_ref, o_ref)
        pl.when(_is_last_step())(lambda: _inproj_tile(xe_ref, n_ref, w_ref, oe_ref))


def _inproj(x2d, n1, w_in, extra=None):
    rows = x2d.shape[0]
    tm = next((m for m in (1024, 512) if rows % m == 0), rows)
    whole = lambda a: pl.BlockSpec(a.shape, lambda i: (0, 0))
    in_specs = [pl.BlockSpec((tm, D_MODEL), lambda i: (i, 0)),
                pl.BlockSpec((1, D_MODEL), lambda i: (0, 0)),
                pl.BlockSpec((D_MODEL, PROJ_W), lambda i: (0, 0), pipeline_mode=pl.Buffered(1))]
    out_shape = jax.ShapeDtypeStruct((rows, PROJ_W), F32)
    out_specs = pl.BlockSpec((tm, PROJ_W), lambda i: (i, 0))
    operands = (x2d, n1, w_in)
    if extra is not None:
        out_e = jax.ShapeDtypeStruct((extra.shape[0], PROJ_W), F32)
        in_specs, operands = in_specs + [whole(extra)], operands + (extra,)
        out_shape, out_specs = (out_shape, out_e), (out_specs, whole(out_e))
    return pl.pallas_call(
        _inproj_kernel,
        out_shape=out_shape,
        grid=(rows // tm,),
        in_specs=in_specs,
        out_specs=out_specs,
        compiler_params=_cparams(("arbitrary",)),
        name="inproj",
    )(*operands)


CONV_PAD = 8


def _causal_conv_silu(xpad_ref, rows, conv_w, conv_b):
    hist = B_CONV - 1
    y = conv_b
    for i in range(B_CONV):
        y = y + xpad_ref[CONV_PAD - hist + i:CONV_PAD - hist + i + rows, :] * conv_w[i:i + 1, :]
    return _silu(y)


def _softplus(x):
    return jnp.maximum(x, 0.0) + jnp.log1p(jnp.exp(-jnp.abs(x)))


FF_CHUNK = 256


def _mlp_tile(x_ref, mix_refs, wo_ref, n2_ref, wgu_ref, wdn_ref, nf_ref, o_ref, mix_ref, act_ref, final):
    for m, r in enumerate(mix_refs):
        mix_ref[:, m * GROUP_WIDTH:(m + 1) * GROUP_WIDTH] = r[...]
    x1 = x_ref[...] + _dot(mix_ref[...], wo_ref[0])
    hb = _rms(x1, n2_ref[...]).astype(BF16)
    for c0 in range(0, D_FF, FF_CHUNK):
        gate = _dot(hb, wgu_ref[0, :, c0:c0 + FF_CHUNK])
        up = _dot(hb, wgu_ref[0, :, D_FF + c0:D_FF + c0 + FF_CHUNK])
        act_ref[:, c0:c0 + FF_CHUNK] = (_silu(gate) * up).astype(BF16)
    out = x1 + _dot(act_ref[...], wdn_ref[0])
    if final:
        out = _rms(out, nf_ref[...])
    o_ref[...] = out


def _mlp_kernel(*refs, final, has_extra):
    n_in = 10
    x_ref, oa_ref, ob_ref, oc_ref, od_ref, wo_ref, n2_ref, wgu_ref, wdn_ref, nf_ref = refs[:n_in]
    weights = (wo_ref, n2_ref, wgu_ref, wdn_ref, nf_ref)
    if not has_extra:
        o_ref, mix_ref, act_ref = refs[n_in:]
        _mlp_tile(x_ref, (oa_ref, ob_ref, oc_ref, od_ref), *weights, o_ref, mix_ref, act_ref, final)
    else:
        xe_ref, ea_ref, eb_ref, ec_ref, ed_ref, o_ref, oe_ref, mix_ref, act_ref, mixe_ref, acte_ref = refs[n_in:]
        _mlp_tile(x_ref, (oa_ref, ob_ref, oc_ref, od_ref), *weights, o_ref, mix_ref, act_ref, final)
        pl.when(_is_last_step())(
            lambda: _mlp_tile(xe_ref, (ea_ref, eb_ref, ec_ref, ed_ref), *weights, oe_ref, mixe_ref, acte_ref, final))


def _mlp(x2d, mix, layer, w_out, n2, w_gu, w_dn, nf, final, extra=None):
    rows = x2d.shape[0]
    tm = 512 if rows % 512 == 0 else rows
    row_spec = lambda w: pl.BlockSpec((tm, w), lambda i: (i, 0))
    whole = lambda a: pl.BlockSpec(a.shape, lambda i: (0, 0))
    wspec = lambda a: pl.BlockSpec((1,) + a.shape[1:], lambda i: (layer, 0, 0), pipeline_mode=pl.Buffered(1))
    in_specs = ([row_spec(D_MODEL)] + [row_spec(GROUP_WIDTH)] * 4
                + [wspec(w_out), whole(n2), wspec(w_gu), wspec(w_dn), whole(nf)])
    operands = (x2d, *mix, w_out, n2, w_gu, w_dn, nf)
    out_shape = jax.ShapeDtypeStruct((rows, D_MODEL), F32)
    out_specs = row_spec(D_MODEL)
    scratch = [pltpu.VMEM((tm, D_MODEL), BF16), pltpu.VMEM((tm, D_FF), BF16)]
    if extra is not None:
        x_e, mix_e = extra
        out_e = jax.ShapeDtypeStruct(x_e.shape, F32)
        in_specs = in_specs + [whole(x_e)] + [whole(m) for m in mix_e]
        operands = operands + (x_e, *mix_e)
        out_shape, out_specs = (out_shape, out_e), (out_specs, whole(out_e))
        scratch = scratch + [pltpu.VMEM((x_e.shape[0], D_MODEL), BF16), pltpu.VMEM((x_e.shape[0], D_FF), BF16)]
    return pl.pallas_call(
        functools.partial(_mlp_kernel, final=final, has_extra=extra is not None),
        out_shape=out_shape,
        grid=(rows // tm,),
        in_specs=in_specs,
        out_specs=out_specs,
        scratch_shapes=scratch,
        compiler_params=_cparams(("arbitrary",)),
        name="mlp",
    )(*operands)


ATTN_BLOCK = 1024
ATTN_UNROLL = 8


def _attn_kernel(q_ref, kh_ref, kc_ref, vh_ref, vc_ref, bias_ref, o_ref, kb_ref, vb_ref, *, qb, p_hist, l_valid):
    i = pl.program_id(1)
    if p_hist == 0:
        @pl.when(i == 0)
        def _():
            kb_ref[0:A_BAND_PREV, :] = jnp.zeros((A_BAND_PREV, GROUP_WIDTH), BF16)
            vb_ref[0:A_BAND_PREV, :] = jnp.zeros((A_BAND_PREV, GROUP_WIDTH), BF16)

        @pl.when(i > 0)
        def _():
            kb_ref[0:A_BAND_PREV, :] = kh_ref[0].astype(BF16)
            vb_ref[0:A_BAND_PREV, :] = vh_ref[0].astype(BF16)
    else:
        kb_ref[0:A_BAND_PREV, :] = kh_ref[0].astype(BF16)
        vb_ref[0:A_BAND_PREV, :] = vh_ref[0].astype(BF16)
    kb_ref[A_BAND_PREV:A_BAND_PREV + qb, :] = kc_ref[0].astype(BF16)
    vb_ref[A_BAND_PREV:A_BAND_PREV + qb, :] = vc_ref[0].astype(BF16)
    lane_head = lax.broadcasted_iota(jnp.int32, (CHUNK, GROUP_WIDTH), 1) // HEAD_DIM
    band_pos = lax.broadcasted_iota(jnp.int32, (1, A_BAND), 1)

    n_chunks = qb // CHUNK
    group = ATTN_UNROLL if n_chunks % ATTN_UNROLL == 0 else 1

    def chunks(masked, gi, carry):
        r0s = [pl.multiple_of((gi * group + u) * CHUNK, CHUNK) for u in range(group)]
        es, dens = [], []
        for r0 in r0s:
            qc = q_ref[0, pl.ds(r0, CHUNK), :] * (HEAD_DIM ** -0.5)
            qs = jnp.concatenate([jnp.where(lane_head == h, qc, 0.0) for h in range(HEADS)], axis=0).astype(BF16)
            s = _dot_nt(qs, kb_ref[pl.ds(r0, A_BAND), :]) + bias_ref[...]
            if masked:
                kpos = i * qb + r0 - A_BAND_PREV + band_pos
                s = jnp.where((kpos >= -p_hist) & (kpos < l_valid), s, NEG)
            e = jnp.exp(s - jnp.max(s, axis=-1, keepdims=True))
            dens.append(jnp.sum(e, axis=-1, keepdims=True))
            es.append(e.astype(BF16))
        o_alls = [_dot(es[u], vb_ref[pl.ds(r0s[u], A_BAND), :]) * (1.0 / dens[u]) for u in range(group)]
        for u in range(group):
            o = jnp.zeros((CHUNK, GROUP_WIDTH), F32)
            for h in range(HEADS):
                o = o + jnp.where(lane_head == h, o_alls[u][h * CHUNK:(h + 1) * CHUNK, :], 0.0)
            o_ref[0, pl.ds(r0s[u], CHUNK), :] = o.astype(MIX_DTYPE)
        return carry

    n_blocks = pl.num_programs(1)
    first_has_invalid = p_hist < A_BAND_PREV
    last_has_invalid = l_valid % qb != 0
    if not (first_has_invalid or last_has_invalid):
        lax.fori_loop(0, n_chunks // group, functools.partial(chunks, False), 0)
    else:
        edge = False
        if first_has_invalid:
            edge = edge | (i == 0)
        if last_has_invalid:
            edge = edge | (i == n_blocks - 1)
        @pl.when(edge)
        def _():
            lax.fori_loop(0, n_chunks // group, functools.partial(chunks, True), 0)

        @pl.when(jnp.logical_not(edge))
        def _():
            lax.fori_loop(0, n_chunks // group, functools.partial(chunks, False), 0)


def _attention(proj3, k_cache, v_cache, bias, l_valid):
    b = proj3.shape[0]
    lq = _round_up(l_valid, CHUNK)
    qb = next((q for q in (ATTN_BLOCK, A_BAND_PREV) if lq % q == 0), lq)
    nblk = lq // qb
    col = lambda c: pl.BlockSpec((1, qb, GROUP_WIDTH), lambda bi, i: (bi, i, c // GROUP_WIDTH))
    if k_cache is None:
        assert qb % A_BAND_PREV == 0
        p_hist = 0
        per = qb // A_BAND_PREV
        prev = lambda c: pl.BlockSpec((1, A_BAND_PREV, GROUP_WIDTH),
                                      lambda bi, i: (bi, jnp.maximum(i * per - 1, 0), c // GROUP_WIDTH))
        kh, vh, kh_spec, vh_spec = proj3, proj3, prev(COL_AK), prev(COL_AV)
    else:
        assert nblk == 1 and k_cache.shape[1] == A_BAND_PREV
        p_hist = A_BAND_PREV
        kh_spec = vh_spec = pl.BlockSpec((1, A_BAND_PREV, GROUP_WIDTH), lambda bi, i: (bi, 0, 0))
        kh, vh = k_cache, v_cache
    return pl.pallas_call(
        functools.partial(_attn_kernel, qb=qb, p_hist=p_hist, l_valid=l_valid),
        out_shape=jax.ShapeDtypeStruct((b, lq, GROUP_WIDTH), MIX_DTYPE),
        grid=(b, nblk),
        in_specs=[col(COL_AQ), kh_spec, col(COL_AK), vh_spec, col(COL_AV),
                  pl.BlockSpec((HEADS * CHUNK, A_BAND), lambda bi, i: (0, 0))],
        out_specs=pl.BlockSpec((1, qb, GROUP_WIDTH), lambda bi, i: (bi, i, 0)),
        scratch_shapes=[pltpu.VMEM((A_BAND_PREV + qb, GROUP_WIDTH), BF16),
                        pltpu.VMEM((A_BAND_PREV + qb, GROUP_WIDTH), BF16)],
        compiler_params=_cparams(("parallel", "arbitrary")),
        name="attention",
    )(proj3, kh, proj3, vh, proj3, bias)


def _attn_cached_kernel(q_ref, kn_ref, vn_ref, kt_ref, vt_ref, bias_ref, o_ref, kt_out_ref, vt_out_ref, *, l_new):
    lane_head = lax.broadcasted_iota(jnp.int32, (l_new, GROUP_WIDTH), 1) // HEAD_DIM
    q = q_ref[0] * (HEAD_DIM ** -0.5)
    qs = jnp.concatenate([jnp.where(lane_head == h, q, 0.0) for h in range(HEADS)], axis=0).astype(BF16)
    kn = kn_ref[0]
    vn = vn_ref[0]
    kt = kt_ref[0, 0]
    vt = vt_ref[0, 0]
    s_old = _dot(qs, kt.astype(BF16)) + bias_ref[:, 0:A_BAND_PREV]
    s_new = _dot_nt(qs, kn.astype(BF16)) + bias_ref[:, A_BAND_PREV:A_BAND_PREV + l_new]
    m = jnp.maximum(jnp.max(s_old, axis=-1, keepdims=True), jnp.max(s_new, axis=-1, keepdims=True))
    e_old = jnp.exp(s_old - m)
    e_new = jnp.exp(s_new - m)
    den = jnp.sum(e_old, axis=-1, keepdims=True) + jnp.sum(e_new, axis=-1, keepdims=True)
    o_all = (_dot_nt(e_old.astype(BF16), vt.astype(BF16)) + _dot(e_new.astype(BF16), vn.astype(BF16))) * (1.0 / den)
    o = jnp.zeros((l_new, GROUP_WIDTH), F32)
    for h in range(HEADS):
        o = o + jnp.where(lane_head == h, o_all[h * l_new:(h + 1) * l_new, :], 0.0)
    o_ref[0] = o.astype(MIX_DTYPE)

    keep = A_BAND_PREV - l_new
    lane = lax.broadcasted_iota(jnp.int32, (GROUP_WIDTH, A_BAND_PREV), 1)
    r = lax.broadcasted_iota(jnp.int32, (l_new, A_BAND_PREV), 0)
    c = lax.broadcasted_iota(jnp.int32, (l_new, A_BAND_PREV), 1)
    place = (c == r + keep).astype(BF16)

    def appended(old_t, new_rows):
        hi, mid, lo = _split3(new_rows)
        new_cols = _dot_tn(hi, place) + _dot_tn(mid, place) + _dot_tn(lo, place)
        return jnp.where(lane < keep, pltpu.roll(old_t, keep, 1), new_cols)

    kt_out_ref[0] = appended(kt, kn)
    vt_out_ref[0] = appended(vt, vn)


def _attention_cached(proj3, kt_cache, vt_cache, layer, bias, l_valid):
    b = proj3.shape[0]
    sb = _seq_batch(b, 1)
    col = lambda c: pl.BlockSpec((sb, l_valid, GROUP_WIDTH), lambda bi: (bi, 0, c // GROUP_WIDTH))
    cache_spec = pl.BlockSpec((1, sb, GROUP_WIDTH, A_BAND_PREV), lambda bi: (layer, bi, 0, 0))
    out_t = pl.BlockSpec((sb, GROUP_WIDTH, A_BAND_PREV), lambda bi: (bi, 0, 0))
    return pl.pallas_call(
        _per_sequence(functools.partial(_attn_cached_kernel, l_new=l_valid), sb, (0, 0, 0, 1, 1, None, 0, 0, 0)),
        out_shape=(jax.ShapeDtypeStruct((b, l_valid, GROUP_WIDTH), MIX_DTYPE),
                   jax.ShapeDtypeStruct((b, GROUP_WIDTH, A_BAND_PREV), F32),
                   jax.ShapeDtypeStruct((b, GROUP_WIDTH, A_BAND_PREV), F32)),
        grid=(b // sb,),
        in_specs=[col(COL_AQ), col(COL_AK), col(COL_AV), cache_spec, cache_spec,
                  pl.BlockSpec((HEADS * l_valid, A_BAND), lambda bi: (0, 0))],
        out_specs=(pl.BlockSpec((sb, l_valid, GROUP_WIDTH), lambda bi: (bi, 0, 0)), out_t, out_t),
        compiler_params=_cparams(("parallel",)),
        name="attention_cached",
    )(proj3, proj3, proj3, kt_cache, vt_cache, bias)


SSD_CHUNK = 128
SSD_INFLIGHT = 4


def _ssd_kernel(xbc_ref, z_ref, dt_ref, conv0_ref, ssm0_ref, cw_ref, cb_ref, dtb_ref, alog_ref, d_ref, nrm_ref,
                o_ref, conv_out_ref, ssm_out_ref, xpad_ref, st_ref, *, t_blk, l_valid, n_blk):
    t = pl.program_id(1)
    hist = B_CONV - 1

    @pl.when(t == 0)
    def _():
        xpad_ref[CONV_PAD - hist:CONV_PAD, :] = conv0_ref[0]
        st_ref[...] = ssm0_ref[0].reshape(GROUP_WIDTH, B_STATE).T

    n_last = l_valid - (n_blk - 1) * t_blk
    conv_out_ref[0] = xbc_ref[0, n_last - hist:n_last, :]

    tc = SSD_CHUNK
    short = l_valid < t_blk

    def rows_of(ref, sl):
        if short:
            return jnp.concatenate([ref[0], jnp.zeros((tc - l_valid, ref.shape[2]), F32)], axis=0)
        return ref[0, sl, :]

    tril = _tril(tc)
    tril_b = tril.astype(BF16)
    lane = lax.broadcasted_iota(jnp.int32, (tc, B_STATE), 1)
    a_neg = -jnp.exp(alog_ref[...])
    conv_b = cb_ref[...]
    conv_w = cw_ref[...]
    dt_bias = dtb_ref[...]
    d_skip = d_ref[...]
    nrm = nrm_ref[...]

    n_chunks = t_blk // tc
    group = SSD_INFLIGHT if n_chunks % SSD_INFLIGHT == 0 else 1
    groups = (slice(0, B_STATE), slice(B_STATE, 2 * B_STATE))

    def chunks(gi, carry):
        rg = group * tc
        base = pl.multiple_of(gi * rg, rg)
        sl_all = pl.ds(base, rg)
        xpad_ref[CONV_PAD:CONV_PAD + rg, :] = rows_of(xbc_ref, sl_all)
        xbc_all = _causal_conv_silu(xpad_ref, rg, conv_w, conv_b)
        xpad_ref[CONV_PAD - hist:CONV_PAD, :] = xpad_ref[CONV_PAD + rg - hist:CONV_PAD + rg, :]
        dt_all = _softplus(rows_of(dt_ref, sl_all) + dt_bias)
        if l_valid < n_blk * t_blk:
            rows_all = lax.broadcasted_iota(jnp.int32, (rg, GROUP_WIDTH), 0)
            dt_all = jnp.where(t * t_blk + base + rows_all < l_valid, dt_all, 0.0)
        z_all = rows_of(z_ref, sl_all)
        xs_all = xbc_all[:, 0:GROUP_WIDTH]
        a_all = dt_all * a_neg
        u_all = xs_all * dt_all

        cut = lambda arr, s: arr[s * tc:(s + 1) * tc, :]
        acss = [_dot01_lhs(tril_b, cut(a_all, s)) for s in range(group)]
        cms = [cut(xbc_all[:, 2 * GROUP_WIDTH:3 * GROUP_WIDTH], s).astype(BF16) for s in range(group)]
        bms = [cut(xbc_all[:, GROUP_WIDTH:2 * GROUP_WIDTH], s).astype(BF16) for s in range(group)]
        cbs = [[_dot_nt(cms[s][:, gs], bms[s][:, gs]) for gs in groups] for s in range(group)]

        y_states = []
        for s in range(group):
            acs = acss[s]
            last = acs[tc - 1:tc, :]
            upb = (cut(u_all, s) * jnp.exp(last - acs)).astype(BF16)
            eacs = jnp.exp(acs)
            elast = jnp.exp(last)
            ys_s = []
            for gs in groups:
                ys_s.append(_dot(cms[s][:, gs], st_ref[:, gs].astype(BF16)) * eacs[:, gs])
                st_ref[:, gs] = st_ref[:, gs] * elast[:, gs] + _dot_tn(bms[s][:, gs], upb[:, gs])
            y_states.append(ys_s)

        for s in range(group):
            acs = acss[s]
            acs_t = acs.T
            ub = cut(u_all, s).astype(BF16)
            ys = []
            for g, gs in enumerate(groups):
                parts = []
                for hh in range(2):
                    c0 = (2 * g + hh) * HEAD_DIM
                    col = jnp.broadcast_to(acs[:, c0:c0 + 1], (tc, tc))
                    rowv = jnp.broadcast_to(acs_t[c0:c0 + 1, :], (tc, tc))
                    dec = jnp.exp(jnp.where(tril, col - rowv, NEG))
                    parts.append(_dot((cbs[s][g] * dec).astype(BF16), ub[:, gs]))
                ys.append(jnp.where(lane < HEAD_DIM, parts[0], parts[1]) + y_states[s][g])
            yv = jnp.concatenate(ys, axis=1) + d_skip * cut(xs_all, s)
            res = _rms(yv * _silu(cut(z_all, s)), nrm).astype(MIX_DTYPE)
            if short:
                o_ref[0] = res[0:l_valid, :]
            else:
                o_ref[0, pl.ds(pl.multiple_of(base + s * tc, tc), tc), :] = res
        return carry

    lax.fori_loop(0, n_chunks // group, chunks, 0)

    @pl.when(t == n_blk - 1)
    def _():
        ssm_out_ref[0] = st_ref[...].T.reshape(HEADS, HEAD_DIM, B_STATE)


def _ssd(proj3, conv0, ssm0, cw, cb, dtb, alog, dvec, nrm, l_valid, t_blk):
    b, lrows, _ = proj3.shape
    n_blk = -(-l_valid // t_blk)
    assert t_blk % SSD_CHUNK == 0 and lrows == l_valid
    assert l_valid == n_blk * t_blk or (t_blk == SSD_CHUNK and B_CONV - 1 <= l_valid < t_blk and l_valid % 8 == 0)
    t_in = min(t_blk, l_valid)
    sb = _seq_batch(b, n_blk)
    vec = lambda w: pl.BlockSpec((1, w), lambda bi, t: (0, 0))
    conv_spec = pl.BlockSpec((sb, B_CONV - 1, B_CONV_DIM), lambda bi, t: (bi, 0, 0))
    ssm_spec = pl.BlockSpec((sb, HEADS, HEAD_DIM, B_STATE), lambda bi, t: (bi, 0, 0, 0))
    return pl.pallas_call(
        _per_sequence(functools.partial(_ssd_kernel, t_blk=t_blk, l_valid=l_valid, n_blk=n_blk), sb,
                      (0,) * 5 + (None,) * 6 + (0,) * 3 + (None,) * 2),
        out_shape=(jax.ShapeDtypeStruct((b, l_valid, GROUP_WIDTH), MIX_DTYPE),
                   jax.ShapeDtypeStruct((b, B_CONV - 1, B_CONV_DIM), F32),
                   jax.ShapeDtypeStruct((b, HEADS, HEAD_DIM, B_STATE), F32)),
        grid=(b // sb, n_blk),
        in_specs=[pl.BlockSpec((sb, t_in, B_CONV_DIM), lambda bi, t: (bi, t, COL_XBC // B_CONV_DIM)),
                  pl.BlockSpec((sb, t_in, GROUP_WIDTH), lambda bi, t: (bi, t, COL_BZ // GROUP_WIDTH)),
                  pl.BlockSpec((sb, t_in, GROUP_WIDTH), lambda bi, t: (bi, t, COL_DT // GROUP_WIDTH)),
                  conv_spec, ssm_spec,
                  pl.BlockSpec((B_CONV, B_CONV_DIM), lambda bi, t: (0, 0)),
                  vec(B_CONV_DIM), vec(GROUP_WIDTH), vec(GROUP_WIDTH), vec(GROUP_WIDTH), vec(GROUP_WIDTH)],
        out_specs=(pl.BlockSpec((sb, t_in, GROUP_WIDTH), lambda bi, t: (bi, t, 0)), conv_spec, ssm_spec),
        scratch_shapes=[pltpu.VMEM((CONV_PAD + SSD_INFLIGHT * SSD_CHUNK, B_CONV_DIM), F32),
                        pltpu.VMEM((B_STATE, GROUP_WIDTH), F32)],
        compiler_params=_cparams(("parallel", "arbitrary")),
        name="ssd",
    )(proj3, proj3, proj3, conv0, ssm0, cw, cb, dtb, alog, dvec, nrm)


HGRN_INFLIGHT = 8
HGRN_SAFE_LOG2 = 96.0


def _hgrn_kernel(q_ref, f_ref, i_ref, g_ref, st0_ref, lb_ref, nrm_ref, o_ref, st_out_ref, st_ref, b_blk, c_blk,
                 y_scr, *, t_blk, l_valid, n_blk, sb):
    t = pl.program_id(1)

    @pl.when(t == 0)
    def _():
        for q in range(sb):
            for h in range(HEADS):
                parts = [jnp.zeros((HEAD_DIM, HEAD_DIM * h), F32)] if h > 0 else []
                parts.append(st0_ref[q, h].T)
                if h < HEADS - 1:
                    parts.append(jnp.zeros((HEAD_DIM, HEAD_DIM * (HEADS - 1 - h)), F32))
                st_ref[q, h * HEAD_DIM:(h + 1) * HEAD_DIM, :] = jnp.concatenate(parts, axis=1)

    lb = lb_ref[...]
    nrm = nrm_ref[...]
    head_eq = _head_ones()
    ones_bd = head_eq.astype(BF16)
    tril = _tril(CHUNK).astype(BF16)
    rows = lax.broadcasted_iota(jnp.int32, (CHUNK, GROUP_WIDTH), 0)
    row_in_sub = rows % SUB
    lane_head = lax.broadcasted_iota(jnp.int32, (SUB, GROUP_WIDTH), 1) // HEAD_DIM
    n_sub = CHUNK // SUB
    n_chunks = t_blk // CHUNK
    sc_row = lax.broadcasted_iota(jnp.int32, (HEADS * SUB, CHUNK), 0) % SUB
    sc_col = lax.broadcasted_iota(jnp.int32, (HEADS * SUB, CHUNK), 1)

    def gates(q, cc, gmax):
        r0 = pl.multiple_of(cc * CHUNK, CHUNK)
        sl = pl.ds(r0, CHUNK)
        f = lb + (1.0 - lb) * jax.nn.sigmoid(f_ref[q, sl, :])
        kk = jnp.maximum(1.0 - f, 0.0)
        lf2 = jnp.log(f) * LOG2E
        if l_valid < n_blk * t_blk:
            ok = t * t_blk + r0 + rows < l_valid
            kk = jnp.where(ok, kk, 0.0)
            lf2 = jnp.where(ok, lf2, 0.0)
        bcs2 = _dot01_lhs(tril, lf2)
        b_blk[q, sl, :] = bcs2
        c_blk[q, sl, :] = jnp.log(kk) * LOG2E - bcs2
        prev = jnp.zeros((1, GROUP_WIDTH), F32)
        for a in range(n_sub):
            end = bcs2[(a + 1) * SUB - 1:(a + 1) * SUB, :]
            gmax = jnp.maximum(gmax, prev - end)
            prev = end
        return gmax

    gmax = jnp.zeros((1, GROUP_WIDTH), F32)
    for q in range(sb):
        gmax = lax.fori_loop(0, n_chunks, functools.partial(gates, q), gmax,
                             unroll=HGRN_INFLIGHT if n_chunks % HGRN_INFLIGHT == 0 else 1)
    factored_ok = jnp.max(gmax) <= HGRN_SAFE_LOG2

    def stack_heads(x):
        return jnp.concatenate([jnp.where(lane_head == h, x, 0.0) for h in range(HEADS)], axis=0).astype(BF16)

    def unstack_heads(x_all):
        out = jnp.zeros((SUB, GROUP_WIDTH), F32)
        for h in range(HEADS):
            out = out + jnp.where(lane_head == h, x_all[h * SUB:(h + 1) * SUB, :], 0.0)
        return out

    def chunks(items, factored):
        n = len(items)
        seqs = [q for q, _ in items]
        r0s = [pl.multiple_of(cc * CHUNK, CHUNK) for _, cc in items]
        sls = [pl.ds(r0, CHUNK) for r0 in r0s]
        qs = [q_ref[seqs[s], sls[s], :] for s in range(n)]
        vs = [i_ref[seqs[s], sls[s], :] for s in range(n)]
        bs = [b_blk[seqs[s], sls[s], :] for s in range(n)]
        cs = [c_blk[seqs[s], sls[s], :] for s in range(n)]
        vbs = [v.astype(BF16) for v in vs]

        os_ = []
        for s in range(n):
            st = st_ref.at[seqs[s]]
            os_.append(_dot_nt((qs[s] * jnp.exp2(bs[s])).astype(BF16), st[...].astype(BF16)))
            last = bs[s][CHUNK - 1:CHUNK, :]
            kt = jnp.exp2(last + cs[s]).astype(BF16)
            st[...] = st[...] * jnp.exp2(last) + jnp.where(head_eq, _dot_tn(vbs[s], kt), 0.0)

        pieces = [[] for _ in range(n)]
        for a in range(n_sub):
            s0 = a * SUB
            s_end = s0 + SUB if factored else s0
            if s_end == 0:
                for s in range(n):
                    pieces[s].append(jnp.zeros((SUB, GROUP_WIDTH), F32))
                continue
            scs = []
            for s in range(n):
                ref_row = bs[s][s0 - 1:s0, :] if a > 0 else jnp.zeros((1, GROUP_WIDTH), F32)
                qa = qs[s][s0:s0 + SUB, :] * jnp.exp2(bs[s][s0:s0 + SUB, :] - ref_row)
                ka = jnp.where(rows < s_end, jnp.exp2(ref_row + cs[s]), 0.0)
                sc = _dot_nt(stack_heads(qa), ka.astype(BF16))
                if factored:
                    sc = jnp.where(sc_col <= s0 + sc_row, sc, 0.0)
                scs.append(sc.astype(BF16))
            for s in range(n):
                pieces[s].append(unstack_heads(_dot(scs[s], vbs[s])))
        for s in range(n):
            os_[s] = os_[s] + jnp.concatenate(pieces[s], axis=0)

        if not factored:
            for s in range(n):
                def sub_rows(ref, jj):
                    return jnp.concatenate(
                        [jnp.broadcast_to(ref[pl.ds(r0s[s] + a * SUB + jj, 1), :], (SUB, GROUP_WIDTH))
                         for a in range(n_sub)], axis=0)

                for jj in range(SUB):
                    w = jnp.exp2(bs[s] + sub_rows(c_blk.at[seqs[s]], jj))
                    y_scr[jj * CHUNK:(jj + 1) * CHUNK, :] = jnp.where(row_in_sub >= jj, qs[s] * w, 0.0).astype(BF16)
                att = _dot(y_scr[...], ones_bd)
                for jj in range(SUB):
                    os_[s] = os_[s] + att[jj * CHUNK:(jj + 1) * CHUNK, :] * sub_rows(i_ref.at[seqs[s]], jj)

        mss = [_dot((o * o).astype(BF16), ones_bd) * (1.0 / HEAD_DIM) for o in os_]
        for s in range(n):
            res = os_[s] * lax.rsqrt(mss[s] + EPS) * nrm * _silu(g_ref[seqs[s], sls[s], :])
            o_ref[seqs[s], sls[s], :] = res.astype(MIX_DTYPE)

    @pl.when(factored_ok)
    def _():
        if sb > 1:
            chunks([(q, 0) for q in range(sb)], True)
        else:
            inflight = HGRN_INFLIGHT if n_chunks % HGRN_INFLIGHT == 0 else 1

            def group(pp, carry):
                chunks([(0, inflight * pp + s) for s in range(inflight)], True)
                return carry

            lax.fori_loop(0, n_chunks // inflight, group, 0)

    @pl.when(jnp.logical_not(factored_ok))
    def _():
        for q in range(sb):
            def one(cc, carry):
                chunks([(q, cc)], False)
                return carry

            lax.fori_loop(0, n_chunks, one, 0)

    @pl.when(t == n_blk - 1)
    def _():
        for q in range(sb):
            for h in range(HEADS):
                hs = slice(h * HEAD_DIM, (h + 1) * HEAD_DIM)
                st_out_ref[q, h] = st_ref[q, hs, hs].T


def _hgrn(proj3, col0, st0, lb, nrm, l_valid, t_blk):
    b, lrows, _ = proj3.shape
    n_blk = -(-l_valid // t_blk)
    assert n_blk * t_blk <= lrows and t_blk % CHUNK == 0
    sb = _seq_batch(b, n_blk)
    col = lambda c: pl.BlockSpec((sb, t_blk, GROUP_WIDTH),
                                 lambda bi, t: (bi, t, (c - COL_CQ + col0) // GROUP_WIDTH))
    vec = pl.BlockSpec((1, GROUP_WIDTH), lambda bi, t: (0, 0))
    st_spec = pl.BlockSpec((sb, HEADS, HEAD_DIM, HEAD_DIM), lambda bi, t: (bi, 0, 0, 0))
    return pl.pallas_call(
        functools.partial(_hgrn_kernel, t_blk=t_blk, l_valid=l_valid, n_blk=n_blk, sb=sb),
        out_shape=(jax.ShapeDtypeStruct((b, n_blk * t_blk, GROUP_WIDTH), MIX_DTYPE),
                   jax.ShapeDtypeStruct((b, HEADS, HEAD_DIM, HEAD_DIM), F32)),
        grid=(b // sb, n_blk),
        in_specs=[col(COL_CQ), col(COL_CF), col(COL_CI), col(COL_CG), st_spec, vec, vec],
        out_specs=(pl.BlockSpec((sb, t_blk, GROUP_WIDTH), lambda bi, t: (bi, t, 0)), st_spec),
        scratch_shapes=[pltpu.VMEM((sb, GROUP_WIDTH, GROUP_WIDTH), F32),
                        pltpu.VMEM((sb, t_blk, GROUP_WIDTH), F32),
                        pltpu.VMEM((sb, t_blk, GROUP_WIDTH), F32),
                        pltpu.VMEM((SUB * CHUNK, GROUP_WIDTH), BF16)],
        compiler_params=_cparams(("parallel", "arbitrary")),
        name="hgrn",
    )(proj3, proj3, proj3, proj3, st0, lb, nrm)


POOL_PAD = 24
POOL_FRONT = 8
D_GROUP = 64


def _pool_kernel(x_ref, buf_ref, w_ref, sc_ref, o_ref, buf_out_ref, xp_ref, s2_ref, s4_ref, *, t_blk):
    t = pl.program_id(1)
    hi = POOL_PAD + t_blk
    half = GROUP_WIDTH // 2

    @pl.when(t == 0)
    def _():
        zeros = jnp.zeros((POOL_PAD - D_BUF, GROUP_WIDTH), F32)
        xp_ref[0:POOL_PAD - D_BUF, :] = zeros
        s2_ref[0:POOL_FRONT, :] = zeros[0:POOL_FRONT]
        s4_ref[0:POOL_FRONT, :] = zeros[0:POOL_FRONT]
        xp_ref[POOL_PAD - D_BUF:POOL_PAD, :] = buf_ref[0]

    x = x_ref[0]
    xp_ref[POOL_PAD:hi, :] = x
    s2 = xp_ref[POOL_FRONT:hi, :] + xp_ref[POOL_FRONT - 1:hi - 1, :]
    s2_ref[POOL_FRONT:hi, :] = s2
    s4 = s2 + s2_ref[POOL_FRONT - 2:hi - 2, :]
    s4_ref[POOL_FRONT:hi, :] = s4
    s8 = s4[:, half:] + s4_ref[POOL_FRONT - 4:hi - 4, half:]
    n_ext = hi - POOL_FRONT
    s16 = s8[8:n_ext, :] + s8[0:n_ext - 8, :]
    lo_sum = s2[n_ext - t_blk:, :half]
    lane = lax.broadcasted_iota(jnp.int32, (t_blk, half), 1)
    pooled_lo = jnp.where(lane < D_GROUP, lo_sum * 0.5, s4[n_ext - t_blk:, :half] * 0.25)
    pooled_hi = jnp.where(lane < D_GROUP, s8[n_ext - t_blk:, :] * 0.125, s16[n_ext - 8 - t_blk:, :] * 0.0625)
    pooled = jnp.concatenate([pooled_lo, pooled_hi], axis=1) - x
    o_ref[0] = (_dot(pooled.astype(BF16), w_ref[...]) * sc_ref[...]).astype(MIX_DTYPE)
    tail = xp_ref[hi - D_BUF:hi, :]
    buf_out_ref[0] = tail
    xp_ref[POOL_PAD - D_BUF:POOL_PAD, :] = tail


def _pool(proj3, buf, w_bd, scale, l_valid, t_blk):
    b = proj3.shape[0]
    n_blk = l_valid // t_blk
    assert n_blk * t_blk == l_valid and t_blk >= D_BUF
    sb = _seq_batch(b, n_blk)
    buf_spec = pl.BlockSpec((sb, D_BUF, GROUP_WIDTH), lambda bi, t: (bi, 0, 0))
    return pl.pallas_call(
        _per_sequence(functools.partial(_pool_kernel, t_blk=t_blk), sb, (0, 0, None, None, 0, 0, None, None, None)),
        out_shape=(jax.ShapeDtypeStruct((b, l_valid, GROUP_WIDTH), MIX_DTYPE),
                   jax.ShapeDtypeStruct((b, D_BUF, GROUP_WIDTH), F32)),
        grid=(b // sb, n_blk),
        in_specs=[pl.BlockSpec((sb, t_blk, GROUP_WIDTH), lambda bi, t: (bi, t, COL_DX // GROUP_WIDTH)),
                  buf_spec,
                  pl.BlockSpec((GROUP_WIDTH, GROUP_WIDTH), lambda bi, t: (0, 0)),
                  pl.BlockSpec((1, GROUP_WIDTH), lambda bi, t: (0, 0))],
        out_specs=(pl.BlockSpec((sb, t_blk, GROUP_WIDTH), lambda bi, t: (bi, t, 0)), buf_spec),
        scratch_shapes=[pltpu.VMEM((POOL_PAD + t_blk, GROUP_WIDTH), F32)] * 3,
        compiler_params=_cparams(("parallel", "arbitrary")),
        name="pool",
    )(proj3, buf, w_bd, scale)


SSD_BLOCK = 1024
HGRN_BLOCK = 1024
POOL_BLOCK = 2048


def _prep_layer(layer, norm1, w_in, a_rel_bias, b_conv_w, b_conv_b, b_dt_bias, b_a_log, b_d, b_norm, lbs, c_norm,
                d_pool_w, d_pool_scale, w_out, norm2, w_gate_up, w_down):
    w_r = _w_in_prep(jnp.transpose(w_in, (0, 2, 1)), layer)
    tab = a_rel_bias[layer].T
    m = A_BAND + CHUNK
    far = tab[:, 2 * REL_CLIP:]
    n_far = A_BAND_PREV - REL_CLIP + 1
    r = jnp.concatenate([jnp.broadcast_to(far, (HEADS, n_far)),
                         tab[:, 2 * REL_CLIP - 1:CHUNK:-1],
                         jnp.broadcast_to(far, (HEADS, m - n_far - (2 * REL_CLIP - 1 - CHUNK)))], axis=1)
    bias = jnp.tile(r, (1, CHUNK))[:, :CHUNK * (m - 1)].reshape(HEADS, CHUNK, m - 1)[:, :, :A_BAND]
    bias = bias.reshape(HEADS * CHUNK, A_BAND)
    rep = lambda p: jnp.repeat(p[layer], HEAD_DIM)[None, :]
    pw = d_pool_w[layer]
    w_bd = jnp.zeros((GROUP_WIDTH, GROUP_WIDTH), F32)
    for g in range(4):
        w_bd = w_bd.at[g * D_GROUP:(g + 1) * D_GROUP, g * D_GROUP:(g + 1) * D_GROUP].set(pw[g])
    return dict(
        n1=norm1[layer][None, :], w_in=w_r, bias=bias,
        cw=b_conv_w[layer], cb=b_conv_b[layer][None, :], dtb=rep(b_dt_bias), alog=rep(b_a_log), dvec=rep(b_d),
        bnrm=b_norm[layer][None, :], lb=lbs[layer][None, :], cnrm=c_norm[layer].reshape(1, GROUP_WIDTH),
        w_bd=w_bd.astype(BF16), psc=d_pool_scale[layer][None, :], n2=norm2[layer][None, :], layer=layer,
        w_out=w_out.astype(BF16), w_gu=w_gate_up.astype(BF16), w_dn=w_down.astype(BF16))


def _round_up(n, m):
    return -(-n // m) * m


def _mixers(proj, st, p, cache_t):
    k_prev, v_prev, conv0, ssm0, hgrn0, pool0 = st
    b, l, _ = proj.shape
    p_hist = k_prev.shape[1]
    lq = _round_up(l, CHUNK)
    pad_rows = lambda a: a if lq == l else jnp.pad(a, ((0, 0), (0, lq - l), (0, 0)))

    keep = min(A_BAND_PREV, p_hist + l)
    n_new = min(l, keep)
    ka = proj[:, l - n_new:, COL_AK:COL_AK + GROUP_WIDTH].reshape(b, n_new, HEADS, HEAD_DIM)
    va = proj[:, l - n_new:, COL_AV:COL_AV + GROUP_WIDTH].reshape(b, n_new, HEADS, HEAD_DIM)
    if cache_t is not None:
        bias_s = p["bias"].reshape(HEADS, CHUNK, A_BAND)[:, :l].reshape(HEADS * l, A_BAND)
        oa, kt_new, vt_new = _attention_cached(proj, cache_t[0], cache_t[1], p["layer"], bias_s, l)
        untranspose = lambda a: a.reshape(b, HEADS, HEAD_DIM, A_BAND_PREV).transpose(0, 3, 1, 2)
        new_k, new_v = untranspose(kt_new), untranspose(vt_new)
    else:
        if p_hist == 0:
            k_cache = v_cache = None
        else:
            k_cache = k_prev.astype(BF16).reshape(b, p_hist, GROUP_WIDTH)
            v_cache = v_prev.astype(BF16).reshape(b, p_hist, GROUP_WIDTH)
        oa = _attention(pad_rows(proj), k_cache, v_cache, p["bias"], l)[:, :l]
        new_k = jnp.concatenate([k_prev[:, p_hist - (keep - n_new):], ka], axis=1)
        new_v = jnp.concatenate([v_prev[:, p_hist - (keep - n_new):], va], axis=1)

    ob, new_conv, new_ssm = _ssd(proj, conv0, ssm0, p["cw"], p["cb"], p["dtb"], p["alog"], p["dvec"], p["bnrm"],
                                 l, SSD_BLOCK if l % SSD_BLOCK == 0 else SSD_CHUNK)

    if l % HGRN_BLOCK == 0:
        oc, new_hgrn = _hgrn(proj, COL_CQ, hgrn0, p["lb"], p["cnrm"], l, HGRN_BLOCK)
    else:
        oc, new_hgrn = _hgrn(pad_rows(proj[:, :, COL_CQ:COL_CQ + 4 * GROUP_WIDTH]), 0, hgrn0, p["lb"], p["cnrm"], l, lq)
        oc = oc[:, :l]

    od, new_pool = _pool(proj, pool0, p["w_bd"], p["psc"], l, POOL_BLOCK if l % POOL_BLOCK == 0 else l)

    flat = lambda a: a.reshape(b * l, GROUP_WIDTH)
    return (flat(oa), flat(ob), flat(oc), flat(od)), (new_k, new_v, new_conv, new_ssm, new_hgrn, new_pool)


def _trunk_groups(groups, params, nf):
    assert len(groups) <= 2
    depth = len(params)
    xs = [x.reshape(-1, D_MODEL) for x, _ in groups]
    shapes = [x.shape for x, _ in groups]
    new = [[[] for _ in range(6)] for _ in groups]
    cache_ts = []
    for x, states in groups:
        b, l, _ = x.shape
        cache_t = None
        if states[0].shape[2] == A_BAND_PREV and l <= CHUNK and l % 8 == 0:
            as_t = lambda c: jnp.transpose(c, (0, 1, 3, 4, 2)).reshape(depth, b, GROUP_WIDTH, A_BAND_PREV)
            cache_t = (as_t(states[0]), as_t(states[1]))
        cache_ts.append(cache_t)
    for layer in range(depth):
        p = params[layer]
        projs = _inproj(xs[0], p["n1"], p["w_in"], *xs[1:])
        projs = projs if len(groups) > 1 else (projs,)
        mixes = []
        for g, (_, states) in enumerate(groups):
            b, l, _ = shapes[g]
            st_l = tuple(s[layer] for s in states)
            mix, ns = _mixers(projs[g].reshape(b, l, PROJ_W), st_l, p, cache_ts[g])
            mixes.append(mix)
            for i in range(6):
                new[g][i].append(ns[i])
        extra = (xs[1], mixes[1]) if len(groups) > 1 else None
        ys = _mlp(xs[0], mixes[0], p["layer"], p["w_out"], p["n2"], p["w_gu"], p["w_dn"], nf, layer == depth - 1, extra)
        xs = list(ys) if len(groups) > 1 else [ys]
    return [(xs[g].reshape(shapes[g]), tuple(jnp.stack(n, axis=0) for n in new[g])) for g in range(len(groups))]


def _trunk(x, states, params, nf):
    return _trunk_groups([(x, states)], params, nf)[0]


def kernel(x_prompt, x_sample, cache_a_k, cache_a_v, state_b_conv, state_b_ssm, state_c_hgrn, state_d_pool, norm1,
           w_in, a_rel_bias, b_conv_w, b_conv_b, b_dt_bias, b_a_log, b_d, b_norm, c_lb_logits, c_norm, d_pool_w,
           d_pool_scale, w_out, norm2, w_gate_up, w_down, norm_f):
    depth = w_in.shape[0]
    lbs = jnp.cumsum(jax.nn.softmax(c_lb_logits.astype(F32), axis=0), axis=0)
    lbs = lbs - lbs[:1]
    params = [_prep_layer(layer, norm1, w_in, a_rel_bias, b_conv_w, b_conv_b, b_dt_bias, b_a_log, b_d, b_norm, lbs,
                          c_norm, d_pool_w, d_pool_scale, w_out, norm2, w_gate_up, w_down)
              for layer in range(depth)]
    nf = norm_f[None, :]
    bp = x_prompt.shape[0]
    prompt_states = (
        jnp.zeros((depth, bp, 0, HEADS, HEAD_DIM), F32),
        jnp.zeros((depth, bp, 0, HEADS, HEAD_DIM), F32),
        jnp.zeros((depth, bp, B_CONV - 1, B_CONV_DIM), F32),
        jnp.zeros((depth, bp, HEADS, HEAD_DIM, B_STATE), F32),
        jnp.zeros((depth, bp, HEADS, HEAD_DIM, HEAD_DIM), F32),
        jnp.zeros((depth, bp, D_BUF, GROUP_WIDTH), F32),
    )
    sample_states = (cache_a_k, cache_a_v, state_b_conv, state_b_ssm, state_c_hgrn, state_d_pool)
    (y_prompt, ps), (y_sample, ss) = _trunk_groups([(x_prompt, prompt_states), (x_sample, sample_states)], params, nf)
    return (y_prompt, y_sample) + ps + ss
```
